```python
import math
import jax, jax.numpy as jnp
from jax import lax
import numpy as np

D_MODEL = 1024
BATCH = 8
SEQ = 4096
DEPTH = 1

CHUNK = 64
EPS = 1e-6
PLE_DIM = 256
POOL_W = D_MODEL
POOL_WINDOWS = (2, 4, 8, 16)
POOL_GROUPS = len(POOL_WINDOWS)
POOL_GROUP_W = POOL_W // POOL_GROUPS
RET_HEADS = 4
RET_QK_HEAD = D_MODEL // RET_HEADS
RET_V_HEAD = 2 * RET_QK_HEAD
RET_QK_W = RET_HEADS * RET_QK_HEAD
RET_V_W = RET_HEADS * RET_V_HEAD
ROPE_BASE = 10000.0
IN_SPLITS = (POOL_W, RET_QK_W, RET_QK_W, RET_V_W, RET_V_W, D_MODEL, D_MODEL)
IN_W = sum(IN_SPLITS)
N_EXPERTS = 32
TOP_K = 4
D_FF = D_MODEL
SWIGLU_ALPHA = 1.702
SWIGLU_LIMIT = 7.0
MOE_BLOCK = 256

kernel_name = "hybrid_pool_retention_moe_block"


def _rmsnorm(x, g):
    x32 = x.astype(jnp.float32)
    y = x32 * lax.rsqrt(jnp.mean(x32 * x32, axis=-1, keepdims=True) + EPS)
    return (y * g.astype(jnp.float32)).astype(x.dtype)


def _multiscale_pool(u, w_pool, scale):
    b, s, _ = u.shape
    u32 = u.astype(jnp.float32).reshape(b, s, POOL_GROUPS, POOL_GROUP_W)
    cs = jnp.cumsum(u32, axis=1)
    t = jnp.arange(s, dtype=jnp.float32)
    outs = []
    for g, w in enumerate(POOL_WINDOWS):
        cg = cs[:, :, g]
        lag = jnp.pad(cg, ((0, 0), (w, 0), (0, 0)))[:, :s]
        count = jnp.minimum(t + 1.0, float(w))[None, :, None]
        pooled = (cg - lag) / count - u32[:, :, g]
        outs.append(jnp.einsum('bsc,cd->bsd', pooled.astype(u.dtype), w_pool[g]))
    return jnp.concatenate(outs, axis=-1) * scale


def _rotary(x, pos):
    half = x.shape[-1] // 2
    inv = ROPE_BASE ** (-jnp.linspace(0.0, 1.0, half, dtype=jnp.float32))
    ang = pos[:, None] * inv[None, :]
    cos = jnp.cos(ang)[None, :, None, :]
    sin = jnp.sin(ang)[None, :, None, :]
    x1, x2 = x[..., :half], x[..., half:]
    return jnp.concatenate([x1 * cos - x2 * sin, x2 * cos + x1 * sin], axis=-1)


def _retention(q, k, v):
    b, s, h, dk = q.shape
    dv = v.shape[-1]
    nc = s // CHUNK
    log_g = jnp.log(1.0 - 2.0 ** (-5.0 - jnp.arange(h, dtype=jnp.float32)))
    idx = jnp.arange(CHUNK, dtype=jnp.float32)
    intra = jnp.exp(log_g[:, None, None] * jnp.abs(idx[:, None] - idx[None, :]))
    q_dec = jnp.exp(log_g[:, None] * (idx + 1.0))[:, :, None]
    k_dec = jnp.exp(log_g[:, None] * (CHUNK - 1.0 - idx))[:, :, None]
    c_dec = jnp.exp(log_g * CHUNK)[:, None, None]

    def to_chunks(t):
        return t.reshape(b, nc, CHUNK, h, t.shape[-1]).transpose(1, 0, 3, 2, 4)

    def step(state, inp):
        qc, kc, vc = inp
        scores = jnp.einsum('bhid,bhjd->bhij', qc, kc) * intra
        o = (jnp.einsum('bhij,bhjv->bhiv', scores, vc)
             + jnp.einsum('bhid,bhdv->bhiv', qc * q_dec, state))
        state = state * c_dec + jnp.einsum('bhjd,bhjv->bhdv', kc * k_dec, vc)
        return state, o

    state0 = jnp.zeros((b, h, dk, dv), jnp.float32)
    _, o = lax.scan(step, state0, (to_chunks(q), to_chunks(k), to_chunks(v)))
    return o.transpose(1, 0, 3, 2, 4).reshape(b, s, h, dv)


def _moe(h2, w_router, b_router, w_gate_up, b_gate_up, w_down, b_down):
    t, d = h2.shape
    logits = h2.astype(jnp.float32) @ w_router.astype(jnp.float32) + b_router.astype(jnp.float32)
    top_v, top_i = lax.top_k(logits, TOP_K)
    gates = jax.nn.softmax(top_v, axis=-1)
    n_assign = t * TOP_K
    flat_e = top_i.reshape(-1)
    flat_tok = jnp.repeat(jnp.arange(t, dtype=jnp.int32), TOP_K)
    flat_w = gates.reshape(-1)
    order = jnp.argsort(flat_e)
    sorted_e = flat_e[order]
    counts = jnp.bincount(flat_e, length=N_EXPERTS)
    padded = ((counts + MOE_BLOCK - 1) // MOE_BLOCK) * MOE_BLOCK
    ends_pad = jnp.cumsum(padded)
    starts_pad = ends_pad - padded
    starts = jnp.cumsum(counts) - counts
    rank = jnp.arange(n_assign, dtype=jnp.int32) - starts[sorted_e]
    dest = starts_pad[sorted_e] + rank
    n_buf = n_assign + N_EXPERTS * MOE_BLOCK
    n_blocks = n_buf // MOE_BLOCK
    tok_buf = jnp.full((n_buf,), t, jnp.int32).at[dest].set(flat_tok[order])
    w_buf = jnp.zeros((n_buf,), jnp.float32).at[dest].set(flat_w[order])
    block_e = jnp.minimum(
        jnp.searchsorted(ends_pad, jnp.arange(n_blocks, dtype=ends_pad.dtype) * MOE_BLOCK, side='right'),
        N_EXPERTS - 1)
    h_pad = jnp.concatenate([h2, jnp.zeros((1, d), h2.dtype)], axis=0)

    def expert_block(args):
        tok, wts, e = args
        xb = h_pad[tok]
        gu = xb @ w_gate_up[e] + b_gate_up[e]
        glu, lin = gu[:, :D_FF], gu[:, D_FF:]
        glu = jnp.minimum(glu, SWIGLU_LIMIT)
        lin = jnp.clip(lin, -SWIGLU_LIMIT, SWIGLU_LIMIT)
        act = glu * jax.nn.sigmoid(SWIGLU_ALPHA * glu) * (lin + 1.0)
        y = act @ w_down[e] + b_down[e]
        return y.astype(jnp.float32) * wts[:, None]

    y_buf = lax.map(expert_block, (tok_buf.reshape(n_blocks, MOE_BLOCK),
                                   w_buf.reshape(n_blocks, MOE_BLOCK), block_e))
    out = jax.ops.segment_sum(y_buf.reshape(n_buf, d), tok_buf, num_segments=t + 1)[:t]
    return out.astype(h2.dtype)


def setup_inputs(seed: int = 0) -> dict:
    key = jax.random.key(seed)
    ks = jax.random.split(key, 20)
    f32 = jnp.float32

    def nrm(k, shape, fan_in):
        return jax.random.normal(k, shape, f32) * (fan_in ** -0.5)

    def gain(k, shape):
        return 1.0 + 0.05 * jax.random.normal(k, shape, f32)

    return {
        "x": jax.random.normal(ks[0], (BATCH, SEQ, D_MODEL), f32),
        "p": jax.random.normal(ks[1], (DEPTH, BATCH, SEQ, PLE_DIM), f32),
        "g_mix": gain(ks[2], (DEPTH, D_MODEL)),
        "w_in": nrm(ks[3], (DEPTH, D_MODEL, IN_W), D_MODEL),
        "w_pool": nrm(ks[4], (DEPTH, POOL_GROUPS, POOL_GROUP_W, POOL_GROUP_W), POOL_GROUP_W),
        "pool_scale": gain(ks[5], (DEPTH, POOL_W)),
        "w_ret_o": nrm(ks[6], (DEPTH, RET_V_W, D_MODEL), RET_V_W),
        "w_out": nrm(ks[7], (DEPTH, D_MODEL, D_MODEL), D_MODEL),
        "g_ffn": gain(ks[8], (DEPTH, D_MODEL)),
        "w_router": nrm(ks[9], (DEPTH, D_MODEL, N_EXPERTS), D_MODEL),
        "b_router": 0.01 * jax.random.normal(ks[10], (DEPTH, N_EXPERTS), f32),
        "w_gate_up": nrm(ks[11], (DEPTH, N_EXPERTS, D_MODEL, 2 * D_FF), D_MODEL),
        "b_gate_up": 0.02 * jax.random.normal(ks[12], (DEPTH, N_EXPERTS, 2 * D_FF), f32),
        "w_down": nrm(ks[13], (DEPTH, N_EXPERTS, D_FF, D_MODEL), D_FF),
        "b_down": 0.02 * jax.random.normal(ks[14], (DEPTH, N_EXPERTS, D_MODEL), f32),
        "g_ple": gain(ks[15], (DEPTH, D_MODEL)),
        "w_ple_gate": nrm(ks[16], (DEPTH, D_MODEL, D_MODEL), D_MODEL),
        "w_ple": nrm(ks[17], (DEPTH, PLE_DIM, D_MODEL), PLE_DIM),
        "g_final": gain(ks[18], (D_MODEL,)),
    }


def reference(x, p, g_mix, w_in, w_pool, pool_scale, w_ret_o, w_out,
              g_ffn, w_router, b_router, w_gate_up, b_gate_up, w_down, b_down,
              g_ple, w_ple_gate, w_ple, g_final):
    b, s, d = x.shape
    pos = jnp.arange(s, dtype=jnp.float32)
    split_at = list(np.cumsum(IN_SPLITS)[:-1])
    for i in range(DEPTH):
        h = _rmsnorm(x, g_mix[i])
        proj = h @ w_in[i]
        u_pool, q, k, v, g_ret, gate_a, gate_b = jnp.split(proj, split_at, axis=-1)
        y_pool = _multiscale_pool(u_pool, w_pool[i], pool_scale[i])
        q = _rotary(q.astype(jnp.float32).reshape(b, s, RET_HEADS, RET_QK_HEAD), pos)
        k = _rotary(k.astype(jnp.float32).reshape(b, s, RET_HEADS, RET_QK_HEAD), pos) * (RET_QK_HEAD ** -0.5)
        v = v.astype(jnp.float32).reshape(b, s, RET_HEADS, RET_V_HEAD)
        o = _retention(q, k, v)
        o = o * lax.rsqrt(jnp.mean(o * o, axis=-1, keepdims=True) + EPS)
        o = o.reshape(b, s, RET_V_W) * jax.nn.silu(g_ret.astype(jnp.float32))
        y_ret = o.astype(x.dtype) @ w_ret_o[i]
        merged = jax.nn.sigmoid(gate_a) * y_pool + jax.nn.sigmoid(gate_b) * y_ret
        x = x + merged @ w_out[i]
        h2 = _rmsnorm(x, g_ffn[i]).reshape(b * s, d)
        x = x + _moe(h2, w_router[i], b_router[i], w_gate_up[i], b_gate_up[i],
                     w_down[i], b_down[i]).reshape(b, s, d)
        u = _rmsnorm(x, g_ple[i])
        x = x + jax.nn.sigmoid(u @ w_ple_gate[i]) * (p[i] @ w_ple[i])
    return _rmsnorm(x, g_final)
```

```python
import functools

import numpy as np
import jax
import jax.numpy as jnp
from jax import lax
from jax.experimental import pallas as pl
from jax.experimental.pallas import tpu as pltpu

F32 = jnp.float32
BF16 = jnp.bfloat16

D_MODEL = 1024
EPS = 1e-6
CHUNK = 64
PLE_DIM = 256
POOL_WINDOWS = (2, 4, 8, 16)
POOL_GROUP_W = D_MODEL // len(POOL_WINDOWS)
POOL_HALO = 16
RET_HEADS = 4
RET_QK_HEAD = 256
RET_V_HEAD = 512
RET_V_W = RET_HEADS * RET_V_HEAD
ROPE_BASE = 10000.0
IN_W = 9 * D_MODEL
N_EXPERTS = 32
TOP_K = 4
D_FF = D_MODEL
SWIGLU_ALPHA = 1.702
SWIGLU_LIMIT = 7.0

VMEM_LIMIT_BYTES = 56 * 1024 * 1024
META_LANES = 128


def _rms(x, g):
    return x * lax.rsqrt(jnp.mean(x * x, axis=-1, keepdims=True) + EPS) * g


def _in_proj_kernel(x_ref, g_ref, w_ref, o_ref, h_ref):
    @pl.when(pl.program_id(1) == 0)
    def _():
        h_ref[...] = _rms(x_ref[...], g_ref[...]).astype(BF16)

    o_ref[...] = jnp.dot(h_ref[...], w_ref[...],
                         preferred_element_type=F32).astype(BF16)


def _in_proj(x2, g_mix, w_in_bf, tm, tn):
    t = x2.shape[0]
    return pl.pallas_call(
        _in_proj_kernel,
        out_shape=jax.ShapeDtypeStruct((t, IN_W), BF16),
        grid=(t // tm, IN_W // tn),
        in_specs=[
            pl.BlockSpec((tm, D_MODEL), lambda i, j: (i, 0)),
            pl.BlockSpec((1, D_MODEL), lambda i, j: (0, 0)),
            pl.BlockSpec((D_MODEL, tn), lambda i, j: (0, j)),
        ],
        out_specs=pl.BlockSpec((tm, tn), lambda i, j: (i, j)),
        scratch_shapes=[pltpu.VMEM((tm, D_MODEL), BF16)],
        compiler_params=pltpu.CompilerParams(
            dimension_semantics=("arbitrary", "arbitrary"),
            vmem_limit_bytes=VMEM_LIMIT_BYTES),
        name="in_proj",
    )(x2, g_mix, w_in_bf)


def _retention_kernel(cdec_ref, q_ref, k_ref, v_ref, g_ref, cos_ref, sin_ref,
                      d_ref, qd_ref, kd_ref, o_ref, state_ref):
    h = pl.program_id(1)

    @pl.when(pl.program_id(2) == 0)
    def _():
        state_ref[...] = jnp.zeros_like(state_ref)

    cos = cos_ref[...]
    sin = sin_ref[...]
    half = RET_QK_HEAD // 2

    def rotary(x):
        x1 = x[:, :half]
        x2 = x[:, half:]
        return jnp.concatenate([x1 * cos - x2 * sin, x2 * cos + x1 * sin], axis=-1)

    q = rotary(q_ref[...].astype(F32))
    k = rotary(k_ref[...].astype(F32)) * (RET_QK_HEAD ** -0.5)
    v = v_ref[...]
    scores = lax.dot_general(q.astype(BF16), k.astype(BF16),
                             (((1,), (1,)), ((), ())),
                             preferred_element_type=F32) * d_ref[...]
    state = state_ref[...]
    o = jnp.dot(scores.astype(BF16), v, preferred_element_type=F32)
    o = o + jnp.dot((q * qd_ref[...]).astype(BF16), state.astype(BF16),
                    preferred_element_type=F32)
    k_dec = (k * kd_ref[...]).astype(BF16)
    state_ref[...] = state * cdec_ref[h] + lax.dot_general(
        k_dec, v, (((0,), (0,)), ((), ())), preferred_element_type=F32)
    o = o * lax.rsqrt(jnp.mean(o * o, axis=-1, keepdims=True) + EPS)
    g = g_ref[...].astype(F32)
    o_ref[...] = (o * (g * jax.nn.sigmoid(g))).astype(BF16)


def _retention_tables(s, blk):
    half = RET_QK_HEAD // 2
    pos = jnp.arange(s, dtype=F32)
    inv = ROPE_BASE ** (-jnp.linspace(0.0, 1.0, half, dtype=F32))
    ang = pos[:, None] * inv[None, :]
    log_g = jnp.log(1.0 - 2.0 ** (-5.0 - jnp.arange(RET_HEADS, dtype=F32)))
    idx = jnp.arange(blk, dtype=F32)
    diff = idx[:, None] - idx[None, :]
    chunk = jnp.arange(blk, dtype=jnp.int32) // CHUNK
    visible = chunk[None, :] <= chunk[:, None]
    dmask = jnp.where(visible[None], jnp.exp(log_g[:, None, None] * jnp.abs(diff)[None]), 0.0)
    q_dec = jnp.exp(log_g[:, None] * (idx + 1.0))[:, :, None]
    k_dec = jnp.exp(log_g[:, None] * (blk - 1.0 - idx))[:, :, None]
    c_dec = jnp.exp(log_g * blk)
    q_dec = jnp.broadcast_to(q_dec, (RET_HEADS, blk, RET_QK_HEAD))
    k_dec = jnp.broadcast_to(k_dec, (RET_HEADS, blk, RET_QK_HEAD))
    return jnp.cos(ang), jnp.sin(ang), dmask.astype(F32), q_dec, k_dec, c_dec


def _retention(proj3, blk):
    b, s, _ = proj3.shape
    cos, sin, dmask, q_dec, k_dec, c_dec = _retention_tables(s, blk)
    q_col = D_MODEL // RET_QK_HEAD
    k_col = 2 * D_MODEL // RET_QK_HEAD
    v_col = 3 * D_MODEL // RET_V_HEAD
    g_col = 5 * D_MODEL // RET_V_HEAD
    half = RET_QK_HEAD // 2
    grid_spec = pltpu.PrefetchScalarGridSpec(
        num_scalar_prefetch=1,
        grid=(b, RET_HEADS, s // blk),
        in_specs=[
            pl.BlockSpec((None, blk, RET_QK_HEAD), lambda bi, h, l, c: (bi, l, q_col + h)),
            pl.BlockSpec((None, blk, RET_QK_HEAD), lambda bi, h, l, c: (bi, l, k_col + h)),
            pl.BlockSpec((None, blk, RET_V_HEAD), lambda bi, h, l, c: (bi, l, v_col + h)),
            pl.BlockSpec((None, blk, RET_V_HEAD), lambda bi, h, l, c: (bi, l, g_col + h)),
            pl.BlockSpec((blk, half), lambda bi, h, l, c: (l, 0)),
            pl.BlockSpec((blk, half), lambda bi, h, l, c: (l, 0)),
            pl.BlockSpec((None, blk, blk), lambda bi, h, l, c: (h, 0, 0)),
            pl.BlockSpec((None, blk, RET_QK_HEAD), lambda bi, h, l, c: (h, 0, 0)),
            pl.BlockSpec((None, blk, RET_QK_HEAD), lambda bi, h, l, c: (h, 0, 0)),
        ],
        out_specs=pl.BlockSpec((None, blk, RET_V_HEAD), lambda bi, h, l, c: (bi, l, h)),
        scratch_shapes=[pltpu.VMEM((RET_QK_HEAD, RET_V_HEAD), F32)],
    )
    return pl.pallas_call(
        _retention_kernel,
        out_shape=jax.ShapeDtypeStruct((b, s, RET_V_W), BF16),
        grid_spec=grid_spec,
        compiler_params=pltpu.CompilerParams(
            dimension_semantics=("arbitrary", "arbitrary", "arbitrary"),
            vmem_limit_bytes=VMEM_LIMIT_BYTES),
        name="retention",
    )(c_dec, proj3, proj3, proj3, proj3, cos, sin, dmask, q_dec, k_dec)


def _window_sum(ext, w, tm):
    cur = ext
    span = 1
    while span < w:
        cur = cur[span:, :] + cur[:-span, :]
        span *= 2
    start = POOL_HALO + 1 - w
    return cur[start:start + tm, :]


def _mix_route_kernel(seq_len, u_ref, halo_ref, ga_ref, gb_ref, o_ref, x_ref,
                      wpool_ref, pscale_ref, wreto_ref, wout_ref, gffn_ref,
                      wr_hi_ref, wr_lo_ref, br_ref,
                      x1_ref, h2_ref, meta_ref, counts_ref):
    i = pl.program_id(0)
    tm = x_ref.shape[0]
    pos0 = lax.rem(i * tm, seq_len)

    @pl.when(i == 0)
    def _():
        counts_ref[...] = jnp.zeros_like(counts_ref)

    u = u_ref[...].astype(F32)
    halo = jnp.where(pos0 == 0, 0.0, halo_ref[...].astype(F32))
    ext = jnp.concatenate([halo, u], axis=0)
    pos = (pos0 + lax.broadcasted_iota(jnp.int32, (tm, 1), 0)).astype(F32)
    pooled_out = []
    for g, w in enumerate(POOL_WINDOWS):
        cols = slice(g * POOL_GROUP_W, (g + 1) * POOL_GROUP_W)
        ws = _window_sum(ext[:, cols], w, tm)
        count = jnp.minimum(pos + 1.0, float(w))
        pooled = ws / count - u[:, cols]
        pooled_out.append(jnp.dot(pooled.astype(BF16), wpool_ref[g],
                                  preferred_element_type=F32))
    y_pool = jnp.concatenate(pooled_out, axis=-1) * pscale_ref[...]

    y_ret = jnp.dot(o_ref[...], wreto_ref[...], preferred_element_type=F32)
    merged = (jax.nn.sigmoid(ga_ref[...].astype(F32)) * y_pool
              + jax.nn.sigmoid(gb_ref[...].astype(F32)) * y_ret)
    x1 = x_ref[...] + jnp.dot(merged.astype(BF16), wout_ref[...],
                              preferred_element_type=F32)
    x1_ref[...] = x1
    h2 = _rms(x1, gffn_ref[...])
    h2_ref[...] = h2

    h_hi = h2.astype(BF16)
    h_lo = (h2 - h_hi.astype(F32)).astype(BF16)
    logits = (jnp.dot(h_hi, wr_hi_ref[...], preferred_element_type=F32)
              + jnp.dot(h_lo, wr_hi_ref[...], preferred_element_type=F32)
              + jnp.dot(h_hi, wr_lo_ref[...], preferred_element_type=F32)
              + br_ref[...])

    lane = lax.broadcasted_iota(jnp.int32, (tm, N_EXPERTS), 1)
    work = logits
    vals, idxs, hots = [], [], []
    for _ in range(TOP_K):
        m = jnp.max(work, axis=-1, keepdims=True)
        idx = jnp.min(jnp.where(work == m, lane, N_EXPERTS), axis=-1, keepdims=True)
        hot = lane == idx
        vals.append(m)
        idxs.append(idx)
        hots.append(hot)
        work = jnp.where(hot, -jnp.inf, work)
    exps = [jnp.exp(v - vals[0]) for v in vals]
    denom = exps[0] + exps[1] + exps[2] + exps[3]
    gates = [e / denom for e in exps]

    sel = jnp.where(hots[0] | hots[1] | hots[2] | hots[3], 1.0, 0.0)
    row = lax.broadcasted_iota(jnp.int32, (tm, tm), 0)
    col = lax.broadcasted_iota(jnp.int32, (tm, tm), 1)
    lower = jnp.where(col < row, 1.0, 0.0).astype(BF16)
    before = jnp.dot(lower, sel.astype(BF16), preferred_element_type=F32)
    rank_full = before + counts_ref[...]
    counts_ref[...] = counts_ref[...] + jnp.sum(sel, axis=0, keepdims=True)

    mlane = lax.broadcasted_iota(jnp.int32, (tm, META_LANES), 1)
    meta = jnp.zeros((tm, META_LANES), F32)
    for r in range(TOP_K):
        rank_r = jnp.sum(jnp.where(hots[r], rank_full, 0.0), axis=-1, keepdims=True)
        meta = jnp.where(mlane == r, idxs[r].astype(F32), meta)
        meta = jnp.where(mlane == TOP_K + r, rank_r, meta)
        meta = jnp.where(mlane == 2 * TOP_K + r, gates[r], meta)
    meta_ref[...] = meta


def _mix_route(proj, o_gated, x2, w_pool_bf, pool_scale, w_ret_o_bf, w_out_bf,
               g_ffn, wr_hi, wr_lo, b_router, seq_len, tm):
    t = x2.shape[0]
    halo_per_tile = tm // POOL_HALO
    full = lambda *shape: pl.BlockSpec(shape, lambda i: (0,) * len(shape))
    return pl.pallas_call(
        functools.partial(_mix_route_kernel, seq_len),
        out_shape=(
            jax.ShapeDtypeStruct((t, D_MODEL), F32),
            jax.ShapeDtypeStruct((t, D_MODEL), F32),
            jax.ShapeDtypeStruct((t, META_LANES), F32),
            jax.ShapeDtypeStruct((1, N_EXPERTS), F32),
        ),
        grid=(t // tm,),
        in_specs=[
            pl.BlockSpec((tm, D_MODEL), lambda i: (i, 0)),
            pl.BlockSpec((POOL_HALO, D_MODEL),
                         lambda i: (jnp.maximum(i * halo_per_tile - 1, 0), 0)),
            pl.BlockSpec((tm, D_MODEL), lambda i: (i, 7)),
            pl.BlockSpec((tm, D_MODEL), lambda i: (i, 8)),
            pl.BlockSpec((tm, RET_V_W), lambda i: (i, 0)),
            pl.BlockSpec((tm, D_MODEL), lambda i: (i, 0)),
            full(len(POOL_WINDOWS), POOL_GROUP_W, POOL_GROUP_W),
            full(1, D_MODEL),
            full(RET_V_W, D_MODEL),
            full(D_MODEL, D_MODEL),
            full(1, D_MODEL),
            full(D_MODEL, N_EXPERTS),
            full(D_MODEL, N_EXPERTS),
            full(1, N_EXPERTS),
        ],
        out_specs=(
            pl.BlockSpec((tm, D_MODEL), lambda i: (i, 0)),
            pl.BlockSpec((tm, D_MODEL), lambda i: (i, 0)),
            pl.BlockSpec((tm, META_LANES), lambda i: (i, 0)),
            pl.BlockSpec((1, N_EXPERTS), lambda i: (0, 0)),
        ),
        compiler_params=pltpu.CompilerParams(
            dimension_semantics=("arbitrary",),
            vmem_limit_bytes=VMEM_LIMIT_BYTES),
        name="mix_route",
    )(proj, proj, proj, proj, o_gated, x2, w_pool_bf, pool_scale, w_ret_o_bf,
      w_out_bf, g_ffn, wr_hi, wr_lo, b_router)


def _dispatch_kernel(blk_rows, zstart_ref, zflag_ref, dest_ref, h_ref, xs_ref,
                     zero_ref, sem, zsem):
    td = h_ref.shape[0]

    @pl.when(pl.program_id(0) == 0)
    def _():
        zero_ref[...] = jnp.zeros_like(zero_ref)

        def zcopy(e):
            start = pl.multiple_of(zstart_ref[e], blk_rows)
            return pltpu.make_async_copy(zero_ref, xs_ref.at[pl.ds(start, blk_rows), :], zsem)

        def zissue(e, c):
            @pl.when(zflag_ref[e] > 0)
            def _():
                zcopy(e).start()
            return c

        def zwait(e, c):
            @pl.when(zflag_ref[e] > 0)
            def _():
                zcopy(e).wait()
            return c

        lax.fori_loop(0, N_EXPERTS, zissue, 0)
        lax.fori_loop(0, N_EXPERTS, zwait, 0)

    def row_copy(tok, r):
        d = dest_ref[tok * TOP_K + r]
        return pltpu.make_async_copy(h_ref.at[pl.ds(tok, 1), :],
                                     xs_ref.at[pl.ds(d, 1), :], sem)

    def issue(tok, c):
        for r in range(TOP_K):
            row_copy(tok, r).start()
        return c

    def drain(tok, c):
        for r in range(TOP_K):
            row_copy(tok, r).wait()
        return c

    lax.fori_loop(0, td, issue, 0)
    lax.fori_loop(0, td, drain, 0)


def _dispatch(h2, dest_flat, zstart, zflag, n_buf, blk_rows, td):
    t = h2.shape[0]
    grid_spec = pltpu.PrefetchScalarGridSpec(
        num_scalar_prefetch=2,
        grid=(t // td,),
        in_specs=[
            pl.BlockSpec((td * TOP_K,), lambda i, zs, zf: (i,), memory_space=pltpu.SMEM),
            pl.BlockSpec((td, D_MODEL), lambda i, zs, zf: (i, 0)),
        ],
        out_specs=pl.BlockSpec(memory_space=pl.ANY),
        scratch_shapes=[pltpu.VMEM((blk_rows, D_MODEL), F32),
                        pltpu.SemaphoreType.DMA(()),
                        pltpu.SemaphoreType.DMA(())],
    )
    return pl.pallas_call(
        functools.partial(_dispatch_kernel, blk_rows),
        out_shape=jax.ShapeDtypeStruct((n_buf, D_MODEL), F32),
        grid_spec=grid_spec,
        compiler_params=pltpu.CompilerParams(
            dimension_semantics=("arbitrary",),
            vmem_limit_bytes=VMEM_LIMIT_BYTES,
            has_side_effects=True),
        name="dispatch",
    )(zstart, zflag, dest_flat, h2)


def _experts_kernel(be_ref, nu_ref, x_ref, wgu_ref, bgu_ref, wd_ref, bd_ref,
                    y_ref, wgu_bf, wd_bf):
    i = pl.program_id(0)

    @pl.when(i < nu_ref[0])
    def _():
        prev = be_ref[jnp.maximum(i - 1, 0)]

        @pl.when((i == 0) | (be_ref[i] != prev))
        def _():
            wgu_bf[...] = wgu_ref[...].astype(BF16)
            wd_bf[...] = wd_ref[...].astype(BF16)

        gu = jnp.dot(x_ref[...].astype(BF16), wgu_bf[...],
                     preferred_element_type=F32) + bgu_ref[...]
        glu = jnp.minimum(gu[:, :D_FF], SWIGLU_LIMIT)
        lin = jnp.clip(gu[:, D_FF:], -SWIGLU_LIMIT, SWIGLU_LIMIT)
        act = glu * jax.nn.sigmoid(SWIGLU_ALPHA * glu) * (lin + 1.0)
        y_ref[...] = jnp.dot(act.astype(BF16), wd_bf[...],
                             preferred_element_type=F32) + bd_ref[...]


def _experts(xs, block_e, n_used, w_gate_up, b_gate_up, w_down, b_down, blk_rows):
    n_buf = xs.shape[0]
    used = lambda i, be, nu: jnp.minimum(i, nu[0] - 1)
    grid_spec = pltpu.PrefetchScalarGridSpec(
        num_scalar_prefetch=2,
        grid=(n_buf // blk_rows,),
        in_specs=[
            pl.BlockSpec((blk_rows, D_MODEL), lambda i, be, nu: (used(i, be, nu), 0)),
            pl.BlockSpec((None, D_MODEL, 2 * D_FF), lambda i, be, nu: (be[i], 0, 0)),
            pl.BlockSpec((None, 1, 2 * D_FF), lambda i, be, nu: (be[i], 0, 0)),
            pl.BlockSpec((None, D_FF, D_MODEL), lambda i, be, nu: (be[i], 0, 0)),
            pl.BlockSpec((None, 1, D_MODEL), lambda i, be, nu: (be[i], 0, 0)),
        ],
        out_specs=pl.BlockSpec((blk_rows, D_MODEL), lambda i, be, nu: (used(i, be, nu), 0)),
        scratch_shapes=[pltpu.VMEM((D_MODEL, 2 * D_FF), BF16),
                        pltpu.VMEM((D_FF, D_MODEL), BF16)],
    )
    return pl.pallas_call(
        _experts_kernel,
        out_shape=jax.ShapeDtypeStruct((n_buf, D_MODEL), F32),
        grid_spec=grid_spec,
        compiler_params=pltpu.CompilerParams(
            dimension_semantics=("arbitrary",),
            vmem_limit_bytes=VMEM_LIMIT_BYTES),
        name="experts",
    )(block_e, n_used, xs, w_gate_up, b_gate_up[:, None, :], w_down, b_down[:, None, :])


def _combine_kernel(dest_ref, meta_ref, x1_ref, p_ref, ys_ref, gple_ref, wpg_ref,
                    wple_ref, gfin_ref, out_ref, rows_ref, sem):
    tc = x1_ref.shape[0]

    def row_copy(tok, r):
        d = dest_ref[tok * TOP_K + r]
        return pltpu.make_async_copy(ys_ref.at[pl.ds(d, 1), :],
                                     rows_ref.at[r, pl.ds(tok, 1), :], sem)

    def issue(tok, c):
        for r in range(TOP_K):
            row_copy(tok, r).start()
        return c

    def drain(tok, c):
        for r in range(TOP_K):
            row_copy(tok, r).wait()
        return c

    lax.fori_loop(0, tc, issue, 0)
    lax.fori_loop(0, tc, drain, 0)

    meta = meta_ref[...]
    moe = jnp.zeros((tc, D_MODEL), F32)
    for r in range(TOP_K):
        gate = meta[:, 2 * TOP_K + r:2 * TOP_K + r + 1]
        moe = moe + gate * rows_ref[r]
    x2 = x1_ref[...] + moe
    u = _rms(x2, gple_ref[...])
    gate = jax.nn.sigmoid(jnp.dot(u.astype(BF16), wpg_ref[...],
                                  preferred_element_type=F32))
    emb = jnp.dot(p_ref[...].astype(BF16), wple_ref[...], preferred_element_type=F32)
    x3 = x2 + gate * emb
    out_ref[...] = _rms(x3, gfin_ref[...])


def _combine(dest_flat, meta, x1, p2, ys, g_ple, w_ple_gate_bf, w_ple_bf, g_final, tc):
    t = x1.shape[0]
    full = lambda *shape: pl.BlockSpec(shape, lambda i: (0,) * len(shape))
    return pl.pallas_call(
        _combine_kernel,
        out_shape=jax.ShapeDtypeStruct((t, D_MODEL), F32),
        grid=(t // tc,),
        in_specs=[
            pl.BlockSpec((tc * TOP_K,), lambda i: (i,), memory_space=pltpu.SMEM),
            pl.BlockSpec((tc, META_LANES), lambda i: (i, 0)),
            pl.BlockSpec((tc, D_MODEL), lambda i: (i, 0)),
            pl.BlockSpec((tc, PLE_DIM), lambda i: (i, 0)),
            pl.BlockSpec(memory_space=pl.ANY),
            full(1, D_MODEL),
            full(D_MODEL, D_MODEL),
            full(PLE_DIM, D_MODEL),
            full(1, D_MODEL),
        ],
        out_specs=pl.BlockSpec((tc, D_MODEL), lambda i: (i, 0)),
        scratch_shapes=[pltpu.VMEM((TOP_K, tc, D_MODEL), F32),
                        pltpu.SemaphoreType.DMA(())],
        compiler_params=pltpu.CompilerParams(
            dimension_semantics=("arbitrary",),
            vmem_limit_bytes=VMEM_LIMIT_BYTES),
        name="combine",
    )(dest_flat, meta, x1, p2, ys, g_ple, w_ple_gate_bf, w_ple_bf, g_final)


def _tiles(seq_len):
    return dict(
        proj_rows=min(1024, seq_len),
        proj_cols=1024,
        ret_block=min(256, seq_len),
        mix_rows=min(512, seq_len),
        dispatch_rows=min(512, seq_len),
        expert_rows=256,
        combine_rows=min(256, seq_len),
    )


def _layer(x2, p2, seq_len, g_mix, w_in, w_pool, pool_scale, w_ret_o, w_out, g_ffn,
           w_router, b_router, w_gate_up, b_gate_up, w_down, b_down, g_ple,
           w_ple_gate, w_ple, g_out):
    t = x2.shape[0]
    cfg = _tiles(seq_len)
    row = lambda a: a.reshape(1, -1)

    proj = _in_proj(x2, row(g_mix), w_in.astype(BF16), cfg["proj_rows"], cfg["proj_cols"])
    o_gated = _retention(proj.reshape(t // seq_len, seq_len, IN_W), cfg["ret_block"])
    o_gated = o_gated.reshape(t, RET_V_W)

    wr_hi = w_router.astype(BF16)
    wr_lo = (w_router - wr_hi.astype(F32)).astype(BF16)
    x1, h2, meta, counts = _mix_route(
        proj, o_gated, x2, w_pool.astype(BF16), row(pool_scale), w_ret_o.astype(BF16),
        w_out.astype(BF16), row(g_ffn), wr_hi, wr_lo, row(b_router), seq_len,
        cfg["mix_rows"])

    blk = cfg["expert_rows"]
    n_buf = t * TOP_K + N_EXPERTS * blk
    counts = counts[0].astype(jnp.int32)
    padded = ((counts + blk - 1) // blk) * blk
    ends_pad = jnp.cumsum(padded)
    starts_pad = ends_pad - padded
    top_e = meta[:, 0:TOP_K].astype(jnp.int32)
    rank = meta[:, TOP_K:2 * TOP_K].astype(jnp.int32)
    dest = (starts_pad[top_e] + rank).reshape(-1)
    n_blocks = n_buf // blk
    block_e = jnp.minimum(
        jnp.searchsorted(ends_pad, jnp.arange(n_blocks, dtype=jnp.int32) * blk, side="right"),
        N_EXPERTS - 1).astype(jnp.int32)
    n_used = (ends_pad[-1:] // blk).astype(jnp.int32)
    zstart = (ends_pad - blk).astype(jnp.int32)
    zflag = (padded > 0).astype(jnp.int32)

    xs = _dispatch(h2, dest, zstart, zflag, n_buf, blk, cfg["dispatch_rows"])
    ys = _experts(xs, block_e, n_used, w_gate_up, b_gate_up, w_down, b_down, blk)
    return _combine(dest, meta, x1, p2, ys, row(g_ple), w_ple_gate.astype(BF16),
                    w_ple.astype(BF16), row(g_out), cfg["combine_rows"])


def kernel(x, p, g_mix, w_in, w_pool, pool_scale, w_ret_o, w_out, g_ffn, w_router,
           b_router, w_gate_up, b_gate_up, w_down, b_down, g_ple, w_ple_gate, w_ple,
           g_final):
    b, s, d = x.shape
    depth = p.shape[0]
    assert depth == 1 and d == D_MODEL
    x2 = x.reshape(b * s, d)
    out = _layer(x2, p[0].reshape(b * s, PLE_DIM), s, g_mix[0], w_in[0], w_pool[0],
                 pool_scale[0], w_ret_o[0], w_out[0], g_ffn[0], w_router[0],
                 b_router[0], w_gate_up[0], b_gate_up[0], w_down[0], b_down[0],
                 g_ple[0], w_ple_gate[0], w_ple[0], g_final)
    return out.reshape(b, s, d)
```

```python
import functools

import numpy as np
import jax
import jax.numpy as jnp
from jax import lax
from jax.experimental import pallas as pl
from jax.experimental.pallas import tpu as pltpu

F32 = jnp.float32
BF16 = jnp.bfloat16

D_MODEL = 1024
EPS = 1e-6
CHUNK = 64
PLE_DIM = 256
POOL_WINDOWS = (2, 4, 8, 16)
POOL_GROUP_W = D_MODEL // len(POOL_WINDOWS)
POOL_HALO = 16
RET_HEADS = 4
RET_QK_HEAD = 256
RET_V_HEAD = 512
RET_V_W = RET_HEADS * RET_V_HEAD
ROPE_BASE = 10000.0
IN_W = 9 * D_MODEL
N_EXPERTS = 32
TOP_K = 4
D_FF = D_MODEL
SWIGLU_ALPHA = 1.702
SWIGLU_LIMIT = 7.0

VMEM_LIMIT_BYTES = 56 * 1024 * 1024
META_LANES = 128
META_EXPERT, META_RANK, META_GATE = 0, TOP_K, 2 * TOP_K
STRIP_ALIGN = 8
N_SEG = N_EXPERTS + 1


def _rms(x, g):
    return x * lax.rsqrt(jnp.mean(x * x, axis=-1, keepdims=True) + EPS) * g


COL_Q, COL_K = 1, 2
COL_V, COL_G = (3, 4), (5, 6)
COL_GATE_A, COL_GATE_B = 7, 8


def _rotary_heads(y, cos, sin):
    half = RET_QK_HEAD // 2
    parts = []
    for h in range(RET_HEADS):
        x1 = y[:, h * RET_QK_HEAD:h * RET_QK_HEAD + half]
        x2 = y[:, h * RET_QK_HEAD + half:(h + 1) * RET_QK_HEAD]
        parts += [x1 * cos - x2 * sin, x2 * cos + x1 * sin]
    return jnp.concatenate(parts, axis=-1)


def _in_proj_kernel(x_ref, g_ref, w_ref, cos_ref, sin_ref, o_ref):
    h = _rms(x_ref[...], g_ref[...]).astype(BF16)
    for c in range(IN_W // D_MODEL):
        cols = slice(c * D_MODEL, (c + 1) * D_MODEL)
        y = jnp.dot(h, w_ref[:, cols], preferred_element_type=F32)
        if c == COL_Q:
            y = _rotary_heads(y, cos_ref[...], sin_ref[...])
        elif c == COL_K:
            y = _rotary_heads(y, cos_ref[...], sin_ref[...]) * (RET_QK_HEAD ** -0.5)
        elif c in COL_G:
            y = y * jax.nn.sigmoid(y)
        o_ref[:, cols] = y.astype(BF16)


def _rotary_tables(s):
    half = RET_QK_HEAD // 2
    pos = jnp.arange(s, dtype=F32)
    inv = ROPE_BASE ** (-jnp.linspace(0.0, 1.0, half, dtype=F32))
    ang = pos[:, None] * inv[None, :]
    return jnp.cos(ang), jnp.sin(ang)


def _in_proj(x2, g_mix, w_in_bf, seq_len, tm):
    t = x2.shape[0]
    half = RET_QK_HEAD // 2
    cos, sin = _rotary_tables(seq_len)
    tiles_per_seq = seq_len // tm
    return pl.pallas_call(
        _in_proj_kernel,
        out_shape=jax.ShapeDtypeStruct((t, IN_W), BF16),
        grid=(t // tm,),
        in_specs=[
            pl.BlockSpec((tm, D_MODEL), lambda i: (i, 0)),
            pl.BlockSpec((1, D_MODEL), lambda i: (0, 0)),
            pl.BlockSpec((D_MODEL, IN_W), lambda i: (0, 0), pipeline_mode=pl.Buffered(1)),
            pl.BlockSpec((tm, half), lambda i: (lax.rem(i, tiles_per_seq), 0)),
            pl.BlockSpec((tm, half), lambda i: (lax.rem(i, tiles_per_seq), 0)),
        ],
        out_specs=pl.BlockSpec((tm, IN_W), lambda i: (i, 0)),
        compiler_params=pltpu.CompilerParams(
            dimension_semantics=("arbitrary",),
            vmem_limit_bytes=VMEM_LIMIT_BYTES),
        name="in_proj",
    )(x2, g_mix, w_in_bf, cos, sin)


def _head_decay_logs():
    return [float(np.log(1.0 - 2.0 ** (-5.0 - h))) for h in range(RET_HEADS)]


def _retention_kernel(blk_decay, q_ref, k_ref, va_ref, vb_ref, ga_ref, gb_ref,
                      d_ref, qd_ref, kd_ref, o_ref, state_ref):
    @pl.when(pl.program_id(1) == 0)
    def _():
        state_ref[...] = jnp.zeros_like(state_ref)

    heads_per_ref = D_MODEL // RET_V_HEAD
    for h in range(RET_HEADS):
        qk_cols = slice(h * RET_QK_HEAD, (h + 1) * RET_QK_HEAD)
        v_cols = slice((h % heads_per_ref) * RET_V_HEAD, (h % heads_per_ref + 1) * RET_V_HEAD)
        q = q_ref[:, qk_cols]
        k = k_ref[:, qk_cols]
        v = (va_ref if h < heads_per_ref else vb_ref)[:, v_cols]
        g = (ga_ref if h < heads_per_ref else gb_ref)[:, v_cols]
        scores = lax.dot_general(q, k, (((1,), (1,)), ((), ())),
                                 preferred_element_type=F32) * d_ref[h]
        state = state_ref[h]
        o = jnp.dot(scores.astype(BF16), v, preferred_element_type=F32)
        o = o + jnp.dot((q.astype(F32) * qd_ref[h]).astype(BF16), state.astype(BF16),
                        preferred_element_type=F32)
        k_dec = (k.astype(F32) * kd_ref[h]).astype(BF16)
        state_ref[h] = state * blk_decay[h] + lax.dot_general(
            k_dec, v, (((0,), (0,)), ((), ())), preferred_element_type=F32)
        o = o * lax.rsqrt(jnp.mean(o * o, axis=-1, keepdims=True) + EPS)
        o_ref[:, h * RET_V_HEAD:(h + 1) * RET_V_HEAD] = (o * g.astype(F32)).astype(BF16)


def _retention_tables(blk):
    log_g = jnp.asarray(_head_decay_logs(), F32)
    idx = jnp.arange(blk, dtype=F32)
    diff = idx[:, None] - idx[None, :]
    chunk = jnp.arange(blk, dtype=jnp.int32) // CHUNK
    visible = chunk[None, :] <= chunk[:, None]
    dmask = jnp.where(visible[None], jnp.exp(log_g[:, None, None] * jnp.abs(diff)[None]), 0.0)
    q_dec = jnp.exp(log_g[:, None] * (idx + 1.0))[:, :, None]
    k_dec = jnp.exp(log_g[:, None] * (blk - 1.0 - idx))[:, :, None]
    q_dec = jnp.broadcast_to(q_dec, (RET_HEADS, blk, RET_QK_HEAD))
    k_dec = jnp.broadcast_to(k_dec, (RET_HEADS, blk, RET_QK_HEAD))
    return dmask.astype(F32), q_dec, k_dec


def _retention(proj3, blk):
    b, s, _ = proj3.shape
    dmask, q_dec, k_dec = _retention_tables(blk)
    blk_decay = [float(np.exp(lg * blk)) for lg in _head_decay_logs()]
    chunk = lambda c: pl.BlockSpec((None, blk, D_MODEL), lambda bi, l: (bi, l, c))
    table = lambda *shape: pl.BlockSpec(shape, lambda bi, l: (0,) * len(shape))
    return pl.pallas_call(
        functools.partial(_retention_kernel, blk_decay),
        out_shape=jax.ShapeDtypeStruct((b, s, RET_V_W), BF16),
        grid=(b, s // blk),
        in_specs=[
            chunk(COL_Q), chunk(COL_K), chunk(COL_V[0]), chunk(COL_V[1]),
            chunk(COL_G[0]), chunk(COL_G[1]),
            table(RET_HEADS, blk, blk),
            table(RET_HEADS, blk, RET_QK_HEAD),
            table(RET_HEADS, blk, RET_QK_HEAD),
        ],
        out_specs=pl.BlockSpec((None, blk, RET_V_W), lambda bi, l: (bi, l, 0)),
        scratch_shapes=[pltpu.VMEM((RET_HEADS, RET_QK_HEAD, RET_V_HEAD), F32)],
        compiler_params=pltpu.CompilerParams(
            dimension_semantics=("arbitrary", "arbitrary"),
            vmem_limit_bytes=VMEM_LIMIT_BYTES),
        name="retention",
    )(proj3, proj3, proj3, proj3, proj3, proj3, dmask, q_dec, k_dec)


def _window_sum(ext, w, tm):
    cur = ext
    span = 1
    while span < w:
        cur = cur[span:, :] + cur[:-span, :]
        span *= 2
    start = POOL_HALO + 1 - w
    return cur[start:start + tm, :]


def _mix_route_kernel(seq_len, u_ref, halo_ref, ga_ref, gb_ref, o_ref, x_ref,
                      wpool_ref, pscale_ref, wreto_ref, wout_ref, gffn_ref,
                      wr_hi_ref, wr_lo_ref, br_ref,
                      x1_ref, h2_ref, meta_ref, counts_ref):
    i = pl.program_id(0)
    tm = x_ref.shape[0]
    pos0 = lax.rem(i * tm, seq_len)

    u = u_ref[...].astype(F32)
    halo = jnp.where(pos0 == 0, 0.0, halo_ref[...].astype(F32))
    ext = jnp.concatenate([halo, u], axis=0)
    pos = (pos0 + lax.broadcasted_iota(jnp.int32, (tm, 1), 0)).astype(F32)
    pooled_out = []
    for g, w in enumerate(POOL_WINDOWS):
        cols = slice(g * POOL_GROUP_W, (g + 1) * POOL_GROUP_W)
        ws = _window_sum(ext[:, cols], w, tm)
        count = jnp.minimum(pos + 1.0, float(w))
        pooled = ws / count - u[:, cols]
        pooled_out.append(jnp.dot(pooled.astype(BF16), wpool_ref[g],
                                  preferred_element_type=F32))
    y_pool = jnp.concatenate(pooled_out, axis=-1) * pscale_ref[...]

    y_ret = jnp.dot(o_ref[...], wreto_ref[...], preferred_element_type=F32)
    merged = (jax.nn.sigmoid(ga_ref[...].astype(F32)) * y_pool
              + jax.nn.sigmoid(gb_ref[...].astype(F32)) * y_ret)
    x1 = x_ref[...] + jnp.dot(merged.astype(BF16), wout_ref[...],
                              preferred_element_type=F32)
    x1_ref[...] = x1
    h2 = _rms(x1, gffn_ref[...])
    h2_ref[...] = h2.astype(BF16)

    h_hi = h2.astype(BF16)
    h_lo = (h2 - h_hi.astype(F32)).astype(BF16)
    logits = (jnp.dot(h_hi, wr_hi_ref[...], preferred_element_type=F32)
              + jnp.dot(h_lo, wr_hi_ref[...], preferred_element_type=F32)
              + jnp.dot(h_hi, wr_lo_ref[...], preferred_element_type=F32)
              + br_ref[...])

    lane = lax.broadcasted_iota(jnp.int32, (tm, N_EXPERTS), 1)
    work = logits
    vals, idxs, hots = [], [], []
    for _ in range(TOP_K):
        m = jnp.max(work, axis=-1, keepdims=True)
        idx = jnp.min(jnp.where(work == m, lane, N_EXPERTS), axis=-1, keepdims=True)
        hot = lane == idx
        vals.append(m)
        idxs.append(idx)
        hots.append(hot)
        work = jnp.where(hot, -jnp.inf, work)
    exps = [jnp.exp(v - vals[0]) for v in vals]
    denom = exps[0] + exps[1] + exps[2] + exps[3]
    gates = [e / denom for e in exps]

    sel = (jnp.where(hots[0], 1.0, 0.0) + jnp.where(hots[1], 1.0, 0.0)
           + jnp.where(hots[2], 1.0, 0.0) + jnp.where(hots[3], 1.0, 0.0))
    row = lax.broadcasted_iota(jnp.int32, (tm, tm), 0)
    col = lax.broadcasted_iota(jnp.int32, (tm, tm), 1)
    lower = jnp.where(col < row, 1.0, 0.0).astype(BF16)
    before = jnp.dot(lower, sel.astype(BF16), preferred_element_type=F32)
    counts_ref[...] = jnp.sum(sel, axis=0, keepdims=True)

    mlane = lax.broadcasted_iota(jnp.int32, (tm, META_LANES), 1)
    meta = jnp.zeros((tm, META_LANES), F32)
    for r in range(TOP_K):
        rank_r = jnp.sum(jnp.where(hots[r], before, 0.0), axis=-1, keepdims=True)
        meta = jnp.where(mlane == META_EXPERT + r, idxs[r].astype(F32), meta)
        meta = jnp.where(mlane == META_RANK + r, rank_r, meta)
        meta = jnp.where(mlane == META_GATE + r, gates[r], meta)
    meta_ref[...] = meta


def _mix_route(proj, o_gated, x2, w_pool_bf, pool_scale, w_ret_o_bf, w_out_bf,
               g_ffn, wr_hi, wr_lo, b_router, seq_len, tm):
    t = x2.shape[0]
    halo_per_tile = tm // POOL_HALO
    full = lambda *shape: pl.BlockSpec(shape, lambda i: (0,) * len(shape))
    return pl.pallas_call(
        functools.partial(_mix_route_kernel, seq_len),
        out_shape=(
            jax.ShapeDtypeStruct((t, D_MODEL), F32),
            jax.ShapeDtypeStruct((t, D_MODEL), BF16),
            jax.ShapeDtypeStruct((t, META_LANES), F32),
            jax.ShapeDtypeStruct((t // tm, 1, N_EXPERTS), F32),
        ),
        grid=(t // tm,),
        in_specs=[
            pl.BlockSpec((tm, D_MODEL), lambda i: (i, 0)),
            pl.BlockSpec((POOL_HALO, D_MODEL),
                         lambda i: (jnp.maximum(i * halo_per_tile - 1, 0), 0)),
            pl.BlockSpec((tm, D_MODEL), lambda i: (i, COL_GATE_A)),
            pl.BlockSpec((tm, D_MODEL), lambda i: (i, COL_GATE_B)),
            pl.BlockSpec((tm, RET_V_W), lambda i: (i, 0)),
            pl.BlockSpec((tm, D_MODEL), lambda i: (i, 0)),
            full(len(POOL_WINDOWS), POOL_GROUP_W, POOL_GROUP_W),
            full(1, D_MODEL),
            full(RET_V_W, D_MODEL),
            full(D_MODEL, D_MODEL),
            full(1, D_MODEL),
            full(D_MODEL, N_EXPERTS),
            full(D_MODEL, N_EXPERTS),
            full(1, N_EXPERTS),
        ],
        out_specs=(
            pl.BlockSpec((tm, D_MODEL), lambda i: (i, 0)),
            pl.BlockSpec((tm, D_MODEL), lambda i: (i, 0)),
            pl.BlockSpec((tm, META_LANES), lambda i: (i, 0)),
            pl.BlockSpec((None, 1, N_EXPERTS), lambda i: (i, 0, 0)),
        ),
        compiler_params=pltpu.CompilerParams(
            dimension_semantics=("arbitrary",),
            vmem_limit_bytes=VMEM_LIMIT_BYTES),
        name="mix_route",
    )(proj, proj, proj, proj, o_gated, x2, w_pool_bf, pool_scale, w_ret_o_bf,
      w_out_bf, g_ffn, wr_hi, wr_lo, b_router)


def _local_positions(meta, lstart):
    tm = meta.shape[0]
    lane_e = lax.broadcasted_iota(jnp.int32, (tm, N_EXPERTS), 1)
    out = []
    for r in range(TOP_K):
        e_r = meta[:, META_EXPERT + r:META_EXPERT + r + 1].astype(jnp.int32)
        first = jnp.sum(jnp.where(lane_e == e_r, lstart, 0.0), axis=-1, keepdims=True)
        out.append((meta[:, META_RANK + r:META_RANK + r + 1] + first).astype(jnp.int32))
    return out


def _local_rows(tm):
    return TOP_K * tm + N_EXPERTS * STRIP_ALIGN


def _for_each_strip(count, max_rows, start_copy):
    piece = 1 << (max_rows.bit_length() - 1)
    off = jnp.int32(0)
    while piece >= STRIP_ALIGN:
        take = count & piece

        @pl.when(take != 0)
        def _(off=off, piece=piece):
            start_copy(off, piece)

        off = off + take
        piece //= 2


def _dispatch_kernel(blk_rows, base_ref, lstart_ref, cnt_ref, zflag_ref,
                     h_ref, meta_ref, lstart_row_ref, xs_ref,
                     sorted_ref, zero_ref, sems, zsem):
    i = pl.program_id(0)
    tm = h_ref.shape[0]
    nrows = _local_rows(tm)
    slot = lax.rem(i, 2)

    @pl.when(i == 0)
    def _():
        zero_ref[...] = jnp.zeros_like(zero_ref)

        def zcopy(b):
            start = pl.multiple_of(b * blk_rows, blk_rows)
            return pltpu.make_async_copy(zero_ref, xs_ref.at[pl.ds(start, blk_rows), :], zsem)

        def zissue(b, c):
            @pl.when(zflag_ref[b] > 0)
            def _():
                zcopy(b).start()
            return c

        def zwait(b, c):
            @pl.when(zflag_ref[b] > 0)
            def _():
                zcopy(b).wait()
            return c

        n_zero_blocks = xs_ref.shape[0] // blk_rows
        lax.fori_loop(0, n_zero_blocks, zissue, 0)
        lax.fori_loop(0, n_zero_blocks, zwait, 0)

    pos = _local_positions(meta_ref[...], lstart_row_ref[...])
    jl = lax.broadcasted_iota(jnp.int32, (tm, nrows), 1)
    onehot = jnp.where(jl == pos[0], 1.0, jnp.where(jl == pos[1], 1.0,
             jnp.where(jl == pos[2], 1.0, jnp.where(jl == pos[3], 1.0, 0.0))))
    srt = lax.dot_general(onehot.astype(BF16), h_ref[...], (((0,), (0,)), ((), ())),
                          preferred_element_type=F32)
    sorted_ref[slot] = srt

    def per_segment(e, c):
        k = i * N_SEG + e
        src0 = lstart_ref[k]
        dst0 = base_ref[k]

        def start_copy(off, rows):
            src = pl.multiple_of(src0 + off, STRIP_ALIGN)
            dst = pl.multiple_of(dst0 + off, STRIP_ALIGN)
            pltpu.make_async_copy(sorted_ref.at[slot, pl.ds(src, rows), :],
                                  xs_ref.at[pl.ds(dst, rows), :],
                                  sems.at[slot]).start()

        _for_each_strip(cnt_ref[k], tm, start_copy)
        return c

    lax.fori_loop(0, N_SEG, per_segment, 0)

    def wait_tile(s):
        pltpu.make_async_copy(sorted_ref.at[s], xs_ref.at[pl.ds(0, nrows), :],
                              sems.at[s]).wait()

    @pl.when(i >= 1)
    def _():
        wait_tile(1 - slot)

    @pl.when(i == pl.num_programs(0) - 1)
    def _():
        wait_tile(slot)


def _dispatch(h2, meta, tables, n_buf, blk_rows, tm):
    t = h2.shape[0]
    grid_spec = pltpu.PrefetchScalarGridSpec(
        num_scalar_prefetch=4,
        grid=(t // tm,),
        in_specs=[
            pl.BlockSpec((tm, D_MODEL), lambda i, *_: (i, 0)),
            pl.BlockSpec((tm, META_LANES), lambda i, *_: (i, 0)),
            pl.BlockSpec((None, 1, N_EXPERTS), lambda i, *_: (i, 0, 0)),
        ],
        out_specs=pl.BlockSpec(memory_space=pl.ANY),
        scratch_shapes=[pltpu.VMEM((2, _local_rows(tm), D_MODEL), F32),
                        pltpu.VMEM((blk_rows, D_MODEL), F32),
                        pltpu.SemaphoreType.DMA((2,)),
                        pltpu.SemaphoreType.DMA(())],
    )
    return pl.pallas_call(
        functools.partial(_dispatch_kernel, blk_rows),
        out_shape=jax.ShapeDtypeStruct((n_buf + 2 * blk_rows, D_MODEL), F32),
        grid_spec=grid_spec,
        compiler_params=pltpu.CompilerParams(
            dimension_semantics=("arbitrary",),
            vmem_limit_bytes=VMEM_LIMIT_BYTES,
            has_side_effects=True),
        name="dispatch",
    )(tables["base_out"], tables["lstart"], tables["cnt"], tables["zflag"],
      h2, meta, tables["lstart_rows"])


def _experts_kernel(be_ref, nu_ref, x_ref, wgu_ref, bgu_ref, wd_ref, bd_ref,
                    y_ref, wgu_bf, wd_bf):
    i = pl.program_id(0)

    @pl.when(i < nu_ref[0])
    def _():
        prev = be_ref[jnp.maximum(i - 1, 0)]

        @pl.when((i == 0) | (be_ref[i] != prev))
        def _():
            wgu_bf[...] = wgu_ref[...].astype(BF16)
            wd_bf[...] = wd_ref[...].astype(BF16)

        gu = jnp.dot(x_ref[...].astype(BF16), wgu_bf[...],
                     preferred_element_type=F32) + bgu_ref[...]
        glu = jnp.minimum(gu[:, :D_FF], SWIGLU_LIMIT)
        lin = jnp.clip(gu[:, D_FF:], -SWIGLU_LIMIT, SWIGLU_LIMIT)
        act = glu * jax.nn.sigmoid(SWIGLU_ALPHA * glu) * (lin + 1.0)
        y_ref[...] = jnp.dot(act.astype(BF16), wd_bf[...],
                             preferred_element_type=F32) + bd_ref[...]

    @pl.when(i >= nu_ref[0])
    def _():
        y_ref[...] = jnp.zeros_like(y_ref)


def _experts(xs, n_buf, block_e, n_used, w_gate_up, b_gate_up, w_down, b_down, blk_rows):
    used = lambda i, be, nu: jnp.minimum(i, nu[0] - 1)
    grid_spec = pltpu.PrefetchScalarGridSpec(
        num_scalar_prefetch=2,
        grid=(n_buf // blk_rows,),
        in_specs=[
            pl.BlockSpec((blk_rows, D_MODEL), lambda i, be, nu: (used(i, be, nu), 0)),
            pl.BlockSpec((None, D_MODEL, 2 * D_FF), lambda i, be, nu: (be[i], 0, 0)),
            pl.BlockSpec((None, 1, 2 * D_FF), lambda i, be, nu: (be[i], 0, 0)),
            pl.BlockSpec((None, D_FF, D_MODEL), lambda i, be, nu: (be[i], 0, 0)),
            pl.BlockSpec((None, 1, D_MODEL), lambda i, be, nu: (be[i], 0, 0)),
        ],
        out_specs=pl.BlockSpec((blk_rows, D_MODEL), lambda i, be, nu: (i, 0)),
        scratch_shapes=[pltpu.VMEM((D_MODEL, 2 * D_FF), BF16),
                        pltpu.VMEM((D_FF, D_MODEL), BF16)],
    )
    return pl.pallas_call(
        _experts_kernel,
        out_shape=jax.ShapeDtypeStruct((n_buf, D_MODEL), F32),
        grid_spec=grid_spec,
        compiler_params=pltpu.CompilerParams(
            dimension_semantics=("arbitrary",),
            vmem_limit_bytes=VMEM_LIMIT_BYTES),
        name="experts",
    )(block_e, n_used, xs, w_gate_up, b_gate_up[:, None, :], w_down, b_down[:, None, :])


def _combine_kernel(base_ref, lstart_ref, cnt_ref, meta_ref, lstart_row_ref, x1_ref,
                    p_ref, ys_ref, gple_ref, wpg_ref, wple_ref, gfin_ref, out_ref,
                    rows_ref, sems):
    i = pl.program_id(0)
    n = pl.num_programs(0)
    tm = x1_ref.shape[0]
    nrows = _local_rows(tm)
    slot = lax.rem(i, 2)

    def fetch_tile(tile, s):
        def per_segment(e, c):
            k = tile * N_SEG + e
            src0 = base_ref[k]
            dst0 = lstart_ref[k]

            def start_copy(off, rows):
                src = pl.multiple_of(src0 + off, STRIP_ALIGN)
                dst = pl.multiple_of(dst0 + off, STRIP_ALIGN)
                pltpu.make_async_copy(ys_ref.at[pl.ds(src, rows), :],
                                      rows_ref.at[s, pl.ds(dst, rows), :],
                                      sems.at[s]).start()

            _for_each_strip(cnt_ref[k], tm, start_copy)
            return c

        lax.fori_loop(0, N_SEG, per_segment, 0)

    @pl.when(i == 0)
    def _():
        fetch_tile(0, 0)

    @pl.when(i + 1 < n)
    def _():
        fetch_tile(i + 1, 1 - slot)

    meta = meta_ref[...]
    pos = _local_positions(meta, lstart_row_ref[...])
    jl = lax.broadcasted_iota(jnp.int32, (tm, nrows), 1)
    gate = [meta[:, META_GATE + r:META_GATE + r + 1] for r in range(TOP_K)]
    weights = jnp.where(jl == pos[0], gate[0], jnp.where(jl == pos[1], gate[1],
              jnp.where(jl == pos[2], gate[2], jnp.where(jl == pos[3], gate[3], 0.0))))

    pltpu.make_async_copy(ys_ref.at[pl.ds(0, nrows), :], rows_ref.at[slot],
                          sems.at[slot]).wait()
    moe = jnp.dot(weights.astype(BF16), rows_ref[slot].astype(BF16),
                  preferred_element_type=F32)
    x2 = x1_ref[...] + moe
    u = _rms(x2, gple_ref[...])
    gate_ple = jax.nn.sigmoid(jnp.dot(u.astype(BF16), wpg_ref[...],
                                      preferred_element_type=F32))
    emb = jnp.dot(p_ref[...].astype(BF16), wple_ref[...], preferred_element_type=F32)
    x3 = x2 + gate_ple * emb
    out_ref[...] = _rms(x3, gfin_ref[...])


def _combine(meta, tables, x1, p2, ys, g_ple, w_ple_gate_bf, w_ple_bf, g_final, tm):
    t = x1.shape[0]
    full = lambda *shape: pl.BlockSpec(shape, lambda i, *_: (0,) * len(shape))
    grid_spec = pltpu.PrefetchScalarGridSpec(
        num_scalar_prefetch=3,
        grid=(t // tm,),
        in_specs=[
            pl.BlockSpec((tm, META_LANES), lambda i, *_: (i, 0)),
            pl.BlockSpec((None, 1, N_EXPERTS), lambda i, *_: (i, 0, 0)),
            pl.BlockSpec((tm, D_MODEL), lambda i, *_: (i, 0)),
            pl.BlockSpec((tm, PLE_DIM), lambda i, *_: (i, 0)),
            pl.BlockSpec(memory_space=pl.ANY),
            full(1, D_MODEL),
            full(D_MODEL, D_MODEL),
            full(PLE_DIM, D_MODEL),
            full(1, D_MODEL),
        ],
        out_specs=pl.BlockSpec((tm, D_MODEL), lambda i, *_: (i, 0)),
        scratch_shapes=[pltpu.VMEM((2, _local_rows(tm), D_MODEL), F32),
                        pltpu.SemaphoreType.DMA((2,))],
    )
    return pl.pallas_call(
        _combine_kernel,
        out_shape=jax.ShapeDtypeStruct((t, D_MODEL), F32),
        grid_spec=grid_spec,
        compiler_params=pltpu.CompilerParams(
            dimension_semantics=("arbitrary",),
            vmem_limit_bytes=VMEM_LIMIT_BYTES),
        name="combine",
    )(tables["base_in"], tables["lstart"], tables["cnt"], meta, tables["lstart_rows"],
      x1, p2, ys, g_ple, w_ple_gate_bf, w_ple_bf, g_final)


def _tiles(seq_len):
    return dict(
        proj_rows=min(512, seq_len),
        ret_block=min(256, seq_len),
        route_rows=min(512, seq_len),
        expert_rows=256,
    )


def _slot_tables(tile_counts, blk, n_buf, nloc):
    cnt = tile_counts[:, 0, :].astype(jnp.int32)
    cnt = ((cnt + STRIP_ALIGN - 1) // STRIP_ALIGN) * STRIP_ALIGN
    seg_rows = jnp.sum(cnt, axis=1)
    counts = jnp.sum(cnt, axis=0)
    padded = ((counts + blk - 1) // blk) * blk
    ends_pad = jnp.cumsum(padded)
    starts_pad = ends_pad - padded
    tile_before = jnp.cumsum(cnt, axis=0) - cnt
    lstart = jnp.cumsum(cnt, axis=1) - cnt
    base = starts_pad[None, :] + tile_before
    with_filler = lambda a, col: jnp.concatenate([a, col[:, None]], axis=1).reshape(-1)
    blk_start = jnp.arange(n_buf // blk + 2, dtype=jnp.int32) * blk
    last_of_expert = jnp.any((padded > 0)[None, :]
                             & (blk_start[:, None] == (ends_pad - blk)[None, :]), axis=1)
    zflag = (last_of_expert | (blk_start >= ends_pad[-1])).astype(jnp.int32)
    filler_dst = n_buf + (jnp.arange(cnt.shape[0], dtype=jnp.int32) % 2) * blk
    return dict(
        cnt=with_filler(cnt, nloc - seg_rows),
        lstart=with_filler(lstart, seg_rows),
        base_out=with_filler(base, filler_dst),
        base_in=with_filler(base, jnp.zeros_like(seg_rows)),
        lstart_rows=lstart.astype(F32)[:, None, :],
        zflag=zflag, ends_pad=ends_pad)


def _layer(x2, p2, seq_len, g_mix, w_in, w_pool, pool_scale, w_ret_o, w_out, g_ffn,
           w_router, b_router, w_gate_up, b_gate_up, w_down, b_down, g_ple,
           w_ple_gate, w_ple, g_out):
    t = x2.shape[0]
    cfg = _tiles(seq_len)
    row = lambda a: a.reshape(1, -1)

    proj = _in_proj(x2, row(g_mix), w_in.astype(BF16), seq_len, cfg["proj_rows"])
    o_gated = _retention(proj.reshape(t // seq_len, seq_len, IN_W), cfg["ret_block"])
    o_gated = o_gated.reshape(t, RET_V_W)

    wr_hi = w_router.astype(BF16)
    wr_lo = (w_router - wr_hi.astype(F32)).astype(BF16)
    tm = cfg["route_rows"]
    x1, h2, meta, tile_counts = _mix_route(
        proj, o_gated, x2, w_pool.astype(BF16), row(pool_scale), w_ret_o.astype(BF16),
        w_out.astype(BF16), row(g_ffn), wr_hi, wr_lo, row(b_router), seq_len, tm)

    blk = cfg["expert_rows"]
    n_buf = t * TOP_K + (t // tm) * N_EXPERTS * STRIP_ALIGN + N_EXPERTS * blk
    n_blocks = n_buf // blk
    tables = _slot_tables(tile_counts, blk, n_buf, _local_rows(tm))
    blk_start = jnp.arange(n_blocks, dtype=jnp.int32) * blk
    block_e = jnp.minimum(
        jnp.sum((tables["ends_pad"][None, :] <= blk_start[:, None]).astype(jnp.int32), axis=1),
        N_EXPERTS - 1)
    n_used = tables["ends_pad"][-1:] // blk

    xs = _dispatch(h2, meta, tables, n_buf, blk, tm)
    ys = _experts(xs, n_buf, block_e, n_used, w_gate_up, b_gate_up, w_down, b_down, blk)
    return _combine(meta, tables, x1, p2, ys, row(g_ple), w_ple_gate.astype(BF16),
                    w_ple.astype(BF16), row(g_out), tm)


def kernel(x, p, g_mix, w_in, w_pool, pool_scale, w_ret_o, w_out, g_ffn, w_router,
           b_router, w_gate_up, b_gate_up, w_down, b_down, g_ple, w_ple_gate, w_ple,
           g_final):
    b, s, d = x.shape
    depth = p.shape[0]
    assert depth == 1 and d == D_MODEL
    x2 = x.reshape(b * s, d)
    out = _layer(x2, p[0].reshape(b * s, PLE_DIM), s, g_mix[0], w_in[0], w_pool[0],
                 pool_scale[0], w_ret_o[0], w_out[0], g_ffn[0], w_router[0],
                 b_router[0], w_gate_up[0], b_gate_up[0], w_down[0], b_down[0],
                 g_ple[0], w_ple_gate[0], w_ple[0], g_final)
    return out.reshape(b, s, d)
```

```python
import functools

import numpy as np
import jax
import jax.numpy as jnp
from jax import lax
from jax.experimental import pallas as pl
from jax.experimental.pallas import tpu as pltpu

F32 = jnp.float32
BF16 = jnp.bfloat16

D_MODEL = 1024
EPS = 1e-6
CHUNK = 64
PLE_DIM = 256
POOL_WINDOWS = (2, 4, 8, 16)
POOL_GROUP_W = D_MODEL // len(POOL_WINDOWS)
POOL_HALO = 16
RET_HEADS = 4
RET_QK_HEAD = 256
RET_V_HEAD = 512
RET_V_W = RET_HEADS * RET_V_HEAD
ROPE_BASE = 10000.0
IN_W = 9 * D_MODEL
N_EXPERTS = 32
TOP_K = 4
D_FF = D_MODEL
SWIGLU_ALPHA = 1.702
SWIGLU_LIMIT = 7.0

VMEM_LIMIT_BYTES = 56 * 1024 * 1024
META_LANES = 128
META_EXPERT, META_RANK, META_GATE = 0, TOP_K, 2 * TOP_K
STRIP_ALIGN = 16
FILLER_ROWS = N_EXPERTS * STRIP_ALIGN
N_SEG = N_EXPERTS + 1


def _rms(x, g):
    return x * lax.rsqrt(jnp.mean(x * x, axis=-1, keepdims=True) + EPS) * g


COL_Q, COL_K = 1, 2
COL_V, COL_G = (3, 4), (5, 6)
COL_GATE_A, COL_GATE_B = 7, 8


def _rotary_heads(y, cos, sin):
    half = RET_QK_HEAD // 2
    parts = []
    for h in range(RET_HEADS):
        x1 = y[:, h * RET_QK_HEAD:h * RET_QK_HEAD + half]
        x2 = y[:, h * RET_QK_HEAD + half:(h + 1) * RET_QK_HEAD]
        parts += [x1 * cos - x2 * sin, x2 * cos + x1 * sin]
    return jnp.concatenate(parts, axis=-1)


def _in_proj_kernel(x_ref, g_ref, w_ref, cos_ref, sin_ref, o_ref):
    h = _rms(x_ref[...], g_ref[...]).astype(BF16)
    for c in range(IN_W // D_MODEL):
        cols = slice(c * D_MODEL, (c + 1) * D_MODEL)
        y = jnp.dot(h, w_ref[:, cols], preferred_element_type=F32)
        if c == COL_Q:
            y = _rotary_heads(y, cos_ref[...], sin_ref[...])
        elif c == COL_K:
            y = _rotary_heads(y, cos_ref[...], sin_ref[...]) * (RET_QK_HEAD ** -0.5)
        elif c in COL_G:
            y = y * jax.nn.sigmoid(y)
        o_ref[:, cols] = y.astype(BF16)


def _rotary_tables(s):
    half = RET_QK_HEAD // 2
    pos = jnp.arange(s, dtype=F32)
    inv = ROPE_BASE ** (-jnp.linspace(0.0, 1.0, half, dtype=F32))
    ang = pos[:, None] * inv[None, :]
    return jnp.cos(ang), jnp.sin(ang)


def _in_proj(x2, g_mix, w_in_bf, seq_len, tm):
    t = x2.shape[0]
    half = RET_QK_HEAD // 2
    cos, sin = _rotary_tables(seq_len)
    tiles_per_seq = seq_len // tm
    return pl.pallas_call(
        _in_proj_kernel,
        out_shape=jax.ShapeDtypeStruct((t, IN_W), BF16),
        grid=(t // tm,),
        in_specs=[
            pl.BlockSpec((tm, D_MODEL), lambda i: (i, 0)),
            pl.BlockSpec((1, D_MODEL), lambda i: (0, 0)),
            pl.BlockSpec((D_MODEL, IN_W), lambda i: (0, 0), pipeline_mode=pl.Buffered(1)),
            pl.BlockSpec((tm, half), lambda i: (lax.rem(i, tiles_per_seq), 0)),
            pl.BlockSpec((tm, half), lambda i: (lax.rem(i, tiles_per_seq), 0)),
        ],
        out_specs=pl.BlockSpec((tm, IN_W), lambda i: (i, 0)),
        compiler_params=pltpu.CompilerParams(
            dimension_semantics=("arbitrary",),
            vmem_limit_bytes=VMEM_LIMIT_BYTES),
        name="in_proj",
    )(x2, g_mix, w_in_bf, cos, sin)


def _head_decay_logs():
    return [float(np.log(1.0 - 2.0 ** (-5.0 - h))) for h in range(RET_HEADS)]


def _retention_kernel(blk_decay, q_ref, k_ref, va_ref, vb_ref, ga_ref, gb_ref,
                      d_ref, qd_ref, kd_ref, o_ref, state_ref):
    @pl.when(pl.program_id(1) == 0)
    def _():
        state_ref[...] = jnp.zeros_like(state_ref)

    heads_per_ref = D_MODEL // RET_V_HEAD
    for h in range(RET_HEADS):
        qk_cols = slice(h * RET_QK_HEAD, (h + 1) * RET_QK_HEAD)
        v_cols = slice((h % heads_per_ref) * RET_V_HEAD, (h % heads_per_ref + 1) * RET_V_HEAD)
        q = q_ref[:, qk_cols]
        k = k_ref[:, qk_cols]
        v = (va_ref if h < heads_per_ref else vb_ref)[:, v_cols]
        g = (ga_ref if h < heads_per_ref else gb_ref)[:, v_cols]
        scores = lax.dot_general(q, k, (((1,), (1,)), ((), ())),
                                 preferred_element_type=F32) * d_ref[h]
        state = state_ref[h]
        o = jnp.dot(scores.astype(BF16), v, preferred_element_type=F32)
        o = o + jnp.dot((q.astype(F32) * qd_ref[h]).astype(BF16), state.astype(BF16),
                        preferred_element_type=F32)
        k_dec = (k.astype(F32) * kd_ref[h]).astype(BF16)
        state_ref[h] = state * blk_decay[h] + lax.dot_general(
            k_dec, v, (((0,), (0,)), ((), ())), preferred_element_type=F32)
        o = o * lax.rsqrt(jnp.mean(o * o, axis=-1, keepdims=True) + EPS)
        o_ref[:, h * RET_V_HEAD:(h + 1) * RET_V_HEAD] = (o * g.astype(F32)).astype(BF16)


def _retention_tables(blk):
    log_g = jnp.asarray(_head_decay_logs(), F32)
    idx = jnp.arange(blk, dtype=F32)
    diff = idx[:, None] - idx[None, :]
    chunk = jnp.arange(blk, dtype=jnp.int32) // CHUNK
    visible = chunk[None, :] <= chunk[:, None]
    dmask = jnp.where(visible[None], jnp.exp(log_g[:, None, None] * jnp.abs(diff)[None]), 0.0)
    q_dec = jnp.exp(log_g[:, None] * (idx + 1.0))[:, :, None]
    k_dec = jnp.exp(log_g[:, None] * (blk - 1.0 - idx))[:, :, None]
    q_dec = jnp.broadcast_to(q_dec, (RET_HEADS, blk, RET_QK_HEAD))
    k_dec = jnp.broadcast_to(k_dec, (RET_HEADS, blk, RET_QK_HEAD))
    return dmask.astype(F32), q_dec, k_dec


def _retention(proj3, blk):
    b, s, _ = proj3.shape
    dmask, q_dec, k_dec = _retention_tables(blk)
    blk_decay = [float(np.exp(lg * blk)) for lg in _head_decay_logs()]
    chunk = lambda c: pl.BlockSpec((None, blk, D_MODEL), lambda bi, l: (bi, l, c))
    table = lambda *shape: pl.BlockSpec(shape, lambda bi, l: (0,) * len(shape))
    return pl.pallas_call(
        functools.partial(_retention_kernel, blk_decay),
        out_shape=jax.ShapeDtypeStruct((b, s, RET_V_W), BF16),
        grid=(b, s // blk),
        in_specs=[
            chunk(COL_Q), chunk(COL_K), chunk(COL_V[0]), chunk(COL_V[1]),
            chunk(COL_G[0]), chunk(COL_G[1]),
            table(RET_HEADS, blk, blk),
            table(RET_HEADS, blk, RET_QK_HEAD),
            table(RET_HEADS, blk, RET_QK_HEAD),
        ],
        out_specs=pl.BlockSpec((None, blk, RET_V_W), lambda bi, l: (bi, l, 0)),
        scratch_shapes=[pltpu.VMEM((RET_HEADS, RET_QK_HEAD, RET_V_HEAD), F32)],
        compiler_params=pltpu.CompilerParams(
            dimension_semantics=("arbitrary", "arbitrary"),
            vmem_limit_bytes=VMEM_LIMIT_BYTES),
        name="retention",
    )(proj3, proj3, proj3, proj3, proj3, proj3, dmask, q_dec, k_dec)


def _window_sum(ext, w, tm):
    cur = ext
    span = 1
    while span < w:
        cur = cur[span:, :] + cur[:-span, :]
        span *= 2
    start = POOL_HALO + 1 - w
    return cur[start:start + tm, :]


def _mix_route_kernel(seq_len, u_ref, halo_ref, ga_ref, gb_ref, o_ref, x_ref,
                      wpool_ref, pscale_ref, wreto_ref, wout_ref, gffn_ref,
                      wr_ref, br_ref,
                      x1_ref, h2_ref, meta_ref, counts_ref):
    i = pl.program_id(0)
    tm = x_ref.shape[0]
    pos0 = lax.rem(i * tm, seq_len)

    u = u_ref[...].astype(F32)
    halo = jnp.where(pos0 == 0, 0.0, halo_ref[...].astype(F32))
    ext = jnp.concatenate([halo, u], axis=0)
    pos = (pos0 + lax.broadcasted_iota(jnp.int32, (tm, 1), 0)).astype(F32)
    pooled_out = []
    for g, w in enumerate(POOL_WINDOWS):
        cols = slice(g * POOL_GROUP_W, (g + 1) * POOL_GROUP_W)
        ws = _window_sum(ext[:, cols], w, tm)
        count = jnp.minimum(pos + 1.0, float(w))
        pooled = ws / count - u[:, cols]
        pooled_out.append(jnp.dot(pooled.astype(BF16), wpool_ref[g],
                                  preferred_element_type=F32))
    y_pool = jnp.concatenate(pooled_out, axis=-1) * pscale_ref[...]

    y_ret = jnp.dot(o_ref[...], wreto_ref[...], preferred_element_type=F32)
    merged = (jax.nn.sigmoid(ga_ref[...].astype(F32)) * y_pool
              + jax.nn.sigmoid(gb_ref[...].astype(F32)) * y_ret)
    x1 = x_ref[...] + jnp.dot(merged.astype(BF16), wout_ref[...],
                              preferred_element_type=F32)
    x1_ref[...] = x1
    h2 = _rms(x1, gffn_ref[...])
    h2_ref[...] = h2.astype(BF16)

    h_hi = h2.astype(BF16)
    h_lo = (h2 - h_hi.astype(F32)).astype(BF16)
    both = (jnp.dot(h_hi, wr_ref[...], preferred_element_type=F32)
            + jnp.dot(h_lo, wr_ref[...], preferred_element_type=F32))
    logits = both[:, :N_EXPERTS] + both[:, N_EXPERTS:] + br_ref[...]

    lane = lax.broadcasted_iota(jnp.int32, (tm, N_EXPERTS), 1)
    work = logits
    vals, idxs, hots = [], [], []
    for _ in range(TOP_K):
        m = jnp.max(work, axis=-1, keepdims=True)
        idx = jnp.min(jnp.where(work == m, lane, N_EXPERTS), axis=-1, keepdims=True)
        hot = lane == idx
        vals.append(m)
        idxs.append(idx)
        hots.append(hot)
        work = jnp.where(hot, -jnp.inf, work)
    exps = [jnp.exp(v - vals[0]) for v in vals]
    denom = exps[0] + exps[1] + exps[2] + exps[3]
    gates = [e / denom for e in exps]

    sel = (jnp.where(hots[0], 1.0, 0.0) + jnp.where(hots[1], 1.0, 0.0)
           + jnp.where(hots[2], 1.0, 0.0) + jnp.where(hots[3], 1.0, 0.0))
    row = lax.broadcasted_iota(jnp.int32, (tm, tm), 0)
    col = lax.broadcasted_iota(jnp.int32, (tm, tm), 1)
    lower = jnp.where(col < row, 1.0, 0.0).astype(BF16)
    before = jnp.dot(lower, sel.astype(BF16), preferred_element_type=F32)
    counts_ref[...] = jnp.sum(sel, axis=0, keepdims=True)

    mlane = lax.broadcasted_iota(jnp.int32, (tm, META_LANES), 1)
    meta = jnp.zeros((tm, META_LANES), F32)
    for r in range(TOP_K):
        rank_r = jnp.sum(jnp.where(hots[r], before, 0.0), axis=-1, keepdims=True)
        meta = jnp.where(mlane == META_EXPERT + r, idxs[r].astype(F32), meta)
        meta = jnp.where(mlane == META_RANK + r, rank_r, meta)
        meta = jnp.where(mlane == META_GATE + r, gates[r], meta)
    meta_ref[...] = meta


def _mix_route(proj, o_gated, x2, w_pool_bf, pool_scale, w_ret_o_bf, w_out_bf,
               g_ffn, wr_split, b_router, seq_len, tm):
    t = x2.shape[0]
    halo_per_tile = tm // POOL_HALO
    full = lambda *shape: pl.BlockSpec(shape, lambda i: (0,) * len(shape))
    return pl.pallas_call(
        functools.partial(_mix_route_kernel, seq_len),
        out_shape=(
            jax.ShapeDtypeStruct((t, D_MODEL), F32),
            jax.ShapeDtypeStruct((t, D_MODEL), BF16),
            jax.ShapeDtypeStruct((t, META_LANES), F32),
            jax.ShapeDtypeStruct((t // tm, 1, N_EXPERTS), F32),
        ),
        grid=(t // tm,),
        in_specs=[
            pl.BlockSpec((tm, D_MODEL), lambda i: (i, 0)),
            pl.BlockSpec((POOL_HALO, D_MODEL),
                         lambda i: (jnp.maximum(i * halo_per_tile - 1, 0), 0)),
            pl.BlockSpec((tm, D_MODEL), lambda i: (i, COL_GATE_A)),
            pl.BlockSpec((tm, D_MODEL), lambda i: (i, COL_GATE_B)),
            pl.BlockSpec((tm, RET_V_W), lambda i: (i, 0)),
            pl.BlockSpec((tm, D_MODEL), lambda i: (i, 0)),
            full(len(POOL_WINDOWS), POOL_GROUP_W, POOL_GROUP_W),
            full(1, D_MODEL),
            full(RET_V_W, D_MODEL),
            full(D_MODEL, D_MODEL),
            full(1, D_MODEL),
            full(D_MODEL, 2 * N_EXPERTS),
            full(1, N_EXPERTS),
        ],
        out_specs=(
            pl.BlockSpec((tm, D_MODEL), lambda i: (i, 0)),
            pl.BlockSpec((tm, D_MODEL), lambda i: (i, 0)),
            pl.BlockSpec((tm, META_LANES), lambda i: (i, 0)),
            pl.BlockSpec((None, 1, N_EXPERTS), lambda i: (i, 0, 0)),
        ),
        compiler_params=pltpu.CompilerParams(
            dimension_semantics=("arbitrary",),
            vmem_limit_bytes=VMEM_LIMIT_BYTES),
        name="mix_route",
    )(proj, proj, proj, proj, o_gated, x2, w_pool_bf, pool_scale, w_ret_o_bf,
      w_out_bf, g_ffn, wr_split, b_router)


def _local_positions(meta, lstart):
    tm = meta.shape[0]
    lane_e = lax.broadcasted_iota(jnp.int32, (tm, N_EXPERTS), 1)
    out = []
    for r in range(TOP_K):
        e_r = meta[:, META_EXPERT + r:META_EXPERT + r + 1].astype(jnp.int32)
        first = jnp.sum(jnp.where(lane_e == e_r, lstart, 0.0), axis=-1, keepdims=True)
        out.append((meta[:, META_RANK + r:META_RANK + r + 1] + first).astype(jnp.int32))
    return out


def _local_rows(tm):
    return TOP_K * tm + FILLER_ROWS


def _for_each_strip(count, max_rows, start_copy):
    piece = 1 << (max(max_rows, FILLER_ROWS).bit_length() - 1)
    off = jnp.int32(0)
    while piece >= STRIP_ALIGN:
        take = count & piece

        @pl.when(take != 0)
        def _(off=off, piece=piece):
            start_copy(off, piece)

        off = off + take
        piece //= 2


def _dispatch_kernel(blk_rows, base_ref, lstart_ref, cnt_ref, zflag_ref,
                     h_ref, meta_ref, lstart_col_ref, xs_ref,
                     sorted_ref, zero_ref, sems, zsem):
    i = pl.program_id(0)
    tm = h_ref.shape[0]
    nrows = _local_rows(tm)
    slot = lax.rem(i, 2)

    @pl.when(i == 0)
    def _():
        zero_ref[...] = jnp.zeros_like(zero_ref)

        def zcopy(b):
            start = pl.multiple_of(b * blk_rows, blk_rows)
            return pltpu.make_async_copy(zero_ref, xs_ref.at[pl.ds(start, blk_rows), :], zsem)

        def zissue(b, c):
            @pl.when(zflag_ref[b] > 0)
            def _():
                zcopy(b).start()
            return c

        def zwait(b, c):
            @pl.when(zflag_ref[b] > 0)
            def _():
                zcopy(b).wait()
            return c

        n_zero_blocks = xs_ref.shape[0] // blk_rows
        lax.fori_loop(0, n_zero_blocks, zissue, 0)
        lax.fori_loop(0, n_zero_blocks, zwait, 0)

    meta_t = meta_ref[...].T
    sub_e = lax.broadcasted_iota(jnp.int32, (N_EXPERTS, tm), 0)
    pos = []
    for r in range(TOP_K):
        e_r = meta_t[META_EXPERT + r:META_EXPERT + r + 1, :].astype(jnp.int32)
        first = jnp.sum(jnp.where(sub_e == e_r, lstart_col_ref[...], 0.0), axis=0, keepdims=True)
        pos.append((meta_t[META_RANK + r:META_RANK + r + 1, :] + first).astype(jnp.int32))
    jr = lax.broadcasted_iota(jnp.int32, (nrows, tm), 0)
    onehot = jnp.where(jr == pos[0], 1.0, jnp.where(jr == pos[1], 1.0,
             jnp.where(jr == pos[2], 1.0, jnp.where(jr == pos[3], 1.0, 0.0))))
    srt = jnp.dot(onehot.astype(BF16), h_ref[...], preferred_element_type=F32)
    sorted_ref[slot] = srt.astype(BF16)

    def per_segment(e, c):
        k = i * N_SEG + e
        src0 = lstart_ref[k]
        dst0 = base_ref[k]

        def start_copy(off, rows):
            src = pl.multiple_of(src0 + off, STRIP_ALIGN)
            dst = pl.multiple_of(dst0 + off, STRIP_ALIGN)
            pltpu.make_async_copy(sorted_ref.at[slot, pl.ds(src, rows), :],
                                  xs_ref.at[pl.ds(dst, rows), :],
                                  sems.at[slot]).start()

        _for_each_strip(cnt_ref[k], tm, start_copy)
        return c

    lax.fori_loop(0, N_SEG, per_segment, 0)

    def wait_tile(s):
        pltpu.make_async_copy(sorted_ref.at[s], xs_ref.at[pl.ds(0, nrows), :],
                              sems.at[s]).wait()

    @pl.when(i >= 1)
    def _():
        wait_tile(1 - slot)

    @pl.when(i == pl.num_programs(0) - 1)
    def _():
        wait_tile(slot)


def _dispatch(h2, meta, tables, n_buf, blk_rows, tm):
    t = h2.shape[0]
    grid_spec = pltpu.PrefetchScalarGridSpec(
        num_scalar_prefetch=4,
        grid=(t // tm,),
        in_specs=[
            pl.BlockSpec((tm, D_MODEL), lambda i, *_: (i, 0)),
            pl.BlockSpec((tm, META_LANES), lambda i, *_: (i, 0)),
            pl.BlockSpec((None, N_EXPERTS, 1), lambda i, *_: (i, 0, 0)),
        ],
        out_specs=pl.BlockSpec(memory_space=pl.ANY),
        scratch_shapes=[pltpu.VMEM((2, _local_rows(tm), D_MODEL), BF16),
                        pltpu.VMEM((blk_rows, D_MODEL), BF16),
                        pltpu.SemaphoreType.DMA((2,)),
                        pltpu.SemaphoreType.DMA(())],
    )
    return pl.pallas_call(
        functools.partial(_dispatch_kernel, blk_rows),
        out_shape=jax.ShapeDtypeStruct((n_buf + 2 * FILLER_ROWS, D_MODEL), BF16),
        grid_spec=grid_spec,
        compiler_params=pltpu.CompilerParams(
            dimension_semantics=("arbitrary",),
            vmem_limit_bytes=VMEM_LIMIT_BYTES,
            has_side_effects=True),
        name="dispatch",
    )(tables["base_out"], tables["lstart"], tables["cnt"], tables["zflag"],
      h2, meta, tables["lstart_cols"])


def _experts_kernel(be_ref, nu_ref, x_ref, wgu_ref, bgu_ref, wd_ref, bd_ref,
                    y_ref, wgu_bf, wd_bf):
    i = pl.program_id(0)

    @pl.when(i < nu_ref[0])
    def _():
        prev = be_ref[jnp.maximum(i - 1, 0)]

        @pl.when((i == 0) | (be_ref[i] != prev))
        def _():
            wgu_bf[...] = wgu_ref[...].astype(BF16)
            wd_bf[...] = wd_ref[...].astype(BF16)

        gu = jnp.dot(x_ref[...], wgu_bf[...], preferred_element_type=F32) + bgu_ref[...]
        glu = jnp.minimum(gu[:, :D_FF], SWIGLU_LIMIT)
        lin = jnp.clip(gu[:, D_FF:], -SWIGLU_LIMIT, SWIGLU_LIMIT)
        act = glu * jax.nn.sigmoid(SWIGLU_ALPHA * glu) * (lin + 1.0)
        y_ref[...] = (jnp.dot(act.astype(BF16), wd_bf[...],
                              preferred_element_type=F32) + bd_ref[...]).astype(BF16)

    @pl.when(i >= nu_ref[0])
    def _():
        y_ref[...] = jnp.zeros_like(y_ref)


def _experts(xs, n_buf, block_e, n_used, w_gate_up, b_gate_up, w_down, b_down, blk_rows):
    used = lambda i, be, nu: jnp.minimum(i, nu[0] - 1)
    grid_spec = pltpu.PrefetchScalarGridSpec(
        num_scalar_prefetch=2,
        grid=(n_buf // blk_rows,),
        in_specs=[
            pl.BlockSpec((blk_rows, D_MODEL), lambda i, be, nu: (used(i, be, nu), 0)),
            pl.BlockSpec((None, D_MODEL, 2 * D_FF), lambda i, be, nu: (be[i], 0, 0)),
            pl.BlockSpec((None, 1, 2 * D_FF), lambda i, be, nu: (be[i], 0, 0)),
            pl.BlockSpec((None, D_FF, D_MODEL), lambda i, be, nu: (be[i], 0, 0)),
            pl.BlockSpec((None, 1, D_MODEL), lambda i, be, nu: (be[i], 0, 0)),
        ],
        out_specs=pl.BlockSpec((blk_rows, D_MODEL), lambda i, be, nu: (i, 0)),
        scratch_shapes=[pltpu.VMEM((D_MODEL, 2 * D_FF), BF16),
                        pltpu.VMEM((D_FF, D_MODEL), BF16)],
    )
    return pl.pallas_call(
        _experts_kernel,
        out_shape=jax.ShapeDtypeStruct((n_buf, D_MODEL), BF16),
        grid_spec=grid_spec,
        compiler_params=pltpu.CompilerParams(
            dimension_semantics=("arbitrary",),
            vmem_limit_bytes=VMEM_LIMIT_BYTES),
        name="experts",
    )(block_e, n_used, xs, w_gate_up, b_gate_up[:, None, :], w_down, b_down[:, None, :])


def _combine_kernel(base_ref, lstart_ref, cnt_ref, meta_ref, lstart_row_ref, x1_ref,
                    p_ref, ys_ref, gple_ref, wpg_ref, wple_ref, gfin_ref, out_ref,
                    rows_ref, sems):
    i = pl.program_id(0)
    n = pl.num_programs(0)
    tm = x1_ref.shape[0]
    nrows = _local_rows(tm)
    slot = lax.rem(i, 2)

    def fetch_tile(tile, s):
        def per_segment(e, c):
            k = tile * N_SEG + e
            src0 = base_ref[k]
            dst0 = lstart_ref[k]

            def start_copy(off, rows):
                src = pl.multiple_of(src0 + off, STRIP_ALIGN)
                dst = pl.multiple_of(dst0 + off, STRIP_ALIGN)
                pltpu.make_async_copy(ys_ref.at[pl.ds(src, rows), :],
                                      rows_ref.at[s, pl.ds(dst, rows), :],
                                      sems.at[s]).start()

            _for_each_strip(cnt_ref[k], tm, start_copy)
            return c

        lax.fori_loop(0, N_SEG, per_segment, 0)

    @pl.when(i == 0)
    def _():
        fetch_tile(0, 0)

    @pl.when(i + 1 < n)
    def _():
        fetch_tile(i + 1, 1 - slot)

    meta = meta_ref[...]
    pos = _local_positions(meta, lstart_row_ref[...])
    jl = lax.broadcasted_iota(jnp.int32, (tm, nrows), 1)
    gate = [meta[:, META_GATE + r:META_GATE + r + 1] for r in range(TOP_K)]
    weights = jnp.where(jl == pos[0], gate[0], jnp.where(jl == pos[1], gate[1],
              jnp.where(jl == pos[2], gate[2], jnp.where(jl == pos[3], gate[3], 0.0))))

    pltpu.make_async_copy(ys_ref.at[pl.ds(0, nrows), :], rows_ref.at[slot],
                          sems.at[slot]).wait()
    moe = jnp.dot(weights.astype(BF16), rows_ref[slot], preferred_element_type=F32)
    x2 = x1_ref[...] + moe
    u = _rms(x2, gple_ref[...])
    gate_ple = jax.nn.sigmoid(jnp.dot(u.astype(BF16), wpg_ref[...],
                                      preferred_element_type=F32))
    emb = jnp.dot(p_ref[...].astype(BF16), wple_ref[...], preferred_element_type=F32)
    x3 = x2 + gate_ple * emb
    out_ref[...] = _rms(x3, gfin_ref[...])


def _combine(meta, tables, x1, p2, ys, g_ple, w_ple_gate_bf, w_ple_bf, g_final, tm):
    t = x1.shape[0]
    full = lambda *shape: pl.BlockSpec(shape, lambda i, *_: (0,) * len(shape))
    grid_spec = pltpu.PrefetchScalarGridSpec(
        num_scalar_prefetch=3,
        grid=(t // tm,),
        in_specs=[
            pl.BlockSpec((tm, META_LANES), lambda i, *_: (i, 0)),
            pl.BlockSpec((None, 1, N_EXPERTS), lambda i, *_: (i, 0, 0)),
            pl.BlockSpec((tm, D_MODEL), lambda i, *_: (i, 0)),
            pl.BlockSpec((tm, PLE_DIM), lambda i, *_: (i, 0)),
            pl.BlockSpec(memory_space=pl.ANY),
            full(1, D_MODEL),
            full(D_MODEL, D_MODEL),
            full(PLE_DIM, D_MODEL),
            full(1, D_MODEL),
        ],
        out_specs=pl.BlockSpec((tm, D_MODEL), lambda i, *_: (i, 0)),
        scratch_shapes=[pltpu.VMEM((2, _local_rows(tm), D_MODEL), BF16),
                        pltpu.SemaphoreType.DMA((2,))],
    )
    return pl.pallas_call(
        _combine_kernel,
        out_shape=jax.ShapeDtypeStruct((t, D_MODEL), F32),
        grid_spec=grid_spec,
        compiler_params=pltpu.CompilerParams(
            dimension_semantics=("arbitrary",),
            vmem_limit_bytes=VMEM_LIMIT_BYTES),
        name="combine",
    )(tables["base_in"], tables["lstart"], tables["cnt"], meta, tables["lstart_rows"],
      x1, p2, ys, g_ple, w_ple_gate_bf, w_ple_bf, g_final)


def _tiles(seq_len):
    return dict(
        proj_rows=min(512, seq_len),
        ret_block=min(256, seq_len),
        route_rows=min(512, seq_len),
        expert_rows=256,
    )


def _slot_tables(tile_counts, blk, n_buf, nloc):
    cnt = tile_counts[:, 0, :].astype(jnp.int32)
    cnt = ((cnt + STRIP_ALIGN - 1) // STRIP_ALIGN) * STRIP_ALIGN
    seg_rows = jnp.sum(cnt, axis=1)
    counts = jnp.sum(cnt, axis=0)
    padded = ((counts + blk - 1) // blk) * blk
    ends_pad = jnp.cumsum(padded)
    starts_pad = ends_pad - padded
    tile_before = jnp.cumsum(cnt, axis=0) - cnt
    lstart = jnp.cumsum(cnt, axis=1) - cnt
    base = starts_pad[None, :] + tile_before
    with_filler = lambda a, col: jnp.concatenate([a, col[:, None]], axis=1).reshape(-1)
    blk_start = jnp.arange((n_buf + 2 * FILLER_ROWS) // blk, dtype=jnp.int32) * blk
    last_of_expert = jnp.any((padded > 0)[None, :]
                             & (blk_start[:, None] == (ends_pad - blk)[None, :]), axis=1)
    zflag = (last_of_expert | (blk_start >= ends_pad[-1])).astype(jnp.int32)
    filler_dst = n_buf + (jnp.arange(cnt.shape[0], dtype=jnp.int32) % 2) * FILLER_ROWS
    return dict(
        cnt=with_filler(cnt, nloc - seg_rows),
        lstart=with_filler(lstart, seg_rows),
        base_out=with_filler(base, filler_dst),
        base_in=with_filler(base, jnp.zeros_like(seg_rows)),
        lstart_rows=lstart.astype(F32)[:, None, :],
        lstart_cols=lstart.astype(F32)[:, :, None],
        zflag=zflag, ends_pad=ends_pad)


def _layer(x2, p2, seq_len, g_mix, w_in, w_pool, pool_scale, w_ret_o, w_out, g_ffn,
           w_router, b_router, w_gate_up, b_gate_up, w_down, b_down, g_ple,
           w_ple_gate, w_ple, g_out):
    t = x2.shape[0]
    cfg = _tiles(seq_len)
    row = lambda a: a.reshape(1, -1)

    proj = _in_proj(x2, row(g_mix), w_in.astype(BF16), seq_len, cfg["proj_rows"])
    o_gated = _retention(proj.reshape(t // seq_len, seq_len, IN_W), cfg["ret_block"])
    o_gated = o_gated.reshape(t, RET_V_W)

    wr_hi = w_router.astype(BF16)
    wr_lo = (w_router - wr_hi.astype(F32)).astype(BF16)
    wr_split = jnp.concatenate([wr_hi, wr_lo], axis=1)
    tm = cfg["route_rows"]
    x1, h2, meta, tile_counts = _mix_route(
        proj, o_gated, x2, w_pool.astype(BF16), row(pool_scale), w_ret_o.astype(BF16),
        w_out.astype(BF16), row(g_ffn), wr_split, row(b_router), seq_len, tm)

    blk = cfg["expert_rows"]
    n_buf = t * TOP_K + (t // tm) * FILLER_ROWS + N_EXPERTS * blk
    n_blocks = n_buf // blk
    tables = _slot_tables(tile_counts, blk, n_buf, _local_rows(tm))
    blk_start = jnp.arange(n_blocks, dtype=jnp.int32) * blk
    block_e = jnp.minimum(
        jnp.sum((tables["ends_pad"][None, :] <= blk_start[:, None]).astype(jnp.int32), axis=1),
        N_EXPERTS - 1)
    n_used = tables["ends_pad"][-1:] // blk

    xs = _dispatch(h2, meta, tables, n_buf, blk, tm)
    ys = _experts(xs, n_buf, block_e, n_used, w_gate_up, b_gate_up, w_down, b_down, blk)
    return _combine(meta, tables, x1, p2, ys, row(g_ple), w_ple_gate.astype(BF16),
                    w_ple.astype(BF16), row(g_out), tm)


def kernel(x, p, g_mix, w_in, w_pool, pool_scale, w_ret_o, w_out, g_ffn, w_router,
           b_router, w_gate_up, b_gate_up, w_down, b_down, g_ple, w_ple_gate, w_ple,
           g_final):
    b, s, d = x.shape
    depth = p.shape[0]
    assert depth == 1 and d == D_MODEL
    x2 = x.reshape(b * s, d)
    out = _layer(x2, p[0].reshape(b * s, PLE_DIM), s, g_mix[0], w_in[0], w_pool[0],
                 pool_scale[0], w_ret_o[0], w_out[0], g_ffn[0], w_router[0],
                 b_router[0], w_gate_up[0], b_gate_up[0], w_down[0], b_down[0],
                 g_ple[0], w_ple_gate[0], w_ple[0], g_final)
    return out.reshape(b, s, d)
```

```python
import functools

import numpy as np
import jax
import jax.numpy as jnp
from jax import lax
from jax.experimental import pallas as pl
from jax.experimental.pallas import tpu as pltpu

F32 = jnp.float32
BF16 = jnp.bfloat16

D_MODEL = 1024
EPS = 1e-6
CHUNK = 64
PLE_DIM = 256
POOL_WINDOWS = (2, 4, 8, 16)
POOL_GROUP_W = D_MODEL // len(POOL_WINDOWS)
POOL_HALO = 16
RET_HEADS = 4
RET_QK_HEAD = 256
RET_V_HEAD = 512
RET_V_W = RET_HEADS * RET_V_HEAD
ROPE_BASE = 10000.0
IN_W = 9 * D_MODEL
N_EXPERTS = 32
TOP_K = 4
D_FF = D_MODEL
SWIGLU_ALPHA = 1.702
SWIGLU_LIMIT = 7.0

VMEM_LIMIT_BYTES = 56 * 1024 * 1024
META_LANES = 128
META_EXPERT, META_RANK, META_GATE = 0, TOP_K, 2 * TOP_K
ROW_DTYPE = F32
SUBLANES = 8
STRIP_ALIGN = SUBLANES * 4 // jnp.dtype(ROW_DTYPE).itemsize
FILLER_ROWS = N_EXPERTS * STRIP_ALIGN
N_SEG = N_EXPERTS + 1


def _rms(x, g):
    return x * lax.rsqrt(jnp.mean(x * x, axis=-1, keepdims=True) + EPS) * g


COL_Q, COL_K = 1, 2
COL_V, COL_G = (3, 4), (5, 6)
COL_GATE_A, COL_GATE_B = 7, 8


def _rotary_heads(y, cos, sin):
    half = RET_QK_HEAD // 2
    parts = []
    for h in range(RET_HEADS):
        x1 = y[:, h * RET_QK_HEAD:h * RET_QK_HEAD + half]
        x2 = y[:, h * RET_QK_HEAD + half:(h + 1) * RET_QK_HEAD]
        parts += [x1 * cos - x2 * sin, x2 * cos + x1 * sin]
    return jnp.concatenate(parts, axis=-1)


def _in_proj_kernel(x_ref, g_ref, w_ref, cos_ref, sin_ref, o_ref):
    h = _rms(x_ref[...], g_ref[...]).astype(BF16)
    for c in range(IN_W // D_MODEL):
        cols = slice(c * D_MODEL, (c + 1) * D_MODEL)
        y = jnp.dot(h, w_ref[:, cols], preferred_element_type=F32)
        if c == COL_Q:
            y = _rotary_heads(y, cos_ref[...], sin_ref[...])
        elif c == COL_K:
            y = _rotary_heads(y, cos_ref[...], sin_ref[...]) * (RET_QK_HEAD ** -0.5)
        elif c in COL_G:
            y = y * jax.nn.sigmoid(y)
        o_ref[:, cols] = y.astype(BF16)


def _rotary_tables(s):
    half = RET_QK_HEAD // 2
    pos = jnp.arange(s, dtype=F32)
    inv = ROPE_BASE ** (-jnp.linspace(0.0, 1.0, half, dtype=F32))
    ang = pos[:, None] * inv[None, :]
    return jnp.cos(ang), jnp.sin(ang)


def _in_proj(x2, g_mix, w_in_bf, seq_len, tm):
    t = x2.shape[0]
    half = RET_QK_HEAD // 2
    cos, sin = _rotary_tables(seq_len)
    tiles_per_seq = seq_len // tm
    return pl.pallas_call(
        _in_proj_kernel,
        out_shape=jax.ShapeDtypeStruct((t, IN_W), BF16),
        grid=(t // tm,),
        in_specs=[
            pl.BlockSpec((tm, D_MODEL), lambda i: (i, 0)),
            pl.BlockSpec((1, D_MODEL), lambda i: (0, 0)),
            pl.BlockSpec((D_MODEL, IN_W), lambda i: (0, 0), pipeline_mode=pl.Buffered(1)),
            pl.BlockSpec((tm, half), lambda i: (lax.rem(i, tiles_per_seq), 0)),
            pl.BlockSpec((tm, half), lambda i: (lax.rem(i, tiles_per_seq), 0)),
        ],
        out_specs=pl.BlockSpec((tm, IN_W), lambda i: (i, 0)),
        compiler_params=pltpu.CompilerParams(
            dimension_semantics=("arbitrary",),
            vmem_limit_bytes=VMEM_LIMIT_BYTES),
        name="in_proj",
    )(x2, g_mix, w_in_bf, cos, sin)


def _head_decay_logs():
    return [float(np.log(1.0 - 2.0 ** (-5.0 - h))) for h in range(RET_HEADS)]


def _retention_kernel(blk_decay, q_ref, k_ref, va_ref, vb_ref, ga_ref, gb_ref,
                      d_ref, qd_ref, kd_ref, o_ref, state_ref):
    @pl.when(pl.program_id(1) == 0)
    def _():
        state_ref[...] = jnp.zeros_like(state_ref)

    heads_per_ref = D_MODEL // RET_V_HEAD
    for h in range(RET_HEADS):
        qk_cols = slice(h * RET_QK_HEAD, (h + 1) * RET_QK_HEAD)
        v_cols = slice((h % heads_per_ref) * RET_V_HEAD, (h % heads_per_ref + 1) * RET_V_HEAD)
        q = q_ref[:, qk_cols]
        k = k_ref[:, qk_cols]
        v = (va_ref if h < heads_per_ref else vb_ref)[:, v_cols]
        g = (ga_ref if h < heads_per_ref else gb_ref)[:, v_cols]
        scores = lax.dot_general(q, k, (((1,), (1,)), ((), ())),
                                 preferred_element_type=F32) * d_ref[h]
        state = state_ref[h]
        o = jnp.dot(scores.astype(BF16), v, preferred_element_type=F32)
        o = o + jnp.dot((q.astype(F32) * qd_ref[h]).astype(BF16), state.astype(BF16),
                        preferred_element_type=F32)
        k_dec = (k.astype(F32) * kd_ref[h]).astype(BF16)
        state_ref[h] = state * blk_decay[h] + lax.dot_general(
            k_dec, v, (((0,), (0,)), ((), ())), preferred_element_type=F32)
        o = o * lax.rsqrt(jnp.mean(o * o, axis=-1, keepdims=True) + EPS)
        o_ref[:, h * RET_V_HEAD:(h + 1) * RET_V_HEAD] = (o * g.astype(F32)).astype(BF16)


def _retention_tables(blk):
    log_g = jnp.asarray(_head_decay_logs(), F32)
    idx = jnp.arange(blk, dtype=F32)
    diff = idx[:, None] - idx[None, :]
    chunk = jnp.arange(blk, dtype=jnp.int32) // CHUNK
    visible = chunk[None, :] <= chunk[:, None]
    dmask = jnp.where(visible[None], jnp.exp(log_g[:, None, None] * jnp.abs(diff)[None]), 0.0)
    q_dec = jnp.exp(log_g[:, None] * (idx + 1.0))[:, :, None]
    k_dec = jnp.exp(log_g[:, None] * (blk - 1.0 - idx))[:, :, None]
    q_dec = jnp.broadcast_to(q_dec, (RET_HEADS, blk, RET_QK_HEAD))
    k_dec = jnp.broadcast_to(k_dec, (RET_HEADS, blk, RET_QK_HEAD))
    return dmask.astype(F32), q_dec, k_dec


def _retention(proj3, blk):
    b, s, _ = proj3.shape
    dmask, q_dec, k_dec = _retention_tables(blk)
    blk_decay = [float(np.exp(lg * blk)) for lg in _head_decay_logs()]
    chunk = lambda c: pl.BlockSpec((None, blk, D_MODEL), lambda bi, l: (bi, l, c))
    table = lambda *shape: pl.BlockSpec(shape, lambda bi, l: (0,) * len(shape))
    return pl.pallas_call(
        functools.partial(_retention_kernel, blk_decay),
        out_shape=jax.ShapeDtypeStruct((b, s, RET_V_W), BF16),
        grid=(b, s // blk),
        in_specs=[
            chunk(COL_Q), chunk(COL_K), chunk(COL_V[0]), chunk(COL_V[1]),
            chunk(COL_G[0]), chunk(COL_G[1]),
            table(RET_HEADS, blk, blk),
            table(RET_HEADS, blk, RET_QK_HEAD),
            table(RET_HEADS, blk, RET_QK_HEAD),
        ],
        out_specs=pl.BlockSpec((None, blk, RET_V_W), lambda bi, l: (bi, l, 0)),
        scratch_shapes=[pltpu.VMEM((RET_HEADS, RET_QK_HEAD, RET_V_HEAD), F32)],
        compiler_params=pltpu.CompilerParams(
            dimension_semantics=("arbitrary", "arbitrary"),
            vmem_limit_bytes=VMEM_LIMIT_BYTES),
        name="retention",
    )(proj3, proj3, proj3, proj3, proj3, proj3, dmask, q_dec, k_dec)


def _window_sum(ext, w, tm):
    cur = ext
    span = 1
    while span < w:
        cur = cur[span:, :] + cur[:-span, :]
        span *= 2
    start = POOL_HALO + 1 - w
    return cur[start:start + tm, :]


def _mix_route_kernel(seq_len, u_ref, halo_ref, ga_ref, gb_ref, o_ref, x_ref,
                      wpool_ref, pscale_ref, wreto_ref, wout_ref, gffn_ref,
                      wr_hi_ref, wr_lo_ref, br_ref,
                      x1_ref, h2_ref, meta_ref, counts_ref):
    i = pl.program_id(0)
    tm = x_ref.shape[0]
    pos0 = lax.rem(i * tm, seq_len)

    u = u_ref[...].astype(F32)
    halo = jnp.where(pos0 == 0, 0.0, halo_ref[...].astype(F32))
    ext = jnp.concatenate([halo, u], axis=0)
    pos = (pos0 + lax.broadcasted_iota(jnp.int32, (tm, 1), 0)).astype(F32)
    pooled_out = []
    for g, w in enumerate(POOL_WINDOWS):
        cols = slice(g * POOL_GROUP_W, (g + 1) * POOL_GROUP_W)
        ws = _window_sum(ext[:, cols], w, tm)
        count = jnp.minimum(pos + 1.0, float(w))
        pooled = ws / count - u[:, cols]
        pooled_out.append(jnp.dot(pooled.astype(BF16), wpool_ref[g],
                                  preferred_element_type=F32))
    y_pool = jnp.concatenate(pooled_out, axis=-1) * pscale_ref[...]

    y_ret = jnp.dot(o_ref[...], wreto_ref[...], preferred_element_type=F32)
    merged = (jax.nn.sigmoid(ga_ref[...].astype(F32)) * y_pool
              + jax.nn.sigmoid(gb_ref[...].astype(F32)) * y_ret)
    x1 = x_ref[...] + jnp.dot(merged.astype(BF16), wout_ref[...],
                              preferred_element_type=F32)
    x1_ref[...] = x1
    h2 = _rms(x1, gffn_ref[...])
    h2_ref[...] = h2.astype(BF16)

    h_hi = h2.astype(BF16)
    h_lo = (h2 - h_hi.astype(F32)).astype(BF16)
    logits = (jnp.dot(h_hi, wr_hi_ref[...], preferred_element_type=F32)
              + jnp.dot(h_lo, wr_hi_ref[...], preferred_element_type=F32)
              + jnp.dot(h_hi, wr_lo_ref[...], preferred_element_type=F32)
              + br_ref[...])

    lane = lax.broadcasted_iota(jnp.int32, (tm, N_EXPERTS), 1)
    work = logits
    vals, idxs, hots = [], [], []
    for _ in range(TOP_K):
        m = jnp.max(work, axis=-1, keepdims=True)
        idx = jnp.min(jnp.where(work == m, lane, N_EXPERTS), axis=-1, keepdims=True)
        hot = lane == idx
        vals.append(m)
        idxs.append(idx)
        hots.append(hot)
        work = jnp.where(hot, -jnp.inf, work)
    exps = [jnp.exp(v - vals[0]) for v in vals]
    denom = exps[0] + exps[1] + exps[2] + exps[3]
    gates = [e / denom for e in exps]

    sel = (jnp.where(hots[0], 1.0, 0.0) + jnp.where(hots[1], 1.0, 0.0)
           + jnp.where(hots[2], 1.0, 0.0) + jnp.where(hots[3], 1.0, 0.0))
    row = lax.broadcasted_iota(jnp.int32, (tm, tm), 0)
    col = lax.broadcasted_iota(jnp.int32, (tm, tm), 1)
    lower = jnp.where(col < row, 1.0, 0.0).astype(BF16)
    before = jnp.dot(lower, sel.astype(BF16), preferred_element_type=F32)
    counts_ref[...] = jnp.sum(sel, axis=0, keepdims=True)

    mlane = lax.broadcasted_iota(jnp.int32, (tm, META_LANES), 1)
    meta = jnp.zeros((tm, META_LANES), F32)
    for r in range(TOP_K):
        rank_r = jnp.sum(jnp.where(hots[r], before, 0.0), axis=-1, keepdims=True)
        meta = jnp.where(mlane == META_EXPERT + r, idxs[r].astype(F32), meta)
        meta = jnp.where(mlane == META_RANK + r, rank_r, meta)
        meta = jnp.where(mlane == META_GATE + r, gates[r], meta)
    meta_ref[...] = meta


def _mix_route(proj, o_gated, x2, w_pool_bf, pool_scale, w_ret_o_bf, w_out_bf,
               g_ffn, wr_hi, wr_lo, b_router, seq_len, tm):
    t = x2.shape[0]
    halo_per_tile = tm // POOL_HALO
    full = lambda *shape: pl.BlockSpec(shape, lambda i: (0,) * len(shape))
    return pl.pallas_call(
        functools.partial(_mix_route_kernel, seq_len),
        out_shape=(
            jax.ShapeDtypeStruct((t, D_MODEL), F32),
            jax.ShapeDtypeStruct((t, D_MODEL), BF16),
            jax.ShapeDtypeStruct((t, META_LANES), F32),
            jax.ShapeDtypeStruct((t // tm, 1, N_EXPERTS), F32),
        ),
        grid=(t // tm,),
        in_specs=[
            pl.BlockSpec((tm, D_MODEL), lambda i: (i, 0)),
            pl.BlockSpec((POOL_HALO, D_MODEL),
                         lambda i: (jnp.maximum(i * halo_per_tile - 1, 0), 0)),
            pl.BlockSpec((tm, D_MODEL), lambda i: (i, COL_GATE_A)),
            pl.BlockSpec((tm, D_MODEL), lambda i: (i, COL_GATE_B)),
            pl.BlockSpec((tm, RET_V_W), lambda i: (i, 0)),
            pl.BlockSpec((tm, D_MODEL), lambda i: (i, 0)),
            full(len(POOL_WINDOWS), POOL_GROUP_W, POOL_GROUP_W),
            full(1, D_MODEL),
            full(RET_V_W, D_MODEL),
            full(D_MODEL, D_MODEL),
            full(1, D_MODEL),
            full(D_MODEL, N_EXPERTS),
            full(D_MODEL, N_EXPERTS),
            full(1, N_EXPERTS),
        ],
        out_specs=(
            pl.BlockSpec((tm, D_MODEL), lambda i: (i, 0)),
            pl.BlockSpec((tm, D_MODEL), lambda i: (i, 0)),
            pl.BlockSpec((tm, META_LANES), lambda i: (i, 0)),
            pl.BlockSpec((None, 1, N_EXPERTS), lambda i: (i, 0, 0)),
        ),
        compiler_params=pltpu.CompilerParams(
            dimension_semantics=("arbitrary",),
            vmem_limit_bytes=VMEM_LIMIT_BYTES),
        name="mix_route",
    )(proj, proj, proj, proj, o_gated, x2, w_pool_bf, pool_scale, w_ret_o_bf,
      w_out_bf, g_ffn, wr_hi, wr_lo, b_router)


def _local_positions(meta, lstart):
    tm = meta.shape[0]
    lane_e = lax.broadcasted_iota(jnp.int32, (tm, N_EXPERTS), 1)
    out = []
    for r in range(TOP_K):
        e_r = meta[:, META_EXPERT + r:META_EXPERT + r + 1].astype(jnp.int32)
        first = jnp.sum(jnp.where(lane_e == e_r, lstart, 0.0), axis=-1, keepdims=True)
        out.append((meta[:, META_RANK + r:META_RANK + r + 1] + first).astype(jnp.int32))
    return out


def _local_rows(tm):
    return TOP_K * tm + FILLER_ROWS


def _for_each_strip(count, max_rows, start_copy):
    piece = 1 << (max(max_rows, FILLER_ROWS).bit_length() - 1)
    off = jnp.int32(0)
    while piece >= STRIP_ALIGN:
        take = count & piece

        @pl.when(take != 0)
        def _(off=off, piece=piece):
            start_copy(off, piece)

        off = off + take
        piece //= 2


def _dispatch_kernel(blk_rows, base_ref, lstart_ref, cnt_ref, zflag_ref,
                     h_ref, meta_ref, lstart_col_ref, xs_ref,
                     sorted_ref, zero_ref, sems, zsem):
    i = pl.program_id(0)
    tm = h_ref.shape[0]
    nrows = _local_rows(tm)
    slot = lax.rem(i, 2)

    @pl.when(i == 0)
    def _():
        zero_ref[...] = jnp.zeros_like(zero_ref)

        def zcopy(b):
            start = pl.multiple_of(b * blk_rows, blk_rows)
            return pltpu.make_async_copy(zero_ref, xs_ref.at[pl.ds(start, blk_rows), :], zsem)

        def zissue(b, c):
            @pl.when(zflag_ref[b] > 0)
            def _():
                zcopy(b).start()
            return c

        def zwait(b, c):
            @pl.when(zflag_ref[b] > 0)
            def _():
                zcopy(b).wait()
            return c

        n_zero_blocks = xs_ref.shape[0] // blk_rows
        lax.fori_loop(0, n_zero_blocks, zissue, 0)
        lax.fori_loop(0, n_zero_blocks, zwait, 0)

    meta_t = meta_ref[...].T
    sub_e = lax.broadcasted_iota(jnp.int32, (N_EXPERTS, tm), 0)
    pos = []
    for r in range(TOP_K):
        e_r = meta_t[META_EXPERT + r:META_EXPERT + r + 1, :].astype(jnp.int32)
        first = jnp.sum(jnp.where(sub_e == e_r, lstart_col_ref[...], 0.0), axis=0, keepdims=True)
        pos.append((meta_t[META_RANK + r:META_RANK + r + 1, :] + first).astype(jnp.int32))
    jr = lax.broadcasted_iota(jnp.int32, (nrows, tm), 0)
    onehot = jnp.where(jr == pos[0], 1.0, jnp.where(jr == pos[1], 1.0,
             jnp.where(jr == pos[2], 1.0, jnp.where(jr == pos[3], 1.0, 0.0))))
    srt = jnp.dot(onehot.astype(BF16), h_ref[...], preferred_element_type=F32)
    sorted_ref[slot] = srt.astype(ROW_DTYPE)

    def per_segment(e, c):
        k = i * N_SEG + e
        src0 = lstart_ref[k]
        dst0 = base_ref[k]

        def start_copy(off, rows):
            src = pl.multiple_of(src0 + off, STRIP_ALIGN)
            dst = pl.multiple_of(dst0 + off, STRIP_ALIGN)
            pltpu.make_async_copy(sorted_ref.at[slot, pl.ds(src, rows), :],
                                  xs_ref.at[pl.ds(dst, rows), :],
                                  sems.at[slot]).start()

        _for_each_strip(cnt_ref[k], tm, start_copy)
        return c

    lax.fori_loop(0, N_SEG, per_segment, 0)

    def wait_tile(s):
        pltpu.make_async_copy(sorted_ref.at[s], xs_ref.at[pl.ds(0, nrows), :],
                              sems.at[s]).wait()

    @pl.when(i >= 1)
    def _():
        wait_tile(1 - slot)

    @pl.when(i == pl.num_programs(0) - 1)
    def _():
        wait_tile(slot)


def _dispatch(h2, meta, tables, n_buf, blk_rows, tm):
    t = h2.shape[0]
    grid_spec = pltpu.PrefetchScalarGridSpec(
        num_scalar_prefetch=4,
        grid=(t // tm,),
        in_specs=[
            pl.BlockSpec((tm, D_MODEL), lambda i, *_: (i, 0)),
            pl.BlockSpec((tm, META_LANES), lambda i, *_: (i, 0)),
            pl.BlockSpec((None, N_EXPERTS, 1), lambda i, *_: (i, 0, 0)),
        ],
        out_specs=pl.BlockSpec(memory_space=pl.ANY),
        scratch_shapes=[pltpu.VMEM((2, _local_rows(tm), D_MODEL), ROW_DTYPE),
                        pltpu.VMEM((blk_rows, D_MODEL), ROW_DTYPE),
                        pltpu.SemaphoreType.DMA((2,)),
                        pltpu.SemaphoreType.DMA(())],
    )
    return pl.pallas_call(
        functools.partial(_dispatch_kernel, blk_rows),
        out_shape=jax.ShapeDtypeStruct((n_buf + 2 * FILLER_ROWS, D_MODEL), ROW_DTYPE),
        grid_spec=grid_spec,
        compiler_params=pltpu.CompilerParams(
            dimension_semantics=("arbitrary",),
            vmem_limit_bytes=VMEM_LIMIT_BYTES,
            has_side_effects=True),
        name="dispatch",
    )(tables["base_out"], tables["lstart"], tables["cnt"], tables["zflag"],
      h2, meta, tables["lstart_cols"])


def _experts_kernel(be_ref, nu_ref, x_ref, wgu_ref, bgu_ref, wd_ref, bd_ref,
                    y_ref, wgu_bf, wd_bf):
    i = pl.program_id(0)

    @pl.when(i < nu_ref[0])
    def _():
        prev = be_ref[jnp.maximum(i - 1, 0)]

        @pl.when((i == 0) | (be_ref[i] != prev))
        def _():
            wgu_bf[...] = wgu_ref[...].astype(BF16)
            wd_bf[...] = wd_ref[...].astype(BF16)

        gu = jnp.dot(x_ref[...].astype(BF16), wgu_bf[...],
                     preferred_element_type=F32) + bgu_ref[...]
        glu = jnp.minimum(gu[:, :D_FF], SWIGLU_LIMIT)
        lin = jnp.clip(gu[:, D_FF:], -SWIGLU_LIMIT, SWIGLU_LIMIT)
        act = glu * jax.nn.sigmoid(SWIGLU_ALPHA * glu) * (lin + 1.0)
        y_ref[...] = (jnp.dot(act.astype(BF16), wd_bf[...],
                              preferred_element_type=F32) + bd_ref[...]).astype(ROW_DTYPE)

    @pl.when(i >= nu_ref[0])
    def _():
        y_ref[...] = jnp.zeros_like(y_ref)


def _experts(xs, n_buf, block_e, n_used, w_gate_up, b_gate_up, w_down, b_down, blk_rows):
    used = lambda i, be, nu: jnp.minimum(i, nu[0] - 1)
    grid_spec = pltpu.PrefetchScalarGridSpec(
        num_scalar_prefetch=2,
        grid=(n_buf // blk_rows,),
        in_specs=[
            pl.BlockSpec((blk_rows, D_MODEL), lambda i, be, nu: (used(i, be, nu), 0)),
            pl.BlockSpec((None, D_MODEL, 2 * D_FF), lambda i, be, nu: (be[i], 0, 0)),
            pl.BlockSpec((None, 1, 2 * D_FF), lambda i, be, nu: (be[i], 0, 0)),
            pl.BlockSpec((None, D_FF, D_MODEL), lambda i, be, nu: (be[i], 0, 0)),
            pl.BlockSpec((None, 1, D_MODEL), lambda i, be, nu: (be[i], 0, 0)),
        ],
        out_specs=pl.BlockSpec((blk_rows, D_MODEL), lambda i, be, nu: (i, 0)),
        scratch_shapes=[pltpu.VMEM((D_MODEL, 2 * D_FF), BF16),
                        pltpu.VMEM((D_FF, D_MODEL), BF16)],
    )
    return pl.pallas_call(
        _experts_kernel,
        out_shape=jax.ShapeDtypeStruct((n_buf, D_MODEL), ROW_DTYPE),
        grid_spec=grid_spec,
        compiler_params=pltpu.CompilerParams(
            dimension_semantics=("arbitrary",),
            vmem_limit_bytes=VMEM_LIMIT_BYTES),
        name="experts",
    )(block_e, n_used, xs, w_gate_up, b_gate_up[:, None, :], w_down, b_down[:, None, :])


def _combine_kernel(base_ref, lstart_ref, cnt_ref, meta_ref, lstart_row_ref, x1_ref,
                    p_ref, ys_ref, gple_ref, wpg_ref, wple_ref, gfin_ref, out_ref,
                    rows_ref, sems):
    i = pl.program_id(0)
    n = pl.num_programs(0)
    tm = x1_ref.shape[0]
    nrows = _local_rows(tm)
    slot = lax.rem(i, 2)

    def fetch_tile(tile, s):
        def per_segment(e, c):
            k = tile * N_SEG + e
            src0 = base_ref[k]
            dst0 = lstart_ref[k]

            def start_copy(off, rows):
                src = pl.multiple_of(src0 + off, STRIP_ALIGN)
                dst = pl.multiple_of(dst0 + off, STRIP_ALIGN)
                pltpu.make_async_copy(ys_ref.at[pl.ds(src, rows), :],
                                      rows_ref.at[s, pl.ds(dst, rows), :],
                                      sems.at[s]).start()

            _for_each_strip(cnt_ref[k], tm, start_copy)
            return c

        lax.fori_loop(0, N_SEG, per_segment, 0)

    @pl.when(i == 0)
    def _():
        fetch_tile(0, 0)

    @pl.when(i + 1 < n)
    def _():
        fetch_tile(i + 1, 1 - slot)

    meta = meta_ref[...]
    pos = _local_positions(meta, lstart_row_ref[...])
    jl = lax.broadcasted_iota(jnp.int32, (tm, nrows), 1)
    gate = [meta[:, META_GATE + r:META_GATE + r + 1] for r in range(TOP_K)]
    weights = jnp.where(jl == pos[0], gate[0], jnp.where(jl == pos[1], gate[1],
              jnp.where(jl == pos[2], gate[2], jnp.where(jl == pos[3], gate[3], 0.0))))

    pltpu.make_async_copy(ys_ref.at[pl.ds(0, nrows), :], rows_ref.at[slot],
                          sems.at[slot]).wait()
    moe = jnp.dot(weights.astype(BF16), rows_ref[slot].astype(BF16),
                  preferred_element_type=F32)
    x2 = x1_ref[...] + moe
    u = _rms(x2, gple_ref[...])
    gate_ple = jax.nn.sigmoid(jnp.dot(u.astype(BF16), wpg_ref[...],
                                      preferred_element_type=F32))
    emb = jnp.dot(p_ref[...].astype(BF16), wple_ref[...], preferred_element_type=F32)
    x3 = x2 + gate_ple * emb
    out_ref[...] = _rms(x3, gfin_ref[...])


def _combine(meta, tables, x1, p2, ys, g_ple, w_ple_gate_bf, w_ple_bf, g_final, tm):
    t = x1.shape[0]
    full = lambda *shape: pl.BlockSpec(shape, lambda i, *_: (0,) * len(shape))
    grid_spec = pltpu.PrefetchScalarGridSpec(
        num_scalar_prefetch=3,
        grid=(t // tm,),
        in_specs=[
            pl.BlockSpec((tm, META_LANES), lambda i, *_: (i, 0)),
            pl.BlockSpec((None, 1, N_EXPERTS), lambda i, *_: (i, 0, 0)),
            pl.BlockSpec((tm, D_MODEL), lambda i, *_: (i, 0)),
            pl.BlockSpec((tm, PLE_DIM), lambda i, *_: (i, 0)),
            pl.BlockSpec(memory_space=pl.ANY),
            full(1, D_MODEL),
            full(D_MODEL, D_MODEL),
            full(PLE_DIM, D_MODEL),
            full(1, D_MODEL),
        ],
        out_specs=pl.BlockSpec((tm, D_MODEL), lambda i, *_: (i, 0)),
        scratch_shapes=[pltpu.VMEM((2, _local_rows(tm), D_MODEL), ROW_DTYPE),
                        pltpu.SemaphoreType.DMA((2,))],
    )
    return pl.pallas_call(
        _combine_kernel,
        out_shape=jax.ShapeDtypeStruct((t, D_MODEL), F32),
        grid_spec=grid_spec,
        compiler_params=pltpu.CompilerParams(
            dimension_semantics=("arbitrary",),
            vmem_limit_bytes=VMEM_LIMIT_BYTES),
        name="combine",
    )(tables["base_in"], tables["lstart"], tables["cnt"], meta, tables["lstart_rows"],
      x1, p2, ys, g_ple, w_ple_gate_bf, w_ple_bf, g_final)


def _tiles(seq_len):
    return dict(
        proj_rows=min(512, seq_len),
        ret_block=min(256, seq_len),
        route_rows=min(512, seq_len),
        expert_rows=512,
    )


def _slot_tables(tile_counts, blk, n_buf, nloc):
    cnt = tile_counts[:, 0, :].astype(jnp.int32)
    cnt = ((cnt + STRIP_ALIGN - 1) // STRIP_ALIGN) * STRIP_ALIGN
    seg_rows = jnp.sum(cnt, axis=1)
    counts = jnp.sum(cnt, axis=0)
    padded = ((counts + blk - 1) // blk) * blk
    ends_pad = jnp.cumsum(padded)
    starts_pad = ends_pad - padded
    tile_before = jnp.cumsum(cnt, axis=0) - cnt
    lstart = jnp.cumsum(cnt, axis=1) - cnt
    base = starts_pad[None, :] + tile_before
    with_filler = lambda a, col: jnp.concatenate([a, col[:, None]], axis=1).reshape(-1)
    blk_start = jnp.arange((n_buf + 2 * FILLER_ROWS) // blk, dtype=jnp.int32) * blk
    last_of_expert = jnp.any((padded > 0)[None, :]
                             & (blk_start[:, None] == (ends_pad - blk)[None, :]), axis=1)
    zflag = (last_of_expert | (blk_start >= ends_pad[-1])).astype(jnp.int32)
    filler_dst = n_buf + (jnp.arange(cnt.shape[0], dtype=jnp.int32) % 2) * FILLER_ROWS
    return dict(
        cnt=with_filler(cnt, nloc - seg_rows),
        lstart=with_filler(lstart, seg_rows),
        base_out=with_filler(base, filler_dst),
        base_in=with_filler(base, jnp.zeros_like(seg_rows)),
        lstart_rows=lstart.astype(F32)[:, None, :],
        lstart_cols=lstart.astype(F32)[:, :, None],
        zflag=zflag, ends_pad=ends_pad)


def _layer(x2, p2, seq_len, g_mix, w_in, w_pool, pool_scale, w_ret_o, w_out, g_ffn,
           w_router, b_router, w_gate_up, b_gate_up, w_down, b_down, g_ple,
           w_ple_gate, w_ple, g_out):
    t = x2.shape[0]
    cfg = _tiles(seq_len)
    row = lambda a: a.reshape(1, -1)

    proj = _in_proj(x2, row(g_mix), w_in.astype(BF16), seq_len, cfg["proj_rows"])
    o_gated = _retention(proj.reshape(t // seq_len, seq_len, IN_W), cfg["ret_block"])
    o_gated = o_gated.reshape(t, RET_V_W)

    wr_hi = w_router.astype(BF16)
    wr_lo = (w_router - wr_hi.astype(F32)).astype(BF16)
    tm = cfg["route_rows"]
    x1, h2, meta, tile_counts = _mix_route(
        proj, o_gated, x2, w_pool.astype(BF16), row(pool_scale), w_ret_o.astype(BF16),
        w_out.astype(BF16), row(g_ffn), wr_hi, wr_lo, row(b_router), seq_len, tm)

    blk = cfg["expert_rows"]
    n_buf = t * TOP_K + (t // tm) * FILLER_ROWS + N_EXPERTS * blk
    n_blocks = n_buf // blk
    tables = _slot_tables(tile_counts, blk, n_buf, _local_rows(tm))
    blk_start = jnp.arange(n_blocks, dtype=jnp.int32) * blk
    block_e = jnp.minimum(
        jnp.sum((tables["ends_pad"][None, :] <= blk_start[:, None]).astype(jnp.int32), axis=1),
        N_EXPERTS - 1)
    n_used = tables["ends_pad"][-1:] // blk

    xs = _dispatch(h2, meta, tables, n_buf, blk, tm)
    ys = _experts(xs, n_buf, block_e, n_used, w_gate_up, b_gate_up, w_down, b_down, blk)
    return _combine(meta, tables, x1, p2, ys, row(g_ple), w_ple_gate.astype(BF16),
                    w_ple.astype(BF16), row(g_out), tm)


def kernel(x, p, g_mix, w_in, w_pool, pool_scale, w_ret_o, w_out, g_ffn, w_router,
           b_router, w_gate_up, b_gate_up, w_down, b_down, g_ple, w_ple_gate, w_ple,
           g_final):
    b, s, d = x.shape
    depth = p.shape[0]
    assert depth == 1 and d == D_MODEL
    x2 = x.reshape(b * s, d)
    out = _layer(x2, p[0].reshape(b * s, PLE_DIM), s, g_mix[0], w_in[0], w_pool[0],
                 pool_scale[0], w_ret_o[0], w_out[0], g_ffn[0], w_router[0],
                 b_router[0], w_gate_up[0], b_gate_up[0], w_down[0], b_down[0],
                 g_ple[0], w_ple_gate[0], w_ple[0], g_final)
    return out.reshape(b, s, d)
```

```python
import functools

import numpy as np
import jax
import jax.numpy as jnp
from jax import lax
from jax.experimental import pallas as pl
from jax.experimental.pallas import tpu as pltpu

F32 = jnp.float32
BF16 = jnp.bfloat16

D_MODEL = 1024
EPS = 1e-6
CHUNK = 64
PLE_DIM = 256
POOL_WINDOWS = (2, 4, 8, 16)
POOL_GROUP_W = D_MODEL // len(POOL_WINDOWS)
POOL_HALO = 16
RET_HEADS = 4
RET_QK_HEAD = 256
RET_V_HEAD = 512
RET_V_W = RET_HEADS * RET_V_HEAD
ROPE_BASE = 10000.0
IN_W = 9 * D_MODEL
N_EXPERTS = 32
TOP_K = 4
D_FF = D_MODEL
SWIGLU_ALPHA = 1.702
SWIGLU_LIMIT = 7.0

VMEM_LIMIT_BYTES = 56 * 1024 * 1024
LANES = 128
META_ROWS = 16
META_EXPERT, META_RANK, META_GATE = 0, TOP_K, 2 * TOP_K
ROW_DTYPE = F32
SUBLANES = 8
STRIP_ALIGN = SUBLANES * 4 // jnp.dtype(ROW_DTYPE).itemsize
FILLER_ROWS = N_EXPERTS * STRIP_ALIGN
N_SEG = N_EXPERTS + 1


def _rms(x, g):
    return x * lax.rsqrt(jnp.mean(x * x, axis=-1, keepdims=True) + EPS) * g


COL_Q, COL_K = 1, 2
COL_V, COL_G = (3, 4), (5, 6)
COL_GATE_A, COL_GATE_B = 7, 8


def _rotary_heads(y, cos, sin):
    half = RET_QK_HEAD // 2
    parts = []
    for h in range(RET_HEADS):
        x1 = y[:, h * RET_QK_HEAD:h * RET_QK_HEAD + half]
        x2 = y[:, h * RET_QK_HEAD + half:(h + 1) * RET_QK_HEAD]
        parts += [x1 * cos - x2 * sin, x2 * cos + x1 * sin]
    return jnp.concatenate(parts, axis=-1)


def _in_proj_kernel(x_ref, g_ref, w_ref, cos_ref, sin_ref, o_ref):
    h = _rms(x_ref[...], g_ref[...]).astype(BF16)
    for c in range(IN_W // D_MODEL):
        cols = slice(c * D_MODEL, (c + 1) * D_MODEL)
        y = jnp.dot(h, w_ref[:, cols], preferred_element_type=F32)
        if c == COL_Q:
            y = _rotary_heads(y, cos_ref[...], sin_ref[...])
        elif c == COL_K:
            y = _rotary_heads(y, cos_ref[...], sin_ref[...]) * (RET_QK_HEAD ** -0.5)
        elif c in COL_G:
            y = y * jax.nn.sigmoid(y)
        o_ref[:, cols] = y.astype(BF16)


def _rotary_tables(s):
    half = RET_QK_HEAD // 2
    pos = jnp.arange(s, dtype=F32)
    inv = ROPE_BASE ** (-jnp.linspace(0.0, 1.0, half, dtype=F32))
    ang = pos[:, None] * inv[None, :]
    return jnp.cos(ang), jnp.sin(ang)


def _in_proj(x2, g_mix, w_in_bf, seq_len, tm):
    t = x2.shape[0]
    half = RET_QK_HEAD // 2
    cos, sin = _rotary_tables(seq_len)
    tiles_per_seq = seq_len // tm
    return pl.pallas_call(
        _in_proj_kernel,
        out_shape=jax.ShapeDtypeStruct((t, IN_W), BF16),
        grid=(t // tm,),
        in_specs=[
            pl.BlockSpec((tm, D_MODEL), lambda i: (i, 0)),
            pl.BlockSpec((1, D_MODEL), lambda i: (0, 0)),
            pl.BlockSpec((D_MODEL, IN_W), lambda i: (0, 0), pipeline_mode=pl.Buffered(1)),
            pl.BlockSpec((tm, half), lambda i: (lax.rem(i, tiles_per_seq), 0)),
            pl.BlockSpec((tm, half), lambda i: (lax.rem(i, tiles_per_seq), 0)),
        ],
        out_specs=pl.BlockSpec((tm, IN_W), lambda i: (i, 0)),
        compiler_params=pltpu.CompilerParams(
            dimension_semantics=("arbitrary",),
            vmem_limit_bytes=VMEM_LIMIT_BYTES),
        name="in_proj",
    )(x2, g_mix, w_in_bf, cos, sin)


def _head_decay_logs():
    return [float(np.log(1.0 - 2.0 ** (-5.0 - h))) for h in range(RET_HEADS)]


def _retention_kernel(blk_decay, q_ref, k_ref, va_ref, vb_ref, ga_ref, gb_ref,
                      d_ref, qd_ref, kd_ref, o_ref, state_ref):
    @pl.when(pl.program_id(1) == 0)
    def _():
        state_ref[...] = jnp.zeros_like(state_ref)

    heads_per_ref = D_MODEL // RET_V_HEAD
    for h in range(RET_HEADS):
        qk_cols = slice(h * RET_QK_HEAD, (h + 1) * RET_QK_HEAD)
        v_cols = slice((h % heads_per_ref) * RET_V_HEAD, (h % heads_per_ref + 1) * RET_V_HEAD)
        q = q_ref[:, qk_cols]
        k = k_ref[:, qk_cols]
        v = (va_ref if h < heads_per_ref else vb_ref)[:, v_cols]
        g = (ga_ref if h < heads_per_ref else gb_ref)[:, v_cols]
        scores = lax.dot_general(q, k, (((1,), (1,)), ((), ())),
                                 preferred_element_type=F32) * d_ref[h]
        state = state_ref[h]
        o = jnp.dot(scores.astype(BF16), v, preferred_element_type=F32)
        o = o + jnp.dot((q.astype(F32) * qd_ref[h]).astype(BF16), state.astype(BF16),
                        preferred_element_type=F32)
        k_dec = (k.astype(F32) * kd_ref[h]).astype(BF16)
        state_ref[h] = state * blk_decay[h] + lax.dot_general(
            k_dec, v, (((0,), (0,)), ((), ())), preferred_element_type=F32)
        o = o * lax.rsqrt(jnp.mean(o * o, axis=-1, keepdims=True) + EPS)
        o_ref[:, h * RET_V_HEAD:(h + 1) * RET_V_HEAD] = (o * g.astype(F32)).astype(BF16)


def _retention_tables(blk):
    log_g = jnp.asarray(_head_decay_logs(), F32)
    idx = jnp.arange(blk, dtype=F32)
    diff = idx[:, None] - idx[None, :]
    chunk = jnp.arange(blk, dtype=jnp.int32) // CHUNK
    visible = chunk[None, :] <= chunk[:, None]
    dmask = jnp.where(visible[None], jnp.exp(log_g[:, None, None] * jnp.abs(diff)[None]), 0.0)
    q_dec = jnp.exp(log_g[:, None] * (idx + 1.0))[:, :, None]
    k_dec = jnp.exp(log_g[:, None] * (blk - 1.0 - idx))[:, :, None]
    q_dec = jnp.broadcast_to(q_dec, (RET_HEADS, blk, RET_QK_HEAD))
    k_dec = jnp.broadcast_to(k_dec, (RET_HEADS, blk, RET_QK_HEAD))
    return dmask.astype(F32), q_dec, k_dec


def _retention(proj3, blk):
    b, s, _ = proj3.shape
    dmask, q_dec, k_dec = _retention_tables(blk)
    blk_decay = [float(np.exp(lg * blk)) for lg in _head_decay_logs()]
    chunk = lambda c: pl.BlockSpec((None, blk, D_MODEL), lambda bi, l: (bi, l, c))
    table = lambda *shape: pl.BlockSpec(shape, lambda bi, l: (0,) * len(shape))
    return pl.pallas_call(
        functools.partial(_retention_kernel, blk_decay),
        out_shape=jax.ShapeDtypeStruct((b, s, RET_V_W), BF16),
        grid=(b, s // blk),
        in_specs=[
            chunk(COL_Q), chunk(COL_K), chunk(COL_V[0]), chunk(COL_V[1]),
            chunk(COL_G[0]), chunk(COL_G[1]),
            table(RET_HEADS, blk, blk),
            table(RET_HEADS, blk, RET_QK_HEAD),
            table(RET_HEADS, blk, RET_QK_HEAD),
        ],
        out_specs=pl.BlockSpec((None, blk, RET_V_W), lambda bi, l: (bi, l, 0)),
        scratch_shapes=[pltpu.VMEM((RET_HEADS, RET_QK_HEAD, RET_V_HEAD), F32)],
        compiler_params=pltpu.CompilerParams(
            dimension_semantics=("arbitrary", "arbitrary"),
            vmem_limit_bytes=VMEM_LIMIT_BYTES),
        name="retention",
    )(proj3, proj3, proj3, proj3, proj3, proj3, dmask, q_dec, k_dec)


def _window_sum(ext, w, tm):
    cur = ext
    span = 1
    while span < w:
        cur = cur[span:, :] + cur[:-span, :]
        span *= 2
    start = POOL_HALO + 1 - w
    return cur[start:start + tm, :]


def _mix_route_kernel(seq_len, u_ref, halo_ref, ga_ref, gb_ref, o_ref, x_ref,
                      wpool_ref, pscale_ref, wreto_ref, wout_ref, gffn_ref,
                      wr_hi_ref, wr_lo_ref, br_ref,
                      x1_ref, h2_ref, meta_ref, counts_ref):
    i = pl.program_id(0)
    tm = x_ref.shape[0]
    pos0 = lax.rem(i * tm, seq_len)

    u = u_ref[...].astype(F32)
    halo = jnp.where(pos0 == 0, 0.0, halo_ref[...].astype(F32))
    ext = jnp.concatenate([halo, u], axis=0)
    pos = (pos0 + lax.broadcasted_iota(jnp.int32, (tm, 1), 0)).astype(F32)
    pooled_out = []
    for g, w in enumerate(POOL_WINDOWS):
        cols = slice(g * POOL_GROUP_W, (g + 1) * POOL_GROUP_W)
        ws = _window_sum(ext[:, cols], w, tm)
        count = jnp.minimum(pos + 1.0, float(w))
        pooled = ws / count - u[:, cols]
        pooled_out.append(jnp.dot(pooled.astype(BF16), wpool_ref[g],
                                  preferred_element_type=F32))
    y_pool = jnp.concatenate(pooled_out, axis=-1) * pscale_ref[...]

    y_ret = jnp.dot(o_ref[...], wreto_ref[...], preferred_element_type=F32)
    merged = (jax.nn.sigmoid(ga_ref[...].astype(F32)) * y_pool
              + jax.nn.sigmoid(gb_ref[...].astype(F32)) * y_ret)
    x1 = x_ref[...] + jnp.dot(merged.astype(BF16), wout_ref[...],
                              preferred_element_type=F32)
    x1_ref[...] = x1
    h2 = _rms(x1, gffn_ref[...])
    h2_ref[...] = h2.astype(BF16)

    h_hi = h2.astype(BF16)
    h_lo = (h2 - h_hi.astype(F32)).astype(BF16)
    logits = (jnp.dot(h_hi, wr_hi_ref[...], preferred_element_type=F32)
              + jnp.dot(h_lo, wr_hi_ref[...], preferred_element_type=F32)
              + jnp.dot(h_hi, wr_lo_ref[...], preferred_element_type=F32)
              + br_ref[...])
    logits_t = logits.T[:N_EXPERTS, :]

    sub = lax.broadcasted_iota(jnp.int32, (N_EXPERTS, tm), 0)
    work = logits_t
    vals, idxs, hots = [], [], []
    for _ in range(TOP_K):
        m = jnp.max(work, axis=0, keepdims=True)
        idx = jnp.min(jnp.where(work == m, sub, N_EXPERTS), axis=0, keepdims=True)
        hot = sub == idx
        vals.append(m)
        idxs.append(idx)
        hots.append(hot)
        work = jnp.where(hot, -jnp.inf, work)
    exps = [jnp.exp(v - vals[0]) for v in vals]
    denom = exps[0] + exps[1] + exps[2] + exps[3]
    gates = [e / denom for e in exps]

    sel = (jnp.where(hots[0], 1.0, 0.0) + jnp.where(hots[1], 1.0, 0.0)
           + jnp.where(hots[2], 1.0, 0.0) + jnp.where(hots[3], 1.0, 0.0))
    row = lax.broadcasted_iota(jnp.int32, (tm, tm), 0)
    col = lax.broadcasted_iota(jnp.int32, (tm, tm), 1)
    earlier = jnp.where(row < col, 1.0, 0.0).astype(BF16)
    before = jnp.dot(sel.astype(BF16), earlier, preferred_element_type=F32)
    counts_ref[...] = jnp.sum(sel, axis=1, keepdims=True)

    msub = lax.broadcasted_iota(jnp.int32, (META_ROWS, tm), 0)
    meta = jnp.zeros((META_ROWS, tm), F32)
    for r in range(TOP_K):
        rank_r = jnp.sum(jnp.where(hots[r], before, 0.0), axis=0, keepdims=True)
        meta = jnp.where(msub == META_EXPERT + r, idxs[r].astype(F32), meta)
        meta = jnp.where(msub == META_RANK + r, rank_r, meta)
        meta = jnp.where(msub == META_GATE + r, gates[r], meta)
    meta_ref[...] = meta


def _mix_route(proj, o_gated, x2, w_pool_bf, pool_scale, w_ret_o_bf, w_out_bf,
               g_ffn, wr_hi, wr_lo, b_router, seq_len, tm):
    t = x2.shape[0]
    halo_per_tile = tm // POOL_HALO
    full = lambda *shape: pl.BlockSpec(shape, lambda i: (0,) * len(shape))
    return pl.pallas_call(
        functools.partial(_mix_route_kernel, seq_len),
        out_shape=(
            jax.ShapeDtypeStruct((t, D_MODEL), F32),
            jax.ShapeDtypeStruct((t, D_MODEL), BF16),
            jax.ShapeDtypeStruct((t // tm, META_ROWS, tm), F32),
            jax.ShapeDtypeStruct((t // tm, N_EXPERTS, 1), F32),
        ),
        grid=(t // tm,),
        in_specs=[
            pl.BlockSpec((tm, D_MODEL), lambda i: (i, 0)),
            pl.BlockSpec((POOL_HALO, D_MODEL),
                         lambda i: (jnp.maximum(i * halo_per_tile - 1, 0), 0)),
            pl.BlockSpec((tm, D_MODEL), lambda i: (i, COL_GATE_A)),
            pl.BlockSpec((tm, D_MODEL), lambda i: (i, COL_GATE_B)),
            pl.BlockSpec((tm, RET_V_W), lambda i: (i, 0)),
            pl.BlockSpec((tm, D_MODEL), lambda i: (i, 0)),
            full(len(POOL_WINDOWS), POOL_GROUP_W, POOL_GROUP_W),
            full(1, D_MODEL),
            full(RET_V_W, D_MODEL),
            full(D_MODEL, D_MODEL),
            full(1, D_MODEL),
            full(D_MODEL, LANES),
            full(D_MODEL, LANES),
            full(1, LANES),
        ],
        out_specs=(
            pl.BlockSpec((tm, D_MODEL), lambda i: (i, 0)),
            pl.BlockSpec((tm, D_MODEL), lambda i: (i, 0)),
            pl.BlockSpec((None, META_ROWS, tm), lambda i: (i, 0, 0)),
            pl.BlockSpec((None, N_EXPERTS, 1), lambda i: (i, 0, 0)),
        ),
        compiler_params=pltpu.CompilerParams(
            dimension_semantics=("arbitrary",),
            vmem_limit_bytes=VMEM_LIMIT_BYTES),
        name="mix_route",
    )(proj, proj, proj, proj, o_gated, x2, w_pool_bf, pool_scale, w_ret_o_bf,
      w_out_bf, g_ffn, wr_hi, wr_lo, b_router)


def _local_positions(meta, lstart):
    tm = meta.shape[0]
    lane_e = lax.broadcasted_iota(jnp.int32, (tm, N_EXPERTS), 1)
    out = []
    for r in range(TOP_K):
        e_r = meta[:, META_EXPERT + r:META_EXPERT + r + 1].astype(jnp.int32)
        first = jnp.sum(jnp.where(lane_e == e_r, lstart, 0.0), axis=-1, keepdims=True)
        out.append((meta[:, META_RANK + r:META_RANK + r + 1] + first).astype(jnp.int32))
    return out


def _local_rows(tm):
    return TOP_K * tm + FILLER_ROWS


def _for_each_strip(count, max_rows, start_copy):
    piece = 1 << (max(max_rows, FILLER_ROWS).bit_length() - 1)
    off = jnp.int32(0)
    while piece >= STRIP_ALIGN:
        take = count & piece

        @pl.when(take != 0)
        def _(off=off, piece=piece):
            start_copy(off, piece)

        off = off + take
        piece //= 2


def _dispatch_kernel(blk_rows, base_ref, lstart_ref, cnt_ref, zflag_ref,
                     h_ref, meta_ref, lstart_col_ref, xs_ref,
                     sorted_ref, zero_ref, sems, zsem):
    i = pl.program_id(0)
    tm = h_ref.shape[0]
    nrows = _local_rows(tm)
    slot = lax.rem(i, 2)

    @pl.when(i == 0)
    def _():
        zero_ref[...] = jnp.zeros_like(zero_ref)

        def zcopy(b):
            start = pl.multiple_of(b * blk_rows, blk_rows)
            return pltpu.make_async_copy(zero_ref, xs_ref.at[pl.ds(start, blk_rows), :], zsem)

        def zissue(b, c):
            @pl.when(zflag_ref[b] > 0)
            def _():
                zcopy(b).start()
            return c

        def zwait(b, c):
            @pl.when(zflag_ref[b] > 0)
            def _():
                zcopy(b).wait()
            return c

        n_zero_blocks = xs_ref.shape[0] // blk_rows
        lax.fori_loop(0, n_zero_blocks, zissue, 0)
        lax.fori_loop(0, n_zero_blocks, zwait, 0)

    meta_t = meta_ref[...]
    sub_e = lax.broadcasted_iota(jnp.int32, (N_EXPERTS, tm), 0)
    pos = []
    for r in range(TOP_K):
        e_r = meta_t[META_EXPERT + r:META_EXPERT + r + 1, :].astype(jnp.int32)
        first = jnp.sum(jnp.where(sub_e == e_r, lstart_col_ref[...], 0.0), axis=0, keepdims=True)
        pos.append((meta_t[META_RANK + r:META_RANK + r + 1, :] + first).astype(jnp.int32))
    jr = lax.broadcasted_iota(jnp.int32, (nrows, tm), 0)
    onehot = jnp.where(jr == pos[0], 1.0, jnp.where(jr == pos[1], 1.0,
             jnp.where(jr == pos[2], 1.0, jnp.where(jr == pos[3], 1.0, 0.0))))
    srt = jnp.dot(onehot.astype(BF16), h_ref[...], preferred_element_type=F32)
    sorted_ref[slot] = srt.astype(ROW_DTYPE)

    def per_segment(e, c):
        k = i * N_SEG + e
        src0 = lstart_ref[k]
        dst0 = base_ref[k]

        def start_copy(off, rows):
            src = pl.multiple_of(src0 + off, STRIP_ALIGN)
            dst = pl.multiple_of(dst0 + off, STRIP_ALIGN)
            pltpu.make_async_copy(sorted_ref.at[slot, pl.ds(src, rows), :],
                                  xs_ref.at[pl.ds(dst, rows), :],
                                  sems.at[slot]).start()

        _for_each_strip(cnt_ref[k], tm, start_copy)
        return c

    lax.fori_loop(0, N_SEG, per_segment, 0)

    def wait_tile(s):
        pltpu.make_async_copy(sorted_ref.at[s], xs_ref.at[pl.ds(0, nrows), :],
                              sems.at[s]).wait()

    @pl.when(i >= 1)
    def _():
        wait_tile(1 - slot)

    @pl.when(i == pl.num_programs(0) - 1)
    def _():
        wait_tile(slot)


def _dispatch(h2, meta, tables, n_buf, blk_rows, tm):
    t = h2.shape[0]
    grid_spec = pltpu.PrefetchScalarGridSpec(
        num_scalar_prefetch=4,
        grid=(t // tm,),
        in_specs=[
            pl.BlockSpec((tm, D_MODEL), lambda i, *_: (i, 0)),
            pl.BlockSpec((None, META_ROWS, tm), lambda i, *_: (i, 0, 0)),
            pl.BlockSpec((None, N_EXPERTS, 1), lambda i, *_: (i, 0, 0)),
        ],
        out_specs=pl.BlockSpec(memory_space=pl.ANY),
        scratch_shapes=[pltpu.VMEM((2, _local_rows(tm), D_MODEL), ROW_DTYPE),
                        pltpu.VMEM((blk_rows, D_MODEL), ROW_DTYPE),
                        pltpu.SemaphoreType.DMA((2,)),
                        pltpu.SemaphoreType.DMA(())],
    )
    return pl.pallas_call(
        functools.partial(_dispatch_kernel, blk_rows),
        out_shape=jax.ShapeDtypeStruct((n_buf + 2 * FILLER_ROWS, D_MODEL), ROW_DTYPE),
        grid_spec=grid_spec,
        compiler_params=pltpu.CompilerParams(
            dimension_semantics=("arbitrary",),
            vmem_limit_bytes=VMEM_LIMIT_BYTES,
            has_side_effects=True),
        name="dispatch",
    )(tables["base_out"], tables["lstart"], tables["cnt"], tables["zflag"],
      h2, meta, tables["lstart_cols"])


def _experts_kernel(be_ref, nu_ref, x_ref, wgu_ref, bgu_ref, wd_ref, bd_ref,
                    y_ref, wgu_bf, wd_bf):
    i = pl.program_id(0)

    @pl.when(i < nu_ref[0])
    def _():
        prev = be_ref[jnp.maximum(i - 1, 0)]

        @pl.when((i == 0) | (be_ref[i] != prev))
        def _():
            wgu_bf[...] = wgu_ref[...].astype(BF16)
            wd_bf[...] = wd_ref[...].astype(BF16)

        gu = jnp.dot(x_ref[...].astype(BF16), wgu_bf[...],
                     preferred_element_type=F32) + bgu_ref[...]
        glu = jnp.minimum(gu[:, :D_FF], SWIGLU_LIMIT)
        lin = jnp.clip(gu[:, D_FF:], -SWIGLU_LIMIT, SWIGLU_LIMIT)
        act = glu * jax.nn.sigmoid(SWIGLU_ALPHA * glu) * (lin + 1.0)
        y_ref[...] = (jnp.dot(act.astype(BF16), wd_bf[...],
                              preferred_element_type=F32) + bd_ref[...]).astype(ROW_DTYPE)

    @pl.when(i >= nu_ref[0])
    def _():
        y_ref[...] = jnp.zeros_like(y_ref)


def _experts(xs, n_buf, block_e, n_used, w_gate_up, b_gate_up, w_down, b_down, blk_rows):
    used = lambda i, be, nu: jnp.minimum(i, nu[0] - 1)
    grid_spec = pltpu.PrefetchScalarGridSpec(
        num_scalar_prefetch=2,
        grid=(n_buf // blk_rows,),
        in_specs=[
            pl.BlockSpec((blk_rows, D_MODEL), lambda i, be, nu: (used(i, be, nu), 0)),
            pl.BlockSpec((None, D_MODEL, 2 * D_FF), lambda i, be, nu: (be[i], 0, 0)),
            pl.BlockSpec((None, 1, 2 * D_FF), lambda i, be, nu: (be[i], 0, 0)),
            pl.BlockSpec((None, D_FF, D_MODEL), lambda i, be, nu: (be[i], 0, 0)),
            pl.BlockSpec((None, 1, D_MODEL), lambda i, be, nu: (be[i], 0, 0)),
        ],
        out_specs=pl.BlockSpec((blk_rows, D_MODEL), lambda i, be, nu: (i, 0)),
        scratch_shapes=[pltpu.VMEM((D_MODEL, 2 * D_FF), BF16),
                        pltpu.VMEM((D_FF, D_MODEL), BF16)],
    )
    return pl.pallas_call(
        _experts_kernel,
        out_shape=jax.ShapeDtypeStruct((n_buf, D_MODEL), ROW_DTYPE),
        grid_spec=grid_spec,
        compiler_params=pltpu.CompilerParams(
            dimension_semantics=("arbitrary",),
            vmem_limit_bytes=VMEM_LIMIT_BYTES),
        name="experts",
    )(block_e, n_used, xs, w_gate_up, b_gate_up[:, None, :], w_down, b_down[:, None, :])


def _combine_kernel(base_ref, lstart_ref, cnt_ref, meta_ref, lstart_row_ref, x1_ref,
                    p_ref, ys_ref, gple_ref, wpg_ref, wple_ref, gfin_ref, out_ref,
                    rows_ref, sems):
    i = pl.program_id(0)
    n = pl.num_programs(0)
    tm = x1_ref.shape[0]
    nrows = _local_rows(tm)
    slot = lax.rem(i, 2)

    def fetch_tile(tile, s):
        def per_segment(e, c):
            k = tile * N_SEG + e
            src0 = base_ref[k]
            dst0 = lstart_ref[k]

            def start_copy(off, rows):
                src = pl.multiple_of(src0 + off, STRIP_ALIGN)
                dst = pl.multiple_of(dst0 + off, STRIP_ALIGN)
                pltpu.make_async_copy(ys_ref.at[pl.ds(src, rows), :],
                                      rows_ref.at[s, pl.ds(dst, rows), :],
                                      sems.at[s]).start()

            _for_each_strip(cnt_ref[k], tm, start_copy)
            return c

        lax.fori_loop(0, N_SEG, per_segment, 0)

    @pl.when(i == 0)
    def _():
        fetch_tile(0, 0)

    @pl.when(i + 1 < n)
    def _():
        fetch_tile(i + 1, 1 - slot)

    meta = jnp.concatenate(
        [meta_ref[...], jnp.zeros((LANES - META_ROWS, tm), F32)], axis=0).T
    pos = _local_positions(meta, lstart_row_ref[...])
    jl = lax.broadcasted_iota(jnp.int32, (tm, nrows), 1)
    gate = [meta[:, META_GATE + r:META_GATE + r + 1] for r in range(TOP_K)]
    weights = jnp.where(jl == pos[0], gate[0], jnp.where(jl == pos[1], gate[1],
              jnp.where(jl == pos[2], gate[2], jnp.where(jl == pos[3], gate[3], 0.0))))

    pltpu.make_async_copy(ys_ref.at[pl.ds(0, nrows), :], rows_ref.at[slot],
                          sems.at[slot]).wait()
    moe = jnp.dot(weights.astype(BF16), rows_ref[slot].astype(BF16),
                  preferred_element_type=F32)
    x2 = x1_ref[...] + moe
    u = _rms(x2, gple_ref[...])
    gate_ple = jax.nn.sigmoid(jnp.dot(u.astype(BF16), wpg_ref[...],
                                      preferred_element_type=F32))
    emb = jnp.dot(p_ref[...].astype(BF16), wple_ref[...], preferred_element_type=F32)
    x3 = x2 + gate_ple * emb
    out_ref[...] = _rms(x3, gfin_ref[...])


def _combine(meta, tables, x1, p2, ys, g_ple, w_ple_gate_bf, w_ple_bf, g_final, tm):
    t = x1.shape[0]
    full = lambda *shape: pl.BlockSpec(shape, lambda i, *_: (0,) * len(shape))
    grid_spec = pltpu.PrefetchScalarGridSpec(
        num_scalar_prefetch=3,
        grid=(t // tm,),
        in_specs=[
            pl.BlockSpec((None, META_ROWS, tm), lambda i, *_: (i, 0, 0)),
            pl.BlockSpec((None, 1, N_EXPERTS), lambda i, *_: (i, 0, 0)),
            pl.BlockSpec((tm, D_MODEL), lambda i, *_: (i, 0)),
            pl.BlockSpec((tm, PLE_DIM), lambda i, *_: (i, 0)),
            pl.BlockSpec(memory_space=pl.ANY),
            full(1, D_MODEL),
            full(D_MODEL, D_MODEL),
            full(PLE_DIM, D_MODEL),
            full(1, D_MODEL),
        ],
        out_specs=pl.BlockSpec((tm, D_MODEL), lambda i, *_: (i, 0)),
        scratch_shapes=[pltpu.VMEM((2, _local_rows(tm), D_MODEL), ROW_DTYPE),
                        pltpu.SemaphoreType.DMA((2,))],
    )
    return pl.pallas_call(
        _combine_kernel,
        out_shape=jax.ShapeDtypeStruct((t, D_MODEL), F32),
        grid_spec=grid_spec,
        compiler_params=pltpu.CompilerParams(
            dimension_semantics=("arbitrary",),
            vmem_limit_bytes=VMEM_LIMIT_BYTES),
        name="combine",
    )(tables["base_in"], tables["lstart"], tables["cnt"], meta, tables["lstart_rows"],
      x1, p2, ys, g_ple, w_ple_gate_bf, w_ple_bf, g_final)


def _tiles(seq_len):
    return dict(
        proj_rows=min(512, seq_len),
        ret_block=min(256, seq_len),
        route_rows=min(512, seq_len),
        expert_rows=512,
    )


def _slot_tables(tile_counts, blk, n_buf, nloc):
    cnt = tile_counts[:, :, 0].astype(jnp.int32)
    cnt = ((cnt + STRIP_ALIGN - 1) // STRIP_ALIGN) * STRIP_ALIGN
    seg_rows = jnp.sum(cnt, axis=1)
    counts = jnp.sum(cnt, axis=0)
    padded = ((counts + blk - 1) // blk) * blk
    ends_pad = jnp.cumsum(padded)
    starts_pad = ends_pad - padded
    tile_before = jnp.cumsum(cnt, axis=0) - cnt
    lstart = jnp.cumsum(cnt, axis=1) - cnt
    base = starts_pad[None, :] + tile_before
    with_filler = lambda a, col: jnp.concatenate([a, col[:, None]], axis=1).reshape(-1)
    blk_start = jnp.arange((n_buf + 2 * FILLER_ROWS) // blk, dtype=jnp.int32) * blk
    last_of_expert = jnp.any((padded > 0)[None, :]
                             & (blk_start[:, None] == (ends_pad - blk)[None, :]), axis=1)
    zflag = (last_of_expert | (blk_start >= ends_pad[-1])).astype(jnp.int32)
    filler_dst = n_buf + (jnp.arange(cnt.shape[0], dtype=jnp.int32) % 2) * FILLER_ROWS
    return dict(
        cnt=with_filler(cnt, nloc - seg_rows),
        lstart=with_filler(lstart, seg_rows),
        base_out=with_filler(base, filler_dst),
        base_in=with_filler(base, jnp.zeros_like(seg_rows)),
        lstart_rows=lstart.astype(F32)[:, None, :],
        lstart_cols=lstart.astype(F32)[:, :, None],
        zflag=zflag, ends_pad=ends_pad)


def _layer(x2, p2, seq_len, g_mix, w_in, w_pool, pool_scale, w_ret_o, w_out, g_ffn,
           w_router, b_router, w_gate_up, b_gate_up, w_down, b_down, g_ple,
           w_ple_gate, w_ple, g_out):
    t = x2.shape[0]
    cfg = _tiles(seq_len)
    row = lambda a: a.reshape(1, -1)

    proj = _in_proj(x2, row(g_mix), w_in.astype(BF16), seq_len, cfg["proj_rows"])
    o_gated = _retention(proj.reshape(t // seq_len, seq_len, IN_W), cfg["ret_block"])
    o_gated = o_gated.reshape(t, RET_V_W)

    lane_pad = ((0, 0), (0, LANES - N_EXPERTS))
    wr_hi = w_router.astype(BF16)
    wr_lo = jnp.pad((w_router - wr_hi.astype(F32)).astype(BF16), lane_pad)
    wr_hi = jnp.pad(wr_hi, lane_pad)
    tm = cfg["route_rows"]
    x1, h2, meta, tile_counts = _mix_route(
        proj, o_gated, x2, w_pool.astype(BF16), row(pool_scale), w_ret_o.astype(BF16),
        w_out.astype(BF16), row(g_ffn), wr_hi, wr_lo, jnp.pad(row(b_router), lane_pad), seq_len, tm)

    blk = cfg["expert_rows"]
    n_buf = t * TOP_K + (t // tm) * FILLER_ROWS + N_EXPERTS * blk
    n_blocks = n_buf // blk
    tables = _slot_tables(tile_counts, blk, n_buf, _local_rows(tm))
    blk_start = jnp.arange(n_blocks, dtype=jnp.int32) * blk
    block_e = jnp.minimum(
        jnp.sum((tables["ends_pad"][None, :] <= blk_start[:, None]).astype(jnp.int32), axis=1),
        N_EXPERTS - 1)
    n_used = tables["ends_pad"][-1:] // blk

    xs = _dispatch(h2, meta, tables, n_buf, blk, tm)
    ys = _experts(xs, n_buf, block_e, n_used, w_gate_up, b_gate_up, w_down, b_down, blk)
    return _combine(meta, tables, x1, p2, ys, row(g_ple), w_ple_gate.astype(BF16),
                    w_ple.astype(BF16), row(g_out), tm)


def kernel(x, p, g_mix, w_in, w_pool, pool_scale, w_ret_o, w_out, g_ffn, w_router,
           b_router, w_gate_up, b_gate_up, w_down, b_down, g_ple, w_ple_gate, w_ple,
           g_final):
    b, s, d = x.shape
    depth = p.shape[0]
    assert depth == 1 and d == D_MODEL
    x2 = x.reshape(b * s, d)
    out = _layer(x2, p[0].reshape(b * s, PLE_DIM), s, g_mix[0], w_in[0], w_pool[0],
                 pool_scale[0], w_ret_o[0], w_out[0], g_ffn[0], w_router[0],
                 b_router[0], w_gate_up[0], b_gate_up[0], w_down[0], b_down[0],
                 g_ple[0], w_ple_gate[0], w_ple[0], g_final)
    return out.reshape(b, s, d)
```

```python
import functools

import numpy as np
import jax
import jax.numpy as jnp
from jax import lax
from jax.experimental import pallas as pl
from jax.experimental.pallas import tpu as pltpu

F32 = jnp.float32
BF16 = jnp.bfloat16

D_MODEL = 1024
EPS = 1e-6
CHUNK = 64
PLE_DIM = 256
POOL_WINDOWS = (2, 4, 8, 16)
POOL_GROUP_W = D_MODEL // len(POOL_WINDOWS)
POOL_HALO = 16
RET_HEADS = 4
RET_QK_HEAD = 256
RET_V_HEAD = 512
RET_V_W = RET_HEADS * RET_V_HEAD
ROPE_BASE = 10000.0
IN_W = 9 * D_MODEL
N_EXPERTS = 32
TOP_K = 4
D_FF = D_MODEL
SWIGLU_ALPHA = 1.702
SWIGLU_LIMIT = 7.0

VMEM_LIMIT_BYTES = 56 * 1024 * 1024
LANES = 128
META_ROWS = 16
META_EXPERT, META_RANK, META_GATE = 0, TOP_K, 2 * TOP_K
ROW_DTYPE = F32
SUBLANES = 8
STRIP_ALIGN = SUBLANES * 4 // jnp.dtype(ROW_DTYPE).itemsize
FILLER_ROWS = N_EXPERTS * STRIP_ALIGN
N_SEG = N_EXPERTS + 1


def _rms(x, g):
    return x * lax.rsqrt(jnp.mean(x * x, axis=-1, keepdims=True) + EPS) * g


COL_Q, COL_K = 1, 2
COL_V, COL_G = (3, 4), (5, 6)
COL_GATE_A, COL_GATE_B = 7, 8


def _rotary_heads(y, cos, sin):
    half = RET_QK_HEAD // 2
    parts = []
    for h in range(RET_HEADS):
        x1 = y[:, h * RET_QK_HEAD:h * RET_QK_HEAD + half]
        x2 = y[:, h * RET_QK_HEAD + half:(h + 1) * RET_QK_HEAD]
        parts += [x1 * cos - x2 * sin, x2 * cos + x1 * sin]
    return jnp.concatenate(parts, axis=-1)


def _in_proj_kernel(x_ref, g_ref, w_ref, cos_ref, sin_ref, o_ref):
    h = _rms(x_ref[...], g_ref[...]).astype(BF16)
    for c in range(IN_W // D_MODEL):
        cols = slice(c * D_MODEL, (c + 1) * D_MODEL)
        y = jnp.dot(h, w_ref[:, cols], preferred_element_type=F32)
        if c == COL_Q:
            y = _rotary_heads(y, cos_ref[...], sin_ref[...])
        elif c == COL_K:
            y = _rotary_heads(y, cos_ref[...], sin_ref[...]) * (RET_QK_HEAD ** -0.5)
        elif c in COL_G:
            y = y * jax.nn.sigmoid(y)
        o_ref[:, cols] = y.astype(BF16)


def _rotary_tables(s):
    half = RET_QK_HEAD // 2
    pos = jnp.arange(s, dtype=F32)
    inv = ROPE_BASE ** (-jnp.linspace(0.0, 1.0, half, dtype=F32))
    ang = pos[:, None] * inv[None, :]
    return jnp.cos(ang), jnp.sin(ang)


def _in_proj(x2, g_mix, w_in_bf, seq_len, tm):
    t = x2.shape[0]
    half = RET_QK_HEAD // 2
    cos, sin = _rotary_tables(seq_len)
    tiles_per_seq = seq_len // tm
    return pl.pallas_call(
        _in_proj_kernel,
        out_shape=jax.ShapeDtypeStruct((t, IN_W), BF16),
        grid=(t // tm,),
        in_specs=[
            pl.BlockSpec((tm, D_MODEL), lambda i: (i, 0)),
            pl.BlockSpec((1, D_MODEL), lambda i: (0, 0)),
            pl.BlockSpec((D_MODEL, IN_W), lambda i: (0, 0), pipeline_mode=pl.Buffered(1)),
            pl.BlockSpec((tm, half), lambda i: (lax.rem(i, tiles_per_seq), 0)),
            pl.BlockSpec((tm, half), lambda i: (lax.rem(i, tiles_per_seq), 0)),
        ],
        out_specs=pl.BlockSpec((tm, IN_W), lambda i: (i, 0)),
        compiler_params=pltpu.CompilerParams(
            dimension_semantics=("arbitrary",),
            vmem_limit_bytes=VMEM_LIMIT_BYTES),
        name="in_proj",
    )(x2, g_mix, w_in_bf, cos, sin)


def _head_decay_logs():
    return [float(np.log(1.0 - 2.0 ** (-5.0 - h))) for h in range(RET_HEADS)]


def _retention_kernel(blk_decay, q_ref, k_ref, va_ref, vb_ref, ga_ref, gb_ref,
                      d_ref, qd_ref, kd_ref, o_ref, state_ref):
    @pl.when(pl.program_id(1) == 0)
    def _():
        state_ref[...] = jnp.zeros_like(state_ref)

    heads_per_ref = D_MODEL // RET_V_HEAD
    for h in range(RET_HEADS):
        qk_cols = slice(h * RET_QK_HEAD, (h + 1) * RET_QK_HEAD)
        v_cols = slice((h % heads_per_ref) * RET_V_HEAD, (h % heads_per_ref + 1) * RET_V_HEAD)
        q = q_ref[:, qk_cols]
        k = k_ref[:, qk_cols]
        v = (va_ref if h < heads_per_ref else vb_ref)[:, v_cols]
        g = (ga_ref if h < heads_per_ref else gb_ref)[:, v_cols]
        scores = lax.dot_general(q, k, (((1,), (1,)), ((), ())),
                                 preferred_element_type=F32) * d_ref[h]
        state = state_ref[h]
        o = jnp.dot(scores.astype(BF16), v, preferred_element_type=F32)
        o = o + jnp.dot((q.astype(F32) * qd_ref[h]).astype(BF16), state.astype(BF16),
                        preferred_element_type=F32)
        k_dec = (k.astype(F32) * kd_ref[h]).astype(BF16)
        state_ref[h] = state * blk_decay[h] + lax.dot_general(
            k_dec, v, (((0,), (0,)), ((), ())), preferred_element_type=F32)
        o = o * lax.rsqrt(jnp.mean(o * o, axis=-1, keepdims=True) + EPS)
        o_ref[:, h * RET_V_HEAD:(h + 1) * RET_V_HEAD] = (o * g.astype(F32)).astype(BF16)


def _retention_tables(blk):
    log_g = jnp.asarray(_head_decay_logs(), F32)
    idx = jnp.arange(blk, dtype=F32)
    diff = idx[:, None] - idx[None, :]
    chunk = jnp.arange(blk, dtype=jnp.int32) // CHUNK
    visible = chunk[None, :] <= chunk[:, None]
    dmask = jnp.where(visible[None], jnp.exp(log_g[:, None, None] * jnp.abs(diff)[None]), 0.0)
    q_dec = jnp.exp(log_g[:, None] * (idx + 1.0))[:, :, None]
    k_dec = jnp.exp(log_g[:, None] * (blk - 1.0 - idx))[:, :, None]
    q_dec = jnp.broadcast_to(q_dec, (RET_HEADS, blk, RET_QK_HEAD))
    k_dec = jnp.broadcast_to(k_dec, (RET_HEADS, blk, RET_QK_HEAD))
    return dmask.astype(F32), q_dec, k_dec


def _retention(proj3, blk):
    b, s, _ = proj3.shape
    dmask, q_dec, k_dec = _retention_tables(blk)
    blk_decay = [float(np.exp(lg * blk)) for lg in _head_decay_logs()]
    chunk = lambda c: pl.BlockSpec((None, blk, D_MODEL), lambda bi, l: (bi, l, c))
    table = lambda *shape: pl.BlockSpec(shape, lambda bi, l: (0,) * len(shape))
    return pl.pallas_call(
        functools.partial(_retention_kernel, blk_decay),
        out_shape=jax.ShapeDtypeStruct((b, s, RET_V_W), BF16),
        grid=(b, s // blk),
        in_specs=[
            chunk(COL_Q), chunk(COL_K), chunk(COL_V[0]), chunk(COL_V[1]),
            chunk(COL_G[0]), chunk(COL_G[1]),
            table(RET_HEADS, blk, blk),
            table(RET_HEADS, blk, RET_QK_HEAD),
            table(RET_HEADS, blk, RET_QK_HEAD),
        ],
        out_specs=pl.BlockSpec((None, blk, RET_V_W), lambda bi, l: (bi, l, 0)),
        scratch_shapes=[pltpu.VMEM((RET_HEADS, RET_QK_HEAD, RET_V_HEAD), F32)],
        compiler_params=pltpu.CompilerParams(
            dimension_semantics=("arbitrary", "arbitrary"),
            vmem_limit_bytes=VMEM_LIMIT_BYTES),
        name="retention",
    )(proj3, proj3, proj3, proj3, proj3, proj3, dmask, q_dec, k_dec)


def _window_sum(ext, w, tm):
    cur = ext
    span = 1
    while span < w:
        cur = cur[span:, :] + cur[:-span, :]
        span *= 2
    start = POOL_HALO + 1 - w
    return cur[start:start + tm, :]


def _mix_route_kernel(seq_len, u_ref, halo_ref, ga_ref, gb_ref, o_ref, x_ref,
                      wpool_ref, pscale_ref, wreto_ref, wout_ref, gffn_ref,
                      wr_hi_ref, wr_lo_ref, br_ref,
                      x1_ref, h2_ref, meta_ref, counts_ref):
    i = pl.program_id(0)
    tm = x_ref.shape[0]
    pos0 = lax.rem(i * tm, seq_len)

    u = u_ref[...].astype(F32)
    halo = jnp.where(pos0 == 0, 0.0, halo_ref[...].astype(F32))
    ext = jnp.concatenate([halo, u], axis=0)
    pos = (pos0 + lax.broadcasted_iota(jnp.int32, (tm, 1), 0)).astype(F32)
    pooled_out = []
    for g, w in enumerate(POOL_WINDOWS):
        cols = slice(g * POOL_GROUP_W, (g + 1) * POOL_GROUP_W)
        ws = _window_sum(ext[:, cols], w, tm)
        count = jnp.minimum(pos + 1.0, float(w))
        pooled = ws / count - u[:, cols]
        pooled_out.append(jnp.dot(pooled.astype(BF16), wpool_ref[g],
                                  preferred_element_type=F32))
    y_pool = jnp.concatenate(pooled_out, axis=-1) * pscale_ref[...]

    y_ret = jnp.dot(o_ref[...], wreto_ref[...], preferred_element_type=F32)
    merged = (jax.nn.sigmoid(ga_ref[...].astype(F32)) * y_pool
              + jax.nn.sigmoid(gb_ref[...].astype(F32)) * y_ret)
    x1 = x_ref[...] + jnp.dot(merged.astype(BF16), wout_ref[...],
                              preferred_element_type=F32)
    x1_ref[...] = x1
    h2 = _rms(x1, gffn_ref[...])
    h2_ref[...] = h2.astype(BF16)

    h_hi = h2.astype(BF16)
    h_lo = (h2 - h_hi.astype(F32)).astype(BF16)
    logits = (jnp.dot(h_hi, wr_hi_ref[...], preferred_element_type=F32)
              + jnp.dot(h_lo, wr_hi_ref[...], preferred_element_type=F32)
              + jnp.dot(h_hi, wr_lo_ref[...], preferred_element_type=F32)
              + br_ref[...])
    logits_t = logits.T[:N_EXPERTS, :]

    sub = lax.broadcasted_iota(jnp.int32, (N_EXPERTS, tm), 0)
    work = logits_t
    vals, idxs, hots = [], [], []
    for _ in range(TOP_K):
        m = jnp.max(work, axis=0, keepdims=True)
        idx = jnp.min(jnp.where(work == m, sub, N_EXPERTS), axis=0, keepdims=True)
        hot = sub == idx
        vals.append(m)
        idxs.append(idx)
        hots.append(hot)
        work = jnp.where(hot, -jnp.inf, work)
    exps = [jnp.exp(v - vals[0]) for v in vals]
    denom = exps[0] + exps[1] + exps[2] + exps[3]
    gates = [e / denom for e in exps]

    sel = (jnp.where(hots[0], 1.0, 0.0) + jnp.where(hots[1], 1.0, 0.0)
           + jnp.where(hots[2], 1.0, 0.0) + jnp.where(hots[3], 1.0, 0.0))
    row = lax.broadcasted_iota(jnp.int32, (tm, tm), 0)
    col = lax.broadcasted_iota(jnp.int32, (tm, tm), 1)
    earlier = jnp.where(row < col, 1.0, 0.0).astype(BF16)
    before = jnp.dot(sel.astype(BF16), earlier, preferred_element_type=F32)
    counts_ref[...] = jnp.sum(sel, axis=1, keepdims=True)

    msub = lax.broadcasted_iota(jnp.int32, (META_ROWS, tm), 0)
    meta = jnp.zeros((META_ROWS, tm), F32)
    for r in range(TOP_K):
        rank_r = jnp.sum(jnp.where(hots[r], before, 0.0), axis=0, keepdims=True)
        meta = jnp.where(msub == META_EXPERT + r, idxs[r].astype(F32), meta)
        meta = jnp.where(msub == META_RANK + r, rank_r, meta)
        meta = jnp.where(msub == META_GATE + r, gates[r], meta)
    meta_ref[...] = meta


def _mix_route(proj, o_gated, x2, w_pool_bf, pool_scale, w_ret_o_bf, w_out_bf,
               g_ffn, wr_hi, wr_lo, b_router, seq_len, tm):
    t = x2.shape[0]
    halo_per_tile = tm // POOL_HALO
    full = lambda *shape: pl.BlockSpec(shape, lambda i: (0,) * len(shape))
    return pl.pallas_call(
        functools.partial(_mix_route_kernel, seq_len),
        out_shape=(
            jax.ShapeDtypeStruct((t, D_MODEL), F32),
            jax.ShapeDtypeStruct((t, D_MODEL), BF16),
            jax.ShapeDtypeStruct((t // tm, META_ROWS, tm), F32),
            jax.ShapeDtypeStruct((t // tm, N_EXPERTS, 1), F32),
        ),
        grid=(t // tm,),
        in_specs=[
            pl.BlockSpec((tm, D_MODEL), lambda i: (i, 0)),
            pl.BlockSpec((POOL_HALO, D_MODEL),
                         lambda i: (jnp.maximum(i * halo_per_tile - 1, 0), 0)),
            pl.BlockSpec((tm, D_MODEL), lambda i: (i, COL_GATE_A)),
            pl.BlockSpec((tm, D_MODEL), lambda i: (i, COL_GATE_B)),
            pl.BlockSpec((tm, RET_V_W), lambda i: (i, 0)),
            pl.BlockSpec((tm, D_MODEL), lambda i: (i, 0)),
            full(len(POOL_WINDOWS), POOL_GROUP_W, POOL_GROUP_W),
            full(1, D_MODEL),
            full(RET_V_W, D_MODEL),
            full(D_MODEL, D_MODEL),
            full(1, D_MODEL),
            full(D_MODEL, LANES),
            full(D_MODEL, LANES),
            full(1, LANES),
        ],
        out_specs=(
            pl.BlockSpec((tm, D_MODEL), lambda i: (i, 0)),
            pl.BlockSpec((tm, D_MODEL), lambda i: (i, 0)),
            pl.BlockSpec((None, META_ROWS, tm), lambda i: (i, 0, 0)),
            pl.BlockSpec((None, N_EXPERTS, 1), lambda i: (i, 0, 0)),
        ),
        compiler_params=pltpu.CompilerParams(
            dimension_semantics=("arbitrary",),
            vmem_limit_bytes=VMEM_LIMIT_BYTES),
        name="mix_route",
    )(proj, proj, proj, proj, o_gated, x2, w_pool_bf, pool_scale, w_ret_o_bf,
      w_out_bf, g_ffn, wr_hi, wr_lo, b_router)


def _local_positions(meta, lstart):
    tm = meta.shape[0]
    lane_e = lax.broadcasted_iota(jnp.int32, (tm, N_EXPERTS), 1)
    out = []
    for r in range(TOP_K):
        e_r = meta[:, META_EXPERT + r:META_EXPERT + r + 1].astype(jnp.int32)
        first = jnp.sum(jnp.where(lane_e == e_r, lstart, 0.0), axis=-1, keepdims=True)
        out.append((meta[:, META_RANK + r:META_RANK + r + 1] + first).astype(jnp.int32))
    return out


def _local_rows(tm):
    return TOP_K * tm + FILLER_ROWS


def _for_each_strip(count, max_rows, start_copy):
    top = 1 << (max(max_rows, FILLER_ROWS).bit_length() - 1)
    pieces = [top >> k for k in range(top.bit_length()) if (top >> k) >= STRIP_ALIGN]
    rare_from = 2 * max_rows * TOP_K // N_EXPERTS
    rare = [p for p in pieces if p >= rare_from]
    rare_bits = sum(rare)

    def cover(sizes, off):
        for piece in sizes:
            take = count & piece

            @pl.when(take != 0)
            def _(off=off, piece=piece):
                start_copy(off, piece)

            off = off + take

    @pl.when((count & rare_bits) != 0)
    def _():
        cover(rare, jnp.int32(0))

    cover([p for p in pieces if p < rare_from], count & rare_bits)


def _dispatch_kernel(blk_rows, base_ref, lstart_ref, cnt_ref, zflag_ref,
                     h_ref, meta_ref, lstart_col_ref, xs_ref,
                     sorted_ref, zero_ref, sems, zsem):
    i = pl.program_id(0)
    tm = h_ref.shape[0]
    nrows = _local_rows(tm)
    slot = lax.rem(i, 2)

    @pl.when(i == 0)
    def _():
        zero_ref[...] = jnp.zeros_like(zero_ref)

        def zcopy(b):
            start = pl.multiple_of(b * blk_rows, blk_rows)
            return pltpu.make_async_copy(zero_ref, xs_ref.at[pl.ds(start, blk_rows), :], zsem)

        def zissue(b, c):
            @pl.when(zflag_ref[b] > 0)
            def _():
                zcopy(b).start()
            return c

        def zwait(b, c):
            @pl.when(zflag_ref[b] > 0)
            def _():
                zcopy(b).wait()
            return c

        n_zero_blocks = xs_ref.shape[0] // blk_rows
        lax.fori_loop(0, n_zero_blocks, zissue, 0)
        lax.fori_loop(0, n_zero_blocks, zwait, 0)

    meta_t = meta_ref[...]
    sub_e = lax.broadcasted_iota(jnp.int32, (N_EXPERTS, tm), 0)
    pos = []
    for r in range(TOP_K):
        e_r = meta_t[META_EXPERT + r:META_EXPERT + r + 1, :].astype(jnp.int32)
        first = jnp.sum(jnp.where(sub_e == e_r, lstart_col_ref[...], 0.0), axis=0, keepdims=True)
        pos.append((meta_t[META_RANK + r:META_RANK + r + 1, :] + first).astype(jnp.int32))
    jr = lax.broadcasted_iota(jnp.int32, (nrows, tm), 0)
    onehot = jnp.where(jr == pos[0], 1.0, jnp.where(jr == pos[1], 1.0,
             jnp.where(jr == pos[2], 1.0, jnp.where(jr == pos[3], 1.0, 0.0))))
    srt = jnp.dot(onehot.astype(BF16), h_ref[...], preferred_element_type=F32)
    sorted_ref[slot] = srt.astype(ROW_DTYPE)

    def per_segment(e, c):
        k = i * N_SEG + e
        src0 = lstart_ref[k]
        dst0 = base_ref[k]

        def start_copy(off, rows):
            src = pl.multiple_of(src0 + off, STRIP_ALIGN)
            dst = pl.multiple_of(dst0 + off, STRIP_ALIGN)
            pltpu.make_async_copy(sorted_ref.at[slot, pl.ds(src, rows), :],
                                  xs_ref.at[pl.ds(dst, rows), :],
                                  sems.at[slot]).start()

        _for_each_strip(cnt_ref[k], tm, start_copy)
        return c

    lax.fori_loop(0, N_SEG, per_segment, 0)

    def wait_tile(s):
        pltpu.make_async_copy(sorted_ref.at[s], xs_ref.at[pl.ds(0, nrows), :],
                              sems.at[s]).wait()

    @pl.when(i >= 1)
    def _():
        wait_tile(1 - slot)

    @pl.when(i == pl.num_programs(0) - 1)
    def _():
        wait_tile(slot)


def _dispatch(h2, meta, tables, n_buf, blk_rows, tm):
    t = h2.shape[0]
    grid_spec = pltpu.PrefetchScalarGridSpec(
        num_scalar_prefetch=4,
        grid=(t // tm,),
        in_specs=[
            pl.BlockSpec((tm, D_MODEL), lambda i, *_: (i, 0)),
            pl.BlockSpec((None, META_ROWS, tm), lambda i, *_: (i, 0, 0)),
            pl.BlockSpec((None, N_EXPERTS, 1), lambda i, *_: (i, 0, 0)),
        ],
        out_specs=pl.BlockSpec(memory_space=pl.ANY),
        scratch_shapes=[pltpu.VMEM((2, _local_rows(tm), D_MODEL), ROW_DTYPE),
                        pltpu.VMEM((blk_rows, D_MODEL), ROW_DTYPE),
                        pltpu.SemaphoreType.DMA((2,)),
                        pltpu.SemaphoreType.DMA(())],
    )
    return pl.pallas_call(
        functools.partial(_dispatch_kernel, blk_rows),
        out_shape=jax.ShapeDtypeStruct((n_buf + 2 * FILLER_ROWS, D_MODEL), ROW_DTYPE),
        grid_spec=grid_spec,
        compiler_params=pltpu.CompilerParams(
            dimension_semantics=("arbitrary",),
            vmem_limit_bytes=VMEM_LIMIT_BYTES,
            has_side_effects=True),
        name="dispatch",
    )(tables["base_out"], tables["lstart"], tables["cnt"], tables["zflag"],
      h2, meta, tables["lstart_cols"])


def _experts_kernel(be_ref, nu_ref, x_ref, wgu_ref, bgu_ref, wd_ref, bd_ref,
                    y_ref, wgu_bf, wd_bf):
    i = pl.program_id(0)

    @pl.when(i < nu_ref[0])
    def _():
        prev = be_ref[jnp.maximum(i - 1, 0)]

        @pl.when((i == 0) | (be_ref[i] != prev))
        def _():
            wgu_bf[...] = wgu_ref[...].astype(BF16)
            wd_bf[...] = wd_ref[...].astype(BF16)

        gu = jnp.dot(x_ref[...].astype(BF16), wgu_bf[...],
                     preferred_element_type=F32) + bgu_ref[...]
        glu = jnp.minimum(gu[:, :D_FF], SWIGLU_LIMIT)
        lin = jnp.clip(gu[:, D_FF:], -SWIGLU_LIMIT, SWIGLU_LIMIT)
        act = glu * jax.nn.sigmoid(SWIGLU_ALPHA * glu) * (lin + 1.0)
        y_ref[...] = (jnp.dot(act.astype(BF16), wd_bf[...],
                              preferred_element_type=F32) + bd_ref[...]).astype(ROW_DTYPE)

    @pl.when(i >= nu_ref[0])
    def _():
        y_ref[...] = jnp.zeros_like(y_ref)


def _experts(xs, n_buf, block_e, n_used, w_gate_up, b_gate_up, w_down, b_down, blk_rows):
    used = lambda i, be, nu: jnp.minimum(i, nu[0] - 1)
    grid_spec = pltpu.PrefetchScalarGridSpec(
        num_scalar_prefetch=2,
        grid=(n_buf // blk_rows,),
        in_specs=[
            pl.BlockSpec((blk_rows, D_MODEL), lambda i, be, nu: (used(i, be, nu), 0)),
            pl.BlockSpec((None, D_MODEL, 2 * D_FF), lambda i, be, nu: (be[i], 0, 0)),
            pl.BlockSpec((None, 1, 2 * D_FF), lambda i, be, nu: (be[i], 0, 0)),
            pl.BlockSpec((None, D_FF, D_MODEL), lambda i, be, nu: (be[i], 0, 0)),
            pl.BlockSpec((None, 1, D_MODEL), lambda i, be, nu: (be[i], 0, 0)),
        ],
        out_specs=pl.BlockSpec((blk_rows, D_MODEL), lambda i, be, nu: (i, 0)),
        scratch_shapes=[pltpu.VMEM((D_MODEL, 2 * D_FF), BF16),
                        pltpu.VMEM((D_FF, D_MODEL), BF16)],
    )
    return pl.pallas_call(
        _experts_kernel,
        out_shape=jax.ShapeDtypeStruct((n_buf, D_MODEL), ROW_DTYPE),
        grid_spec=grid_spec,
        compiler_params=pltpu.CompilerParams(
            dimension_semantics=("arbitrary",),
            vmem_limit_bytes=VMEM_LIMIT_BYTES),
        name="experts",
    )(block_e, n_used, xs, w_gate_up, b_gate_up[:, None, :], w_down, b_down[:, None, :])


def _unsort_weights(meta_t, lstart_row, nrows):
    tm = meta_t.shape[1]
    meta = jnp.concatenate(
        [meta_t, jnp.zeros((LANES - META_ROWS, tm), F32)], axis=0).T
    pos = _local_positions(meta, lstart_row)
    jl = lax.broadcasted_iota(jnp.int32, (tm, nrows), 1)
    gate = [meta[:, META_GATE + r:META_GATE + r + 1] for r in range(TOP_K)]
    weights = jnp.where(jl == pos[0], gate[0], jnp.where(jl == pos[1], gate[1],
              jnp.where(jl == pos[2], gate[2], jnp.where(jl == pos[3], gate[3], 0.0))))
    return weights.astype(BF16)


def _combine_kernel(base_ref, lstart_ref, cnt_ref, meta_ref, lstart_row_ref,
                    meta_next_ref, lstart_row_next_ref, x1_ref,
                    p_ref, ys_ref, gple_ref, wpg_ref, wple_ref, gfin_ref, out_ref,
                    rows_ref, unsort_ref, sems):
    i = pl.program_id(0)
    n = pl.num_programs(0)
    tm = x1_ref.shape[0]
    nrows = _local_rows(tm)
    slot = lax.rem(i, 2)

    def fetch_tile(tile, s):
        def per_segment(e, c):
            k = tile * N_SEG + e
            src0 = base_ref[k]
            dst0 = lstart_ref[k]

            def start_copy(off, rows):
                src = pl.multiple_of(src0 + off, STRIP_ALIGN)
                dst = pl.multiple_of(dst0 + off, STRIP_ALIGN)
                pltpu.make_async_copy(ys_ref.at[pl.ds(src, rows), :],
                                      rows_ref.at[s, pl.ds(dst, rows), :],
                                      sems.at[s]).start()

            _for_each_strip(cnt_ref[k], tm, start_copy)
            return c

        lax.fori_loop(0, N_SEG, per_segment, 0)

    @pl.when(i == 0)
    def _():
        fetch_tile(0, 0)
        unsort_ref[0] = _unsort_weights(meta_ref[...], lstart_row_ref[...], nrows)

    @pl.when(i + 1 < n)
    def _():
        fetch_tile(i + 1, 1 - slot)

    pltpu.make_async_copy(ys_ref.at[pl.ds(0, nrows), :], rows_ref.at[slot],
                          sems.at[slot]).wait()
    moe = jnp.dot(unsort_ref[slot], rows_ref[slot].astype(BF16),
                  preferred_element_type=F32)
    x2 = x1_ref[...] + moe
    u = _rms(x2, gple_ref[...])
    gate_ple = jax.nn.sigmoid(jnp.dot(u.astype(BF16), wpg_ref[...],
                                      preferred_element_type=F32))
    emb = jnp.dot(p_ref[...].astype(BF16), wple_ref[...], preferred_element_type=F32)
    x3 = x2 + gate_ple * emb
    out_ref[...] = _rms(x3, gfin_ref[...])

    unsort_ref[1 - slot] = _unsort_weights(meta_next_ref[...], lstart_row_next_ref[...], nrows)


def _combine(meta, tables, x1, p2, ys, g_ple, w_ple_gate_bf, w_ple_bf, g_final, tm):
    t = x1.shape[0]
    full = lambda *shape: pl.BlockSpec(shape, lambda i, *_: (0,) * len(shape))
    last = t // tm - 1
    nxt = lambda i: jnp.minimum(i + 1, last)
    grid_spec = pltpu.PrefetchScalarGridSpec(
        num_scalar_prefetch=3,
        grid=(t // tm,),
        in_specs=[
            pl.BlockSpec((None, META_ROWS, tm), lambda i, *_: (i, 0, 0)),
            pl.BlockSpec((None, 1, N_EXPERTS), lambda i, *_: (i, 0, 0)),
            pl.BlockSpec((None, META_ROWS, tm), lambda i, *_: (nxt(i), 0, 0)),
            pl.BlockSpec((None, 1, N_EXPERTS), lambda i, *_: (nxt(i), 0, 0)),
            pl.BlockSpec((tm, D_MODEL), lambda i, *_: (i, 0)),
            pl.BlockSpec((tm, PLE_DIM), lambda i, *_: (i, 0)),
            pl.BlockSpec(memory_space=pl.ANY),
            full(1, D_MODEL),
            full(D_MODEL, D_MODEL),
            full(PLE_DIM, D_MODEL),
            full(1, D_MODEL),
        ],
        out_specs=pl.BlockSpec((tm, D_MODEL), lambda i, *_: (i, 0)),
        scratch_shapes=[pltpu.VMEM((2, _local_rows(tm), D_MODEL), ROW_DTYPE),
                        pltpu.VMEM((2, tm, _local_rows(tm)), BF16),
                        pltpu.SemaphoreType.DMA((2,))],
    )
    return pl.pallas_call(
        _combine_kernel,
        out_shape=jax.ShapeDtypeStruct((t, D_MODEL), F32),
        grid_spec=grid_spec,
        compiler_params=pltpu.CompilerParams(
            dimension_semantics=("arbitrary",),
            vmem_limit_bytes=VMEM_LIMIT_BYTES),
        name="combine",
    )(tables["base_in"], tables["lstart"], tables["cnt"], meta, tables["lstart_rows"],
      meta, tables["lstart_rows"], x1, p2, ys, g_ple, w_ple_gate_bf, w_ple_bf, g_final)


def _tiles(seq_len):
    return dict(
        proj_rows=min(512, seq_len),
        ret_block=min(256, seq_len),
        route_rows=min(512, seq_len),
        expert_rows=512,
    )


def _slot_tables(tile_counts, blk, n_buf, nloc):
    cnt = tile_counts[:, :, 0].astype(jnp.int32)
    cnt = ((cnt + STRIP_ALIGN - 1) // STRIP_ALIGN) * STRIP_ALIGN
    seg_rows = jnp.sum(cnt, axis=1)
    counts = jnp.sum(cnt, axis=0)
    padded = ((counts + blk - 1) // blk) * blk
    ends_pad = jnp.cumsum(padded)
    starts_pad = ends_pad - padded
    tile_before = jnp.cumsum(cnt, axis=0) - cnt
    lstart = jnp.cumsum(cnt, axis=1) - cnt
    base = starts_pad[None, :] + tile_before
    with_filler = lambda a, col: jnp.concatenate([a, col[:, None]], axis=1).reshape(-1)
    blk_start = jnp.arange((n_buf + 2 * FILLER_ROWS) // blk, dtype=jnp.int32) * blk
    last_of_expert = jnp.any((padded > 0)[None, :]
                             & (blk_start[:, None] == (ends_pad - blk)[None, :]), axis=1)
    zflag = (last_of_expert | (blk_start >= ends_pad[-1])).astype(jnp.int32)
    filler_dst = n_buf + (jnp.arange(cnt.shape[0], dtype=jnp.int32) % 2) * FILLER_ROWS
    return dict(
        cnt=with_filler(cnt, nloc - seg_rows),
        lstart=with_filler(lstart, seg_rows),
        base_out=with_filler(base, filler_dst),
        base_in=with_filler(base, jnp.zeros_like(seg_rows)),
        lstart_rows=lstart.astype(F32)[:, None, :],
        lstart_cols=lstart.astype(F32)[:, :, None],
        zflag=zflag, ends_pad=ends_pad)


def _layer(x2, p2, seq_len, g_mix, w_in, w_pool, pool_scale, w_ret_o, w_out, g_ffn,
           w_router, b_router, w_gate_up, b_gate_up, w_down, b_down, g_ple,
           w_ple_gate, w_ple, g_out):
    t = x2.shape[0]
    cfg = _tiles(seq_len)
    row = lambda a: a.reshape(1, -1)

    proj = _in_proj(x2, row(g_mix), w_in.astype(BF16), seq_len, cfg["proj_rows"])
    o_gated = _retention(proj.reshape(t // seq_len, seq_len, IN_W), cfg["ret_block"])
    o_gated = o_gated.reshape(t, RET_V_W)

    lane_pad = ((0, 0), (0, LANES - N_EXPERTS))
    wr_hi = w_router.astype(BF16)
    wr_lo = jnp.pad((w_router - wr_hi.astype(F32)).astype(BF16), lane_pad)
    wr_hi = jnp.pad(wr_hi, lane_pad)
    tm = cfg["route_rows"]
    x1, h2, meta, tile_counts = _mix_route(
        proj, o_gated, x2, w_pool.astype(BF16), row(pool_scale), w_ret_o.astype(BF16),
        w_out.astype(BF16), row(g_ffn), wr_hi, wr_lo, jnp.pad(row(b_router), lane_pad), seq_len, tm)

    blk = cfg["expert_rows"]
    n_buf = t * TOP_K + (t // tm) * FILLER_ROWS + N_EXPERTS * blk
    n_blocks = n_buf // blk
    tables = _slot_tables(tile_counts, blk, n_buf, _local_rows(tm))
    blk_start = jnp.arange(n_blocks, dtype=jnp.int32) * blk
    block_e = jnp.minimum(
        jnp.sum((tables["ends_pad"][None, :] <= blk_start[:, None]).astype(jnp.int32), axis=1),
        N_EXPERTS - 1)
    n_used = tables["ends_pad"][-1:] // blk

    xs = _dispatch(h2, meta, tables, n_buf, blk, tm)
    ys = _experts(xs, n_buf, block_e, n_used, w_gate_up, b_gate_up, w_down, b_down, blk)
    return _combine(meta, tables, x1, p2, ys, row(g_ple), w_ple_gate.astype(BF16),
                    w_ple.astype(BF16), row(g_out), tm)


def kernel(x, p, g_mix, w_in, w_pool, pool_scale, w_ret_o, w_out, g_ffn, w_router,
           b_router, w_gate_up, b_gate_up, w_down, b_down, g_ple, w_ple_gate, w_ple,
           g_final):
    b, s, d = x.shape
    depth = p.shape[0]
    assert depth == 1 and d == D_MODEL
    x2 = x.reshape(b * s, d)
    out = _layer(x2, p[0].reshape(b * s, PLE_DIM), s, g_mix[0], w_in[0], w_pool[0],
                 pool_scale[0], w_ret_o[0], w_out[0], g_ffn[0], w_router[0],
                 b_router[0], w_gate_up[0], b_gate_up[0], w_down[0], b_down[0],
                 g_ple[0], w_ple_gate[0], w_ple[0], g_final)
    return out.reshape(b, s, d)
```

```python
import functools

import numpy as np
import jax
import jax.numpy as jnp
from jax import lax
from jax.experimental import pallas as pl
from jax.experimental.pallas import tpu as pltpu

F32 = jnp.float32
BF16 = jnp.bfloat16

D_MODEL = 1024
EPS = 1e-6
CHUNK = 64
PLE_DIM = 256
POOL_WINDOWS = (2, 4, 8, 16)
POOL_GROUP_W = D_MODEL // len(POOL_WINDOWS)
POOL_HALO = 16
RET_HEADS = 4
RET_QK_HEAD = 256
RET_V_HEAD = 512
RET_V_W = RET_HEADS * RET_V_HEAD
ROPE_BASE = 10000.0
IN_W = 9 * D_MODEL
N_EXPERTS = 32
TOP_K = 4
D_FF = D_MODEL
SWIGLU_ALPHA = 1.702
SWIGLU_LIMIT = 7.0

VMEM_LIMIT_BYTES = 56 * 1024 * 1024
LANES = 128
META_ROWS = 16
META_EXPERT, META_RANK, META_GATE = 0, TOP_K, 2 * TOP_K
ROW_DTYPE = F32
SUBLANES = 8
STRIP_ALIGN = SUBLANES * 4 // jnp.dtype(ROW_DTYPE).itemsize
FILLER_ROWS = N_EXPERTS * STRIP_ALIGN
N_SEG = N_EXPERTS + 1


def _rms(x, g):
    return x * lax.rsqrt(jnp.mean(x * x, axis=-1, keepdims=True) + EPS) * g


COL_Q, COL_K = 1, 2
COL_V, COL_G = (3, 4), (5, 6)
COL_GATE_A, COL_GATE_B = 7, 8


def _rotary_heads(y, cos, sin):
    half = RET_QK_HEAD // 2
    parts = []
    for h in range(RET_HEADS):
        x1 = y[:, h * RET_QK_HEAD:h * RET_QK_HEAD + half]
        x2 = y[:, h * RET_QK_HEAD + half:(h + 1) * RET_QK_HEAD]
        parts += [x1 * cos - x2 * sin, x2 * cos + x1 * sin]
    return jnp.concatenate(parts, axis=-1)


def _in_proj_kernel(x_ref, g_ref, w_ref, cos_ref, sin_ref, o_ref):
    h = _rms(x_ref[...], g_ref[...]).astype(BF16)
    for c in range(IN_W // D_MODEL):
        cols = slice(c * D_MODEL, (c + 1) * D_MODEL)
        y = jnp.dot(h, w_ref[:, cols], preferred_element_type=F32)
        if c == COL_Q:
            y = _rotary_heads(y, cos_ref[...], sin_ref[...])
        elif c == COL_K:
            y = _rotary_heads(y, cos_ref[...], sin_ref[...]) * (RET_QK_HEAD ** -0.5)
        elif c in COL_G:
            y = y * jax.nn.sigmoid(y)
        o_ref[:, cols] = y.astype(BF16)


def _rotary_tables(s):
    half = RET_QK_HEAD // 2
    pos = jnp.arange(s, dtype=F32)
    inv = ROPE_BASE ** (-jnp.linspace(0.0, 1.0, half, dtype=F32))
    ang = pos[:, None] * inv[None, :]
    return jnp.cos(ang), jnp.sin(ang)


def _in_proj(x2, g_mix, w_in_bf, seq_len, tm):
    t = x2.shape[0]
    half = RET_QK_HEAD // 2
    cos, sin = _rotary_tables(seq_len)
    tiles_per_seq = seq_len // tm
    return pl.pallas_call(
        _in_proj_kernel,
        out_shape=jax.ShapeDtypeStruct((t, IN_W), BF16),
        grid=(t // tm,),
        in_specs=[
            pl.BlockSpec((tm, D_MODEL), lambda i: (i, 0)),
            pl.BlockSpec((1, D_MODEL), lambda i: (0, 0)),
            pl.BlockSpec((D_MODEL, IN_W), lambda i: (0, 0), pipeline_mode=pl.Buffered(1)),
            pl.BlockSpec((tm, half), lambda i: (lax.rem(i, tiles_per_seq), 0)),
            pl.BlockSpec((tm, half), lambda i: (lax.rem(i, tiles_per_seq), 0)),
        ],
        out_specs=pl.BlockSpec((tm, IN_W), lambda i: (i, 0)),
        compiler_params=pltpu.CompilerParams(
            dimension_semantics=("arbitrary",),
            vmem_limit_bytes=VMEM_LIMIT_BYTES),
        name="in_proj",
    )(x2, g_mix, w_in_bf, cos, sin)


def _head_decay_logs():
    return [float(np.log(1.0 - 2.0 ** (-5.0 - h))) for h in range(RET_HEADS)]


def _retention_kernel(blk_decay, q_ref, k_ref, va_ref, vb_ref, ga_ref, gb_ref,
                      d_ref, qd_ref, kd_ref, o_ref, state_ref):
    @pl.when(pl.program_id(1) == 0)
    def _():
        state_ref[...] = jnp.zeros_like(state_ref)

    heads_per_ref = D_MODEL // RET_V_HEAD
    for h in range(RET_HEADS):
        qk_cols = slice(h * RET_QK_HEAD, (h + 1) * RET_QK_HEAD)
        v_cols = slice((h % heads_per_ref) * RET_V_HEAD, (h % heads_per_ref + 1) * RET_V_HEAD)
        q = q_ref[:, qk_cols]
        k = k_ref[:, qk_cols]
        v = (va_ref if h < heads_per_ref else vb_ref)[:, v_cols]
        g = (ga_ref if h < heads_per_ref else gb_ref)[:, v_cols]
        scores = lax.dot_general(q, k, (((1,), (1,)), ((), ())),
                                 preferred_element_type=F32) * d_ref[h]
        state = state_ref[h]
        o = jnp.dot(scores.astype(BF16), v, preferred_element_type=F32)
        o = o + jnp.dot((q.astype(F32) * qd_ref[h]).astype(BF16), state.astype(BF16),
                        preferred_element_type=F32)
        k_dec = (k.astype(F32) * kd_ref[h]).astype(BF16)
        state_ref[h] = state * blk_decay[h] + lax.dot_general(
            k_dec, v, (((0,), (0,)), ((), ())), preferred_element_type=F32)
        o = o * lax.rsqrt(jnp.mean(o * o, axis=-1, keepdims=True) + EPS)
        o_ref[:, h * RET_V_HEAD:(h + 1) * RET_V_HEAD] = (o * g.astype(F32)).astype(BF16)


def _retention_tables(blk):
    log_g = jnp.asarray(_head_decay_logs(), F32)
    idx = jnp.arange(blk, dtype=F32)
    diff = idx[:, None] - idx[None, :]
    chunk = jnp.arange(blk, dtype=jnp.int32) // CHUNK
    visible = chunk[None, :] <= chunk[:, None]
    dmask = jnp.where(visible[None], jnp.exp(log_g[:, None, None] * jnp.abs(diff)[None]), 0.0)
    q_dec = jnp.exp(log_g[:, None] * (idx + 1.0))[:, :, None]
    k_dec = jnp.exp(log_g[:, None] * (blk - 1.0 - idx))[:, :, None]
    q_dec = jnp.broadcast_to(q_dec, (RET_HEADS, blk, RET_QK_HEAD))
    k_dec = jnp.broadcast_to(k_dec, (RET_HEADS, blk, RET_QK_HEAD))
    return dmask.astype(F32), q_dec, k_dec


def _retention(proj3, blk):
    b, s, _ = proj3.shape
    dmask, q_dec, k_dec = _retention_tables(blk)
    blk_decay = [float(np.exp(lg * blk)) for lg in _head_decay_logs()]
    chunk = lambda c: pl.BlockSpec((None, blk, D_MODEL), lambda bi, l: (bi, l, c))
    table = lambda *shape: pl.BlockSpec(shape, lambda bi, l: (0,) * len(shape))
    return pl.pallas_call(
        functools.partial(_retention_kernel, blk_decay),
        out_shape=jax.ShapeDtypeStruct((b, s, RET_V_W), BF16),
        grid=(b, s // blk),
        in_specs=[
            chunk(COL_Q), chunk(COL_K), chunk(COL_V[0]), chunk(COL_V[1]),
            chunk(COL_G[0]), chunk(COL_G[1]),
            table(RET_HEADS, blk, blk),
            table(RET_HEADS, blk, RET_QK_HEAD),
            table(RET_HEADS, blk, RET_QK_HEAD),
        ],
        out_specs=pl.BlockSpec((None, blk, RET_V_W), lambda bi, l: (bi, l, 0)),
        scratch_shapes=[pltpu.VMEM((RET_HEADS, RET_QK_HEAD, RET_V_HEAD), F32)],
        compiler_params=pltpu.CompilerParams(
            dimension_semantics=("arbitrary", "arbitrary"),
            vmem_limit_bytes=VMEM_LIMIT_BYTES),
        name="retention",
    )(proj3, proj3, proj3, proj3, proj3, proj3, dmask, q_dec, k_dec)


def _window_sum(ext, w, tm):
    cur = ext
    span = 1
    while span < w:
        cur = cur[span:, :] + cur[:-span, :]
        span *= 2
    start = POOL_HALO + 1 - w
    return cur[start:start + tm, :]


def _mix_route_kernel(seq_len, u_ref, halo_ref, ga_ref, gb_ref, o_ref, x_ref,
                      wpool_ref, pscale_ref, wreto_ref, wout_ref, gffn_ref,
                      wr_hi_ref, wr_lo_ref, br_ref,
                      x1_ref, h2_ref, meta_ref, counts_ref):
    i = pl.program_id(0)
    tm = x_ref.shape[0]
    pos0 = lax.rem(i * tm, seq_len)

    u = u_ref[...].astype(F32)
    halo = jnp.where(pos0 == 0, 0.0, halo_ref[...].astype(F32))
    ext = jnp.concatenate([halo, u], axis=0)
    pos = (pos0 + lax.broadcasted_iota(jnp.int32, (tm, 1), 0)).astype(F32)
    pooled_out = []
    for g, w in enumerate(POOL_WINDOWS):
        cols = slice(g * POOL_GROUP_W, (g + 1) * POOL_GROUP_W)
        ws = _window_sum(ext[:, cols], w, tm)
        count = jnp.minimum(pos + 1.0, float(w))
        pooled = ws / count - u[:, cols]
        pooled_out.append(jnp.dot(pooled.astype(BF16), wpool_ref[g],
                                  preferred_element_type=F32))
    y_pool = jnp.concatenate(pooled_out, axis=-1) * pscale_ref[...]

    y_ret = jnp.dot(o_ref[...], wreto_ref[...], preferred_element_type=F32)
    merged = (jax.nn.sigmoid(ga_ref[...].astype(F32)) * y_pool
              + jax.nn.sigmoid(gb_ref[...].astype(F32)) * y_ret)
    x1 = x_ref[...] + jnp.dot(merged.astype(BF16), wout_ref[...],
                              preferred_element_type=F32)
    x1_ref[...] = x1
    h2 = _rms(x1, gffn_ref[...])
    h2_ref[...] = h2.astype(BF16)

    h_hi = h2.astype(BF16)
    h_lo = (h2 - h_hi.astype(F32)).astype(BF16)
    logits = (jnp.dot(h_hi, wr_hi_ref[...], preferred_element_type=F32)
              + jnp.dot(h_lo, wr_hi_ref[...], preferred_element_type=F32)
              + jnp.dot(h_hi, wr_lo_ref[...], preferred_element_type=F32)
              + br_ref[...])
    logits_t = logits.T[:N_EXPERTS, :]

    sub = lax.broadcasted_iota(jnp.int32, (N_EXPERTS, tm), 0)
    work = logits_t
    vals, idxs, hots = [], [], []
    for _ in range(TOP_K):
        m = jnp.max(work, axis=0, keepdims=True)
        idx = jnp.min(jnp.where(work == m, sub, N_EXPERTS), axis=0, keepdims=True)
        hot = sub == idx
        vals.append(m)
        idxs.append(idx)
        hots.append(hot)
        work = jnp.where(hot, -jnp.inf, work)
    exps = [jnp.exp(v - vals[0]) for v in vals]
    denom = exps[0] + exps[1] + exps[2] + exps[3]
    gates = [e / denom for e in exps]

    sel = (jnp.where(hots[0], 1.0, 0.0) + jnp.where(hots[1], 1.0, 0.0)
           + jnp.where(hots[2], 1.0, 0.0) + jnp.where(hots[3], 1.0, 0.0))
    row = lax.broadcasted_iota(jnp.int32, (tm, tm), 0)
    col = lax.broadcasted_iota(jnp.int32, (tm, tm), 1)
    earlier = jnp.where(row < col, 1.0, 0.0).astype(BF16)
    before = jnp.dot(sel.astype(BF16), earlier, preferred_element_type=F32)
    counts_ref[...] = jnp.sum(sel, axis=1, keepdims=True)

    msub = lax.broadcasted_iota(jnp.int32, (META_ROWS, tm), 0)
    meta = jnp.zeros((META_ROWS, tm), F32)
    for r in range(TOP_K):
        rank_r = jnp.sum(jnp.where(hots[r], before, 0.0), axis=0, keepdims=True)
        meta = jnp.where(msub == META_EXPERT + r, idxs[r].astype(F32), meta)
        meta = jnp.where(msub == META_RANK + r, rank_r, meta)
        meta = jnp.where(msub == META_GATE + r, gates[r], meta)
    meta_ref[...] = meta


def _mix_route(proj, o_gated, x2, w_pool_bf, pool_scale, w_ret_o_bf, w_out_bf,
               g_ffn, wr_hi, wr_lo, b_router, seq_len, tm):
    t = x2.shape[0]
    halo_per_tile = tm // POOL_HALO
    full = lambda *shape: pl.BlockSpec(shape, lambda i: (0,) * len(shape))
    return pl.pallas_call(
        functools.partial(_mix_route_kernel, seq_len),
        out_shape=(
            jax.ShapeDtypeStruct((t, D_MODEL), F32),
            jax.ShapeDtypeStruct((t, D_MODEL), BF16),
            jax.ShapeDtypeStruct((t // tm, META_ROWS, tm), F32),
            jax.ShapeDtypeStruct((t // tm, N_EXPERTS, 1), F32),
        ),
        grid=(t // tm,),
        in_specs=[
            pl.BlockSpec((tm, D_MODEL), lambda i: (i, 0)),
            pl.BlockSpec((POOL_HALO, D_MODEL),
                         lambda i: (jnp.maximum(i * halo_per_tile - 1, 0), 0)),
            pl.BlockSpec((tm, D_MODEL), lambda i: (i, COL_GATE_A)),
            pl.BlockSpec((tm, D_MODEL), lambda i: (i, COL_GATE_B)),
            pl.BlockSpec((tm, RET_V_W), lambda i: (i, 0)),
            pl.BlockSpec((tm, D_MODEL), lambda i: (i, 0)),
            full(len(POOL_WINDOWS), POOL_GROUP_W, POOL_GROUP_W),
            full(1, D_MODEL),
            full(RET_V_W, D_MODEL),
            full(D_MODEL, D_MODEL),
            full(1, D_MODEL),
            full(D_MODEL, LANES),
            full(D_MODEL, LANES),
            full(1, LANES),
        ],
        out_specs=(
            pl.BlockSpec((tm, D_MODEL), lambda i: (i, 0)),
            pl.BlockSpec((tm, D_MODEL), lambda i: (i, 0)),
            pl.BlockSpec((None, META_ROWS, tm), lambda i: (i, 0, 0)),
            pl.BlockSpec((None, N_EXPERTS, 1), lambda i: (i, 0, 0)),
        ),
        compiler_params=pltpu.CompilerParams(
            dimension_semantics=("arbitrary",),
            vmem_limit_bytes=VMEM_LIMIT_BYTES),
        name="mix_route",
    )(proj, proj, proj, proj, o_gated, x2, w_pool_bf, pool_scale, w_ret_o_bf,
      w_out_bf, g_ffn, wr_hi, wr_lo, b_router)


def _local_positions(meta, lstart):
    tm = meta.shape[0]
    lane_e = lax.broadcasted_iota(jnp.int32, (tm, N_EXPERTS), 1)
    out = []
    for r in range(TOP_K):
        e_r = meta[:, META_EXPERT + r:META_EXPERT + r + 1].astype(jnp.int32)
        first = jnp.sum(jnp.where(lane_e == e_r, lstart, 0.0), axis=-1, keepdims=True)
        out.append((meta[:, META_RANK + r:META_RANK + r + 1] + first).astype(jnp.int32))
    return out


def _local_rows(tm):
    return TOP_K * tm + FILLER_ROWS


def _for_each_strip(count, max_rows, start_copy):
    top = 1 << (max(max_rows, FILLER_ROWS).bit_length() - 1)
    pieces = [top >> k for k in range(top.bit_length()) if (top >> k) >= STRIP_ALIGN]
    rare_from = 2 * max_rows * TOP_K // N_EXPERTS
    rare = [p for p in pieces if p >= rare_from]
    rare_bits = sum(rare)

    def cover(sizes, off):
        for piece in sizes:
            take = count & piece

            @pl.when(take != 0)
            def _(off=off, piece=piece):
                start_copy(off, piece)

            off = off + take

    @pl.when((count & rare_bits) != 0)
    def _():
        cover(rare, jnp.int32(0))

    cover([p for p in pieces if p < rare_from], count & rare_bits)


def _dispatch_kernel(blk_rows, base_ref, lstart_ref, cnt_ref, zflag_ref,
                     h_ref, meta_ref, lstart_col_ref, lstart_row_ref, xs_ref, unsort_ref,
                     sorted_ref, zero_ref, sems, zsem):
    i = pl.program_id(0)
    tm = h_ref.shape[0]
    nrows = _local_rows(tm)
    slot = lax.rem(i, 2)

    @pl.when(i == 0)
    def _():
        zero_ref[...] = jnp.zeros_like(zero_ref)

        def zcopy(b):
            start = pl.multiple_of(b * blk_rows, blk_rows)
            return pltpu.make_async_copy(zero_ref, xs_ref.at[pl.ds(start, blk_rows), :], zsem)

        def zissue(b, c):
            @pl.when(zflag_ref[b] > 0)
            def _():
                zcopy(b).start()
            return c

        def zwait(b, c):
            @pl.when(zflag_ref[b] > 0)
            def _():
                zcopy(b).wait()
            return c

        n_zero_blocks = xs_ref.shape[0] // blk_rows
        lax.fori_loop(0, n_zero_blocks, zissue, 0)
        lax.fori_loop(0, n_zero_blocks, zwait, 0)

    meta_t = meta_ref[...]
    sub_e = lax.broadcasted_iota(jnp.int32, (N_EXPERTS, tm), 0)
    pos = []
    for r in range(TOP_K):
        e_r = meta_t[META_EXPERT + r:META_EXPERT + r + 1, :].astype(jnp.int32)
        first = jnp.sum(jnp.where(sub_e == e_r, lstart_col_ref[...], 0.0), axis=0, keepdims=True)
        pos.append((meta_t[META_RANK + r:META_RANK + r + 1, :] + first).astype(jnp.int32))
    jr = lax.broadcasted_iota(jnp.int32, (nrows, tm), 0)
    onehot = jnp.where(jr == pos[0], 1.0, jnp.where(jr == pos[1], 1.0,
             jnp.where(jr == pos[2], 1.0, jnp.where(jr == pos[3], 1.0, 0.0))))
    srt = jnp.dot(onehot.astype(BF16), h_ref[...], preferred_element_type=F32)
    sorted_ref[slot] = srt.astype(ROW_DTYPE)

    def per_segment(e, c):
        k = i * N_SEG + e
        src0 = lstart_ref[k]
        dst0 = base_ref[k]

        def start_copy(off, rows):
            src = pl.multiple_of(src0 + off, STRIP_ALIGN)
            dst = pl.multiple_of(dst0 + off, STRIP_ALIGN)
            pltpu.make_async_copy(sorted_ref.at[slot, pl.ds(src, rows), :],
                                  xs_ref.at[pl.ds(dst, rows), :],
                                  sems.at[slot]).start()

        _for_each_strip(cnt_ref[k], tm, start_copy)
        return c

    lax.fori_loop(0, N_SEG, per_segment, 0)

    for g in range(tm // LANES):
        toks = slice(g * LANES, (g + 1) * LANES)
        unsort_ref[toks, :] = _unsort_weights(meta_ref[:, toks], lstart_row_ref[...], nrows)

    def wait_tile(s):
        pltpu.make_async_copy(sorted_ref.at[s], xs_ref.at[pl.ds(0, nrows), :],
                              sems.at[s]).wait()

    @pl.when(i >= 1)
    def _():
        wait_tile(1 - slot)

    @pl.when(i == pl.num_programs(0) - 1)
    def _():
        wait_tile(slot)


def _dispatch(h2, meta, tables, n_buf, blk_rows, tm):
    t = h2.shape[0]
    grid_spec = pltpu.PrefetchScalarGridSpec(
        num_scalar_prefetch=4,
        grid=(t // tm,),
        in_specs=[
            pl.BlockSpec((tm, D_MODEL), lambda i, *_: (i, 0)),
            pl.BlockSpec((None, META_ROWS, tm), lambda i, *_: (i, 0, 0)),
            pl.BlockSpec((None, N_EXPERTS, 1), lambda i, *_: (i, 0, 0)),
            pl.BlockSpec((None, 1, N_EXPERTS), lambda i, *_: (i, 0, 0)),
        ],
        out_specs=(pl.BlockSpec(memory_space=pl.ANY),
                   pl.BlockSpec((tm, _local_rows(tm)), lambda i, *_: (i, 0))),
        scratch_shapes=[pltpu.VMEM((2, _local_rows(tm), D_MODEL), ROW_DTYPE),
                        pltpu.VMEM((blk_rows, D_MODEL), ROW_DTYPE),
                        pltpu.SemaphoreType.DMA((2,)),
                        pltpu.SemaphoreType.DMA(())],
    )
    return pl.pallas_call(
        functools.partial(_dispatch_kernel, blk_rows),
        out_shape=(jax.ShapeDtypeStruct((n_buf + 2 * FILLER_ROWS, D_MODEL), ROW_DTYPE),
                   jax.ShapeDtypeStruct((t, _local_rows(tm)), BF16)),
        grid_spec=grid_spec,
        compiler_params=pltpu.CompilerParams(
            dimension_semantics=("arbitrary",),
            vmem_limit_bytes=VMEM_LIMIT_BYTES,
            has_side_effects=True),
        name="dispatch",
    )(tables["base_out"], tables["lstart"], tables["cnt"], tables["zflag"],
      h2, meta, tables["lstart_cols"], tables["lstart_rows"])


def _unsort_weights(meta_t, lstart_row, nrows):
    tm = meta_t.shape[1]
    meta = jnp.concatenate(
        [meta_t, jnp.zeros((LANES - META_ROWS, tm), F32)], axis=0).T
    pos = _local_positions(meta, lstart_row)
    jl = lax.broadcasted_iota(jnp.int32, (tm, nrows), 1)
    gate = [meta[:, META_GATE + r:META_GATE + r + 1] for r in range(TOP_K)]
    weights = jnp.where(jl == pos[0], gate[0], jnp.where(jl == pos[1], gate[1],
              jnp.where(jl == pos[2], gate[2], jnp.where(jl == pos[3], gate[3], 0.0))))
    return weights.astype(BF16)


def _experts_kernel(be_ref, nu_ref, x_ref, wgu_ref, bgu_ref, wd_ref, bd_ref,
                    y_ref, wgu_bf, wd_bf):
    i = pl.program_id(0)

    @pl.when(i < nu_ref[0])
    def _():
        prev = be_ref[jnp.maximum(i - 1, 0)]

        @pl.when((i == 0) | (be_ref[i] != prev))
        def _():
            wgu_bf[...] = wgu_ref[...].astype(BF16)
            wd_bf[...] = wd_ref[...].astype(BF16)

        gu = jnp.dot(x_ref[...].astype(BF16), wgu_bf[...],
                     preferred_element_type=F32) + bgu_ref[...]
        glu = jnp.minimum(gu[:, :D_FF], SWIGLU_LIMIT)
        lin = jnp.clip(gu[:, D_FF:], -SWIGLU_LIMIT, SWIGLU_LIMIT)
        act = glu * jax.nn.sigmoid(SWIGLU_ALPHA * glu) * (lin + 1.0)
        y_ref[...] = (jnp.dot(act.astype(BF16), wd_bf[...],
                              preferred_element_type=F32) + bd_ref[...]).astype(ROW_DTYPE)

    @pl.when(i >= nu_ref[0])
    def _():
        y_ref[...] = jnp.zeros_like(y_ref)


def _experts(xs, n_buf, block_e, n_used, w_gate_up, b_gate_up, w_down, b_down, blk_rows):
    used = lambda i, be, nu: jnp.minimum(i, nu[0] - 1)
    grid_spec = pltpu.PrefetchScalarGridSpec(
        num_scalar_prefetch=2,
        grid=(n_buf // blk_rows,),
        in_specs=[
            pl.BlockSpec((blk_rows, D_MODEL), lambda i, be, nu: (used(i, be, nu), 0)),
            pl.BlockSpec((None, D_MODEL, 2 * D_FF), lambda i, be, nu: (be[i], 0, 0)),
            pl.BlockSpec((None, 1, 2 * D_FF), lambda i, be, nu: (be[i], 0, 0)),
            pl.BlockSpec((None, D_FF, D_MODEL), lambda i, be, nu: (be[i], 0, 0)),
            pl.BlockSpec((None, 1, D_MODEL), lambda i, be, nu: (be[i], 0, 0)),
        ],
        out_specs=pl.BlockSpec((blk_rows, D_MODEL), lambda i, be, nu: (i, 0)),
        scratch_shapes=[pltpu.VMEM((D_MODEL, 2 * D_FF), BF16),
                        pltpu.VMEM((D_FF, D_MODEL), BF16)],
    )
    return pl.pallas_call(
        _experts_kernel,
        out_shape=jax.ShapeDtypeStruct((n_buf, D_MODEL), ROW_DTYPE),
        grid_spec=grid_spec,
        compiler_params=pltpu.CompilerParams(
            dimension_semantics=("arbitrary",),
            vmem_limit_bytes=VMEM_LIMIT_BYTES),
        name="experts",
    )(block_e, n_used, xs, w_gate_up, b_gate_up[:, None, :], w_down, b_down[:, None, :])


def _combine_kernel(base_ref, lstart_ref, cnt_ref, unsort_ref, x1_ref,
                    p_ref, ys_ref, gple_ref, wpg_ref, wple_ref, gfin_ref, out_ref,
                    rows_ref, sems):
    i = pl.program_id(0)
    n = pl.num_programs(0)
    tm = x1_ref.shape[0]
    nrows = _local_rows(tm)
    slot = lax.rem(i, 2)

    def fetch_tile(tile, s):
        def per_segment(e, c):
            k = tile * N_SEG + e
            src0 = base_ref[k]
            dst0 = lstart_ref[k]

            def start_copy(off, rows):
                src = pl.multiple_of(src0 + off, STRIP_ALIGN)
                dst = pl.multiple_of(dst0 + off, STRIP_ALIGN)
                pltpu.make_async_copy(ys_ref.at[pl.ds(src, rows), :],
                                      rows_ref.at[s, pl.ds(dst, rows), :],
                                      sems.at[s]).start()

            _for_each_strip(cnt_ref[k], tm, start_copy)
            return c

        lax.fori_loop(0, N_SEG, per_segment, 0)

    @pl.when(i == 0)
    def _():
        fetch_tile(0, 0)

    @pl.when(i + 1 < n)
    def _():
        fetch_tile(i + 1, 1 - slot)

    pltpu.make_async_copy(ys_ref.at[pl.ds(0, nrows), :], rows_ref.at[slot],
                          sems.at[slot]).wait()
    moe = jnp.dot(unsort_ref[...], rows_ref[slot].astype(BF16),
                  preferred_element_type=F32)
    x2 = x1_ref[...] + moe
    u = _rms(x2, gple_ref[...])
    gate_ple = jax.nn.sigmoid(jnp.dot(u.astype(BF16), wpg_ref[...],
                                      preferred_element_type=F32))
    emb = jnp.dot(p_ref[...].astype(BF16), wple_ref[...], preferred_element_type=F32)
    x3 = x2 + gate_ple * emb
    out_ref[...] = _rms(x3, gfin_ref[...])


def _combine(unsort, tables, x1, p2, ys, g_ple, w_ple_gate_bf, w_ple_bf, g_final, tm):
    t = x1.shape[0]
    full = lambda *shape: pl.BlockSpec(shape, lambda i, *_: (0,) * len(shape))
    grid_spec = pltpu.PrefetchScalarGridSpec(
        num_scalar_prefetch=3,
        grid=(t // tm,),
        in_specs=[
            pl.BlockSpec((tm, _local_rows(tm)), lambda i, *_: (i, 0)),
            pl.BlockSpec((tm, D_MODEL), lambda i, *_: (i, 0)),
            pl.BlockSpec((tm, PLE_DIM), lambda i, *_: (i, 0)),
            pl.BlockSpec(memory_space=pl.ANY),
            full(1, D_MODEL),
            full(D_MODEL, D_MODEL),
            full(PLE_DIM, D_MODEL),
            full(1, D_MODEL),
        ],
        out_specs=pl.BlockSpec((tm, D_MODEL), lambda i, *_: (i, 0)),
        scratch_shapes=[pltpu.VMEM((2, _local_rows(tm), D_MODEL), ROW_DTYPE),
                        pltpu.SemaphoreType.DMA((2,))],
    )
    return pl.pallas_call(
        _combine_kernel,
        out_shape=jax.ShapeDtypeStruct((t, D_MODEL), F32),
        grid_spec=grid_spec,
        compiler_params=pltpu.CompilerParams(
            dimension_semantics=("arbitrary",),
            vmem_limit_bytes=VMEM_LIMIT_BYTES),
        name="combine",
    )(tables["base_in"], tables["lstart"], tables["cnt"], unsort,
      x1, p2, ys, g_ple, w_ple_gate_bf, w_ple_bf, g_final)


def _tiles(seq_len):
    return dict(
        proj_rows=min(512, seq_len),
        ret_block=min(256, seq_len),
        route_rows=min(512, seq_len),
        expert_rows=512,
    )


def _slot_tables(tile_counts, blk, n_buf, nloc):
    cnt = tile_counts[:, :, 0].astype(jnp.int32)
    cnt = ((cnt + STRIP_ALIGN - 1) // STRIP_ALIGN) * STRIP_ALIGN
    seg_rows = jnp.sum(cnt, axis=1)
    counts = jnp.sum(cnt, axis=0)
    padded = ((counts + blk - 1) // blk) * blk
    ends_pad = jnp.cumsum(padded)
    starts_pad = ends_pad - padded
    tile_before = jnp.cumsum(cnt, axis=0) - cnt
    lstart = jnp.cumsum(cnt, axis=1) - cnt
    base = starts_pad[None, :] + tile_before
    with_filler = lambda a, col: jnp.concatenate([a, col[:, None]], axis=1).reshape(-1)
    blk_start = jnp.arange((n_buf + 2 * FILLER_ROWS) // blk, dtype=jnp.int32) * blk
    last_of_expert = jnp.any((padded > 0)[None, :]
                             & (blk_start[:, None] == (ends_pad - blk)[None, :]), axis=1)
    zflag = (last_of_expert | (blk_start >= ends_pad[-1])).astype(jnp.int32)
    filler_dst = n_buf + (jnp.arange(cnt.shape[0], dtype=jnp.int32) % 2) * FILLER_ROWS
    return dict(
        cnt=with_filler(cnt, nloc - seg_rows),
        lstart=with_filler(lstart, seg_rows),
        base_out=with_filler(base, filler_dst),
        base_in=with_filler(base, jnp.zeros_like(seg_rows)),
        lstart_rows=lstart.astype(F32)[:, None, :],
        lstart_cols=lstart.astype(F32)[:, :, None],
        zflag=zflag, ends_pad=ends_pad)


def _layer(x2, p2, seq_len, g_mix, w_in, w_pool, pool_scale, w_ret_o, w_out, g_ffn,
           w_router, b_router, w_gate_up, b_gate_up, w_down, b_down, g_ple,
           w_ple_gate, w_ple, g_out):
    t = x2.shape[0]
    cfg = _tiles(seq_len)
    row = lambda a: a.reshape(1, -1)

    proj = _in_proj(x2, row(g_mix), w_in.astype(BF16), seq_len, cfg["proj_rows"])
    o_gated = _retention(proj.reshape(t // seq_len, seq_len, IN_W), cfg["ret_block"])
    o_gated = o_gated.reshape(t, RET_V_W)

    lane_pad = ((0, 0), (0, LANES - N_EXPERTS))
    wr_hi = w_router.astype(BF16)
    wr_lo = jnp.pad((w_router - wr_hi.astype(F32)).astype(BF16), lane_pad)
    wr_hi = jnp.pad(wr_hi, lane_pad)
    tm = cfg["route_rows"]
    x1, h2, meta, tile_counts = _mix_route(
        proj, o_gated, x2, w_pool.astype(BF16), row(pool_scale), w_ret_o.astype(BF16),
        w_out.astype(BF16), row(g_ffn), wr_hi, wr_lo, jnp.pad(row(b_router), lane_pad), seq_len, tm)

    blk = cfg["expert_rows"]
    n_buf = t * TOP_K + (t // tm) * FILLER_ROWS + N_EXPERTS * blk
    n_blocks = n_buf // blk
    tables = _slot_tables(tile_counts, blk, n_buf, _local_rows(tm))
    blk_start = jnp.arange(n_blocks, dtype=jnp.int32) * blk
    block_e = jnp.minimum(
        jnp.sum((tables["ends_pad"][None, :] <= blk_start[:, None]).astype(jnp.int32), axis=1),
        N_EXPERTS - 1)
    n_used = tables["ends_pad"][-1:] // blk

    xs, unsort = _dispatch(h2, meta, tables, n_buf, blk, tm)
    ys = _experts(xs, n_buf, block_e, n_used, w_gate_up, b_gate_up, w_down, b_down, blk)
    return _combine(unsort, tables, x1, p2, ys, row(g_ple), w_ple_gate.astype(BF16),
                    w_ple.astype(BF16), row(g_out), tm)


def kernel(x, p, g_mix, w_in, w_pool, pool_scale, w_ret_o, w_out, g_ffn, w_router,
           b_router, w_gate_up, b_gate_up, w_down, b_down, g_ple, w_ple_gate, w_ple,
           g_final):
    b, s, d = x.shape
    depth = p.shape[0]
    assert depth == 1 and d == D_MODEL
    x2 = x.reshape(b * s, d)
    out = _layer(x2, p[0].reshape(b * s, PLE_DIM), s, g_mix[0], w_in[0], w_pool[0],
                 pool_scale[0], w_ret_o[0], w_out[0], g_ffn[0], w_router[0],
                 b_router[0], w_gate_up[0], b_gate_up[0], w_down[0], b_down[0],
                 g_ple[0], w_ple_gate[0], w_ple[0], g_final)
    return out.reshape(b, s, d)
```

```python
import functools

import numpy as np
import jax
import jax.numpy as jnp
from jax import lax
from jax.experimental import pallas as pl
from jax.experimental.pallas import tpu as pltpu

F32 = jnp.float32
BF16 = jnp.bfloat16

D_MODEL = 1024
EPS = 1e-6
CHUNK = 64
PLE_DIM = 256
POOL_WINDOWS = (2, 4, 8, 16)
POOL_GROUP_W = D_MODEL // len(POOL_WINDOWS)
POOL_HALO = 16
RET_HEADS = 4
RET_QK_HEAD = 256
RET_V_HEAD = 512
RET_V_W = RET_HEADS * RET_V_HEAD
ROPE_BASE = 10000.0
IN_W = 9 * D_MODEL
N_EXPERTS = 32
TOP_K = 4
D_FF = D_MODEL
SWIGLU_ALPHA = 1.702
SWIGLU_LIMIT = 7.0

VMEM_LIMIT_BYTES = 56 * 1024 * 1024
LANES = 128
META_ROWS = 16
META_EXPERT, META_RANK, META_GATE = 0, TOP_K, 2 * TOP_K
ROW_DTYPE = F32
SUBLANES = 8
STRIP_ALIGN = SUBLANES * 4 // jnp.dtype(ROW_DTYPE).itemsize
FILLER_ROWS = N_EXPERTS * STRIP_ALIGN
N_SEG = N_EXPERTS + 1
ZERO_FIRST, ZERO_ANYTIME = 1, 2


def _rms(x, g):
    return x * lax.rsqrt(jnp.mean(x * x, axis=-1, keepdims=True) + EPS) * g


COL_Q, COL_K = 1, 2
COL_V, COL_G = (3, 4), (5, 6)
COL_GATE_A, COL_GATE_B = 7, 8


def _rotary_heads(y, cos, sin):
    half = RET_QK_HEAD // 2
    parts = []
    for h in range(RET_HEADS):
        x1 = y[:, h * RET_QK_HEAD:h * RET_QK_HEAD + half]
        x2 = y[:, h * RET_QK_HEAD + half:(h + 1) * RET_QK_HEAD]
        parts += [x1 * cos - x2 * sin, x2 * cos + x1 * sin]
    return jnp.concatenate(parts, axis=-1)


def _in_proj_kernel(x_ref, g_ref, w_ref, cos_ref, sin_ref, o_ref):
    h = _rms(x_ref[...], g_ref[...]).astype(BF16)
    for c in range(IN_W // D_MODEL):
        cols = slice(c * D_MODEL, (c + 1) * D_MODEL)
        y = jnp.dot(h, w_ref[:, cols], preferred_element_type=F32)
        if c == COL_Q:
            y = _rotary_heads(y, cos_ref[...], sin_ref[...])
        elif c == COL_K:
            y = _rotary_heads(y, cos_ref[...], sin_ref[...]) * (RET_QK_HEAD ** -0.5)
        elif c in COL_G:
            y = y * jax.nn.sigmoid(y)
        o_ref[:, cols] = y.astype(BF16)


def _rotary_tables(s):
    half = RET_QK_HEAD // 2
    pos = jnp.arange(s, dtype=F32)
    inv = ROPE_BASE ** (-jnp.linspace(0.0, 1.0, half, dtype=F32))
    ang = pos[:, None] * inv[None, :]
    return jnp.cos(ang), jnp.sin(ang)


def _in_proj(x2, g_mix, w_in_bf, seq_len, tm):
    t = x2.shape[0]
    half = RET_QK_HEAD // 2
    cos, sin = _rotary_tables(seq_len)
    tiles_per_seq = seq_len // tm
    return pl.pallas_call(
        _in_proj_kernel,
        out_shape=jax.ShapeDtypeStruct((t, IN_W), BF16),
        grid=(t // tm,),
        in_specs=[
            pl.BlockSpec((tm, D_MODEL), lambda i: (i, 0)),
            pl.BlockSpec((1, D_MODEL), lambda i: (0, 0)),
            pl.BlockSpec((D_MODEL, IN_W), lambda i: (0, 0), pipeline_mode=pl.Buffered(1)),
            pl.BlockSpec((tm, half), lambda i: (lax.rem(i, tiles_per_seq), 0)),
            pl.BlockSpec((tm, half), lambda i: (lax.rem(i, tiles_per_seq), 0)),
        ],
        out_specs=pl.BlockSpec((tm, IN_W), lambda i: (i, 0)),
        compiler_params=pltpu.CompilerParams(
            dimension_semantics=("arbitrary",),
            vmem_limit_bytes=VMEM_LIMIT_BYTES),
        name="in_proj",
    )(x2, g_mix, w_in_bf, cos, sin)


def _head_decay_logs():
    return [float(np.log(1.0 - 2.0 ** (-5.0 - h))) for h in range(RET_HEADS)]


def _retention_kernel(blk_decay, q_ref, k_ref, va_ref, vb_ref, ga_ref, gb_ref,
                      d_ref, qd_ref, kd_ref, o_ref, state_ref):
    @pl.when(pl.program_id(1) == 0)
    def _():
        state_ref[...] = jnp.zeros_like(state_ref)

    heads_per_ref = D_MODEL // RET_V_HEAD
    for h in range(RET_HEADS):
        qk_cols = slice(h * RET_QK_HEAD, (h + 1) * RET_QK_HEAD)
        v_cols = slice((h % heads_per_ref) * RET_V_HEAD, (h % heads_per_ref + 1) * RET_V_HEAD)
        q = q_ref[:, qk_cols]
        k = k_ref[:, qk_cols]
        v = (va_ref if h < heads_per_ref else vb_ref)[:, v_cols]
        g = (ga_ref if h < heads_per_ref else gb_ref)[:, v_cols]
        scores = lax.dot_general(q, k, (((1,), (1,)), ((), ())),
                                 preferred_element_type=F32) * d_ref[h]
        state = state_ref[h]
        o = jnp.dot(scores.astype(BF16), v, preferred_element_type=F32)
        o = o + jnp.dot((q.astype(F32) * qd_ref[h]).astype(BF16), state.astype(BF16),
                        preferred_element_type=F32)
        k_dec = (k.astype(F32) * kd_ref[h]).astype(BF16)
        state_ref[h] = state * blk_decay[h] + lax.dot_general(
            k_dec, v, (((0,), (0,)), ((), ())), preferred_element_type=F32)
        o = o * lax.rsqrt(jnp.mean(o * o, axis=-1, keepdims=True) + EPS)
        o_ref[:, h * RET_V_HEAD:(h + 1) * RET_V_HEAD] = (o * g.astype(F32)).astype(BF16)


def _retention_tables(blk):
    log_g = jnp.asarray(_head_decay_logs(), F32)
    idx = jnp.arange(blk, dtype=F32)
    diff = idx[:, None] - idx[None, :]
    chunk = jnp.arange(blk, dtype=jnp.int32) // CHUNK
    visible = chunk[None, :] <= chunk[:, None]
    dmask = jnp.where(visible[None], jnp.exp(log_g[:, None, None] * jnp.abs(diff)[None]), 0.0)
    q_dec = jnp.exp(log_g[:, None] * (idx + 1.0))[:, :, None]
    k_dec = jnp.exp(log_g[:, None] * (blk - 1.0 - idx))[:, :, None]
    q_dec = jnp.broadcast_to(q_dec, (RET_HEADS, blk, RET_QK_HEAD))
    k_dec = jnp.broadcast_to(k_dec, (RET_HEADS, blk, RET_QK_HEAD))
    return dmask.astype(F32), q_dec, k_dec


def _retention(proj3, blk):
    b, s, _ = proj3.shape
    dmask, q_dec, k_dec = _retention_tables(blk)
    blk_decay = [float(np.exp(lg * blk)) for lg in _head_decay_logs()]
    chunk = lambda c: pl.BlockSpec((None, blk, D_MODEL), lambda bi, l: (bi, l, c))
    table = lambda *shape: pl.BlockSpec(shape, lambda bi, l: (0,) * len(shape))
    return pl.pallas_call(
        functools.partial(_retention_kernel, blk_decay),
        out_shape=jax.ShapeDtypeStruct((b, s, RET_V_W), BF16),
        grid=(b, s // blk),
        in_specs=[
            chunk(COL_Q), chunk(COL_K), chunk(COL_V[0]), chunk(COL_V[1]),
            chunk(COL_G[0]), chunk(COL_G[1]),
            table(RET_HEADS, blk, blk),
            table(RET_HEADS, blk, RET_QK_HEAD),
            table(RET_HEADS, blk, RET_QK_HEAD),
        ],
        out_specs=pl.BlockSpec((None, blk, RET_V_W), lambda bi, l: (bi, l, 0)),
        scratch_shapes=[pltpu.VMEM((RET_HEADS, RET_QK_HEAD, RET_V_HEAD), F32)],
        compiler_params=pltpu.CompilerParams(
            dimension_semantics=("arbitrary", "arbitrary"),
            vmem_limit_bytes=VMEM_LIMIT_BYTES),
        name="retention",
    )(proj3, proj3, proj3, proj3, proj3, proj3, dmask, q_dec, k_dec)


def _window_sum(ext, w, tm):
    cur = ext
    span = 1
    while span < w:
        cur = cur[span:, :] + cur[:-span, :]
        span *= 2
    start = POOL_HALO + 1 - w
    return cur[start:start + tm, :]


def _mix_route_kernel(seq_len, u_ref, halo_ref, ga_ref, gb_ref, o_ref, x_ref,
                      wpool_ref, pscale_ref, wreto_ref, wout_ref, gffn_ref,
                      wr_hi_ref, wr_lo_ref, br_ref,
                      x1_ref, h2_ref, meta_ref, counts_ref):
    i = pl.program_id(0)
    tm = x_ref.shape[0]
    pos0 = lax.rem(i * tm, seq_len)

    u = u_ref[...].astype(F32)
    halo = jnp.where(pos0 == 0, 0.0, halo_ref[...].astype(F32))
    ext = jnp.concatenate([halo, u], axis=0)
    pos = (pos0 + lax.broadcasted_iota(jnp.int32, (tm, 1), 0)).astype(F32)
    pooled_out = []
    for g, w in enumerate(POOL_WINDOWS):
        cols = slice(g * POOL_GROUP_W, (g + 1) * POOL_GROUP_W)
        ws = _window_sum(ext[:, cols], w, tm)
        count = jnp.minimum(pos + 1.0, float(w))
        pooled = ws / count - u[:, cols]
        pooled_out.append(jnp.dot(pooled.astype(BF16), wpool_ref[g],
                                  preferred_element_type=F32))
    y_pool = jnp.concatenate(pooled_out, axis=-1) * pscale_ref[...]

    y_ret = jnp.dot(o_ref[...], wreto_ref[...], preferred_element_type=F32)
    merged = (jax.nn.sigmoid(ga_ref[...].astype(F32)) * y_pool
              + jax.nn.sigmoid(gb_ref[...].astype(F32)) * y_ret)
    x1 = x_ref[...] + jnp.dot(merged.astype(BF16), wout_ref[...],
                              preferred_element_type=F32)
    x1_ref[...] = x1
    h2 = _rms(x1, gffn_ref[...])
    h2_ref[...] = h2.astype(BF16)

    h_hi = h2.astype(BF16)
    h_lo = (h2 - h_hi.astype(F32)).astype(BF16)
    logits = (jnp.dot(h_hi, wr_hi_ref[...], preferred_element_type=F32)
              + jnp.dot(h_lo, wr_hi_ref[...], preferred_element_type=F32)
              + jnp.dot(h_hi, wr_lo_ref[...], preferred_element_type=F32)
              + br_ref[...])
    logits_t = logits.T[:N_EXPERTS, :]

    sub = lax.broadcasted_iota(jnp.int32, (N_EXPERTS, tm), 0)
    work = logits_t
    vals, idxs, hots = [], [], []
    for _ in range(TOP_K):
        m = jnp.max(work, axis=0, keepdims=True)
        idx = jnp.min(jnp.where(work == m, sub, N_EXPERTS), axis=0, keepdims=True)
        hot = sub == idx
        vals.append(m)
        idxs.append(idx)
        hots.append(hot)
        work = jnp.where(hot, -jnp.inf, work)
    exps = [jnp.exp(v - vals[0]) for v in vals]
    denom = exps[0] + exps[1] + exps[2] + exps[3]
    gates = [e / denom for e in exps]

    sel = (jnp.where(hots[0], 1.0, 0.0) + jnp.where(hots[1], 1.0, 0.0)
           + jnp.where(hots[2], 1.0, 0.0) + jnp.where(hots[3], 1.0, 0.0))
    row = lax.broadcasted_iota(jnp.int32, (tm, tm), 0)
    col = lax.broadcasted_iota(jnp.int32, (tm, tm), 1)
    earlier = jnp.where(row < col, 1.0, 0.0).astype(BF16)
    before = jnp.dot(sel.astype(BF16), earlier, preferred_element_type=F32)
    counts_ref[...] = jnp.sum(sel, axis=1, keepdims=True)

    msub = lax.broadcasted_iota(jnp.int32, (META_ROWS, tm), 0)
    meta = jnp.zeros((META_ROWS, tm), F32)
    for r in range(TOP_K):
        rank_r = jnp.sum(jnp.where(hots[r], before, 0.0), axis=0, keepdims=True)
        meta = jnp.where(msub == META_EXPERT + r, idxs[r].astype(F32), meta)
        meta = jnp.where(msub == META_RANK + r, rank_r, meta)
        meta = jnp.where(msub == META_GATE + r, gates[r], meta)
    meta_ref[...] = meta


def _mix_route(proj, o_gated, x2, w_pool_bf, pool_scale, w_ret_o_bf, w_out_bf,
               g_ffn, wr_hi, wr_lo, b_router, seq_len, tm):
    t = x2.shape[0]
    halo_per_tile = tm // POOL_HALO
    full = lambda *shape: pl.BlockSpec(shape, lambda i: (0,) * len(shape))
    return pl.pallas_call(
        functools.partial(_mix_route_kernel, seq_len),
        out_shape=(
            jax.ShapeDtypeStruct((t, D_MODEL), F32),
            jax.ShapeDtypeStruct((t, D_MODEL), BF16),
            jax.ShapeDtypeStruct((t // tm, META_ROWS, tm), F32),
            jax.ShapeDtypeStruct((t // tm, N_EXPERTS, 1), F32),
        ),
        grid=(t // tm,),
        in_specs=[
            pl.BlockSpec((tm, D_MODEL), lambda i: (i, 0)),
            pl.BlockSpec((POOL_HALO, D_MODEL),
                         lambda i: (jnp.maximum(i * halo_per_tile - 1, 0), 0)),
            pl.BlockSpec((tm, D_MODEL), lambda i: (i, COL_GATE_A)),
            pl.BlockSpec((tm, D_MODEL), lambda i: (i, COL_GATE_B)),
            pl.BlockSpec((tm, RET_V_W), lambda i: (i, 0)),
            pl.BlockSpec((tm, D_MODEL), lambda i: (i, 0)),
            full(len(POOL_WINDOWS), POOL_GROUP_W, POOL_GROUP_W),
            full(1, D_MODEL),
            full(RET_V_W, D_MODEL),
            full(D_MODEL, D_MODEL),
            full(1, D_MODEL),
            full(D_MODEL, LANES),
            full(D_MODEL, LANES),
            full(1, LANES),
        ],
        out_specs=(
            pl.BlockSpec((tm, D_MODEL), lambda i: (i, 0)),
            pl.BlockSpec((tm, D_MODEL), lambda i: (i, 0)),
            pl.BlockSpec((None, META_ROWS, tm), lambda i: (i, 0, 0)),
            pl.BlockSpec((None, N_EXPERTS, 1), lambda i: (i, 0, 0)),
        ),
        compiler_params=pltpu.CompilerParams(
            dimension_semantics=("arbitrary",),
            vmem_limit_bytes=VMEM_LIMIT_BYTES),
        name="mix_route",
    )(proj, proj, proj, proj, o_gated, x2, w_pool_bf, pool_scale, w_ret_o_bf,
      w_out_bf, g_ffn, wr_hi, wr_lo, b_router)


def _local_positions(meta, lstart):
    tm = meta.shape[0]
    lane_e = lax.broadcasted_iota(jnp.int32, (tm, N_EXPERTS), 1)
    out = []
    for r in range(TOP_K):
        e_r = meta[:, META_EXPERT + r:META_EXPERT + r + 1].astype(jnp.int32)
        first = jnp.sum(jnp.where(lane_e == e_r, lstart, 0.0), axis=-1, keepdims=True)
        out.append((meta[:, META_RANK + r:META_RANK + r + 1] + first).astype(jnp.int32))
    return out


def _local_rows(tm):
    return TOP_K * tm + FILLER_ROWS


def _for_each_strip(count, max_rows, start_copy):
    top = 1 << (max(max_rows, FILLER_ROWS).bit_length() - 1)
    pieces = [top >> k for k in range(top.bit_length()) if (top >> k) >= STRIP_ALIGN]
    rare_from = 2 * max_rows * TOP_K // N_EXPERTS
    rare = [p for p in pieces if p >= rare_from]
    rare_bits = sum(rare)

    def cover(sizes, off):
        for piece in sizes:
            take = count & piece

            @pl.when(take != 0)
            def _(off=off, piece=piece):
                start_copy(off, piece)

            off = off + take

    @pl.when((count & rare_bits) != 0)
    def _():
        cover(rare, jnp.int32(0))

    cover([p for p in pieces if p < rare_from], count & rare_bits)


def _dispatch_kernel(blk_rows, base_ref, lstart_ref, cnt_ref, zflag_ref,
                     h_ref, meta_ref, lstart_col_ref, xs_ref,
                     sorted_ref, zero_ref, sems, zsem, tail_sem):
    i = pl.program_id(0)
    tm = h_ref.shape[0]
    nrows = _local_rows(tm)
    slot = lax.rem(i, 2)

    n_zero_blocks = xs_ref.shape[0] // blk_rows

    def zcopy(b, sem):
        start = pl.multiple_of(b * blk_rows, blk_rows)
        return pltpu.make_async_copy(zero_ref, xs_ref.at[pl.ds(start, blk_rows), :], sem)

    def for_flagged(flag, sem, act):
        def body(b, c):
            @pl.when(zflag_ref[b] == flag)
            def _():
                act(zcopy(b, sem))
            return c
        lax.fori_loop(0, n_zero_blocks, body, 0)

    @pl.when(i == 0)
    def _():
        zero_ref[...] = jnp.zeros_like(zero_ref)
        for_flagged(ZERO_FIRST, zsem, lambda cp: cp.start())
        for_flagged(ZERO_ANYTIME, tail_sem, lambda cp: cp.start())
        for_flagged(ZERO_FIRST, zsem, lambda cp: cp.wait())

    meta_t = meta_ref[...]
    sub_e = lax.broadcasted_iota(jnp.int32, (N_EXPERTS, tm), 0)
    pos = []
    for r in range(TOP_K):
        e_r = meta_t[META_EXPERT + r:META_EXPERT + r + 1, :].astype(jnp.int32)
        first = jnp.sum(jnp.where(sub_e == e_r, lstart_col_ref[...], 0.0), axis=0, keepdims=True)
        pos.append((meta_t[META_RANK + r:META_RANK + r + 1, :] + first).astype(jnp.int32))
    jr = lax.broadcasted_iota(jnp.int32, (nrows, tm), 0)
    onehot = jnp.where(jr == pos[0], 1.0, jnp.where(jr == pos[1], 1.0,
             jnp.where(jr == pos[2], 1.0, jnp.where(jr == pos[3], 1.0, 0.0))))
    srt = jnp.dot(onehot.astype(BF16), h_ref[...], preferred_element_type=F32)
    sorted_ref[slot] = srt.astype(ROW_DTYPE)

    def per_segment(e, c):
        k = i * N_SEG + e
        src0 = lstart_ref[k]
        dst0 = base_ref[k]

        def start_copy(off, rows):
            src = pl.multiple_of(src0 + off, STRIP_ALIGN)
            dst = pl.multiple_of(dst0 + off, STRIP_ALIGN)
            pltpu.make_async_copy(sorted_ref.at[slot, pl.ds(src, rows), :],
                                  xs_ref.at[pl.ds(dst, rows), :],
                                  sems.at[slot]).start()

        _for_each_strip(cnt_ref[k], tm, start_copy)
        return c

    lax.fori_loop(0, N_SEG, per_segment, 0)

    def wait_tile(s):
        pltpu.make_async_copy(sorted_ref.at[s], xs_ref.at[pl.ds(0, nrows), :],
                              sems.at[s]).wait()

    @pl.when(i >= 1)
    def _():
        wait_tile(1 - slot)

    @pl.when(i == pl.num_programs(0) - 1)
    def _():
        wait_tile(slot)
        for_flagged(ZERO_ANYTIME, tail_sem, lambda cp: cp.wait())


def _dispatch(h2, meta, tables, n_buf, blk_rows, tm):
    t = h2.shape[0]
    grid_spec = pltpu.PrefetchScalarGridSpec(
        num_scalar_prefetch=4,
        grid=(t // tm,),
        in_specs=[
            pl.BlockSpec((tm, D_MODEL), lambda i, *_: (i, 0)),
            pl.BlockSpec((None, META_ROWS, tm), lambda i, *_: (i, 0, 0)),
            pl.BlockSpec((None, N_EXPERTS, 1), lambda i, *_: (i, 0, 0)),
        ],
        out_specs=pl.BlockSpec(memory_space=pl.ANY),
        scratch_shapes=[pltpu.VMEM((2, _local_rows(tm), D_MODEL), ROW_DTYPE),
                        pltpu.VMEM((blk_rows, D_MODEL), ROW_DTYPE),
                        pltpu.SemaphoreType.DMA((2,)),
                        pltpu.SemaphoreType.DMA(()),
                        pltpu.SemaphoreType.DMA(())],
    )
    return pl.pallas_call(
        functools.partial(_dispatch_kernel, blk_rows),
        out_shape=jax.ShapeDtypeStruct((n_buf + 2 * FILLER_ROWS, D_MODEL), ROW_DTYPE),
        grid_spec=grid_spec,
        compiler_params=pltpu.CompilerParams(
            dimension_semantics=("arbitrary",),
            vmem_limit_bytes=VMEM_LIMIT_BYTES,
            has_side_effects=True),
        name="dispatch",
    )(tables["base_out"], tables["lstart"], tables["cnt"], tables["zflag"],
      h2, meta, tables["lstart_cols"])


def _unsort_weights(meta_t, lstart_row, nrows):
    tm = meta_t.shape[1]
    meta = jnp.concatenate(
        [meta_t, jnp.zeros((LANES - META_ROWS, tm), F32)], axis=0).T
    pos = _local_positions(meta, lstart_row)
    jl = lax.broadcasted_iota(jnp.int32, (tm, nrows), 1)
    gate = [meta[:, META_GATE + r:META_GATE + r + 1] for r in range(TOP_K)]
    weights = jnp.where(jl == pos[0], gate[0], jnp.where(jl == pos[1], gate[1],
              jnp.where(jl == pos[2], gate[2], jnp.where(jl == pos[3], gate[3], 0.0))))
    return weights.astype(BF16)


def _experts_kernel(be_ref, nu_ref, x_ref, wgu_ref, bgu_ref, wd_ref, bd_ref,
                    meta_ref, lstart_row_ref, y_ref, unsort_ref, wgu_bf, wd_bf):
    i = pl.program_id(0)

    @pl.when(i < nu_ref[0])
    def _():
        prev = be_ref[jnp.maximum(i - 1, 0)]

        @pl.when((i == 0) | (be_ref[i] != prev))
        def _():
            wgu_bf[...] = wgu_ref[...].astype(BF16)
            wd_bf[...] = wd_ref[...].astype(BF16)

        gu = jnp.dot(x_ref[...].astype(BF16), wgu_bf[...],
                     preferred_element_type=F32) + bgu_ref[...]
        glu = jnp.minimum(gu[:, :D_FF], SWIGLU_LIMIT)
        lin = jnp.clip(gu[:, D_FF:], -SWIGLU_LIMIT, SWIGLU_LIMIT)
        act = glu * jax.nn.sigmoid(SWIGLU_ALPHA * glu) * (lin + 1.0)
        y_ref[...] = (jnp.dot(act.astype(BF16), wd_bf[...],
                              preferred_element_type=F32) + bd_ref[...]).astype(ROW_DTYPE)

        unsort_ref[...] = _unsort_weights(meta_ref[...], lstart_row_ref[...],
                                          unsort_ref.shape[1])

    @pl.when(i >= nu_ref[0])
    def _():
        y_ref[...] = jnp.zeros_like(y_ref)


def _experts(xs, n_buf, block_e, n_used, w_gate_up, b_gate_up, w_down, b_down,
             meta, lstart_rows, blk_rows):
    n_tiles, _, tm = meta.shape
    groups_per_tile = tm // LANES
    n_groups = n_tiles * groups_per_tile
    assert LANES * TOP_K >= blk_rows
    nloc = _local_rows(tm)
    used = lambda i, be, nu: jnp.minimum(i, nu[0] - 1)
    group = lambda i: jnp.minimum(i, n_groups - 1)
    grid_spec = pltpu.PrefetchScalarGridSpec(
        num_scalar_prefetch=2,
        grid=(n_buf // blk_rows,),
        in_specs=[
            pl.BlockSpec((blk_rows, D_MODEL), lambda i, be, nu: (used(i, be, nu), 0)),
            pl.BlockSpec((None, D_MODEL, 2 * D_FF), lambda i, be, nu: (be[i], 0, 0)),
            pl.BlockSpec((None, 1, 2 * D_FF), lambda i, be, nu: (be[i], 0, 0)),
            pl.BlockSpec((None, D_FF, D_MODEL), lambda i, be, nu: (be[i], 0, 0)),
            pl.BlockSpec((None, 1, D_MODEL), lambda i, be, nu: (be[i], 0, 0)),
            pl.BlockSpec((None, META_ROWS, LANES),
                         lambda i, be, nu: (group(i) // groups_per_tile, 0,
                                            lax.rem(group(i), groups_per_tile))),
            pl.BlockSpec((None, 1, N_EXPERTS),
                         lambda i, be, nu: (group(i) // groups_per_tile, 0, 0)),
        ],
        out_specs=(
            pl.BlockSpec((blk_rows, D_MODEL), lambda i, be, nu: (i, 0)),
            pl.BlockSpec((LANES, nloc), lambda i, be, nu: (group(i), 0)),
        ),
        scratch_shapes=[pltpu.VMEM((D_MODEL, 2 * D_FF), BF16),
                        pltpu.VMEM((D_FF, D_MODEL), BF16)],
    )
    return pl.pallas_call(
        _experts_kernel,
        out_shape=(jax.ShapeDtypeStruct((n_buf, D_MODEL), ROW_DTYPE),
                   jax.ShapeDtypeStruct((n_groups * LANES, nloc), BF16)),
        grid_spec=grid_spec,
        compiler_params=pltpu.CompilerParams(
            dimension_semantics=("arbitrary",),
            vmem_limit_bytes=VMEM_LIMIT_BYTES),
        name="experts",
    )(block_e, n_used, xs, w_gate_up, b_gate_up[:, None, :], w_down, b_down[:, None, :],
      meta, lstart_rows)


def _combine_kernel(base_ref, lstart_ref, cnt_ref, unsort_ref, x1_ref,
                    p_ref, ys_ref, gple_ref, wpg_ref, wple_ref, gfin_ref, out_ref,
                    rows_ref, sems):
    i = pl.program_id(0)
    n = pl.num_programs(0)
    tm = x1_ref.shape[0]
    nrows = _local_rows(tm)
    slot = lax.rem(i, 2)

    def fetch_tile(tile, s):
        def per_segment(e, c):
            k = tile * N_SEG + e
            src0 = base_ref[k]
            dst0 = lstart_ref[k]

            def start_copy(off, rows):
                src = pl.multiple_of(src0 + off, STRIP_ALIGN)
                dst = pl.multiple_of(dst0 + off, STRIP_ALIGN)
                pltpu.make_async_copy(ys_ref.at[pl.ds(src, rows), :],
                                      rows_ref.at[s, pl.ds(dst, rows), :],
                                      sems.at[s]).start()

            _for_each_strip(cnt_ref[k], tm, start_copy)
            return c

        lax.fori_loop(0, N_SEG, per_segment, 0)

    @pl.when(i == 0)
    def _():
        fetch_tile(0, 0)

    @pl.when(i + 1 < n)
    def _():
        fetch_tile(i + 1, 1 - slot)

    pltpu.make_async_copy(ys_ref.at[pl.ds(0, nrows), :], rows_ref.at[slot],
                          sems.at[slot]).wait()
    moe = jnp.dot(unsort_ref[...], rows_ref[slot].astype(BF16),
                  preferred_element_type=F32)
    x2 = x1_ref[...] + moe
    u = _rms(x2, gple_ref[...])
    gate_ple = jax.nn.sigmoid(jnp.dot(u.astype(BF16), wpg_ref[...],
                                      preferred_element_type=F32))
    emb = jnp.dot(p_ref[...].astype(BF16), wple_ref[...], preferred_element_type=F32)
    x3 = x2 + gate_ple * emb
    out_ref[...] = _rms(x3, gfin_ref[...])


def _combine(unsort, tables, x1, p2, ys, g_ple, w_ple_gate_bf, w_ple_bf, g_final, tm):
    t = x1.shape[0]
    full = lambda *shape: pl.BlockSpec(shape, lambda i, *_: (0,) * len(shape))
    grid_spec = pltpu.PrefetchScalarGridSpec(
        num_scalar_prefetch=3,
        grid=(t // tm,),
        in_specs=[
            pl.BlockSpec((tm, _local_rows(tm)), lambda i, *_: (i, 0)),
            pl.BlockSpec((tm, D_MODEL), lambda i, *_: (i, 0)),
            pl.BlockSpec((tm, PLE_DIM), lambda i, *_: (i, 0)),
            pl.BlockSpec(memory_space=pl.ANY),
            full(1, D_MODEL),
            full(D_MODEL, D_MODEL),
            full(PLE_DIM, D_MODEL),
            full(1, D_MODEL),
        ],
        out_specs=pl.BlockSpec((tm, D_MODEL), lambda i, *_: (i, 0)),
        scratch_shapes=[pltpu.VMEM((2, _local_rows(tm), D_MODEL), ROW_DTYPE),
                        pltpu.SemaphoreType.DMA((2,))],
    )
    return pl.pallas_call(
        _combine_kernel,
        out_shape=jax.ShapeDtypeStruct((t, D_MODEL), F32),
        grid_spec=grid_spec,
        compiler_params=pltpu.CompilerParams(
            dimension_semantics=("arbitrary",),
            vmem_limit_bytes=VMEM_LIMIT_BYTES),
        name="combine",
    )(tables["base_in"], tables["lstart"], tables["cnt"], unsort,
      x1, p2, ys, g_ple, w_ple_gate_bf, w_ple_bf, g_final)


def _tiles(seq_len):
    return dict(
        proj_rows=min(512, seq_len),
        ret_block=min(256, seq_len),
        route_rows=min(512, seq_len),
        expert_rows=512,
    )


def _slot_tables(tile_counts, blk, n_buf, nloc):
    cnt = tile_counts[:, :, 0].astype(jnp.int32)
    cnt = ((cnt + STRIP_ALIGN - 1) // STRIP_ALIGN) * STRIP_ALIGN
    seg_rows = jnp.sum(cnt, axis=1)
    counts = jnp.sum(cnt, axis=0)
    padded = ((counts + blk - 1) // blk) * blk
    ends_pad = jnp.cumsum(padded)
    starts_pad = ends_pad - padded
    tile_before = jnp.cumsum(cnt, axis=0) - cnt
    lstart = jnp.cumsum(cnt, axis=1) - cnt
    base = starts_pad[None, :] + tile_before
    with_filler = lambda a, col: jnp.concatenate([a, col[:, None]], axis=1).reshape(-1)
    blk_start = jnp.arange((n_buf + 2 * FILLER_ROWS) // blk, dtype=jnp.int32) * blk
    last_of_expert = jnp.any((padded > 0)[None, :]
                             & (blk_start[:, None] == (ends_pad - blk)[None, :]), axis=1)
    zflag = jnp.where(last_of_expert | (blk_start >= n_buf), ZERO_FIRST,
                      jnp.where(blk_start >= ends_pad[-1], ZERO_ANYTIME, 0)).astype(jnp.int32)
    filler_dst = n_buf + (jnp.arange(cnt.shape[0], dtype=jnp.int32) % 2) * FILLER_ROWS
    return dict(
        cnt=with_filler(cnt, nloc - seg_rows),
        lstart=with_filler(lstart, seg_rows),
        base_out=with_filler(base, filler_dst),
        base_in=with_filler(base, jnp.zeros_like(seg_rows)),
        lstart_rows=lstart.astype(F32)[:, None, :],
        lstart_cols=lstart.astype(F32)[:, :, None],
        zflag=zflag, ends_pad=ends_pad)


def _layer(x2, p2, seq_len, g_mix, w_in, w_pool, pool_scale, w_ret_o, w_out, g_ffn,
           w_router, b_router, w_gate_up, b_gate_up, w_down, b_down, g_ple,
           w_ple_gate, w_ple, g_out):
    t = x2.shape[0]
    cfg = _tiles(seq_len)
    row = lambda a: a.reshape(1, -1)

    proj = _in_proj(x2, row(g_mix), w_in.astype(BF16), seq_len, cfg["proj_rows"])
    o_gated = _retention(proj.reshape(t // seq_len, seq_len, IN_W), cfg["ret_block"])
    o_gated = o_gated.reshape(t, RET_V_W)

    lane_pad = ((0, 0), (0, LANES - N_EXPERTS))
    wr_hi = w_router.astype(BF16)
    wr_lo = jnp.pad((w_router - wr_hi.astype(F32)).astype(BF16), lane_pad)
    wr_hi = jnp.pad(wr_hi, lane_pad)
    tm = cfg["route_rows"]
    x1, h2, meta, tile_counts = _mix_route(
        proj, o_gated, x2, w_pool.astype(BF16), row(pool_scale), w_ret_o.astype(BF16),
        w_out.astype(BF16), row(g_ffn), wr_hi, wr_lo, jnp.pad(row(b_router), lane_pad), seq_len, tm)

    blk = cfg["expert_rows"]
    n_buf = t * TOP_K + (t // tm) * FILLER_ROWS + N_EXPERTS * blk
    n_blocks = n_buf // blk
    tables = _slot_tables(tile_counts, blk, n_buf, _local_rows(tm))
    blk_start = jnp.arange(n_blocks, dtype=jnp.int32) * blk
    block_e = jnp.minimum(
        jnp.sum((tables["ends_pad"][None, :] <= blk_start[:, None]).astype(jnp.int32), axis=1),
        N_EXPERTS - 1)
    n_used = tables["ends_pad"][-1:] // blk

    xs = _dispatch(h2, meta, tables, n_buf, blk, tm)
    ys, unsort = _experts(xs, n_buf, block_e, n_used, w_gate_up, b_gate_up, w_down, b_down,
                          meta, tables["lstart_rows"], blk)
    return _combine(unsort, tables, x1, p2, ys, row(g_ple), w_ple_gate.astype(BF16),
                    w_ple.astype(BF16), row(g_out), tm)


def kernel(x, p, g_mix, w_in, w_pool, pool_scale, w_ret_o, w_out, g_ffn, w_router,
           b_router, w_gate_up, b_gate_up, w_down, b_down, g_ple, w_ple_gate, w_ple,
           g_final):
    b, s, d = x.shape
    depth = p.shape[0]
    assert depth == 1 and d == D_MODEL
    x2 = x.reshape(b * s, d)
    out = _layer(x2, p[0].reshape(b * s, PLE_DIM), s, g_mix[0], w_in[0], w_pool[0],
                 pool_scale[0], w_ret_o[0], w_out[0], g_ffn[0], w_router[0],
                 b_router[0], w_gate_up[0], b_gate_up[0], w_down[0], b_down[0],
                 g_ple[0], w_ple_gate[0], w_ple[0], g_final)
    return out.reshape(b, s, d)
```

```python
import functools

import numpy as np
import jax
import jax.numpy as jnp
from jax import lax
from jax.experimental import pallas as pl
from jax.experimental.pallas import tpu as pltpu

F32 = jnp.float32
BF16 = jnp.bfloat16

D_MODEL = 1024
EPS = 1e-6
CHUNK = 64
PLE_DIM = 256
POOL_WINDOWS = (2, 4, 8, 16)
POOL_GROUP_W = D_MODEL // len(POOL_WINDOWS)
POOL_HALO = 16
RET_HEADS = 4
RET_QK_HEAD = 256
RET_V_HEAD = 512
RET_V_W = RET_HEADS * RET_V_HEAD
ROPE_BASE = 10000.0
IN_W = 9 * D_MODEL
N_EXPERTS = 32
TOP_K = 4
D_FF = D_MODEL
SWIGLU_ALPHA = 1.702
SWIGLU_LIMIT = 7.0

VMEM_LIMIT_BYTES = 56 * 1024 * 1024
LANES = 128
META_ROWS = 16
META_EXPERT, META_RANK, META_GATE = 0, TOP_K, 2 * TOP_K
ROW_DTYPE = F32
SUBLANES = 8
STRIP_ALIGN = SUBLANES * 4 // jnp.dtype(ROW_DTYPE).itemsize
FILLER_ROWS = N_EXPERTS * STRIP_ALIGN
N_SEG = N_EXPERTS + 1
ZERO_FIRST, ZERO_ANYTIME = 1, 2


def _rms(x, g):
    return x * lax.rsqrt(jnp.mean(x * x, axis=-1, keepdims=True) + EPS) * g


COL_U, COL_Q, COL_K = 0, 1, 2
COL_V, COL_G = (3, 4), (5, 6)
COL_GATE_A, COL_GATE_B = 7, 8
SIDE_U, SIDE_GATE_A, SIDE_GATE_B = 0, 1, 2


def _rotary_heads(y, cos, sin):
    half = RET_QK_HEAD // 2
    parts = []
    for h in range(RET_HEADS):
        x1 = y[:, h * RET_QK_HEAD:h * RET_QK_HEAD + half]
        x2 = y[:, h * RET_QK_HEAD + half:(h + 1) * RET_QK_HEAD]
        parts += [x1 * cos - x2 * sin, x2 * cos + x1 * sin]
    return jnp.concatenate(parts, axis=-1)


def _head_decay_logs():
    return [float(np.log(1.0 - 2.0 ** (-5.0 - h))) for h in range(RET_HEADS)]


def _proj_retention_kernel(blk_decay, ret_blk, x_ref, g_ref, w_ref, cos_ref, sin_ref,
                           d_ref, qd_ref, kd_ref, side_ref, o_ref,
                           q_s, k_s, v_s, g_s, state_ref):
    @pl.when(pl.program_id(1) == 0)
    def _():
        state_ref[...] = jnp.zeros_like(state_ref)

    tm = x_ref.shape[0]
    h_in = _rms(x_ref[...], g_ref[...]).astype(BF16)

    def proj(c):
        return jnp.dot(h_in, w_ref[:, c * D_MODEL:(c + 1) * D_MODEL],
                       preferred_element_type=F32)

    def chunk(n):
        return slice(n * D_MODEL, (n + 1) * D_MODEL)

    side_ref[:, chunk(SIDE_U)] = proj(COL_U).astype(BF16)
    q_s[...] = _rotary_heads(proj(COL_Q), cos_ref[...], sin_ref[...]).astype(BF16)
    k_s[...] = (_rotary_heads(proj(COL_K), cos_ref[...], sin_ref[...])
                * (RET_QK_HEAD ** -0.5)).astype(BF16)
    for n, c in enumerate(COL_V):
        v_s[:, chunk(n)] = proj(c).astype(BF16)
    for n, c in enumerate(COL_G):
        y = proj(c)
        g_s[:, chunk(n)] = (y * jax.nn.sigmoid(y)).astype(BF16)
    side_ref[:, chunk(SIDE_GATE_A)] = proj(COL_GATE_A).astype(BF16)
    side_ref[:, chunk(SIDE_GATE_B)] = proj(COL_GATE_B).astype(BF16)

    for b in range(tm // ret_blk):
        rows = slice(b * ret_blk, (b + 1) * ret_blk)
        for h in range(RET_HEADS):
            qk_cols = slice(h * RET_QK_HEAD, (h + 1) * RET_QK_HEAD)
            v_cols = slice(h * RET_V_HEAD, (h + 1) * RET_V_HEAD)
            q = q_s[rows, qk_cols]
            k = k_s[rows, qk_cols]
            v = v_s[rows, v_cols]
            scores = lax.dot_general(q, k, (((1,), (1,)), ((), ())),
                                     preferred_element_type=F32) * d_ref[h]
            state = state_ref[h]
            o = jnp.dot(scores.astype(BF16), v, preferred_element_type=F32)
            o = o + jnp.dot((q.astype(F32) * qd_ref[h]).astype(BF16), state.astype(BF16),
                            preferred_element_type=F32)
            k_dec = (k.astype(F32) * kd_ref[h]).astype(BF16)
            state_ref[h] = state * blk_decay[h] + lax.dot_general(
                k_dec, v, (((0,), (0,)), ((), ())), preferred_element_type=F32)
            o = o * lax.rsqrt(jnp.mean(o * o, axis=-1, keepdims=True) + EPS)
            o_ref[rows, v_cols] = (o * g_s[rows, v_cols].astype(F32)).astype(BF16)


def _rotary_tables(s):
    half = RET_QK_HEAD // 2
    pos = jnp.arange(s, dtype=F32)
    inv = ROPE_BASE ** (-jnp.linspace(0.0, 1.0, half, dtype=F32))
    ang = pos[:, None] * inv[None, :]
    return jnp.cos(ang), jnp.sin(ang)


def _retention_tables(blk):
    log_g = jnp.asarray(_head_decay_logs(), F32)
    idx = jnp.arange(blk, dtype=F32)
    diff = idx[:, None] - idx[None, :]
    chunk = jnp.arange(blk, dtype=jnp.int32) // CHUNK
    visible = chunk[None, :] <= chunk[:, None]
    dmask = jnp.where(visible[None], jnp.exp(log_g[:, None, None] * jnp.abs(diff)[None]), 0.0)
    q_dec = jnp.exp(log_g[:, None] * (idx + 1.0))[:, :, None]
    k_dec = jnp.exp(log_g[:, None] * (blk - 1.0 - idx))[:, :, None]
    q_dec = jnp.broadcast_to(q_dec, (RET_HEADS, blk, RET_QK_HEAD))
    k_dec = jnp.broadcast_to(k_dec, (RET_HEADS, blk, RET_QK_HEAD))
    return dmask.astype(F32), q_dec, k_dec


def _proj_retention(x3, g_mix, w_in_bf, tm, ret_blk):
    b, s, _ = x3.shape
    half = RET_QK_HEAD // 2
    cos, sin = _rotary_tables(s)
    dmask, q_dec, k_dec = _retention_tables(ret_blk)
    blk_decay = [float(np.exp(lg * ret_blk)) for lg in _head_decay_logs()]
    once = lambda *shape: pl.BlockSpec(shape, lambda bi, j: (0,) * len(shape),
                                       pipeline_mode=pl.Buffered(1))
    side_w = 3 * D_MODEL
    return pl.pallas_call(
        functools.partial(_proj_retention_kernel, blk_decay, ret_blk),
        out_shape=(jax.ShapeDtypeStruct((b, s, side_w), BF16),
                   jax.ShapeDtypeStruct((b, s, RET_V_W), BF16)),
        grid=(b, s // tm),
        in_specs=[
            pl.BlockSpec((None, tm, D_MODEL), lambda bi, j: (bi, j, 0)),
            pl.BlockSpec((1, D_MODEL), lambda bi, j: (0, 0)),
            once(D_MODEL, IN_W),
            pl.BlockSpec((tm, half), lambda bi, j: (j, 0)),
            pl.BlockSpec((tm, half), lambda bi, j: (j, 0)),
            once(RET_HEADS, ret_blk, ret_blk),
            once(RET_HEADS, ret_blk, RET_QK_HEAD),
            once(RET_HEADS, ret_blk, RET_QK_HEAD),
        ],
        out_specs=(pl.BlockSpec((None, tm, side_w), lambda bi, j: (bi, j, 0)),
                   pl.BlockSpec((None, tm, RET_V_W), lambda bi, j: (bi, j, 0))),
        scratch_shapes=[pltpu.VMEM((tm, D_MODEL), BF16),
                        pltpu.VMEM((tm, D_MODEL), BF16),
                        pltpu.VMEM((tm, RET_V_W), BF16),
                        pltpu.VMEM((tm, RET_V_W), BF16),
                        pltpu.VMEM((RET_HEADS, RET_QK_HEAD, RET_V_HEAD), F32)],
        compiler_params=pltpu.CompilerParams(
            dimension_semantics=("arbitrary", "arbitrary"),
            vmem_limit_bytes=VMEM_LIMIT_BYTES),
        name="proj_retention",
    )(x3, g_mix, w_in_bf, cos, sin, dmask, q_dec, k_dec)


def _window_sum(ext, w, tm):
    cur = ext
    span = 1
    while span < w:
        cur = cur[span:, :] + cur[:-span, :]
        span *= 2
    start = POOL_HALO + 1 - w
    return cur[start:start + tm, :]


def _mix_route_kernel(seq_len, u_ref, halo_ref, ga_ref, gb_ref, o_ref, x_ref,
                      wpool_ref, pscale_ref, wreto_ref, wout_ref, gffn_ref,
                      wr_hi_ref, wr_lo_ref, br_ref,
                      x1_ref, h2_ref, meta_ref, counts_ref):
    i = pl.program_id(0)
    tm = x_ref.shape[0]
    pos0 = lax.rem(i * tm, seq_len)

    u = u_ref[...].astype(F32)
    halo = jnp.where(pos0 == 0, 0.0, halo_ref[...].astype(F32))
    ext = jnp.concatenate([halo, u], axis=0)
    pos = (pos0 + lax.broadcasted_iota(jnp.int32, (tm, 1), 0)).astype(F32)
    pooled_out = []
    for g, w in enumerate(POOL_WINDOWS):
        cols = slice(g * POOL_GROUP_W, (g + 1) * POOL_GROUP_W)
        ws = _window_sum(ext[:, cols], w, tm)
        count = jnp.minimum(pos + 1.0, float(w))
        pooled = ws / count - u[:, cols]
        pooled_out.append(jnp.dot(pooled.astype(BF16), wpool_ref[g],
                                  preferred_element_type=F32))
    y_pool = jnp.concatenate(pooled_out, axis=-1) * pscale_ref[...]

    y_ret = jnp.dot(o_ref[...], wreto_ref[...], preferred_element_type=F32)
    merged = (jax.nn.sigmoid(ga_ref[...].astype(F32)) * y_pool
              + jax.nn.sigmoid(gb_ref[...].astype(F32)) * y_ret)
    x1 = x_ref[...] + jnp.dot(merged.astype(BF16), wout_ref[...],
                              preferred_element_type=F32)
    x1_ref[...] = x1
    h2 = _rms(x1, gffn_ref[...])
    h2_ref[...] = h2.astype(BF16)

    h_hi = h2.astype(BF16)
    h_lo = (h2 - h_hi.astype(F32)).astype(BF16)
    logits = (jnp.dot(h_hi, wr_hi_ref[...], preferred_element_type=F32)
              + jnp.dot(h_lo, wr_hi_ref[...], preferred_element_type=F32)
              + jnp.dot(h_hi, wr_lo_ref[...], preferred_element_type=F32)
              + br_ref[...])
    logits_t = logits.T[:N_EXPERTS, :]

    sub = lax.broadcasted_iota(jnp.int32, (N_EXPERTS, tm), 0)
    work = logits_t
    vals, idxs, hots = [], [], []
    for _ in range(TOP_K):
        m = jnp.max(work, axis=0, keepdims=True)
        idx = jnp.min(jnp.where(work == m, sub, N_EXPERTS), axis=0, keepdims=True)
        hot = sub == idx
        vals.append(m)
        idxs.append(idx)
        hots.append(hot)
        work = jnp.where(hot, -jnp.inf, work)
    exps = [jnp.exp(v - vals[0]) for v in vals]
    denom = exps[0] + exps[1] + exps[2] + exps[3]
    gates = [e / denom for e in exps]

    sel = (jnp.where(hots[0], 1.0, 0.0) + jnp.where(hots[1], 1.0, 0.0)
           + jnp.where(hots[2], 1.0, 0.0) + jnp.where(hots[3], 1.0, 0.0))
    row = lax.broadcasted_iota(jnp.int32, (tm, tm), 0)
    col = lax.broadcasted_iota(jnp.int32, (tm, tm), 1)
    earlier = jnp.where(row < col, 1.0, 0.0).astype(BF16)
    before = jnp.dot(sel.astype(BF16), earlier, preferred_element_type=F32)
    counts_ref[...] = jnp.sum(sel, axis=1, keepdims=True)

    msub = lax.broadcasted_iota(jnp.int32, (META_ROWS, tm), 0)
    meta = jnp.zeros((META_ROWS, tm), F32)
    for r in range(TOP_K):
        rank_r = jnp.sum(jnp.where(hots[r], before, 0.0), axis=0, keepdims=True)
        meta = jnp.where(msub == META_EXPERT + r, idxs[r].astype(F32), meta)
        meta = jnp.where(msub == META_RANK + r, rank_r, meta)
        meta = jnp.where(msub == META_GATE + r, gates[r], meta)
    meta_ref[...] = meta


def _mix_route(side, o_gated, x2, w_pool_bf, pool_scale, w_ret_o_bf, w_out_bf,
               g_ffn, wr_hi, wr_lo, b_router, seq_len, tm):
    t = x2.shape[0]
    halo_per_tile = tm // POOL_HALO
    full = lambda *shape: pl.BlockSpec(shape, lambda i: (0,) * len(shape))
    return pl.pallas_call(
        functools.partial(_mix_route_kernel, seq_len),
        out_shape=(
            jax.ShapeDtypeStruct((t, D_MODEL), F32),
            jax.ShapeDtypeStruct((t, D_MODEL), BF16),
            jax.ShapeDtypeStruct((t // tm, META_ROWS, tm), F32),
            jax.ShapeDtypeStruct((t // tm, N_EXPERTS, 1), F32),
        ),
        grid=(t // tm,),
        in_specs=[
            pl.BlockSpec((tm, D_MODEL), lambda i: (i, SIDE_U)),
            pl.BlockSpec((POOL_HALO, D_MODEL),
                         lambda i: (jnp.maximum(i * halo_per_tile - 1, 0), SIDE_U)),
            pl.BlockSpec((tm, D_MODEL), lambda i: (i, SIDE_GATE_A)),
            pl.BlockSpec((tm, D_MODEL), lambda i: (i, SIDE_GATE_B)),
            pl.BlockSpec((tm, RET_V_W), lambda i: (i, 0)),
            pl.BlockSpec((tm, D_MODEL), lambda i: (i, 0)),
            full(len(POOL_WINDOWS), POOL_GROUP_W, POOL_GROUP_W),
            full(1, D_MODEL),
            full(RET_V_W, D_MODEL),
            full(D_MODEL, D_MODEL),
            full(1, D_MODEL),
            full(D_MODEL, LANES),
            full(D_MODEL, LANES),
            full(1, LANES),
        ],
        out_specs=(
            pl.BlockSpec((tm, D_MODEL), lambda i: (i, 0)),
            pl.BlockSpec((tm, D_MODEL), lambda i: (i, 0)),
            pl.BlockSpec((None, META_ROWS, tm), lambda i: (i, 0, 0)),
            pl.BlockSpec((None, N_EXPERTS, 1), lambda i: (i, 0, 0)),
        ),
        compiler_params=pltpu.CompilerParams(
            dimension_semantics=("arbitrary",),
            vmem_limit_bytes=VMEM_LIMIT_BYTES),
        name="mix_route",
    )(side, side, side, side, o_gated, x2, w_pool_bf, pool_scale, w_ret_o_bf,
      w_out_bf, g_ffn, wr_hi, wr_lo, b_router)


def _local_positions(meta, lstart):
    tm = meta.shape[0]
    lane_e = lax.broadcasted_iota(jnp.int32, (tm, N_EXPERTS), 1)
    out = []
    for r in range(TOP_K):
        e_r = meta[:, META_EXPERT + r:META_EXPERT + r + 1].astype(jnp.int32)
        first = jnp.sum(jnp.where(lane_e == e_r, lstart, 0.0), axis=-1, keepdims=True)
        out.append((meta[:, META_RANK + r:META_RANK + r + 1] + first).astype(jnp.int32))
    return out


def _local_rows(tm):
    return TOP_K * tm + FILLER_ROWS


def _for_each_strip(count, max_rows, start_copy):
    top = 1 << (max(max_rows, FILLER_ROWS).bit_length() - 1)
    pieces = [top >> k for k in range(top.bit_length()) if (top >> k) >= STRIP_ALIGN]
    rare_from = 2 * max_rows * TOP_K // N_EXPERTS
    rare = [p for p in pieces if p >= rare_from]
    rare_bits = sum(rare)

    def cover(sizes, off):
        for piece in sizes:
            take = count & piece

            @pl.when(take != 0)
            def _(off=off, piece=piece):
                start_copy(off, piece)

            off = off + take

    @pl.when((count & rare_bits) != 0)
    def _():
        cover(rare, jnp.int32(0))

    cover([p for p in pieces if p < rare_from], count & rare_bits)


def _dispatch_kernel(blk_rows, base_ref, lstart_ref, cnt_ref, zflag_ref,
                     h_ref, meta_ref, lstart_col_ref, xs_ref,
                     sorted_ref, zero_ref, sems, zsem, tail_sem):
    i = pl.program_id(0)
    tm = h_ref.shape[0]
    nrows = _local_rows(tm)
    slot = lax.rem(i, 2)

    n_zero_blocks = xs_ref.shape[0] // blk_rows

    def zcopy(b, sem):
        start = pl.multiple_of(b * blk_rows, blk_rows)
        return pltpu.make_async_copy(zero_ref, xs_ref.at[pl.ds(start, blk_rows), :], sem)

    def for_flagged(flag, sem, act):
        def body(b, c):
            @pl.when(zflag_ref[b] == flag)
            def _():
                act(zcopy(b, sem))
            return c
        lax.fori_loop(0, n_zero_blocks, body, 0)

    @pl.when(i == 0)
    def _():
        zero_ref[...] = jnp.zeros_like(zero_ref)
        for_flagged(ZERO_FIRST, zsem, lambda cp: cp.start())
        for_flagged(ZERO_ANYTIME, tail_sem, lambda cp: cp.start())
        for_flagged(ZERO_FIRST, zsem, lambda cp: cp.wait())

    meta_t = meta_ref[...]
    sub_e = lax.broadcasted_iota(jnp.int32, (N_EXPERTS, tm), 0)
    pos = []
    for r in range(TOP_K):
        e_r = meta_t[META_EXPERT + r:META_EXPERT + r + 1, :].astype(jnp.int32)
        first = jnp.sum(jnp.where(sub_e == e_r, lstart_col_ref[...], 0.0), axis=0, keepdims=True)
        pos.append((meta_t[META_RANK + r:META_RANK + r + 1, :] + first).astype(jnp.int32))
    jr = lax.broadcasted_iota(jnp.int32, (nrows, tm), 0)
    onehot = jnp.where(jr == pos[0], 1.0, jnp.where(jr == pos[1], 1.0,
             jnp.where(jr == pos[2], 1.0, jnp.where(jr == pos[3], 1.0, 0.0))))
    srt = jnp.dot(onehot.astype(BF16), h_ref[...], preferred_element_type=F32)
    sorted_ref[slot] = srt.astype(ROW_DTYPE)

    def per_segment(e, c):
        k = i * N_SEG + e
        src0 = lstart_ref[k]
        dst0 = base_ref[k]

        def start_copy(off, rows):
            src = pl.multiple_of(src0 + off, STRIP_ALIGN)
            dst = pl.multiple_of(dst0 + off, STRIP_ALIGN)
            pltpu.make_async_copy(sorted_ref.at[slot, pl.ds(src, rows), :],
                                  xs_ref.at[pl.ds(dst, rows), :],
                                  sems.at[slot]).start()

        _for_each_strip(cnt_ref[k], tm, start_copy)
        return c

    lax.fori_loop(0, N_SEG, per_segment, 0)

    def wait_tile(s):
        pltpu.make_async_copy(sorted_ref.at[s], xs_ref.at[pl.ds(0, nrows), :],
                              sems.at[s]).wait()

    @pl.when(i >= 1)
    def _():
        wait_tile(1 - slot)

    @pl.when(i == pl.num_programs(0) - 1)
    def _():
        wait_tile(slot)
        for_flagged(ZERO_ANYTIME, tail_sem, lambda cp: cp.wait())


def _dispatch(h2, meta, tables, n_buf, blk_rows, tm):
    t = h2.shape[0]
    grid_spec = pltpu.PrefetchScalarGridSpec(
        num_scalar_prefetch=4,
        grid=(t // tm,),
        in_specs=[
            pl.BlockSpec((tm, D_MODEL), lambda i, *_: (i, 0)),
            pl.BlockSpec((None, META_ROWS, tm), lambda i, *_: (i, 0, 0)),
            pl.BlockSpec((None, N_EXPERTS, 1), lambda i, *_: (i, 0, 0)),
        ],
        out_specs=pl.BlockSpec(memory_space=pl.ANY),
        scratch_shapes=[pltpu.VMEM((2, _local_rows(tm), D_MODEL), ROW_DTYPE),
                        pltpu.VMEM((blk_rows, D_MODEL), ROW_DTYPE),
                        pltpu.SemaphoreType.DMA((2,)),
                        pltpu.SemaphoreType.DMA(()),
                        pltpu.SemaphoreType.DMA(())],
    )
    return pl.pallas_call(
        functools.partial(_dispatch_kernel, blk_rows),
        out_shape=jax.ShapeDtypeStruct((n_buf + 2 * FILLER_ROWS, D_MODEL), ROW_DTYPE),
        grid_spec=grid_spec,
        compiler_params=pltpu.CompilerParams(
            dimension_semantics=("arbitrary",),
            vmem_limit_bytes=VMEM_LIMIT_BYTES,
            has_side_effects=True),
        name="dispatch",
    )(tables["base_out"], tables["lstart"], tables["cnt"], tables["zflag"],
      h2, meta, tables["lstart_cols"])


def _experts_kernel(be_ref, nu_ref, x_ref, wgu_ref, bgu_ref, wd_ref, bd_ref,
                    y_ref, wgu_bf, wd_bf):
    i = pl.program_id(0)

    @pl.when(i < nu_ref[0])
    def _():
        prev = be_ref[jnp.maximum(i - 1, 0)]

        @pl.when((i == 0) | (be_ref[i] != prev))
        def _():
            wgu_bf[...] = wgu_ref[...].astype(BF16)
            wd_bf[...] = wd_ref[...].astype(BF16)

        gu = jnp.dot(x_ref[...].astype(BF16), wgu_bf[...],
                     preferred_element_type=F32) + bgu_ref[...]
        glu = jnp.minimum(gu[:, :D_FF], SWIGLU_LIMIT)
        lin = jnp.clip(gu[:, D_FF:], -SWIGLU_LIMIT, SWIGLU_LIMIT)
        act = glu * jax.nn.sigmoid(SWIGLU_ALPHA * glu) * (lin + 1.0)
        y_ref[...] = (jnp.dot(act.astype(BF16), wd_bf[...],
                              preferred_element_type=F32) + bd_ref[...]).astype(ROW_DTYPE)

    @pl.when(i >= nu_ref[0])
    def _():
        y_ref[...] = jnp.zeros_like(y_ref)


def _experts(xs, n_buf, block_e, n_used, w_gate_up, b_gate_up, w_down, b_down, blk_rows):
    used = lambda i, be, nu: jnp.minimum(i, nu[0] - 1)
    grid_spec = pltpu.PrefetchScalarGridSpec(
        num_scalar_prefetch=2,
        grid=(n_buf // blk_rows,),
        in_specs=[
            pl.BlockSpec((blk_rows, D_MODEL), lambda i, be, nu: (used(i, be, nu), 0)),
            pl.BlockSpec((None, D_MODEL, 2 * D_FF), lambda i, be, nu: (be[i], 0, 0)),
            pl.BlockSpec((None, 1, 2 * D_FF), lambda i, be, nu: (be[i], 0, 0)),
            pl.BlockSpec((None, D_FF, D_MODEL), lambda i, be, nu: (be[i], 0, 0)),
            pl.BlockSpec((None, 1, D_MODEL), lambda i, be, nu: (be[i], 0, 0)),
        ],
        out_specs=pl.BlockSpec((blk_rows, D_MODEL), lambda i, be, nu: (i, 0)),
        scratch_shapes=[pltpu.VMEM((D_MODEL, 2 * D_FF), BF16),
                        pltpu.VMEM((D_FF, D_MODEL), BF16)],
    )
    return pl.pallas_call(
        _experts_kernel,
        out_shape=jax.ShapeDtypeStruct((n_buf, D_MODEL), ROW_DTYPE),
        grid_spec=grid_spec,
        compiler_params=pltpu.CompilerParams(
            dimension_semantics=("arbitrary",),
            vmem_limit_bytes=VMEM_LIMIT_BYTES),
        name="experts",
    )(block_e, n_used, xs, w_gate_up, b_gate_up[:, None, :], w_down, b_down[:, None, :])


def _unsort_weights(meta_t, lstart_row, nrows):
    tm = meta_t.shape[1]
    meta = jnp.concatenate(
        [meta_t, jnp.zeros((LANES - META_ROWS, tm), F32)], axis=0).T
    pos = _local_positions(meta, lstart_row)
    jl = lax.broadcasted_iota(jnp.int32, (tm, nrows), 1)
    gate = [meta[:, META_GATE + r:META_GATE + r + 1] for r in range(TOP_K)]
    weights = jnp.where(jl == pos[0], gate[0], jnp.where(jl == pos[1], gate[1],
              jnp.where(jl == pos[2], gate[2], jnp.where(jl == pos[3], gate[3], 0.0))))
    return weights.astype(BF16)


def _combine_kernel(base_ref, lstart_ref, cnt_ref, meta_ref, lstart_row_ref,
                    meta_next_ref, lstart_row_next_ref, x1_ref,
                    p_ref, ys_ref, gple_ref, wpg_ref, wple_ref, gfin_ref, out_ref,
                    rows_ref, unsort_ref, sems):
    i = pl.program_id(0)
    n = pl.num_programs(0)
    tm = x1_ref.shape[0]
    nrows = _local_rows(tm)
    slot = lax.rem(i, 2)

    def fetch_tile(tile, s):
        def per_segment(e, c):
            k = tile * N_SEG + e
            src0 = base_ref[k]
            dst0 = lstart_ref[k]

            def start_copy(off, rows):
                src = pl.multiple_of(src0 + off, STRIP_ALIGN)
                dst = pl.multiple_of(dst0 + off, STRIP_ALIGN)
                pltpu.make_async_copy(ys_ref.at[pl.ds(src, rows), :],
                                      rows_ref.at[s, pl.ds(dst, rows), :],
                                      sems.at[s]).start()

            _for_each_strip(cnt_ref[k], tm, start_copy)
            return c

        lax.fori_loop(0, N_SEG, per_segment, 0)

    @pl.when(i == 0)
    def _():
        fetch_tile(0, 0)
        unsort_ref[0] = _unsort_weights(meta_ref[...], lstart_row_ref[...], nrows)

    @pl.when(i + 1 < n)
    def _():
        fetch_tile(i + 1, 1 - slot)

    pltpu.make_async_copy(ys_ref.at[pl.ds(0, nrows), :], rows_ref.at[slot],
                          sems.at[slot]).wait()
    moe = jnp.dot(unsort_ref[slot], rows_ref[slot].astype(BF16),
                  preferred_element_type=F32)
    x2 = x1_ref[...] + moe
    u = _rms(x2, gple_ref[...])
    gate_ple = jax.nn.sigmoid(jnp.dot(u.astype(BF16), wpg_ref[...],
                                      preferred_element_type=F32))
    emb = jnp.dot(p_ref[...].astype(BF16), wple_ref[...], preferred_element_type=F32)
    x3 = x2 + gate_ple * emb
    out_ref[...] = _rms(x3, gfin_ref[...])

    unsort_ref[1 - slot] = _unsort_weights(meta_next_ref[...], lstart_row_next_ref[...], nrows)


def _combine(meta, tables, x1, p2, ys, g_ple, w_ple_gate_bf, w_ple_bf, g_final, tm):
    t = x1.shape[0]
    full = lambda *shape: pl.BlockSpec(shape, lambda i, *_: (0,) * len(shape))
    last = t // tm - 1
    nxt = lambda i: jnp.minimum(i + 1, last)
    grid_spec = pltpu.PrefetchScalarGridSpec(
        num_scalar_prefetch=3,
        grid=(t // tm,),
        in_specs=[
            pl.BlockSpec((None, META_ROWS, tm), lambda i, *_: (i, 0, 0)),
            pl.BlockSpec((None, 1, N_EXPERTS), lambda i, *_: (i, 0, 0)),
            pl.BlockSpec((None, META_ROWS, tm), lambda i, *_: (nxt(i), 0, 0)),
            pl.BlockSpec((None, 1, N_EXPERTS), lambda i, *_: (nxt(i), 0, 0)),
            pl.BlockSpec((tm, D_MODEL), lambda i, *_: (i, 0)),
            pl.BlockSpec((tm, PLE_DIM), lambda i, *_: (i, 0)),
            pl.BlockSpec(memory_space=pl.ANY),
            full(1, D_MODEL),
            full(D_MODEL, D_MODEL),
            full(PLE_DIM, D_MODEL),
            full(1, D_MODEL),
        ],
        out_specs=pl.BlockSpec((tm, D_MODEL), lambda i, *_: (i, 0)),
        scratch_shapes=[pltpu.VMEM((2, _local_rows(tm), D_MODEL), ROW_DTYPE),
                        pltpu.VMEM((2, tm, _local_rows(tm)), BF16),
                        pltpu.SemaphoreType.DMA((2,))],
    )
    return pl.pallas_call(
        _combine_kernel,
        out_shape=jax.ShapeDtypeStruct((t, D_MODEL), F32),
        grid_spec=grid_spec,
        compiler_params=pltpu.CompilerParams(
            dimension_semantics=("arbitrary",),
            vmem_limit_bytes=VMEM_LIMIT_BYTES),
        name="combine",
    )(tables["base_in"], tables["lstart"], tables["cnt"], meta, tables["lstart_rows"],
      meta, tables["lstart_rows"], x1, p2, ys, g_ple, w_ple_gate_bf, w_ple_bf, g_final)


def _tiles(seq_len):
    return dict(
        proj_rows=min(512, seq_len),
        ret_block=min(256, seq_len),
        route_rows=min(512, seq_len),
        expert_rows=512,
    )


def _slot_tables(tile_counts, blk, n_buf, nloc):
    cnt = tile_counts[:, :, 0].astype(jnp.int32)
    cnt = ((cnt + STRIP_ALIGN - 1) // STRIP_ALIGN) * STRIP_ALIGN
    seg_rows = jnp.sum(cnt, axis=1)
    counts = jnp.sum(cnt, axis=0)
    padded = ((counts + blk - 1) // blk) * blk
    ends_pad = jnp.cumsum(padded)
    starts_pad = ends_pad - padded
    tile_before = jnp.cumsum(cnt, axis=0) - cnt
    lstart = jnp.cumsum(cnt, axis=1) - cnt
    base = starts_pad[None, :] + tile_before
    with_filler = lambda a, col: jnp.concatenate([a, col[:, None]], axis=1).reshape(-1)
    blk_start = jnp.arange((n_buf + 2 * FILLER_ROWS) // blk, dtype=jnp.int32) * blk
    last_of_expert = jnp.any((padded > 0)[None, :]
                             & (blk_start[:, None] == (ends_pad - blk)[None, :]), axis=1)
    zflag = jnp.where(last_of_expert | (blk_start >= n_buf), ZERO_FIRST,
                      jnp.where(blk_start >= ends_pad[-1], ZERO_ANYTIME, 0)).astype(jnp.int32)
    filler_dst = n_buf + (jnp.arange(cnt.shape[0], dtype=jnp.int32) % 2) * FILLER_ROWS
    return dict(
        cnt=with_filler(cnt, nloc - seg_rows),
        lstart=with_filler(lstart, seg_rows),
        base_out=with_filler(base, filler_dst),
        base_in=with_filler(base, jnp.zeros_like(seg_rows)),
        lstart_rows=lstart.astype(F32)[:, None, :],
        lstart_cols=lstart.astype(F32)[:, :, None],
        zflag=zflag, ends_pad=ends_pad)


def _layer(x2, p2, seq_len, g_mix, w_in, w_pool, pool_scale, w_ret_o, w_out, g_ffn,
           w_router, b_router, w_gate_up, b_gate_up, w_down, b_down, g_ple,
           w_ple_gate, w_ple, g_out):
    t = x2.shape[0]
    cfg = _tiles(seq_len)
    row = lambda a: a.reshape(1, -1)

    side, o_gated = _proj_retention(x2.reshape(t // seq_len, seq_len, D_MODEL), row(g_mix),
                                    w_in.astype(BF16), cfg["proj_rows"], cfg["ret_block"])
    side = side.reshape(t, -1)
    o_gated = o_gated.reshape(t, RET_V_W)

    lane_pad = ((0, 0), (0, LANES - N_EXPERTS))
    wr_hi = w_router.astype(BF16)
    wr_lo = jnp.pad((w_router - wr_hi.astype(F32)).astype(BF16), lane_pad)
    wr_hi = jnp.pad(wr_hi, lane_pad)
    tm = cfg["route_rows"]
    x1, h2, meta, tile_counts = _mix_route(
        side, o_gated, x2, w_pool.astype(BF16), row(pool_scale), w_ret_o.astype(BF16),
        w_out.astype(BF16), row(g_ffn), wr_hi, wr_lo, jnp.pad(row(b_router), lane_pad), seq_len, tm)

    blk = cfg["expert_rows"]
    n_buf = t * TOP_K + (t // tm) * FILLER_ROWS + N_EXPERTS * blk
    n_blocks = n_buf // blk
    tables = _slot_tables(tile_counts, blk, n_buf, _local_rows(tm))
    blk_start = jnp.arange(n_blocks, dtype=jnp.int32) * blk
    block_e = jnp.minimum(
        jnp.sum((tables["ends_pad"][None, :] <= blk_start[:, None]).astype(jnp.int32), axis=1),
        N_EXPERTS - 1)
    n_used = tables["ends_pad"][-1:] // blk

    xs = _dispatch(h2, meta, tables, n_buf, blk, tm)
    ys = _experts(xs, n_buf, block_e, n_used, w_gate_up, b_gate_up, w_down, b_down, blk)
    return _combine(meta, tables, x1, p2, ys, row(g_ple), w_ple_gate.astype(BF16),
                    w_ple.astype(BF16), row(g_out), tm)


def kernel(x, p, g_mix, w_in, w_pool, pool_scale, w_ret_o, w_out, g_ffn, w_router,
           b_router, w_gate_up, b_gate_up, w_down, b_down, g_ple, w_ple_gate, w_ple,
           g_final):
    b, s, d = x.shape
    depth = p.shape[0]
    assert depth == 1 and d == D_MODEL
    x2 = x.reshape(b * s, d)
    out = _layer(x2, p[0].reshape(b * s, PLE_DIM), s, g_mix[0], w_in[0], w_pool[0],
                 pool_scale[0], w_ret_o[0], w_out[0], g_ffn[0], w_router[0],
                 b_router[0], w_gate_up[0], b_gate_up[0], w_down[0], b_down[0],
                 g_ple[0], w_ple_gate[0], w_ple[0], g_final)
    return out.reshape(b, s, d)
```

```python
import functools

import numpy as np
import jax
import jax.numpy as jnp
from jax import lax
from jax.experimental import pallas as pl
from jax.experimental.pallas import tpu as pltpu

F32 = jnp.float32
BF16 = jnp.bfloat16

D_MODEL = 1024
EPS = 1e-6
CHUNK = 64
PLE_DIM = 256
POOL_WINDOWS = (2, 4, 8, 16)
POOL_GROUP_W = D_MODEL // len(POOL_WINDOWS)
POOL_HALO = 16
RET_HEADS = 4
RET_QK_HEAD = 256
RET_V_HEAD = 512
RET_V_W = RET_HEADS * RET_V_HEAD
ROPE_BASE = 10000.0
IN_W = 9 * D_MODEL
N_EXPERTS = 32
TOP_K = 4
D_FF = D_MODEL
SWIGLU_ALPHA = 1.702
SWIGLU_LIMIT = 7.0

VMEM_LIMIT_BYTES = 56 * 1024 * 1024
LANES = 128
META_ROWS = 16
META_EXPERT, META_RANK, META_GATE = 0, TOP_K, 2 * TOP_K
ROW_DTYPE = F32
SUBLANES = 8
STRIP_ALIGN = SUBLANES * 4 // jnp.dtype(ROW_DTYPE).itemsize
FILLER_ROWS = N_EXPERTS * STRIP_ALIGN
N_SEG = N_EXPERTS + 1
ZERO_FIRST, ZERO_ANYTIME = 1, 2


def _rms(x, g):
    return x * lax.rsqrt(jnp.mean(x * x, axis=-1, keepdims=True) + EPS) * g


COL_U, COL_Q, COL_K = 0, 1, 2
COL_V, COL_G = (3, 4), (5, 6)
COL_GATE_A, COL_GATE_B = 7, 8
SIDE_U, SIDE_GATE_A, SIDE_GATE_B = 0, 1, 2


def _rotary_heads(y, cos, sin):
    half = RET_QK_HEAD // 2
    parts = []
    for h in range(RET_HEADS):
        x1 = y[:, h * RET_QK_HEAD:h * RET_QK_HEAD + half]
        x2 = y[:, h * RET_QK_HEAD + half:(h + 1) * RET_QK_HEAD]
        parts += [x1 * cos - x2 * sin, x2 * cos + x1 * sin]
    return jnp.concatenate(parts, axis=-1)


def _head_decay_logs():
    return [float(np.log(1.0 - 2.0 ** (-5.0 - h))) for h in range(RET_HEADS)]


def _token_mixer_kernel(blk_decay, ret_blk, x_ref, g_ref, w_ref, cos_ref, sin_ref,
                        d_ref, qd_ref, kd_ref,
                        wpool_ref, pscale_ref, wreto_ref, wout_ref, gffn_ref,
                        wr_hi_ref, wr_lo_ref, br_ref,
                        x1_ref, h2_ref, meta_ref, counts_ref,
                        side_ref, o_ref, q_s, k_s, v_s, g_s, state_ref, halo_ref):
    @pl.when(pl.program_id(1) == 0)
    def _():
        state_ref[...] = jnp.zeros_like(state_ref)
        halo_ref[...] = jnp.zeros_like(halo_ref)

    tm = x_ref.shape[0]
    h_in = _rms(x_ref[...], g_ref[...]).astype(BF16)

    def proj(c):
        return jnp.dot(h_in, w_ref[:, c * D_MODEL:(c + 1) * D_MODEL],
                       preferred_element_type=F32)

    def chunk(n):
        return slice(n * D_MODEL, (n + 1) * D_MODEL)

    side_ref[:, chunk(SIDE_U)] = proj(COL_U).astype(BF16)
    q_s[...] = _rotary_heads(proj(COL_Q), cos_ref[...], sin_ref[...]).astype(BF16)
    k_s[...] = (_rotary_heads(proj(COL_K), cos_ref[...], sin_ref[...])
                * (RET_QK_HEAD ** -0.5)).astype(BF16)
    for n, c in enumerate(COL_V):
        v_s[:, chunk(n)] = proj(c).astype(BF16)
    for n, c in enumerate(COL_G):
        y = proj(c)
        g_s[:, chunk(n)] = (y * jax.nn.sigmoid(y)).astype(BF16)
    side_ref[:, chunk(SIDE_GATE_A)] = proj(COL_GATE_A).astype(BF16)
    side_ref[:, chunk(SIDE_GATE_B)] = proj(COL_GATE_B).astype(BF16)

    for b in range(tm // ret_blk):
        rows = slice(b * ret_blk, (b + 1) * ret_blk)
        for h in range(RET_HEADS):
            qk_cols = slice(h * RET_QK_HEAD, (h + 1) * RET_QK_HEAD)
            v_cols = slice(h * RET_V_HEAD, (h + 1) * RET_V_HEAD)
            q = q_s[rows, qk_cols]
            k = k_s[rows, qk_cols]
            v = v_s[rows, v_cols]
            scores = lax.dot_general(q, k, (((1,), (1,)), ((), ())),
                                     preferred_element_type=F32) * d_ref[h]
            state = state_ref[h]
            o = jnp.dot(scores.astype(BF16), v, preferred_element_type=F32)
            o = o + jnp.dot((q.astype(F32) * qd_ref[h]).astype(BF16), state.astype(BF16),
                            preferred_element_type=F32)
            k_dec = (k.astype(F32) * kd_ref[h]).astype(BF16)
            state_ref[h] = state * blk_decay[h] + lax.dot_general(
                k_dec, v, (((0,), (0,)), ((), ())), preferred_element_type=F32)
            o = o * lax.rsqrt(jnp.mean(o * o, axis=-1, keepdims=True) + EPS)
            o_ref[rows, v_cols] = (o * g_s[rows, v_cols].astype(F32)).astype(BF16)

    u_bf = side_ref[:, chunk(SIDE_U)]
    _mix_route(pl.program_id(1) * tm, u_bf, halo_ref[...], side_ref[:, chunk(SIDE_GATE_A)],
               side_ref[:, chunk(SIDE_GATE_B)], o_ref[...], x_ref[...],
               wpool_ref, pscale_ref, wreto_ref, wout_ref, gffn_ref,
               wr_hi_ref, wr_lo_ref, br_ref, x1_ref, h2_ref, meta_ref, counts_ref)
    halo_ref[...] = u_bf[tm - POOL_HALO:, :]


def _rotary_tables(s):
    half = RET_QK_HEAD // 2
    pos = jnp.arange(s, dtype=F32)
    inv = ROPE_BASE ** (-jnp.linspace(0.0, 1.0, half, dtype=F32))
    ang = pos[:, None] * inv[None, :]
    return jnp.cos(ang), jnp.sin(ang)


def _retention_tables(blk):
    log_g = jnp.asarray(_head_decay_logs(), F32)
    idx = jnp.arange(blk, dtype=F32)
    diff = idx[:, None] - idx[None, :]
    chunk = jnp.arange(blk, dtype=jnp.int32) // CHUNK
    visible = chunk[None, :] <= chunk[:, None]
    dmask = jnp.where(visible[None], jnp.exp(log_g[:, None, None] * jnp.abs(diff)[None]), 0.0)
    q_dec = jnp.exp(log_g[:, None] * (idx + 1.0))[:, :, None]
    k_dec = jnp.exp(log_g[:, None] * (blk - 1.0 - idx))[:, :, None]
    q_dec = jnp.broadcast_to(q_dec, (RET_HEADS, blk, RET_QK_HEAD))
    k_dec = jnp.broadcast_to(k_dec, (RET_HEADS, blk, RET_QK_HEAD))
    return dmask.astype(F32), q_dec, k_dec


def _token_mixer(x3, g_mix, w_in_bf, w_pool_bf, pool_scale, w_ret_o_bf, w_out_bf,
                 g_ffn, wr_hi, wr_lo, b_router, tm, ret_blk):
    b, s, _ = x3.shape
    half = RET_QK_HEAD // 2
    tiles_per_seq = s // tm
    n_tiles = b * tiles_per_seq
    cos, sin = _rotary_tables(s)
    dmask, q_dec, k_dec = _retention_tables(ret_blk)
    blk_decay = [float(np.exp(lg * ret_blk)) for lg in _head_decay_logs()]
    once = lambda *shape: pl.BlockSpec(shape, lambda bi, j: (0,) * len(shape),
                                       pipeline_mode=pl.Buffered(1))
    tile = lambda bi, j: bi * tiles_per_seq + j
    return pl.pallas_call(
        functools.partial(_token_mixer_kernel, blk_decay, ret_blk),
        out_shape=(
            jax.ShapeDtypeStruct((b, s, D_MODEL), F32),
            jax.ShapeDtypeStruct((b, s, D_MODEL), BF16),
            jax.ShapeDtypeStruct((n_tiles, META_ROWS, tm), F32),
            jax.ShapeDtypeStruct((n_tiles, N_EXPERTS, 1), F32),
        ),
        grid=(b, tiles_per_seq),
        in_specs=[
            pl.BlockSpec((None, tm, D_MODEL), lambda bi, j: (bi, j, 0)),
            pl.BlockSpec((1, D_MODEL), lambda bi, j: (0, 0)),
            once(D_MODEL, IN_W),
            pl.BlockSpec((tm, half), lambda bi, j: (j, 0)),
            pl.BlockSpec((tm, half), lambda bi, j: (j, 0)),
            once(RET_HEADS, ret_blk, ret_blk),
            once(RET_HEADS, ret_blk, RET_QK_HEAD),
            once(RET_HEADS, ret_blk, RET_QK_HEAD),
            once(len(POOL_WINDOWS), POOL_GROUP_W, POOL_GROUP_W),
            once(1, D_MODEL),
            once(RET_V_W, D_MODEL),
            once(D_MODEL, D_MODEL),
            once(1, D_MODEL),
            once(D_MODEL, LANES),
            once(D_MODEL, LANES),
            once(1, LANES),
        ],
        out_specs=(
            pl.BlockSpec((None, tm, D_MODEL), lambda bi, j: (bi, j, 0)),
            pl.BlockSpec((None, tm, D_MODEL), lambda bi, j: (bi, j, 0)),
            pl.BlockSpec((None, META_ROWS, tm), lambda bi, j: (tile(bi, j), 0, 0)),
            pl.BlockSpec((None, N_EXPERTS, 1), lambda bi, j: (tile(bi, j), 0, 0)),
        ),
        scratch_shapes=[pltpu.VMEM((tm, 3 * D_MODEL), BF16),
                        pltpu.VMEM((tm, RET_V_W), BF16),
                        pltpu.VMEM((tm, D_MODEL), BF16),
                        pltpu.VMEM((tm, D_MODEL), BF16),
                        pltpu.VMEM((tm, RET_V_W), BF16),
                        pltpu.VMEM((tm, RET_V_W), BF16),
                        pltpu.VMEM((RET_HEADS, RET_QK_HEAD, RET_V_HEAD), F32),
                        pltpu.VMEM((POOL_HALO, D_MODEL), BF16)],
        compiler_params=pltpu.CompilerParams(
            dimension_semantics=("arbitrary", "arbitrary"),
            vmem_limit_bytes=VMEM_LIMIT_BYTES),
        name="token_mixer",
    )(x3, g_mix, w_in_bf, cos, sin, dmask, q_dec, k_dec, w_pool_bf, pool_scale,
      w_ret_o_bf, w_out_bf, g_ffn, wr_hi, wr_lo, b_router)


def _window_sum(ext, w, tm):
    cur = ext
    span = 1
    while span < w:
        cur = cur[span:, :] + cur[:-span, :]
        span *= 2
    start = POOL_HALO + 1 - w
    return cur[start:start + tm, :]


def _mix_route(pos0, u_bf, halo_bf, gate_a, gate_b, o_gated, x,
               wpool_ref, pscale_ref, wreto_ref, wout_ref, gffn_ref,
               wr_hi_ref, wr_lo_ref, br_ref,
               x1_ref, h2_ref, meta_ref, counts_ref):
    tm = x.shape[0]

    u = u_bf.astype(F32)
    ext = jnp.concatenate([halo_bf.astype(F32), u], axis=0)
    pos = (pos0 + lax.broadcasted_iota(jnp.int32, (tm, 1), 0)).astype(F32)
    pooled_out = []
    for g, w in enumerate(POOL_WINDOWS):
        cols = slice(g * POOL_GROUP_W, (g + 1) * POOL_GROUP_W)
        ws = _window_sum(ext[:, cols], w, tm)
        count = jnp.minimum(pos + 1.0, float(w))
        pooled = ws / count - u[:, cols]
        pooled_out.append(jnp.dot(pooled.astype(BF16), wpool_ref[g],
                                  preferred_element_type=F32))
    y_pool = jnp.concatenate(pooled_out, axis=-1) * pscale_ref[...]

    y_ret = jnp.dot(o_gated, wreto_ref[...], preferred_element_type=F32)
    merged = (jax.nn.sigmoid(gate_a.astype(F32)) * y_pool
              + jax.nn.sigmoid(gate_b.astype(F32)) * y_ret)
    x1 = x + jnp.dot(merged.astype(BF16), wout_ref[...], preferred_element_type=F32)
    x1_ref[...] = x1
    h2 = _rms(x1, gffn_ref[...])
    h2_ref[...] = h2.astype(BF16)

    h_hi = h2.astype(BF16)
    h_lo = (h2 - h_hi.astype(F32)).astype(BF16)
    logits = (jnp.dot(h_hi, wr_hi_ref[...], preferred_element_type=F32)
              + jnp.dot(h_lo, wr_hi_ref[...], preferred_element_type=F32)
              + jnp.dot(h_hi, wr_lo_ref[...], preferred_element_type=F32)
              + br_ref[...])
    logits_t = logits.T[:N_EXPERTS, :]

    sub = lax.broadcasted_iota(jnp.int32, (N_EXPERTS, tm), 0)
    work = logits_t
    vals, idxs, hots = [], [], []
    for _ in range(TOP_K):
        m = jnp.max(work, axis=0, keepdims=True)
        idx = jnp.min(jnp.where(work == m, sub, N_EXPERTS), axis=0, keepdims=True)
        hot = sub == idx
        vals.append(m)
        idxs.append(idx)
        hots.append(hot)
        work = jnp.where(hot, -jnp.inf, work)
    exps = [jnp.exp(v - vals[0]) for v in vals]
    denom = exps[0] + exps[1] + exps[2] + exps[3]
    gates = [e / denom for e in exps]

    sel = (jnp.where(hots[0], 1.0, 0.0) + jnp.where(hots[1], 1.0, 0.0)
           + jnp.where(hots[2], 1.0, 0.0) + jnp.where(hots[3], 1.0, 0.0))
    row = lax.broadcasted_iota(jnp.int32, (tm, tm), 0)
    col = lax.broadcasted_iota(jnp.int32, (tm, tm), 1)
    earlier = jnp.where(row < col, 1.0, 0.0).astype(BF16)
    before = jnp.dot(sel.astype(BF16), earlier, preferred_element_type=F32)
    counts_ref[...] = jnp.sum(sel, axis=1, keepdims=True)

    msub = lax.broadcasted_iota(jnp.int32, (META_ROWS, tm), 0)
    meta = jnp.zeros((META_ROWS, tm), F32)
    for r in range(TOP_K):
        rank_r = jnp.sum(jnp.where(hots[r], before, 0.0), axis=0, keepdims=True)
        meta = jnp.where(msub == META_EXPERT + r, idxs[r].astype(F32), meta)
        meta = jnp.where(msub == META_RANK + r, rank_r, meta)
        meta = jnp.where(msub == META_GATE + r, gates[r], meta)
    meta_ref[...] = meta


def _local_positions(meta, lstart):
    tm = meta.shape[0]
    lane_e = lax.broadcasted_iota(jnp.int32, (tm, N_EXPERTS), 1)
    out = []
    for r in range(TOP_K):
        e_r = meta[:, META_EXPERT + r:META_EXPERT + r + 1].astype(jnp.int32)
        first = jnp.sum(jnp.where(lane_e == e_r, lstart, 0.0), axis=-1, keepdims=True)
        out.append((meta[:, META_RANK + r:META_RANK + r + 1] + first).astype(jnp.int32))
    return out


def _local_rows(tm):
    return TOP_K * tm + FILLER_ROWS


def _for_each_strip(count, max_rows, start_copy):
    top = 1 << (max(max_rows, FILLER_ROWS).bit_length() - 1)
    pieces = [top >> k for k in range(top.bit_length()) if (top >> k) >= STRIP_ALIGN]
    rare_from = 2 * max_rows * TOP_K // N_EXPERTS
    rare = [p for p in pieces if p >= rare_from]
    rare_bits = sum(rare)

    def cover(sizes, off):
        for piece in sizes:
            take = count & piece

            @pl.when(take != 0)
            def _(off=off, piece=piece):
                start_copy(off, piece)

            off = off + take

    @pl.when((count & rare_bits) != 0)
    def _():
        cover(rare, jnp.int32(0))

    cover([p for p in pieces if p < rare_from], count & rare_bits)


def _dispatch_kernel(blk_rows, base_ref, lstart_ref, cnt_ref, zflag_ref,
                     h_ref, meta_ref, lstart_col_ref, xs_ref,
                     sorted_ref, zero_ref, sems, zsem, tail_sem):
    i = pl.program_id(0)
    tm = h_ref.shape[0]
    nrows = _local_rows(tm)
    slot = lax.rem(i, 2)

    n_zero_blocks = xs_ref.shape[0] // blk_rows

    def zcopy(b, sem):
        start = pl.multiple_of(b * blk_rows, blk_rows)
        return pltpu.make_async_copy(zero_ref, xs_ref.at[pl.ds(start, blk_rows), :], sem)

    def for_flagged(flag, sem, act):
        def body(b, c):
            @pl.when(zflag_ref[b] == flag)
            def _():
                act(zcopy(b, sem))
            return c
        lax.fori_loop(0, n_zero_blocks, body, 0)

    @pl.when(i == 0)
    def _():
        zero_ref[...] = jnp.zeros_like(zero_ref)
        for_flagged(ZERO_FIRST, zsem, lambda cp: cp.start())
        for_flagged(ZERO_ANYTIME, tail_sem, lambda cp: cp.start())
        for_flagged(ZERO_FIRST, zsem, lambda cp: cp.wait())

    meta_t = meta_ref[...]
    sub_e = lax.broadcasted_iota(jnp.int32, (N_EXPERTS, tm), 0)
    pos = []
    for r in range(TOP_K):
        e_r = meta_t[META_EXPERT + r:META_EXPERT + r + 1, :].astype(jnp.int32)
        first = jnp.sum(jnp.where(sub_e == e_r, lstart_col_ref[...], 0.0), axis=0, keepdims=True)
        pos.append((meta_t[META_RANK + r:META_RANK + r + 1, :] + first).astype(jnp.int32))
    jr = lax.broadcasted_iota(jnp.int32, (nrows, tm), 0)
    onehot = jnp.where(jr == pos[0], 1.0, jnp.where(jr == pos[1], 1.0,
             jnp.where(jr == pos[2], 1.0, jnp.where(jr == pos[3], 1.0, 0.0))))
    srt = jnp.dot(onehot.astype(BF16), h_ref[...], preferred_element_type=F32)
    sorted_ref[slot] = srt.astype(ROW_DTYPE)

    def per_segment(e, c):
        k = i * N_SEG + e
        src0 = lstart_ref[k]
        dst0 = base_ref[k]

        def start_copy(off, rows):
            src = pl.multiple_of(src0 + off, STRIP_ALIGN)
            dst = pl.multiple_of(dst0 + off, STRIP_ALIGN)
            pltpu.make_async_copy(sorted_ref.at[slot, pl.ds(src, rows), :],
                                  xs_ref.at[pl.ds(dst, rows), :],
                                  sems.at[slot]).start()

        _for_each_strip(cnt_ref[k], tm, start_copy)
        return c

    lax.fori_loop(0, N_SEG, per_segment, 0)

    def wait_tile(s):
        pltpu.make_async_copy(sorted_ref.at[s], xs_ref.at[pl.ds(0, nrows), :],
                              sems.at[s]).wait()

    @pl.when(i >= 1)
    def _():
        wait_tile(1 - slot)

    @pl.when(i == pl.num_programs(0) - 1)
    def _():
        wait_tile(slot)
        for_flagged(ZERO_ANYTIME, tail_sem, lambda cp: cp.wait())


def _dispatch(h2, meta, tables, n_buf, blk_rows, tm):
    t = h2.shape[0]
    grid_spec = pltpu.PrefetchScalarGridSpec(
        num_scalar_prefetch=4,
        grid=(t // tm,),
        in_specs=[
            pl.BlockSpec((tm, D_MODEL), lambda i, *_: (i, 0)),
            pl.BlockSpec((None, META_ROWS, tm), lambda i, *_: (i, 0, 0)),
            pl.BlockSpec((None, N_EXPERTS, 1), lambda i, *_: (i, 0, 0)),
        ],
        out_specs=pl.BlockSpec(memory_space=pl.ANY),
        scratch_shapes=[pltpu.VMEM((2, _local_rows(tm), D_MODEL), ROW_DTYPE),
                        pltpu.VMEM((blk_rows, D_MODEL), ROW_DTYPE),
                        pltpu.SemaphoreType.DMA((2,)),
                        pltpu.SemaphoreType.DMA(()),
                        pltpu.SemaphoreType.DMA(())],
    )
    return pl.pallas_call(
        functools.partial(_dispatch_kernel, blk_rows),
        out_shape=jax.ShapeDtypeStruct((n_buf + 2 * FILLER_ROWS, D_MODEL), ROW_DTYPE),
        grid_spec=grid_spec,
        compiler_params=pltpu.CompilerParams(
            dimension_semantics=("arbitrary",),
            vmem_limit_bytes=VMEM_LIMIT_BYTES,
            has_side_effects=True),
        name="dispatch",
    )(tables["base_out"], tables["lstart"], tables["cnt"], tables["zflag"],
      h2, meta, tables["lstart_cols"])


def _experts_kernel(be_ref, nu_ref, x_ref, wgu_ref, bgu_ref, wd_ref, bd_ref,
                    y_ref, wgu_bf, wd_bf):
    i = pl.program_id(0)

    @pl.when(i < nu_ref[0])
    def _():
        prev = be_ref[jnp.maximum(i - 1, 0)]

        @pl.when((i == 0) | (be_ref[i] != prev))
        def _():
            wgu_bf[...] = wgu_ref[...].astype(BF16)
            wd_bf[...] = wd_ref[...].astype(BF16)

        gu = jnp.dot(x_ref[...].astype(BF16), wgu_bf[...],
                     preferred_element_type=F32) + bgu_ref[...]
        glu = jnp.minimum(gu[:, :D_FF], SWIGLU_LIMIT)
        lin = jnp.clip(gu[:, D_FF:], -SWIGLU_LIMIT, SWIGLU_LIMIT)
        act = glu * jax.nn.sigmoid(SWIGLU_ALPHA * glu) * (lin + 1.0)
        y_ref[...] = (jnp.dot(act.astype(BF16), wd_bf[...],
                              preferred_element_type=F32) + bd_ref[...]).astype(ROW_DTYPE)

    @pl.when(i >= nu_ref[0])
    def _():
        y_ref[...] = jnp.zeros_like(y_ref)


def _experts(xs, n_buf, block_e, n_used, w_gate_up, b_gate_up, w_down, b_down, blk_rows):
    used = lambda i, be, nu: jnp.minimum(i, nu[0] - 1)
    grid_spec = pltpu.PrefetchScalarGridSpec(
        num_scalar_prefetch=2,
        grid=(n_buf // blk_rows,),
        in_specs=[
            pl.BlockSpec((blk_rows, D_MODEL), lambda i, be, nu: (used(i, be, nu), 0)),
            pl.BlockSpec((None, D_MODEL, 2 * D_FF), lambda i, be, nu: (be[i], 0, 0)),
            pl.BlockSpec((None, 1, 2 * D_FF), lambda i, be, nu: (be[i], 0, 0)),
            pl.BlockSpec((None, D_FF, D_MODEL), lambda i, be, nu: (be[i], 0, 0)),
            pl.BlockSpec((None, 1, D_MODEL), lambda i, be, nu: (be[i], 0, 0)),
        ],
        out_specs=pl.BlockSpec((blk_rows, D_MODEL), lambda i, be, nu: (i, 0)),
        scratch_shapes=[pltpu.VMEM((D_MODEL, 2 * D_FF), BF16),
                        pltpu.VMEM((D_FF, D_MODEL), BF16)],
    )
    return pl.pallas_call(
        _experts_kernel,
        out_shape=jax.ShapeDtypeStruct((n_buf, D_MODEL), ROW_DTYPE),
        grid_spec=grid_spec,
        compiler_params=pltpu.CompilerParams(
            dimension_semantics=("arbitrary",),
            vmem_limit_bytes=VMEM_LIMIT_BYTES),
        name="experts",
    )(block_e, n_used, xs, w_gate_up, b_gate_up[:, None, :], w_down, b_down[:, None, :])


def _unsort_weights(meta_t, lstart_row, nrows):
    tm = meta_t.shape[1]
    meta = jnp.concatenate(
        [meta_t, jnp.zeros((LANES - META_ROWS, tm), F32)], axis=0).T
    pos = _local_positions(meta, lstart_row)
    jl = lax.broadcasted_iota(jnp.int32, (tm, nrows), 1)
    gate = [meta[:, META_GATE + r:META_GATE + r + 1] for r in range(TOP_K)]
    weights = jnp.where(jl == pos[0], gate[0], jnp.where(jl == pos[1], gate[1],
              jnp.where(jl == pos[2], gate[2], jnp.where(jl == pos[3], gate[3], 0.0))))
    return weights.astype(BF16)


def _combine_kernel(base_ref, lstart_ref, cnt_ref, meta_ref, lstart_row_ref,
                    meta_next_ref, lstart_row_next_ref, x1_ref,
                    p_ref, ys_ref, gple_ref, wpg_ref, wple_ref, gfin_ref, out_ref,
                    rows_ref, unsort_ref, sems):
    i = pl.program_id(0)
    n = pl.num_programs(0)
    tm = x1_ref.shape[0]
    nrows = _local_rows(tm)
    slot = lax.rem(i, 2)

    def fetch_tile(tile, s):
        def per_segment(e, c):
            k = tile * N_SEG + e
            src0 = base_ref[k]
            dst0 = lstart_ref[k]

            def start_copy(off, rows):
                src = pl.multiple_of(src0 + off, STRIP_ALIGN)
                dst = pl.multiple_of(dst0 + off, STRIP_ALIGN)
                pltpu.make_async_copy(ys_ref.at[pl.ds(src, rows), :],
                                      rows_ref.at[s, pl.ds(dst, rows), :],
                                      sems.at[s]).start()

            _for_each_strip(cnt_ref[k], tm, start_copy)
            return c

        lax.fori_loop(0, N_SEG, per_segment, 0)

    @pl.when(i == 0)
    def _():
        fetch_tile(0, 0)
        unsort_ref[0] = _unsort_weights(meta_ref[...], lstart_row_ref[...], nrows)

    @pl.when(i + 1 < n)
    def _():
        fetch_tile(i + 1, 1 - slot)

    pltpu.make_async_copy(ys_ref.at[pl.ds(0, nrows), :], rows_ref.at[slot],
                          sems.at[slot]).wait()
    moe = jnp.dot(unsort_ref[slot], rows_ref[slot].astype(BF16),
                  preferred_element_type=F32)
    x2 = x1_ref[...] + moe
    u = _rms(x2, gple_ref[...])
    gate_ple = jax.nn.sigmoid(jnp.dot(u.astype(BF16), wpg_ref[...],
                                      preferred_element_type=F32))
    emb = jnp.dot(p_ref[...].astype(BF16), wple_ref[...], preferred_element_type=F32)
    x3 = x2 + gate_ple * emb
    out_ref[...] = _rms(x3, gfin_ref[...])

    unsort_ref[1 - slot] = _unsort_weights(meta_next_ref[...], lstart_row_next_ref[...], nrows)


def _combine(meta, tables, x1, p2, ys, g_ple, w_ple_gate_bf, w_ple_bf, g_final, tm):
    t = x1.shape[0]
    full = lambda *shape: pl.BlockSpec(shape, lambda i, *_: (0,) * len(shape))
    last = t // tm - 1
    nxt = lambda i: jnp.minimum(i + 1, last)
    grid_spec = pltpu.PrefetchScalarGridSpec(
        num_scalar_prefetch=3,
        grid=(t // tm,),
        in_specs=[
            pl.BlockSpec((None, META_ROWS, tm), lambda i, *_: (i, 0, 0)),
            pl.BlockSpec((None, 1, N_EXPERTS), lambda i, *_: (i, 0, 0)),
            pl.BlockSpec((None, META_ROWS, tm), lambda i, *_: (nxt(i), 0, 0)),
            pl.BlockSpec((None, 1, N_EXPERTS), lambda i, *_: (nxt(i), 0, 0)),
            pl.BlockSpec((tm, D_MODEL), lambda i, *_: (i, 0)),
            pl.BlockSpec((tm, PLE_DIM), lambda i, *_: (i, 0)),
            pl.BlockSpec(memory_space=pl.ANY),
            full(1, D_MODEL),
            full(D_MODEL, D_MODEL),
            full(PLE_DIM, D_MODEL),
            full(1, D_MODEL),
        ],
        out_specs=pl.BlockSpec((tm, D_MODEL), lambda i, *_: (i, 0)),
        scratch_shapes=[pltpu.VMEM((2, _local_rows(tm), D_MODEL), ROW_DTYPE),
                        pltpu.VMEM((2, tm, _local_rows(tm)), BF16),
                        pltpu.SemaphoreType.DMA((2,))],
    )
    return pl.pallas_call(
        _combine_kernel,
        out_shape=jax.ShapeDtypeStruct((t, D_MODEL), F32),
        grid_spec=grid_spec,
        compiler_params=pltpu.CompilerParams(
            dimension_semantics=("arbitrary",),
            vmem_limit_bytes=VMEM_LIMIT_BYTES),
        name="combine",
    )(tables["base_in"], tables["lstart"], tables["cnt"], meta, tables["lstart_rows"],
      meta, tables["lstart_rows"], x1, p2, ys, g_ple, w_ple_gate_bf, w_ple_bf, g_final)


def _tiles(seq_len):
    return dict(
        proj_rows=min(512, seq_len),
        ret_block=min(256, seq_len),
        route_rows=min(512, seq_len),
        expert_rows=512,
    )


def _slot_tables(tile_counts, blk, n_buf, nloc):
    cnt = tile_counts[:, :, 0].astype(jnp.int32)
    cnt = ((cnt + STRIP_ALIGN - 1) // STRIP_ALIGN) * STRIP_ALIGN
    seg_rows = jnp.sum(cnt, axis=1)
    counts = jnp.sum(cnt, axis=0)
    padded = ((counts + blk - 1) // blk) * blk
    ends_pad = jnp.cumsum(padded)
    starts_pad = ends_pad - padded
    tile_before = jnp.cumsum(cnt, axis=0) - cnt
    lstart = jnp.cumsum(cnt, axis=1) - cnt
    base = starts_pad[None, :] + tile_before
    with_filler = lambda a, col: jnp.concatenate([a, col[:, None]], axis=1).reshape(-1)
    blk_start = jnp.arange((n_buf + 2 * FILLER_ROWS) // blk, dtype=jnp.int32) * blk
    last_of_expert = jnp.any((padded > 0)[None, :]
                             & (blk_start[:, None] == (ends_pad - blk)[None, :]), axis=1)
    zflag = jnp.where(last_of_expert | (blk_start >= n_buf), ZERO_FIRST,
                      jnp.where(blk_start >= ends_pad[-1], ZERO_ANYTIME, 0)).astype(jnp.int32)
    filler_dst = n_buf + (jnp.arange(cnt.shape[0], dtype=jnp.int32) % 2) * FILLER_ROWS
    return dict(
        cnt=with_filler(cnt, nloc - seg_rows),
        lstart=with_filler(lstart, seg_rows),
        base_out=with_filler(base, filler_dst),
        base_in=with_filler(base, jnp.zeros_like(seg_rows)),
        lstart_rows=lstart.astype(F32)[:, None, :],
        lstart_cols=lstart.astype(F32)[:, :, None],
        zflag=zflag, ends_pad=ends_pad)


def _layer(x2, p2, seq_len, g_mix, w_in, w_pool, pool_scale, w_ret_o, w_out, g_ffn,
           w_router, b_router, w_gate_up, b_gate_up, w_down, b_down, g_ple,
           w_ple_gate, w_ple, g_out):
    t = x2.shape[0]
    cfg = _tiles(seq_len)
    row = lambda a: a.reshape(1, -1)

    lane_pad = ((0, 0), (0, LANES - N_EXPERTS))
    wr_hi = w_router.astype(BF16)
    wr_lo = jnp.pad((w_router - wr_hi.astype(F32)).astype(BF16), lane_pad)
    wr_hi = jnp.pad(wr_hi, lane_pad)
    tm = cfg["route_rows"]
    x1, h2, meta, tile_counts = _token_mixer(
        x2.reshape(t // seq_len, seq_len, D_MODEL), row(g_mix), w_in.astype(BF16),
        w_pool.astype(BF16), row(pool_scale), w_ret_o.astype(BF16), w_out.astype(BF16),
        row(g_ffn), wr_hi, wr_lo, jnp.pad(row(b_router), lane_pad), tm, cfg["ret_block"])
    x1 = x1.reshape(t, D_MODEL)
    h2 = h2.reshape(t, D_MODEL)

    blk = cfg["expert_rows"]
    n_buf = t * TOP_K + (t // tm) * FILLER_ROWS + N_EXPERTS * blk
    n_blocks = n_buf // blk
    tables = _slot_tables(tile_counts, blk, n_buf, _local_rows(tm))
    blk_start = jnp.arange(n_blocks, dtype=jnp.int32) * blk
    block_e = jnp.minimum(
        jnp.sum((tables["ends_pad"][None, :] <= blk_start[:, None]).astype(jnp.int32), axis=1),
        N_EXPERTS - 1)
    n_used = tables["ends_pad"][-1:] // blk

    xs = _dispatch(h2, meta, tables, n_buf, blk, tm)
    ys = _experts(xs, n_buf, block_e, n_used, w_gate_up, b_gate_up, w_down, b_down, blk)
    return _combine(meta, tables, x1, p2, ys, row(g_ple), w_ple_gate.astype(BF16),
                    w_ple.astype(BF16), row(g_out), tm)


def kernel(x, p, g_mix, w_in, w_pool, pool_scale, w_ret_o, w_out, g_ffn, w_router,
           b_router, w_gate_up, b_gate_up, w_down, b_down, g_ple, w_ple_gate, w_ple,
           g_final):
    b, s, d = x.shape
    depth = p.shape[0]
    assert depth == 1 and d == D_MODEL
    x2 = x.reshape(b * s, d)
    out = _layer(x2, p[0].reshape(b * s, PLE_DIM), s, g_mix[0], w_in[0], w_pool[0],
                 pool_scale[0], w_ret_o[0], w_out[0], g_ffn[0], w_router[0],
                 b_router[0], w_gate_up[0], b_gate_up[0], w_down[0], b_down[0],
                 g_ple[0], w_ple_gate[0], w_ple[0], g_final)
    return out.reshape(b, s, d)
```

```python
import functools

import numpy as np
import jax
import jax.numpy as jnp
from jax import lax
from jax.experimental import pallas as pl
from jax.experimental.pallas import tpu as pltpu

F32 = jnp.float32
BF16 = jnp.bfloat16

D_MODEL = 1024
EPS = 1e-6
CHUNK = 64
PLE_DIM = 256
POOL_WINDOWS = (2, 4, 8, 16)
POOL_GROUP_W = D_MODEL // len(POOL_WINDOWS)
POOL_HALO = 16
RET_HEADS = 4
RET_QK_HEAD = 256
RET_V_HEAD = 512
RET_V_W = RET_HEADS * RET_V_HEAD
ROPE_BASE = 10000.0
IN_W = 9 * D_MODEL
N_EXPERTS = 32
TOP_K = 4
D_FF = D_MODEL
SWIGLU_ALPHA = 1.702
SWIGLU_LIMIT = 7.0

VMEM_LIMIT_BYTES = 56 * 1024 * 1024
LANES = 128
META_ROWS = 16
META_EXPERT, META_RANK, META_GATE = 0, TOP_K, 2 * TOP_K
ROW_DTYPE = F32
SUBLANES = 8
STRIP_ALIGN = SUBLANES * 4 // jnp.dtype(ROW_DTYPE).itemsize
FILLER_ROWS = N_EXPERTS * STRIP_ALIGN
N_SEG = N_EXPERTS + 1
ZERO_FIRST, ZERO_ANYTIME = 1, 2
DISPATCH_SLOTS = 3
SPARE_ROWS = 4 * FILLER_ROWS


def _rms(x, g):
    return x * lax.rsqrt(jnp.mean(x * x, axis=-1, keepdims=True) + EPS) * g


COL_U, COL_Q, COL_K = 0, 1, 2
COL_V, COL_G = (3, 4), (5, 6)
COL_GATE_A, COL_GATE_B = 7, 8
SIDE_U, SIDE_GATE_A, SIDE_GATE_B = 0, 1, 2


def _rotary_heads(y, cos, sin):
    half = RET_QK_HEAD // 2
    parts = []
    for h in range(RET_HEADS):
        x1 = y[:, h * RET_QK_HEAD:h * RET_QK_HEAD + half]
        x2 = y[:, h * RET_QK_HEAD + half:(h + 1) * RET_QK_HEAD]
        parts += [x1 * cos - x2 * sin, x2 * cos + x1 * sin]
    return jnp.concatenate(parts, axis=-1)


def _head_decay_logs():
    return [float(np.log(1.0 - 2.0 ** (-5.0 - h))) for h in range(RET_HEADS)]


def _token_mixer_kernel(blk_decay, ret_blk, x_ref, g_ref, w_ref, cos_ref, sin_ref,
                        d_ref, qd_ref, kd_ref,
                        wpool_ref, pscale_ref, wreto_ref, wout_ref, gffn_ref,
                        wr_hi_ref, wr_lo_ref, br_ref,
                        x1_ref, h2_ref, meta_ref, counts_ref,
                        side_ref, o_ref, q_s, k_s, v_s, g_s, state_ref, halo_ref):
    @pl.when(pl.program_id(1) == 0)
    def _():
        state_ref[...] = jnp.zeros_like(state_ref)
        halo_ref[...] = jnp.zeros_like(halo_ref)

    tm = x_ref.shape[0]
    h_in = _rms(x_ref[...], g_ref[...]).astype(BF16)

    def proj(c):
        return jnp.dot(h_in, w_ref[:, c * D_MODEL:(c + 1) * D_MODEL],
                       preferred_element_type=F32)

    def chunk(n):
        return slice(n * D_MODEL, (n + 1) * D_MODEL)

    side_ref[:, chunk(SIDE_U)] = proj(COL_U).astype(BF16)
    q_s[...] = _rotary_heads(proj(COL_Q), cos_ref[...], sin_ref[...]).astype(BF16)
    k_s[...] = (_rotary_heads(proj(COL_K), cos_ref[...], sin_ref[...])
                * (RET_QK_HEAD ** -0.5)).astype(BF16)
    for n, c in enumerate(COL_V):
        v_s[:, chunk(n)] = proj(c).astype(BF16)
    for n, c in enumerate(COL_G):
        y = proj(c)
        g_s[:, chunk(n)] = (y * jax.nn.sigmoid(y)).astype(BF16)
    side_ref[:, chunk(SIDE_GATE_A)] = proj(COL_GATE_A).astype(BF16)
    side_ref[:, chunk(SIDE_GATE_B)] = proj(COL_GATE_B).astype(BF16)

    for b in range(tm // ret_blk):
        rows = slice(b * ret_blk, (b + 1) * ret_blk)
        for h in range(RET_HEADS):
            qk_cols = slice(h * RET_QK_HEAD, (h + 1) * RET_QK_HEAD)
            v_cols = slice(h * RET_V_HEAD, (h + 1) * RET_V_HEAD)
            q = q_s[rows, qk_cols]
            k = k_s[rows, qk_cols]
            v = v_s[rows, v_cols]
            scores = lax.dot_general(q, k, (((1,), (1,)), ((), ())),
                                     preferred_element_type=F32) * d_ref[h]
            state = state_ref[h]
            o = jnp.dot(scores.astype(BF16), v, preferred_element_type=F32)
            o = o + jnp.dot((q.astype(F32) * qd_ref[h]).astype(BF16), state.astype(BF16),
                            preferred_element_type=F32)
            k_dec = (k.astype(F32) * kd_ref[h]).astype(BF16)
            state_ref[h] = state * blk_decay[h] + lax.dot_general(
                k_dec, v, (((0,), (0,)), ((), ())), preferred_element_type=F32)
            o = o * lax.rsqrt(jnp.mean(o * o, axis=-1, keepdims=True) + EPS)
            o_ref[rows, v_cols] = (o * g_s[rows, v_cols].astype(F32)).astype(BF16)

    u_bf = side_ref[:, chunk(SIDE_U)]
    _mix_route(pl.program_id(1) * tm, u_bf, halo_ref[...], side_ref[:, chunk(SIDE_GATE_A)],
               side_ref[:, chunk(SIDE_GATE_B)], o_ref[...], x_ref[...],
               wpool_ref, pscale_ref, wreto_ref, wout_ref, gffn_ref,
               wr_hi_ref, wr_lo_ref, br_ref, x1_ref, h2_ref, meta_ref, counts_ref)
    halo_ref[...] = u_bf[tm - POOL_HALO:, :]


def _rotary_tables(s):
    half = RET_QK_HEAD // 2
    pos = jnp.arange(s, dtype=F32)
    inv = ROPE_BASE ** (-jnp.linspace(0.0, 1.0, half, dtype=F32))
    ang = pos[:, None] * inv[None, :]
    return jnp.cos(ang), jnp.sin(ang)


def _retention_tables(blk):
    log_g = jnp.asarray(_head_decay_logs(), F32)
    idx = jnp.arange(blk, dtype=F32)
    diff = idx[:, None] - idx[None, :]
    chunk = jnp.arange(blk, dtype=jnp.int32) // CHUNK
    visible = chunk[None, :] <= chunk[:, None]
    dmask = jnp.where(visible[None], jnp.exp(log_g[:, None, None] * jnp.abs(diff)[None]), 0.0)
    q_dec = jnp.exp(log_g[:, None] * (idx + 1.0))[:, :, None]
    k_dec = jnp.exp(log_g[:, None] * (blk - 1.0 - idx))[:, :, None]
    q_dec = jnp.broadcast_to(q_dec, (RET_HEADS, blk, RET_QK_HEAD))
    k_dec = jnp.broadcast_to(k_dec, (RET_HEADS, blk, RET_QK_HEAD))
    return dmask.astype(F32), q_dec, k_dec


def _token_mixer(x3, g_mix, w_in_bf, w_pool_bf, pool_scale, w_ret_o_bf, w_out_bf,
                 g_ffn, wr_hi, wr_lo, b_router, tm, ret_blk):
    b, s, _ = x3.shape
    half = RET_QK_HEAD // 2
    tiles_per_seq = s // tm
    n_tiles = b * tiles_per_seq
    cos, sin = _rotary_tables(s)
    dmask, q_dec, k_dec = _retention_tables(ret_blk)
    blk_decay = [float(np.exp(lg * ret_blk)) for lg in _head_decay_logs()]
    once = lambda *shape: pl.BlockSpec(shape, lambda bi, j: (0,) * len(shape),
                                       pipeline_mode=pl.Buffered(1))
    tile = lambda bi, j: bi * tiles_per_seq + j
    return pl.pallas_call(
        functools.partial(_token_mixer_kernel, blk_decay, ret_blk),
        out_shape=(
            jax.ShapeDtypeStruct((b, s, D_MODEL), F32),
            jax.ShapeDtypeStruct((b, s, D_MODEL), BF16),
            jax.ShapeDtypeStruct((n_tiles, META_ROWS, tm), F32),
            jax.ShapeDtypeStruct((n_tiles, N_EXPERTS, 1), F32),
        ),
        grid=(b, tiles_per_seq),
        in_specs=[
            pl.BlockSpec((None, tm, D_MODEL), lambda bi, j: (bi, j, 0)),
            pl.BlockSpec((1, D_MODEL), lambda bi, j: (0, 0)),
            once(D_MODEL, IN_W),
            pl.BlockSpec((tm, half), lambda bi, j: (j, 0)),
            pl.BlockSpec((tm, half), lambda bi, j: (j, 0)),
            once(RET_HEADS, ret_blk, ret_blk),
            once(RET_HEADS, ret_blk, RET_QK_HEAD),
            once(RET_HEADS, ret_blk, RET_QK_HEAD),
            once(len(POOL_WINDOWS), POOL_GROUP_W, POOL_GROUP_W),
            once(1, D_MODEL),
            once(RET_V_W, D_MODEL),
            once(D_MODEL, D_MODEL),
            once(1, D_MODEL),
            once(D_MODEL, LANES),
            once(D_MODEL, LANES),
            once(1, LANES),
        ],
        out_specs=(
            pl.BlockSpec((None, tm, D_MODEL), lambda bi, j: (bi, j, 0)),
            pl.BlockSpec((None, tm, D_MODEL), lambda bi, j: (bi, j, 0)),
            pl.BlockSpec((None, META_ROWS, tm), lambda bi, j: (tile(bi, j), 0, 0)),
            pl.BlockSpec((None, N_EXPERTS, 1), lambda bi, j: (tile(bi, j), 0, 0)),
        ),
        scratch_shapes=[pltpu.VMEM((tm, 3 * D_MODEL), BF16),
                        pltpu.VMEM((tm, RET_V_W), BF16),
                        pltpu.VMEM((tm, D_MODEL), BF16),
                        pltpu.VMEM((tm, D_MODEL), BF16),
                        pltpu.VMEM((tm, RET_V_W), BF16),
                        pltpu.VMEM((tm, RET_V_W), BF16),
                        pltpu.VMEM((RET_HEADS, RET_QK_HEAD, RET_V_HEAD), F32),
                        pltpu.VMEM((POOL_HALO, D_MODEL), BF16)],
        compiler_params=pltpu.CompilerParams(
            dimension_semantics=("arbitrary", "arbitrary"),
            vmem_limit_bytes=VMEM_LIMIT_BYTES),
        name="token_mixer",
    )(x3, g_mix, w_in_bf, cos, sin, dmask, q_dec, k_dec, w_pool_bf, pool_scale,
      w_ret_o_bf, w_out_bf, g_ffn, wr_hi, wr_lo, b_router)


def _window_sum(ext, w, tm):
    cur = ext
    span = 1
    while span < w:
        cur = cur[span:, :] + cur[:-span, :]
        span *= 2
    start = POOL_HALO + 1 - w
    return cur[start:start + tm, :]


def _mix_route(pos0, u_bf, halo_bf, gate_a, gate_b, o_gated, x,
               wpool_ref, pscale_ref, wreto_ref, wout_ref, gffn_ref,
               wr_hi_ref, wr_lo_ref, br_ref,
               x1_ref, h2_ref, meta_ref, counts_ref):
    tm = x.shape[0]

    u = u_bf.astype(F32)
    ext = jnp.concatenate([halo_bf.astype(F32), u], axis=0)
    pos = (pos0 + lax.broadcasted_iota(jnp.int32, (tm, 1), 0)).astype(F32)
    pooled_out = []
    for g, w in enumerate(POOL_WINDOWS):
        cols = slice(g * POOL_GROUP_W, (g + 1) * POOL_GROUP_W)
        ws = _window_sum(ext[:, cols], w, tm)
        count = jnp.minimum(pos + 1.0, float(w))
        pooled = ws / count - u[:, cols]
        pooled_out.append(jnp.dot(pooled.astype(BF16), wpool_ref[g],
                                  preferred_element_type=F32))
    y_pool = jnp.concatenate(pooled_out, axis=-1) * pscale_ref[...]

    y_ret = jnp.dot(o_gated, wreto_ref[...], preferred_element_type=F32)
    merged = (jax.nn.sigmoid(gate_a.astype(F32)) * y_pool
              + jax.nn.sigmoid(gate_b.astype(F32)) * y_ret)
    x1 = x + jnp.dot(merged.astype(BF16), wout_ref[...], preferred_element_type=F32)
    x1_ref[...] = x1
    h2 = _rms(x1, gffn_ref[...])
    h2_ref[...] = h2.astype(BF16)

    h_hi = h2.astype(BF16)
    h_lo = (h2 - h_hi.astype(F32)).astype(BF16)
    logits = (jnp.dot(h_hi, wr_hi_ref[...], preferred_element_type=F32)
              + jnp.dot(h_lo, wr_hi_ref[...], preferred_element_type=F32)
              + jnp.dot(h_hi, wr_lo_ref[...], preferred_element_type=F32)
              + br_ref[...])
    logits_t = logits.T[:N_EXPERTS, :]

    sub = lax.broadcasted_iota(jnp.int32, (N_EXPERTS, tm), 0)
    work = logits_t
    vals, idxs, hots = [], [], []
    for _ in range(TOP_K):
        m = jnp.max(work, axis=0, keepdims=True)
        idx = jnp.min(jnp.where(work == m, sub, N_EXPERTS), axis=0, keepdims=True)
        hot = sub == idx
        vals.append(m)
        idxs.append(idx)
        hots.append(hot)
        work = jnp.where(hot, -jnp.inf, work)
    exps = [jnp.exp(v - vals[0]) for v in vals]
    denom = exps[0] + exps[1] + exps[2] + exps[3]
    gates = [e / denom for e in exps]

    sel = (jnp.where(hots[0], 1.0, 0.0) + jnp.where(hots[1], 1.0, 0.0)
           + jnp.where(hots[2], 1.0, 0.0) + jnp.where(hots[3], 1.0, 0.0))
    row = lax.broadcasted_iota(jnp.int32, (tm, tm), 0)
    col = lax.broadcasted_iota(jnp.int32, (tm, tm), 1)
    earlier = jnp.where(row < col, 1.0, 0.0).astype(BF16)
    before = jnp.dot(sel.astype(BF16), earlier, preferred_element_type=F32)
    counts_ref[...] = jnp.sum(sel, axis=1, keepdims=True)

    msub = lax.broadcasted_iota(jnp.int32, (META_ROWS, tm), 0)
    meta = jnp.zeros((META_ROWS, tm), F32)
    for r in range(TOP_K):
        rank_r = jnp.sum(jnp.where(hots[r], before, 0.0), axis=0, keepdims=True)
        meta = jnp.where(msub == META_EXPERT + r, idxs[r].astype(F32), meta)
        meta = jnp.where(msub == META_RANK + r, rank_r, meta)
        meta = jnp.where(msub == META_GATE + r, gates[r], meta)
    meta_ref[...] = meta


def _local_positions(meta, lstart):
    tm = meta.shape[0]
    lane_e = lax.broadcasted_iota(jnp.int32, (tm, N_EXPERTS), 1)
    out = []
    for r in range(TOP_K):
        e_r = meta[:, META_EXPERT + r:META_EXPERT + r + 1].astype(jnp.int32)
        first = jnp.sum(jnp.where(lane_e == e_r, lstart, 0.0), axis=-1, keepdims=True)
        out.append((meta[:, META_RANK + r:META_RANK + r + 1] + first).astype(jnp.int32))
    return out


def _local_rows(tm):
    return TOP_K * tm + FILLER_ROWS


def _for_each_strip(count, max_rows, start_copy):
    top = 1 << (max(max_rows, FILLER_ROWS).bit_length() - 1)
    pieces = [top >> k for k in range(top.bit_length()) if (top >> k) >= STRIP_ALIGN]
    rare_from = 2 * max_rows * TOP_K // N_EXPERTS
    rare = [p for p in pieces if p >= rare_from]
    rare_bits = sum(rare)

    def cover(sizes, off):
        for piece in sizes:
            take = count & piece

            @pl.when(take != 0)
            def _(off=off, piece=piece):
                start_copy(off, piece)

            off = off + take

    @pl.when((count & rare_bits) != 0)
    def _():
        cover(rare, jnp.int32(0))

    cover([p for p in pieces if p < rare_from], count & rare_bits)


def _dispatch_kernel(blk_rows, base_ref, lstart_ref, cnt_ref, zflag_ref,
                     h_ref, meta_ref, lstart_col_ref, xs_ref,
                     sorted_ref, zero_ref, sems, zsem, tail_sem):
    i = pl.program_id(0)
    tm = h_ref.shape[0]
    nrows = _local_rows(tm)
    slot = lax.rem(i, DISPATCH_SLOTS)

    n_zero_blocks = xs_ref.shape[0] // blk_rows

    def zcopy(b, sem):
        start = pl.multiple_of(b * blk_rows, blk_rows)
        return pltpu.make_async_copy(zero_ref, xs_ref.at[pl.ds(start, blk_rows), :], sem)

    def for_flagged(flag, sem, act):
        def body(b, c):
            @pl.when(zflag_ref[b] == flag)
            def _():
                act(zcopy(b, sem))
            return c
        lax.fori_loop(0, n_zero_blocks, body, 0)

    @pl.when(i == 0)
    def _():
        zero_ref[...] = jnp.zeros_like(zero_ref)
        for_flagged(ZERO_FIRST, zsem, lambda cp: cp.start())
        for_flagged(ZERO_ANYTIME, tail_sem, lambda cp: cp.start())
        for_flagged(ZERO_FIRST, zsem, lambda cp: cp.wait())

    meta_t = meta_ref[...]
    sub_e = lax.broadcasted_iota(jnp.int32, (N_EXPERTS, tm), 0)
    pos = []
    for r in range(TOP_K):
        e_r = meta_t[META_EXPERT + r:META_EXPERT + r + 1, :].astype(jnp.int32)
        first = jnp.sum(jnp.where(sub_e == e_r, lstart_col_ref[...], 0.0), axis=0, keepdims=True)
        pos.append((meta_t[META_RANK + r:META_RANK + r + 1, :] + first).astype(jnp.int32))
    jr = lax.broadcasted_iota(jnp.int32, (nrows, tm), 0)
    onehot = jnp.where(jr == pos[0], 1.0, jnp.where(jr == pos[1], 1.0,
             jnp.where(jr == pos[2], 1.0, jnp.where(jr == pos[3], 1.0, 0.0))))
    srt = jnp.dot(onehot.astype(BF16), h_ref[...], preferred_element_type=F32)
    sorted_ref[slot] = srt.astype(ROW_DTYPE)

    def per_segment(e, c):
        k = i * N_SEG + e
        src0 = lstart_ref[k]
        dst0 = base_ref[k]

        def start_copy(off, rows):
            src = pl.multiple_of(src0 + off, STRIP_ALIGN)
            dst = pl.multiple_of(dst0 + off, STRIP_ALIGN)
            pltpu.make_async_copy(sorted_ref.at[slot, pl.ds(src, rows), :],
                                  xs_ref.at[pl.ds(dst, rows), :],
                                  sems.at[slot]).start()

        _for_each_strip(cnt_ref[k], tm, start_copy)
        return c

    lax.fori_loop(0, N_SEG, per_segment, 0)

    def wait_tile(s):
        pltpu.make_async_copy(sorted_ref.at[s], xs_ref.at[pl.ds(0, nrows), :],
                              sems.at[s]).wait()

    @pl.when(i >= DISPATCH_SLOTS - 1)
    def _():
        wait_tile(lax.rem(i + 1, DISPATCH_SLOTS))

    @pl.when(i == pl.num_programs(0) - 1)
    def _():
        for age in range(DISPATCH_SLOTS - 2, -1, -1):
            @pl.when(i >= age)
            def _(age=age):
                wait_tile(lax.rem(i - age, DISPATCH_SLOTS))
        for_flagged(ZERO_ANYTIME, tail_sem, lambda cp: cp.wait())


def _dispatch(h2, meta, tables, n_buf, blk_rows, tm):
    t = h2.shape[0]
    grid_spec = pltpu.PrefetchScalarGridSpec(
        num_scalar_prefetch=4,
        grid=(t // tm,),
        in_specs=[
            pl.BlockSpec((tm, D_MODEL), lambda i, *_: (i, 0)),
            pl.BlockSpec((None, META_ROWS, tm), lambda i, *_: (i, 0, 0)),
            pl.BlockSpec((None, N_EXPERTS, 1), lambda i, *_: (i, 0, 0)),
        ],
        out_specs=pl.BlockSpec(memory_space=pl.ANY),
        scratch_shapes=[pltpu.VMEM((DISPATCH_SLOTS, _local_rows(tm), D_MODEL), ROW_DTYPE),
                        pltpu.VMEM((blk_rows, D_MODEL), ROW_DTYPE),
                        pltpu.SemaphoreType.DMA((DISPATCH_SLOTS,)),
                        pltpu.SemaphoreType.DMA(()),
                        pltpu.SemaphoreType.DMA(())],
    )
    return pl.pallas_call(
        functools.partial(_dispatch_kernel, blk_rows),
        out_shape=jax.ShapeDtypeStruct((n_buf + SPARE_ROWS, D_MODEL), ROW_DTYPE),
        grid_spec=grid_spec,
        compiler_params=pltpu.CompilerParams(
            dimension_semantics=("arbitrary",),
            vmem_limit_bytes=VMEM_LIMIT_BYTES,
            has_side_effects=True),
        name="dispatch",
    )(tables["base_out"], tables["lstart"], tables["cnt"], tables["zflag"],
      h2, meta, tables["lstart_cols"])


def _experts_kernel(be_ref, nu_ref, x_ref, wgu_ref, bgu_ref, wd_ref, bd_ref,
                    y_ref, wgu_bf, wd_bf):
    i = pl.program_id(0)

    @pl.when(i < nu_ref[0])
    def _():
        prev = be_ref[jnp.maximum(i - 1, 0)]

        @pl.when((i == 0) | (be_ref[i] != prev))
        def _():
            wgu_bf[...] = wgu_ref[...].astype(BF16)
            wd_bf[...] = wd_ref[...].astype(BF16)

        gu = jnp.dot(x_ref[...].astype(BF16), wgu_bf[...],
                     preferred_element_type=F32) + bgu_ref[...]
        glu = jnp.minimum(gu[:, :D_FF], SWIGLU_LIMIT)
        lin = jnp.clip(gu[:, D_FF:], -SWIGLU_LIMIT, SWIGLU_LIMIT)
        act = glu * jax.nn.sigmoid(SWIGLU_ALPHA * glu) * (lin + 1.0)
        y_ref[...] = (jnp.dot(act.astype(BF16), wd_bf[...],
                              preferred_element_type=F32) + bd_ref[...]).astype(ROW_DTYPE)

    @pl.when(i >= nu_ref[0])
    def _():
        y_ref[...] = jnp.zeros_like(y_ref)


def _experts(xs, n_buf, block_e, n_used, w_gate_up, b_gate_up, w_down, b_down, blk_rows):
    used = lambda i, be, nu: jnp.minimum(i, nu[0] - 1)
    grid_spec = pltpu.PrefetchScalarGridSpec(
        num_scalar_prefetch=2,
        grid=(n_buf // blk_rows,),
        in_specs=[
            pl.BlockSpec((blk_rows, D_MODEL), lambda i, be, nu: (used(i, be, nu), 0)),
            pl.BlockSpec((None, D_MODEL, 2 * D_FF), lambda i, be, nu: (be[i], 0, 0)),
            pl.BlockSpec((None, 1, 2 * D_FF), lambda i, be, nu: (be[i], 0, 0)),
            pl.BlockSpec((None, D_FF, D_MODEL), lambda i, be, nu: (be[i], 0, 0)),
            pl.BlockSpec((None, 1, D_MODEL), lambda i, be, nu: (be[i], 0, 0)),
        ],
        out_specs=pl.BlockSpec((blk_rows, D_MODEL), lambda i, be, nu: (i, 0)),
        scratch_shapes=[pltpu.VMEM((D_MODEL, 2 * D_FF), BF16),
                        pltpu.VMEM((D_FF, D_MODEL), BF16)],
    )
    return pl.pallas_call(
        _experts_kernel,
        out_shape=jax.ShapeDtypeStruct((n_buf, D_MODEL), ROW_DTYPE),
        grid_spec=grid_spec,
        compiler_params=pltpu.CompilerParams(
            dimension_semantics=("arbitrary",),
            vmem_limit_bytes=VMEM_LIMIT_BYTES),
        name="experts",
    )(block_e, n_used, xs, w_gate_up, b_gate_up[:, None, :], w_down, b_down[:, None, :])


def _unsort_weights(meta_t, lstart_row, nrows):
    tm = meta_t.shape[1]
    meta = jnp.concatenate(
        [meta_t, jnp.zeros((LANES - META_ROWS, tm), F32)], axis=0).T
    pos = _local_positions(meta, lstart_row)
    jl = lax.broadcasted_iota(jnp.int32, (tm, nrows), 1)
    gate = [meta[:, META_GATE + r:META_GATE + r + 1] for r in range(TOP_K)]
    weights = jnp.where(jl == pos[0], gate[0], jnp.where(jl == pos[1], gate[1],
              jnp.where(jl == pos[2], gate[2], jnp.where(jl == pos[3], gate[3], 0.0))))
    return weights.astype(BF16)


def _combine_kernel(base_ref, lstart_ref, cnt_ref, meta_ref, lstart_row_ref,
                    meta_next_ref, lstart_row_next_ref, x1_ref,
                    p_ref, ys_ref, gple_ref, wpg_ref, wple_ref, gfin_ref, out_ref,
                    rows_ref, unsort_ref, sems):
    i = pl.program_id(0)
    n = pl.num_programs(0)
    tm = x1_ref.shape[0]
    nrows = _local_rows(tm)
    slot = lax.rem(i, 2)

    def fetch_tile(tile, s):
        def per_segment(e, c):
            k = tile * N_SEG + e
            src0 = base_ref[k]
            dst0 = lstart_ref[k]

            def start_copy(off, rows):
                src = pl.multiple_of(src0 + off, STRIP_ALIGN)
                dst = pl.multiple_of(dst0 + off, STRIP_ALIGN)
                pltpu.make_async_copy(ys_ref.at[pl.ds(src, rows), :],
                                      rows_ref.at[s, pl.ds(dst, rows), :],
                                      sems.at[s]).start()

            _for_each_strip(cnt_ref[k], tm, start_copy)
            return c

        lax.fori_loop(0, N_SEG, per_segment, 0)

    @pl.when(i == 0)
    def _():
        fetch_tile(0, 0)
        unsort_ref[0] = _unsort_weights(meta_ref[...], lstart_row_ref[...], nrows)

    @pl.when(i + 1 < n)
    def _():
        fetch_tile(i + 1, 1 - slot)

    pltpu.make_async_copy(ys_ref.at[pl.ds(0, nrows), :], rows_ref.at[slot],
                          sems.at[slot]).wait()
    moe = jnp.dot(unsort_ref[slot], rows_ref[slot].astype(BF16),
                  preferred_element_type=F32)
    x2 = x1_ref[...] + moe
    u = _rms(x2, gple_ref[...])
    gate_ple = jax.nn.sigmoid(jnp.dot(u.astype(BF16), wpg_ref[...],
                                      preferred_element_type=F32))
    emb = jnp.dot(p_ref[...].astype(BF16), wple_ref[...], preferred_element_type=F32)
    x3 = x2 + gate_ple * emb
    out_ref[...] = _rms(x3, gfin_ref[...])

    unsort_ref[1 - slot] = _unsort_weights(meta_next_ref[...], lstart_row_next_ref[...], nrows)


def _combine(meta, tables, x1, p2, ys, g_ple, w_ple_gate_bf, w_ple_bf, g_final, tm):
    t = x1.shape[0]
    full = lambda *shape: pl.BlockSpec(shape, lambda i, *_: (0,) * len(shape))
    last = t // tm - 1
    nxt = lambda i: jnp.minimum(i + 1, last)
    grid_spec = pltpu.PrefetchScalarGridSpec(
        num_scalar_prefetch=3,
        grid=(t // tm,),
        in_specs=[
            pl.BlockSpec((None, META_ROWS, tm), lambda i, *_: (i, 0, 0)),
            pl.BlockSpec((None, 1, N_EXPERTS), lambda i, *_: (i, 0, 0)),
            pl.BlockSpec((None, META_ROWS, tm), lambda i, *_: (nxt(i), 0, 0)),
            pl.BlockSpec((None, 1, N_EXPERTS), lambda i, *_: (nxt(i), 0, 0)),
            pl.BlockSpec((tm, D_MODEL), lambda i, *_: (i, 0)),
            pl.BlockSpec((tm, PLE_DIM), lambda i, *_: (i, 0)),
            pl.BlockSpec(memory_space=pl.ANY),
            full(1, D_MODEL),
            full(D_MODEL, D_MODEL),
            full(PLE_DIM, D_MODEL),
            full(1, D_MODEL),
        ],
        out_specs=pl.BlockSpec((tm, D_MODEL), lambda i, *_: (i, 0)),
        scratch_shapes=[pltpu.VMEM((2, _local_rows(tm), D_MODEL), ROW_DTYPE),
                        pltpu.VMEM((2, tm, _local_rows(tm)), BF16),
                        pltpu.SemaphoreType.DMA((2,))],
    )
    return pl.pallas_call(
        _combine_kernel,
        out_shape=jax.ShapeDtypeStruct((t, D_MODEL), F32),
        grid_spec=grid_spec,
        compiler_params=pltpu.CompilerParams(
            dimension_semantics=("arbitrary",),
            vmem_limit_bytes=VMEM_LIMIT_BYTES),
        name="combine",
    )(tables["base_in"], tables["lstart"], tables["cnt"], meta, tables["lstart_rows"],
      meta, tables["lstart_rows"], x1, p2, ys, g_ple, w_ple_gate_bf, w_ple_bf, g_final)


def _tiles(seq_len):
    return dict(
        proj_rows=min(512, seq_len),
        ret_block=min(256, seq_len),
        route_rows=min(512, seq_len),
        expert_rows=512,
    )


def _slot_tables(tile_counts, blk, n_buf, nloc):
    cnt = tile_counts[:, :, 0].astype(jnp.int32)
    cnt = ((cnt + STRIP_ALIGN - 1) // STRIP_ALIGN) * STRIP_ALIGN
    seg_rows = jnp.sum(cnt, axis=1)
    counts = jnp.sum(cnt, axis=0)
    padded = ((counts + blk - 1) // blk) * blk
    ends_pad = jnp.cumsum(padded)
    starts_pad = ends_pad - padded
    tile_before = jnp.cumsum(cnt, axis=0) - cnt
    lstart = jnp.cumsum(cnt, axis=1) - cnt
    base = starts_pad[None, :] + tile_before
    with_filler = lambda a, col: jnp.concatenate([a, col[:, None]], axis=1).reshape(-1)
    blk_start = jnp.arange((n_buf + SPARE_ROWS) // blk, dtype=jnp.int32) * blk
    last_of_expert = jnp.any((padded > 0)[None, :]
                             & (blk_start[:, None] == (ends_pad - blk)[None, :]), axis=1)
    zflag = jnp.where(last_of_expert | (blk_start >= n_buf), ZERO_FIRST,
                      jnp.where(blk_start >= ends_pad[-1], ZERO_ANYTIME, 0)).astype(jnp.int32)
    filler_dst = n_buf + (jnp.arange(cnt.shape[0], dtype=jnp.int32) % DISPATCH_SLOTS) * FILLER_ROWS
    return dict(
        cnt=with_filler(cnt, nloc - seg_rows),
        lstart=with_filler(lstart, seg_rows),
        base_out=with_filler(base, filler_dst),
        base_in=with_filler(base, jnp.zeros_like(seg_rows)),
        lstart_rows=lstart.astype(F32)[:, None, :],
        lstart_cols=lstart.astype(F32)[:, :, None],
        zflag=zflag, ends_pad=ends_pad)


def _layer(x2, p2, seq_len, g_mix, w_in, w_pool, pool_scale, w_ret_o, w_out, g_ffn,
           w_router, b_router, w_gate_up, b_gate_up, w_down, b_down, g_ple,
           w_ple_gate, w_ple, g_out):
    t = x2.shape[0]
    cfg = _tiles(seq_len)
    row = lambda a: a.reshape(1, -1)

    lane_pad = ((0, 0), (0, LANES - N_EXPERTS))
    wr_hi = w_router.astype(BF16)
    wr_lo = jnp.pad((w_router - wr_hi.astype(F32)).astype(BF16), lane_pad)
    wr_hi = jnp.pad(wr_hi, lane_pad)
    tm = cfg["route_rows"]
    x1, h2, meta, tile_counts = _token_mixer(
        x2.reshape(t // seq_len, seq_len, D_MODEL), row(g_mix), w_in.astype(BF16),
        w_pool.astype(BF16), row(pool_scale), w_ret_o.astype(BF16), w_out.astype(BF16),
        row(g_ffn), wr_hi, wr_lo, jnp.pad(row(b_router), lane_pad), tm, cfg["ret_block"])
    x1 = x1.reshape(t, D_MODEL)
    h2 = h2.reshape(t, D_MODEL)

    blk = cfg["expert_rows"]
    n_buf = t * TOP_K + (t // tm) * FILLER_ROWS + N_EXPERTS * blk
    n_blocks = n_buf // blk
    tables = _slot_tables(tile_counts, blk, n_buf, _local_rows(tm))
    blk_start = jnp.arange(n_blocks, dtype=jnp.int32) * blk
    block_e = jnp.minimum(
        jnp.sum((tables["ends_pad"][None, :] <= blk_start[:, None]).astype(jnp.int32), axis=1),
        N_EXPERTS - 1)
    n_used = tables["ends_pad"][-1:] // blk

    xs = _dispatch(h2, meta, tables, n_buf, blk, tm)
    ys = _experts(xs, n_buf, block_e, n_used, w_gate_up, b_gate_up, w_down, b_down, blk)
    return _combine(meta, tables, x1, p2, ys, row(g_ple), w_ple_gate.astype(BF16),
                    w_ple.astype(BF16), row(g_out), tm)


def kernel(x, p, g_mix, w_in, w_pool, pool_scale, w_ret_o, w_out, g_ffn, w_router,
           b_router, w_gate_up, b_gate_up, w_down, b_down, g_ple, w_ple_gate, w_ple,
           g_final):
    b, s, d = x.shape
    depth = p.shape[0]
    assert depth == 1 and d == D_MODEL
    x2 = x.reshape(b * s, d)
    out = _layer(x2, p[0].reshape(b * s, PLE_DIM), s, g_mix[0], w_in[0], w_pool[0],
                 pool_scale[0], w_ret_o[0], w_out[0], g_ffn[0], w_router[0],
                 b_router[0], w_gate_up[0], b_gate_up[0], w_down[0], b_down[0],
                 g_ple[0], w_ple_gate[0], w_ple[0], g_final)
    return out.reshape(b, s, d)
```

```python
import functools

import numpy as np
import jax
import jax.numpy as jnp
from jax import lax
from jax.experimental import pallas as pl
from jax.experimental.pallas import tpu as pltpu

F32 = jnp.float32
BF16 = jnp.bfloat16

D_MODEL = 1024
EPS = 1e-6
CHUNK = 64
PLE_DIM = 256
POOL_WINDOWS = (2, 4, 8, 16)
POOL_GROUP_W = D_MODEL // len(POOL_WINDOWS)
POOL_HALO = 16
RET_HEADS = 4
RET_QK_HEAD = 256
RET_V_HEAD = 512
RET_V_W = RET_HEADS * RET_V_HEAD
ROPE_BASE = 10000.0
IN_W = 9 * D_MODEL
N_EXPERTS = 32
TOP_K = 4
D_FF = D_MODEL
SWIGLU_ALPHA = 1.702
SWIGLU_LIMIT = 7.0

VMEM_LIMIT_BYTES = 56 * 1024 * 1024
LANES = 128
META_ROWS = 16
META_EXPERT, META_RANK, META_GATE = 0, TOP_K, 2 * TOP_K
ROW_DTYPE = F32
SUBLANES = 8
STRIP_ALIGN = SUBLANES * 4 // jnp.dtype(ROW_DTYPE).itemsize
FILLER_ROWS = N_EXPERTS * STRIP_ALIGN
N_SEG = N_EXPERTS + 1
ZERO_FIRST, ZERO_ANYTIME = 1, 2
EXPERT_ROW_STEP = 128
DISPATCH_SLOTS = 3
SPARE_ROWS = 4 * FILLER_ROWS


def _rms(x, g):
    return x * lax.rsqrt(jnp.mean(x * x, axis=-1, keepdims=True) + EPS) * g


COL_U, COL_Q, COL_K = 0, 1, 2
COL_V, COL_G = (3, 4), (5, 6)
COL_GATE_A, COL_GATE_B = 7, 8
SIDE_U, SIDE_GATE_A, SIDE_GATE_B = 0, 1, 2


def _rotary_heads(y, cos, sin):
    half = RET_QK_HEAD // 2
    parts = []
    for h in range(RET_HEADS):
        x1 = y[:, h * RET_QK_HEAD:h * RET_QK_HEAD + half]
        x2 = y[:, h * RET_QK_HEAD + half:(h + 1) * RET_QK_HEAD]
        parts += [x1 * cos - x2 * sin, x2 * cos + x1 * sin]
    return jnp.concatenate(parts, axis=-1)


def _head_decay_logs():
    return [float(np.log(1.0 - 2.0 ** (-5.0 - h))) for h in range(RET_HEADS)]


def _token_mixer_kernel(blk_decay, ret_blk, x_ref, g_ref, w_ref, cos_ref, sin_ref,
                        d_ref, qd_ref, kd_ref,
                        wpool_ref, pscale_ref, wreto_ref, wout_ref, gffn_ref,
                        wr_hi_ref, wr_lo_ref, br_ref,
                        x1_ref, h2_ref, meta_ref, counts_ref,
                        side_ref, o_ref, q_s, k_s, v_s, g_s, state_ref, halo_ref):
    @pl.when(pl.program_id(1) == 0)
    def _():
        state_ref[...] = jnp.zeros_like(state_ref)
        halo_ref[...] = jnp.zeros_like(halo_ref)

    tm = x_ref.shape[0]
    h_in = _rms(x_ref[...], g_ref[...]).astype(BF16)

    def proj(c):
        return jnp.dot(h_in, w_ref[:, c * D_MODEL:(c + 1) * D_MODEL],
                       preferred_element_type=F32)

    def chunk(n):
        return slice(n * D_MODEL, (n + 1) * D_MODEL)

    side_ref[:, chunk(SIDE_U)] = proj(COL_U).astype(BF16)
    q_s[...] = _rotary_heads(proj(COL_Q), cos_ref[...], sin_ref[...]).astype(BF16)
    k_s[...] = (_rotary_heads(proj(COL_K), cos_ref[...], sin_ref[...])
                * (RET_QK_HEAD ** -0.5)).astype(BF16)
    for n, c in enumerate(COL_V):
        v_s[:, chunk(n)] = proj(c).astype(BF16)
    for n, c in enumerate(COL_G):
        y = proj(c)
        g_s[:, chunk(n)] = (y * jax.nn.sigmoid(y)).astype(BF16)
    side_ref[:, chunk(SIDE_GATE_A)] = proj(COL_GATE_A).astype(BF16)
    side_ref[:, chunk(SIDE_GATE_B)] = proj(COL_GATE_B).astype(BF16)

    for b in range(tm // ret_blk):
        rows = slice(b * ret_blk, (b + 1) * ret_blk)
        for h in range(RET_HEADS):
            qk_cols = slice(h * RET_QK_HEAD, (h + 1) * RET_QK_HEAD)
            v_cols = slice(h * RET_V_HEAD, (h + 1) * RET_V_HEAD)
            q = q_s[rows, qk_cols]
            k = k_s[rows, qk_cols]
            v = v_s[rows, v_cols]
            scores = lax.dot_general(q, k, (((1,), (1,)), ((), ())),
                                     preferred_element_type=F32) * d_ref[h]
            state = state_ref[h]
            o = jnp.dot(scores.astype(BF16), v, preferred_element_type=F32)
            o = o + jnp.dot((q.astype(F32) * qd_ref[h]).astype(BF16), state.astype(BF16),
                            preferred_element_type=F32)
            k_dec = (k.astype(F32) * kd_ref[h]).astype(BF16)
            state_ref[h] = state * blk_decay[h] + lax.dot_general(
                k_dec, v, (((0,), (0,)), ((), ())), preferred_element_type=F32)
            o = o * lax.rsqrt(jnp.mean(o * o, axis=-1, keepdims=True) + EPS)
            o_ref[rows, v_cols] = (o * g_s[rows, v_cols].astype(F32)).astype(BF16)

    u_bf = side_ref[:, chunk(SIDE_U)]
    _mix_route(pl.program_id(1) * tm, u_bf, halo_ref[...], side_ref[:, chunk(SIDE_GATE_A)],
               side_ref[:, chunk(SIDE_GATE_B)], o_ref[...], x_ref[...],
               wpool_ref, pscale_ref, wreto_ref, wout_ref, gffn_ref,
               wr_hi_ref, wr_lo_ref, br_ref, x1_ref, h2_ref, meta_ref, counts_ref)
    halo_ref[...] = u_bf[tm - POOL_HALO:, :]


def _rotary_tables(s):
    half = RET_QK_HEAD // 2
    pos = jnp.arange(s, dtype=F32)
    inv = ROPE_BASE ** (-jnp.linspace(0.0, 1.0, half, dtype=F32))
    ang = pos[:, None] * inv[None, :]
    return jnp.cos(ang), jnp.sin(ang)


def _retention_tables(blk):
    log_g = jnp.asarray(_head_decay_logs(), F32)
    idx = jnp.arange(blk, dtype=F32)
    diff = idx[:, None] - idx[None, :]
    chunk = jnp.arange(blk, dtype=jnp.int32) // CHUNK
    visible = chunk[None, :] <= chunk[:, None]
    dmask = jnp.where(visible[None], jnp.exp(log_g[:, None, None] * jnp.abs(diff)[None]), 0.0)
    q_dec = jnp.exp(log_g[:, None] * (idx + 1.0))[:, :, None]
    k_dec = jnp.exp(log_g[:, None] * (blk - 1.0 - idx))[:, :, None]
    q_dec = jnp.broadcast_to(q_dec, (RET_HEADS, blk, RET_QK_HEAD))
    k_dec = jnp.broadcast_to(k_dec, (RET_HEADS, blk, RET_QK_HEAD))
    return dmask.astype(F32), q_dec, k_dec


def _token_mixer(x3, g_mix, w_in_bf, w_pool_bf, pool_scale, w_ret_o_bf, w_out_bf,
                 g_ffn, wr_hi, wr_lo, b_router, tm, ret_blk):
    b, s, _ = x3.shape
    half = RET_QK_HEAD // 2
    tiles_per_seq = s // tm
    n_tiles = b * tiles_per_seq
    cos, sin = _rotary_tables(s)
    dmask, q_dec, k_dec = _retention_tables(ret_blk)
    blk_decay = [float(np.exp(lg * ret_blk)) for lg in _head_decay_logs()]
    once = lambda *shape: pl.BlockSpec(shape, lambda bi, j: (0,) * len(shape),
                                       pipeline_mode=pl.Buffered(1))
    tile = lambda bi, j: bi * tiles_per_seq + j
    return pl.pallas_call(
        functools.partial(_token_mixer_kernel, blk_decay, ret_blk),
        out_shape=(
            jax.ShapeDtypeStruct((b, s, D_MODEL), F32),
            jax.ShapeDtypeStruct((b, s, D_MODEL), BF16),
            jax.ShapeDtypeStruct((n_tiles, META_ROWS, tm), F32),
            jax.ShapeDtypeStruct((n_tiles, N_EXPERTS, 1), F32),
        ),
        grid=(b, tiles_per_seq),
        in_specs=[
            pl.BlockSpec((None, tm, D_MODEL), lambda bi, j: (bi, j, 0)),
            pl.BlockSpec((1, D_MODEL), lambda bi, j: (0, 0)),
            once(D_MODEL, IN_W),
            pl.BlockSpec((tm, half), lambda bi, j: (j, 0)),
            pl.BlockSpec((tm, half), lambda bi, j: (j, 0)),
            once(RET_HEADS, ret_blk, ret_blk),
            once(RET_HEADS, ret_blk, RET_QK_HEAD),
            once(RET_HEADS, ret_blk, RET_QK_HEAD),
            once(len(POOL_WINDOWS), POOL_GROUP_W, POOL_GROUP_W),
            once(1, D_MODEL),
            once(RET_V_W, D_MODEL),
            once(D_MODEL, D_MODEL),
            once(1, D_MODEL),
            once(D_MODEL, LANES),
            once(D_MODEL, LANES),
            once(1, LANES),
        ],
        out_specs=(
            pl.BlockSpec((None, tm, D_MODEL), lambda bi, j: (bi, j, 0)),
            pl.BlockSpec((None, tm, D_MODEL), lambda bi, j: (bi, j, 0)),
            pl.BlockSpec((None, META_ROWS, tm), lambda bi, j: (tile(bi, j), 0, 0)),
            pl.BlockSpec((None, N_EXPERTS, 1), lambda bi, j: (tile(bi, j), 0, 0)),
        ),
        scratch_shapes=[pltpu.VMEM((tm, 3 * D_MODEL), BF16),
                        pltpu.VMEM((tm, RET_V_W), BF16),
                        pltpu.VMEM((tm, D_MODEL), BF16),
                        pltpu.VMEM((tm, D_MODEL), BF16),
                        pltpu.VMEM((tm, RET_V_W), BF16),
                        pltpu.VMEM((tm, RET_V_W), BF16),
                        pltpu.VMEM((RET_HEADS, RET_QK_HEAD, RET_V_HEAD), F32),
                        pltpu.VMEM((POOL_HALO, D_MODEL), BF16)],
        compiler_params=pltpu.CompilerParams(
            dimension_semantics=("arbitrary", "arbitrary"),
            vmem_limit_bytes=VMEM_LIMIT_BYTES),
        name="token_mixer",
    )(x3, g_mix, w_in_bf, cos, sin, dmask, q_dec, k_dec, w_pool_bf, pool_scale,
      w_ret_o_bf, w_out_bf, g_ffn, wr_hi, wr_lo, b_router)


def _window_sum(ext, w, tm):
    cur = ext
    span = 1
    while span < w:
        cur = cur[span:, :] + cur[:-span, :]
        span *= 2
    start = POOL_HALO + 1 - w
    return cur[start:start + tm, :]


def _mix_route(pos0, u_bf, halo_bf, gate_a, gate_b, o_gated, x,
               wpool_ref, pscale_ref, wreto_ref, wout_ref, gffn_ref,
               wr_hi_ref, wr_lo_ref, br_ref,
               x1_ref, h2_ref, meta_ref, counts_ref):
    tm = x.shape[0]

    u = u_bf.astype(F32)
    ext = jnp.concatenate([halo_bf.astype(F32), u], axis=0)
    pos = (pos0 + lax.broadcasted_iota(jnp.int32, (tm, 1), 0)).astype(F32)
    pooled_out = []
    for g, w in enumerate(POOL_WINDOWS):
        cols = slice(g * POOL_GROUP_W, (g + 1) * POOL_GROUP_W)
        ws = _window_sum(ext[:, cols], w, tm)
        count = jnp.minimum(pos + 1.0, float(w))
        pooled = ws / count - u[:, cols]
        pooled_out.append(jnp.dot(pooled.astype(BF16), wpool_ref[g],
                                  preferred_element_type=F32))
    y_pool = jnp.concatenate(pooled_out, axis=-1) * pscale_ref[...]

    y_ret = jnp.dot(o_gated, wreto_ref[...], preferred_element_type=F32)
    merged = (jax.nn.sigmoid(gate_a.astype(F32)) * y_pool
              + jax.nn.sigmoid(gate_b.astype(F32)) * y_ret)
    x1 = x + jnp.dot(merged.astype(BF16), wout_ref[...], preferred_element_type=F32)
    x1_ref[...] = x1
    h2 = _rms(x1, gffn_ref[...])
    h2_ref[...] = h2.astype(BF16)

    h_hi = h2.astype(BF16)
    h_lo = (h2 - h_hi.astype(F32)).astype(BF16)
    logits = (jnp.dot(h_hi, wr_hi_ref[...], preferred_element_type=F32)
              + jnp.dot(h_lo, wr_hi_ref[...], preferred_element_type=F32)
              + jnp.dot(h_hi, wr_lo_ref[...], preferred_element_type=F32)
              + br_ref[...])
    logits_t = logits.T[:N_EXPERTS, :]

    sub = lax.broadcasted_iota(jnp.int32, (N_EXPERTS, tm), 0)
    work = logits_t
    vals, idxs, hots = [], [], []
    for _ in range(TOP_K):
        m = jnp.max(work, axis=0, keepdims=True)
        idx = jnp.min(jnp.where(work == m, sub, N_EXPERTS), axis=0, keepdims=True)
        hot = sub == idx
        vals.append(m)
        idxs.append(idx)
        hots.append(hot)
        work = jnp.where(hot, -jnp.inf, work)
    exps = [jnp.exp(v - vals[0]) for v in vals]
    denom = exps[0] + exps[1] + exps[2] + exps[3]
    gates = [e / denom for e in exps]

    sel = (jnp.where(hots[0], 1.0, 0.0) + jnp.where(hots[1], 1.0, 0.0)
           + jnp.where(hots[2], 1.0, 0.0) + jnp.where(hots[3], 1.0, 0.0))
    row = lax.broadcasted_iota(jnp.int32, (tm, tm), 0)
    col = lax.broadcasted_iota(jnp.int32, (tm, tm), 1)
    earlier = jnp.where(row < col, 1.0, 0.0).astype(BF16)
    before = jnp.dot(sel.astype(BF16), earlier, preferred_element_type=F32)
    counts_ref[...] = jnp.sum(sel, axis=1, keepdims=True)

    msub = lax.broadcasted_iota(jnp.int32, (META_ROWS, tm), 0)
    meta = jnp.zeros((META_ROWS, tm), F32)
    for r in range(TOP_K):
        rank_r = jnp.sum(jnp.where(hots[r], before, 0.0), axis=0, keepdims=True)
        meta = jnp.where(msub == META_EXPERT + r, idxs[r].astype(F32), meta)
        meta = jnp.where(msub == META_RANK + r, rank_r, meta)
        meta = jnp.where(msub == META_GATE + r, gates[r], meta)
    meta_ref[...] = meta


def _local_positions(meta, lstart):
    tm = meta.shape[0]
    lane_e = lax.broadcasted_iota(jnp.int32, (tm, N_EXPERTS), 1)
    out = []
    for r in range(TOP_K):
        e_r = meta[:, META_EXPERT + r:META_EXPERT + r + 1].astype(jnp.int32)
        first = jnp.sum(jnp.where(lane_e == e_r, lstart, 0.0), axis=-1, keepdims=True)
        out.append((meta[:, META_RANK + r:META_RANK + r + 1] + first).astype(jnp.int32))
    return out


def _local_rows(tm):
    return TOP_K * tm + FILLER_ROWS


def _for_each_strip(count, max_rows, start_copy):
    top = 1 << (max(max_rows, FILLER_ROWS).bit_length() - 1)
    pieces = [top >> k for k in range(top.bit_length()) if (top >> k) >= STRIP_ALIGN]
    rare_from = 2 * max_rows * TOP_K // N_EXPERTS
    rare = [p for p in pieces if p >= rare_from]
    rare_bits = sum(rare)

    def cover(sizes, off):
        for piece in sizes:
            take = count & piece

            @pl.when(take != 0)
            def _(off=off, piece=piece):
                start_copy(off, piece)

            off = off + take

    @pl.when((count & rare_bits) != 0)
    def _():
        cover(rare, jnp.int32(0))

    cover([p for p in pieces if p < rare_from], count & rare_bits)


def _dispatch_kernel(blk_rows, base_ref, lstart_ref, cnt_ref, zflag_ref,
                     h_ref, meta_ref, lstart_col_ref, xs_ref,
                     sorted_ref, zero_ref, sems, zsem, tail_sem):
    i = pl.program_id(0)
    tm = h_ref.shape[0]
    nrows = _local_rows(tm)
    slot = lax.rem(i, DISPATCH_SLOTS)

    n_zero_blocks = xs_ref.shape[0] // blk_rows

    def zcopy(b, sem):
        start = pl.multiple_of(b * blk_rows, blk_rows)
        return pltpu.make_async_copy(zero_ref, xs_ref.at[pl.ds(start, blk_rows), :], sem)

    def for_flagged(flag, sem, act):
        def body(b, c):
            @pl.when(zflag_ref[b] == flag)
            def _():
                act(zcopy(b, sem))
            return c
        lax.fori_loop(0, n_zero_blocks, body, 0)

    @pl.when(i == 0)
    def _():
        zero_ref[...] = jnp.zeros_like(zero_ref)
        for_flagged(ZERO_FIRST, zsem, lambda cp: cp.start())
        for_flagged(ZERO_ANYTIME, tail_sem, lambda cp: cp.start())
        for_flagged(ZERO_FIRST, zsem, lambda cp: cp.wait())

    meta_t = meta_ref[...]
    sub_e = lax.broadcasted_iota(jnp.int32, (N_EXPERTS, tm), 0)
    pos = []
    for r in range(TOP_K):
        e_r = meta_t[META_EXPERT + r:META_EXPERT + r + 1, :].astype(jnp.int32)
        first = jnp.sum(jnp.where(sub_e == e_r, lstart_col_ref[...], 0.0), axis=0, keepdims=True)
        pos.append((meta_t[META_RANK + r:META_RANK + r + 1, :] + first).astype(jnp.int32))
    jr = lax.broadcasted_iota(jnp.int32, (nrows, tm), 0)
    onehot = jnp.where(jr == pos[0], 1.0, jnp.where(jr == pos[1], 1.0,
             jnp.where(jr == pos[2], 1.0, jnp.where(jr == pos[3], 1.0, 0.0))))
    srt = jnp.dot(onehot.astype(BF16), h_ref[...], preferred_element_type=F32)
    sorted_ref[slot] = srt.astype(ROW_DTYPE)

    def per_segment(e, c):
        k = i * N_SEG + e
        src0 = lstart_ref[k]
        dst0 = base_ref[k]

        def start_copy(off, rows):
            src = pl.multiple_of(src0 + off, STRIP_ALIGN)
            dst = pl.multiple_of(dst0 + off, STRIP_ALIGN)
            pltpu.make_async_copy(sorted_ref.at[slot, pl.ds(src, rows), :],
                                  xs_ref.at[pl.ds(dst, rows), :],
                                  sems.at[slot]).start()

        _for_each_strip(cnt_ref[k], tm, start_copy)
        return c

    lax.fori_loop(0, N_SEG, per_segment, 0)

    def wait_tile(s):
        pltpu.make_async_copy(sorted_ref.at[s], xs_ref.at[pl.ds(0, nrows), :],
                              sems.at[s]).wait()

    @pl.when(i >= DISPATCH_SLOTS - 1)
    def _():
        wait_tile(lax.rem(i + 1, DISPATCH_SLOTS))

    @pl.when(i == pl.num_programs(0) - 1)
    def _():
        for age in range(DISPATCH_SLOTS - 2, -1, -1):
            @pl.when(i >= age)
            def _(age=age):
                wait_tile(lax.rem(i - age, DISPATCH_SLOTS))
        for_flagged(ZERO_ANYTIME, tail_sem, lambda cp: cp.wait())


def _dispatch(h2, meta, tables, n_buf, blk_rows, tm):
    t = h2.shape[0]
    grid_spec = pltpu.PrefetchScalarGridSpec(
        num_scalar_prefetch=4,
        grid=(t // tm,),
        in_specs=[
            pl.BlockSpec((tm, D_MODEL), lambda i, *_: (i, 0)),
            pl.BlockSpec((None, META_ROWS, tm), lambda i, *_: (i, 0, 0)),
            pl.BlockSpec((None, N_EXPERTS, 1), lambda i, *_: (i, 0, 0)),
        ],
        out_specs=pl.BlockSpec(memory_space=pl.ANY),
        scratch_shapes=[pltpu.VMEM((DISPATCH_SLOTS, _local_rows(tm), D_MODEL), ROW_DTYPE),
                        pltpu.VMEM((blk_rows, D_MODEL), ROW_DTYPE),
                        pltpu.SemaphoreType.DMA((DISPATCH_SLOTS,)),
                        pltpu.SemaphoreType.DMA(()),
                        pltpu.SemaphoreType.DMA(())],
    )
    return pl.pallas_call(
        functools.partial(_dispatch_kernel, blk_rows),
        out_shape=jax.ShapeDtypeStruct((n_buf + SPARE_ROWS, D_MODEL), ROW_DTYPE),
        grid_spec=grid_spec,
        compiler_params=pltpu.CompilerParams(
            dimension_semantics=("arbitrary",),
            vmem_limit_bytes=VMEM_LIMIT_BYTES,
            has_side_effects=True),
        name="dispatch",
    )(tables["base_out"], tables["lstart"], tables["cnt"], tables["zflag"],
      h2, meta, tables["lstart_cols"])


def _experts_kernel(be_ref, nu_ref, live_ref, x_ref, wgu_ref, bgu_ref, wd_ref, bd_ref,
                    y_ref, wgu_bf, wd_bf):
    i = pl.program_id(0)
    blk_rows = x_ref.shape[0]
    live = live_ref[i]

    @pl.when(live > 0)
    def _():
        prev = be_ref[jnp.maximum(i - 1, 0)]

        @pl.when((i == 0) | (be_ref[i] != prev))
        def _():
            wgu_bf[...] = wgu_ref[...].astype(BF16)
            wd_bf[...] = wd_ref[...].astype(BF16)

    for rows in range(EXPERT_ROW_STEP, blk_rows + 1, EXPERT_ROW_STEP):
        @pl.when((live > rows - EXPERT_ROW_STEP) & (live <= rows))
        def _(rows=rows):
            gu = jnp.dot(x_ref[:rows, :].astype(BF16), wgu_bf[...],
                         preferred_element_type=F32) + bgu_ref[...]
            glu = jnp.minimum(gu[:, :D_FF], SWIGLU_LIMIT)
            lin = jnp.clip(gu[:, D_FF:], -SWIGLU_LIMIT, SWIGLU_LIMIT)
            act = glu * jax.nn.sigmoid(SWIGLU_ALPHA * glu) * (lin + 1.0)
            y_ref[:rows, :] = (jnp.dot(act.astype(BF16), wd_bf[...],
                                       preferred_element_type=F32) + bd_ref[...]).astype(ROW_DTYPE)
            if rows < blk_rows:
                y_ref[rows:, :] = jnp.zeros((blk_rows - rows, D_MODEL), ROW_DTYPE)

    @pl.when(live == 0)
    def _():
        y_ref[...] = jnp.zeros_like(y_ref)


def _experts(xs, n_buf, block_e, n_used, live_rows, w_gate_up, b_gate_up, w_down, b_down,
             blk_rows):
    assert blk_rows % EXPERT_ROW_STEP == 0
    used = lambda i, be, nu: jnp.minimum(i, nu[0] - 1)
    grid_spec = pltpu.PrefetchScalarGridSpec(
        num_scalar_prefetch=3,
        grid=(n_buf // blk_rows,),
        in_specs=[
            pl.BlockSpec((blk_rows, D_MODEL), lambda i, be, nu, lv: (used(i, be, nu), 0)),
            pl.BlockSpec((None, D_MODEL, 2 * D_FF), lambda i, be, nu, lv: (be[i], 0, 0)),
            pl.BlockSpec((None, 1, 2 * D_FF), lambda i, be, nu, lv: (be[i], 0, 0)),
            pl.BlockSpec((None, D_FF, D_MODEL), lambda i, be, nu, lv: (be[i], 0, 0)),
            pl.BlockSpec((None, 1, D_MODEL), lambda i, be, nu, lv: (be[i], 0, 0)),
        ],
        out_specs=pl.BlockSpec((blk_rows, D_MODEL), lambda i, be, nu, lv: (i, 0)),
        scratch_shapes=[pltpu.VMEM((D_MODEL, 2 * D_FF), BF16),
                        pltpu.VMEM((D_FF, D_MODEL), BF16)],
    )
    return pl.pallas_call(
        _experts_kernel,
        out_shape=jax.ShapeDtypeStruct((n_buf, D_MODEL), ROW_DTYPE),
        grid_spec=grid_spec,
        compiler_params=pltpu.CompilerParams(
            dimension_semantics=("arbitrary",),
            vmem_limit_bytes=VMEM_LIMIT_BYTES),
        name="experts",
    )(block_e, n_used, live_rows, xs, w_gate_up, b_gate_up[:, None, :], w_down,
      b_down[:, None, :])


def _unsort_weights(meta_t, lstart_row, nrows):
    tm = meta_t.shape[1]
    meta = jnp.concatenate(
        [meta_t, jnp.zeros((LANES - META_ROWS, tm), F32)], axis=0).T
    pos = _local_positions(meta, lstart_row)
    jl = lax.broadcasted_iota(jnp.int32, (tm, nrows), 1)
    gate = [meta[:, META_GATE + r:META_GATE + r + 1] for r in range(TOP_K)]
    weights = jnp.where(jl == pos[0], gate[0], jnp.where(jl == pos[1], gate[1],
              jnp.where(jl == pos[2], gate[2], jnp.where(jl == pos[3], gate[3], 0.0))))
    return weights.astype(BF16)


def _combine_kernel(base_ref, lstart_ref, cnt_ref, meta_ref, lstart_row_ref,
                    meta_next_ref, lstart_row_next_ref, x1_ref,
                    p_ref, ys_ref, gple_ref, wpg_ref, wple_ref, gfin_ref, out_ref,
                    rows_ref, unsort_ref, sems):
    i = pl.program_id(0)
    n = pl.num_programs(0)
    tm = x1_ref.shape[0]
    nrows = _local_rows(tm)
    slot = lax.rem(i, 2)

    def fetch_tile(tile, s):
        def per_segment(e, c):
            k = tile * N_SEG + e
            src0 = base_ref[k]
            dst0 = lstart_ref[k]

            def start_copy(off, rows):
                src = pl.multiple_of(src0 + off, STRIP_ALIGN)
                dst = pl.multiple_of(dst0 + off, STRIP_ALIGN)
                pltpu.make_async_copy(ys_ref.at[pl.ds(src, rows), :],
                                      rows_ref.at[s, pl.ds(dst, rows), :],
                                      sems.at[s]).start()

            _for_each_strip(cnt_ref[k], tm, start_copy)
            return c

        lax.fori_loop(0, N_SEG, per_segment, 0)

    @pl.when(i == 0)
    def _():
        fetch_tile(0, 0)
        unsort_ref[0] = _unsort_weights(meta_ref[...], lstart_row_ref[...], nrows)

    @pl.when(i + 1 < n)
    def _():
        fetch_tile(i + 1, 1 - slot)

    pltpu.make_async_copy(ys_ref.at[pl.ds(0, nrows), :], rows_ref.at[slot],
                          sems.at[slot]).wait()
    moe = jnp.dot(unsort_ref[slot], rows_ref[slot].astype(BF16),
                  preferred_element_type=F32)
    x2 = x1_ref[...] + moe
    u = _rms(x2, gple_ref[...])
    gate_ple = jax.nn.sigmoid(jnp.dot(u.astype(BF16), wpg_ref[...],
                                      preferred_element_type=F32))
    emb = jnp.dot(p_ref[...].astype(BF16), wple_ref[...], preferred_element_type=F32)
    x3 = x2 + gate_ple * emb
    out_ref[...] = _rms(x3, gfin_ref[...])

    unsort_ref[1 - slot] = _unsort_weights(meta_next_ref[...], lstart_row_next_ref[...], nrows)


def _combine(meta, tables, x1, p2, ys, g_ple, w_ple_gate_bf, w_ple_bf, g_final, tm):
    t = x1.shape[0]
    full = lambda *shape: pl.BlockSpec(shape, lambda i, *_: (0,) * len(shape))
    last = t // tm - 1
    nxt = lambda i: jnp.minimum(i + 1, last)
    grid_spec = pltpu.PrefetchScalarGridSpec(
        num_scalar_prefetch=3,
        grid=(t // tm,),
        in_specs=[
            pl.BlockSpec((None, META_ROWS, tm), lambda i, *_: (i, 0, 0)),
            pl.BlockSpec((None, 1, N_EXPERTS), lambda i, *_: (i, 0, 0)),
            pl.BlockSpec((None, META_ROWS, tm), lambda i, *_: (nxt(i), 0, 0)),
            pl.BlockSpec((None, 1, N_EXPERTS), lambda i, *_: (nxt(i), 0, 0)),
            pl.BlockSpec((tm, D_MODEL), lambda i, *_: (i, 0)),
            pl.BlockSpec((tm, PLE_DIM), lambda i, *_: (i, 0)),
            pl.BlockSpec(memory_space=pl.ANY),
            full(1, D_MODEL),
            full(D_MODEL, D_MODEL),
            full(PLE_DIM, D_MODEL),
            full(1, D_MODEL),
        ],
        out_specs=pl.BlockSpec((tm, D_MODEL), lambda i, *_: (i, 0)),
        scratch_shapes=[pltpu.VMEM((2, _local_rows(tm), D_MODEL), ROW_DTYPE),
                        pltpu.VMEM((2, tm, _local_rows(tm)), BF16),
                        pltpu.SemaphoreType.DMA((2,))],
    )
    return pl.pallas_call(
        _combine_kernel,
        out_shape=jax.ShapeDtypeStruct((t, D_MODEL), F32),
        grid_spec=grid_spec,
        compiler_params=pltpu.CompilerParams(
            dimension_semantics=("arbitrary",),
            vmem_limit_bytes=VMEM_LIMIT_BYTES),
        name="combine",
    )(tables["base_in"], tables["lstart"], tables["cnt"], meta, tables["lstart_rows"],
      meta, tables["lstart_rows"], x1, p2, ys, g_ple, w_ple_gate_bf, w_ple_bf, g_final)


def _tiles(seq_len):
    return dict(
        proj_rows=min(512, seq_len),
        ret_block=min(256, seq_len),
        route_rows=min(512, seq_len),
        expert_rows=512,
    )


def _slot_tables(tile_counts, blk, n_buf, nloc):
    cnt = tile_counts[:, :, 0].astype(jnp.int32)
    cnt = ((cnt + STRIP_ALIGN - 1) // STRIP_ALIGN) * STRIP_ALIGN
    seg_rows = jnp.sum(cnt, axis=1)
    counts = jnp.sum(cnt, axis=0)
    padded = ((counts + blk - 1) // blk) * blk
    ends_pad = jnp.cumsum(padded)
    starts_pad = ends_pad - padded
    tile_before = jnp.cumsum(cnt, axis=0) - cnt
    lstart = jnp.cumsum(cnt, axis=1) - cnt
    base = starts_pad[None, :] + tile_before
    with_filler = lambda a, col: jnp.concatenate([a, col[:, None]], axis=1).reshape(-1)
    blk_start = jnp.arange((n_buf + SPARE_ROWS) // blk, dtype=jnp.int32) * blk
    last_of_expert = jnp.any((padded > 0)[None, :]
                             & (blk_start[:, None] == (ends_pad - blk)[None, :]), axis=1)
    zflag = jnp.where(last_of_expert | (blk_start >= n_buf), ZERO_FIRST,
                      jnp.where(blk_start >= ends_pad[-1], ZERO_ANYTIME, 0)).astype(jnp.int32)
    filler_dst = n_buf + (jnp.arange(cnt.shape[0], dtype=jnp.int32) % DISPATCH_SLOTS) * FILLER_ROWS
    return dict(
        cnt=with_filler(cnt, nloc - seg_rows),
        lstart=with_filler(lstart, seg_rows),
        base_out=with_filler(base, filler_dst),
        base_in=with_filler(base, jnp.zeros_like(seg_rows)),
        lstart_rows=lstart.astype(F32)[:, None, :],
        lstart_cols=lstart.astype(F32)[:, :, None],
        zflag=zflag, ends_pad=ends_pad, seg_ends=starts_pad + counts)


def _layer(x2, p2, seq_len, g_mix, w_in, w_pool, pool_scale, w_ret_o, w_out, g_ffn,
           w_router, b_router, w_gate_up, b_gate_up, w_down, b_down, g_ple,
           w_ple_gate, w_ple, g_out):
    t = x2.shape[0]
    cfg = _tiles(seq_len)
    row = lambda a: a.reshape(1, -1)

    lane_pad = ((0, 0), (0, LANES - N_EXPERTS))
    wr_hi = w_router.astype(BF16)
    wr_lo = jnp.pad((w_router - wr_hi.astype(F32)).astype(BF16), lane_pad)
    wr_hi = jnp.pad(wr_hi, lane_pad)
    tm = cfg["route_rows"]
    x1, h2, meta, tile_counts = _token_mixer(
        x2.reshape(t // seq_len, seq_len, D_MODEL), row(g_mix), w_in.astype(BF16),
        w_pool.astype(BF16), row(pool_scale), w_ret_o.astype(BF16), w_out.astype(BF16),
        row(g_ffn), wr_hi, wr_lo, jnp.pad(row(b_router), lane_pad), tm, cfg["ret_block"])
    x1 = x1.reshape(t, D_MODEL)
    h2 = h2.reshape(t, D_MODEL)

    blk = cfg["expert_rows"]
    n_buf = t * TOP_K + (t // tm) * FILLER_ROWS + N_EXPERTS * blk
    n_blocks = n_buf // blk
    tables = _slot_tables(tile_counts, blk, n_buf, _local_rows(tm))
    blk_start = jnp.arange(n_blocks, dtype=jnp.int32) * blk
    block_e = jnp.minimum(
        jnp.sum((tables["ends_pad"][None, :] <= blk_start[:, None]).astype(jnp.int32), axis=1),
        N_EXPERTS - 1)
    n_used = tables["ends_pad"][-1:] // blk
    live_rows = jnp.where(blk_start < tables["ends_pad"][-1],
                          jnp.clip(tables["seg_ends"][block_e] - blk_start, 0, blk), 0)

    xs = _dispatch(h2, meta, tables, n_buf, blk, tm)
    ys = _experts(xs, n_buf, block_e, n_used, live_rows.astype(jnp.int32), w_gate_up,
                  b_gate_up, w_down, b_down, blk)
    return _combine(meta, tables, x1, p2, ys, row(g_ple), w_ple_gate.astype(BF16),
                    w_ple.astype(BF16), row(g_out), tm)


def kernel(x, p, g_mix, w_in, w_pool, pool_scale, w_ret_o, w_out, g_ffn, w_router,
           b_router, w_gate_up, b_gate_up, w_down, b_down, g_ple, w_ple_gate, w_ple,
           g_final):
    b, s, d = x.shape
    depth = p.shape[0]
    assert depth == 1 and d == D_MODEL
    x2 = x.reshape(b * s, d)
    out = _layer(x2, p[0].reshape(b * s, PLE_DIM), s, g_mix[0], w_in[0], w_pool[0],
                 pool_scale[0], w_ret_o[0], w_out[0], g_ffn[0], w_router[0],
                 b_router[0], w_gate_up[0], b_gate_up[0], w_down[0], b_down[0],
                 g_ple[0], w_ple_gate[0], w_ple[0], g_final)
    return out.reshape(b, s, d)
```

```python
import functools

import numpy as np
import jax
import jax.numpy as jnp
from jax import lax
from jax.experimental import pallas as pl
from jax.experimental.pallas import tpu as pltpu

F32 = jnp.float32
BF16 = jnp.bfloat16

D_MODEL = 1024
EPS = 1e-6
CHUNK = 64
PLE_DIM = 256
POOL_WINDOWS = (2, 4, 8, 16)
POOL_GROUP_W = D_MODEL // len(POOL_WINDOWS)
POOL_HALO = 16
RET_HEADS = 4
RET_QK_HEAD = 256
RET_V_HEAD = 512
RET_V_W = RET_HEADS * RET_V_HEAD
ROPE_BASE = 10000.0
IN_W = 9 * D_MODEL
N_EXPERTS = 32
TOP_K = 4
D_FF = D_MODEL
SWIGLU_ALPHA = 1.702
SWIGLU_LIMIT = 7.0

VMEM_LIMIT_BYTES = 56 * 1024 * 1024
LANES = 128
META_ROWS = 16
META_EXPERT, META_RANK, META_GATE = 0, TOP_K, 2 * TOP_K
ROW_DTYPE = F32
SUBLANES = 8
STRIP_ALIGN = SUBLANES * 4 // jnp.dtype(ROW_DTYPE).itemsize
FILLER_ROWS = N_EXPERTS * STRIP_ALIGN
N_SEG = N_EXPERTS + 1
ZERO_FIRST, ZERO_ANYTIME = 1, 2
DISPATCH_SLOTS = 3
SPARE_ROWS = 4 * FILLER_ROWS


def _rms(x, g):
    return x * lax.rsqrt(jnp.mean(x * x, axis=-1, keepdims=True) + EPS) * g


COL_U, COL_Q, COL_K = 0, 1, 2
COL_V, COL_G = (3, 4), (5, 6)
COL_GATE_A, COL_GATE_B = 7, 8
SIDE_U, SIDE_GATE_A, SIDE_GATE_B = 0, 1, 2


def _rotary_heads(y, cos, sin):
    half = RET_QK_HEAD // 2
    parts = []
    for h in range(RET_HEADS):
        x1 = y[:, h * RET_QK_HEAD:h * RET_QK_HEAD + half]
        x2 = y[:, h * RET_QK_HEAD + half:(h + 1) * RET_QK_HEAD]
        parts += [x1 * cos - x2 * sin, x2 * cos + x1 * sin]
    return jnp.concatenate(parts, axis=-1)


def _head_decay_logs():
    return [float(np.log(1.0 - 2.0 ** (-5.0 - h))) for h in range(RET_HEADS)]


def _token_mixer_kernel(blk_decay, ret_blk, x_ref, g_ref, w_ref, cos_ref, sin_ref,
                        d_ref, qd_ref, kd_ref,
                        wpool_ref, pscale_ref, wreto_ref, wout_ref, gffn_ref,
                        wr_hi_ref, wr_lo_ref, br_ref,
                        x1_ref, h2_ref, meta_ref, counts_ref,
                        side_ref, o_ref, q_s, k_s, v_s, g_s, state_ref, halo_ref):
    @pl.when(pl.program_id(1) == 0)
    def _():
        state_ref[...] = jnp.zeros_like(state_ref)
        halo_ref[...] = jnp.zeros_like(halo_ref)

    tm = x_ref.shape[0]
    h_in = _rms(x_ref[...], g_ref[...]).astype(BF16)

    def proj(c):
        return jnp.dot(h_in, w_ref[:, c * D_MODEL:(c + 1) * D_MODEL],
                       preferred_element_type=F32)

    def chunk(n):
        return slice(n * D_MODEL, (n + 1) * D_MODEL)

    side_ref[:, chunk(SIDE_U)] = proj(COL_U).astype(BF16)
    q_s[...] = _rotary_heads(proj(COL_Q), cos_ref[...], sin_ref[...]).astype(BF16)
    k_s[...] = (_rotary_heads(proj(COL_K), cos_ref[...], sin_ref[...])
                * (RET_QK_HEAD ** -0.5)).astype(BF16)
    for n, c in enumerate(COL_V):
        v_s[:, chunk(n)] = proj(c).astype(BF16)
    for n, c in enumerate(COL_G):
        y = proj(c)
        g_s[:, chunk(n)] = (y * jax.nn.sigmoid(y)).astype(BF16)
    side_ref[:, chunk(SIDE_GATE_A)] = proj(COL_GATE_A).astype(BF16)
    side_ref[:, chunk(SIDE_GATE_B)] = proj(COL_GATE_B).astype(BF16)

    for b in range(tm // ret_blk):
        rows = slice(b * ret_blk, (b + 1) * ret_blk)
        for h in range(RET_HEADS):
            qk_cols = slice(h * RET_QK_HEAD, (h + 1) * RET_QK_HEAD)
            v_cols = slice(h * RET_V_HEAD, (h + 1) * RET_V_HEAD)
            q = q_s[rows, qk_cols]
            k = k_s[rows, qk_cols]
            v = v_s[rows, v_cols]
            scores = lax.dot_general(q, k, (((1,), (1,)), ((), ())),
                                     preferred_element_type=F32) * d_ref[h]
            state = state_ref[h]
            o = jnp.dot(scores.astype(BF16), v, preferred_element_type=F32)
            o = o + jnp.dot((q.astype(F32) * qd_ref[h]).astype(BF16), state.astype(BF16),
                            preferred_element_type=F32)
            k_dec = (k.astype(F32) * kd_ref[h]).astype(BF16)
            state_ref[h] = state * blk_decay[h] + lax.dot_general(
                k_dec, v, (((0,), (0,)), ((), ())), preferred_element_type=F32)
            o = o * lax.rsqrt(jnp.mean(o * o, axis=-1, keepdims=True) + EPS)
            o_ref[rows, v_cols] = (o * g_s[rows, v_cols].astype(F32)).astype(BF16)

    u_bf = side_ref[:, chunk(SIDE_U)]
    _mix_route(pl.program_id(1) * tm, u_bf, halo_ref[...], side_ref[:, chunk(SIDE_GATE_A)],
               side_ref[:, chunk(SIDE_GATE_B)], o_ref[...], x_ref[...],
               wpool_ref, pscale_ref, wreto_ref, wout_ref, gffn_ref,
               wr_hi_ref, wr_lo_ref, br_ref, x1_ref, h2_ref, meta_ref, counts_ref)
    halo_ref[...] = u_bf[tm - POOL_HALO:, :]


def _rotary_tables(s):
    half = RET_QK_HEAD // 2
    pos = jnp.arange(s, dtype=F32)
    inv = ROPE_BASE ** (-jnp.linspace(0.0, 1.0, half, dtype=F32))
    ang = pos[:, None] * inv[None, :]
    return jnp.cos(ang), jnp.sin(ang)


def _retention_tables(blk):
    log_g = jnp.asarray(_head_decay_logs(), F32)
    idx = jnp.arange(blk, dtype=F32)
    diff = idx[:, None] - idx[None, :]
    chunk = jnp.arange(blk, dtype=jnp.int32) // CHUNK
    visible = chunk[None, :] <= chunk[:, None]
    dmask = jnp.where(visible[None], jnp.exp(log_g[:, None, None] * jnp.abs(diff)[None]), 0.0)
    q_dec = jnp.exp(log_g[:, None] * (idx + 1.0))[:, :, None]
    k_dec = jnp.exp(log_g[:, None] * (blk - 1.0 - idx))[:, :, None]
    q_dec = jnp.broadcast_to(q_dec, (RET_HEADS, blk, RET_QK_HEAD))
    k_dec = jnp.broadcast_to(k_dec, (RET_HEADS, blk, RET_QK_HEAD))
    return dmask.astype(F32), q_dec, k_dec


def _token_mixer(x3, g_mix, w_in_bf, w_pool_bf, pool_scale, w_ret_o_bf, w_out_bf,
                 g_ffn, wr_hi, wr_lo, b_router, tm, ret_blk):
    b, s, _ = x3.shape
    half = RET_QK_HEAD // 2
    tiles_per_seq = s // tm
    n_tiles = b * tiles_per_seq
    cos, sin = _rotary_tables(s)
    dmask, q_dec, k_dec = _retention_tables(ret_blk)
    blk_decay = [float(np.exp(lg * ret_blk)) for lg in _head_decay_logs()]
    once = lambda *shape: pl.BlockSpec(shape, lambda bi, j: (0,) * len(shape),
                                       pipeline_mode=pl.Buffered(1))
    tile = lambda bi, j: bi * tiles_per_seq + j
    return pl.pallas_call(
        functools.partial(_token_mixer_kernel, blk_decay, ret_blk),
        out_shape=(
            jax.ShapeDtypeStruct((b, s, D_MODEL), F32),
            jax.ShapeDtypeStruct((b, s, D_MODEL), BF16),
            jax.ShapeDtypeStruct((n_tiles, META_ROWS, tm), F32),
            jax.ShapeDtypeStruct((n_tiles, N_EXPERTS, 1), F32),
        ),
        grid=(b, tiles_per_seq),
        in_specs=[
            pl.BlockSpec((None, tm, D_MODEL), lambda bi, j: (bi, j, 0)),
            pl.BlockSpec((1, D_MODEL), lambda bi, j: (0, 0)),
            once(D_MODEL, IN_W),
            pl.BlockSpec((tm, half), lambda bi, j: (j, 0)),
            pl.BlockSpec((tm, half), lambda bi, j: (j, 0)),
            once(RET_HEADS, ret_blk, ret_blk),
            once(RET_HEADS, ret_blk, RET_QK_HEAD),
            once(RET_HEADS, ret_blk, RET_QK_HEAD),
            once(len(POOL_WINDOWS), POOL_GROUP_W, POOL_GROUP_W),
            once(1, D_MODEL),
            once(RET_V_W, D_MODEL),
            once(D_MODEL, D_MODEL),
            once(1, D_MODEL),
            once(D_MODEL, LANES),
            once(D_MODEL, LANES),
            once(1, LANES),
        ],
        out_specs=(
            pl.BlockSpec((None, tm, D_MODEL), lambda bi, j: (bi, j, 0)),
            pl.BlockSpec((None, tm, D_MODEL), lambda bi, j: (bi, j, 0)),
            pl.BlockSpec((None, META_ROWS, tm), lambda bi, j: (tile(bi, j), 0, 0)),
            pl.BlockSpec((None, N_EXPERTS, 1), lambda bi, j: (tile(bi, j), 0, 0)),
        ),
        scratch_shapes=[pltpu.VMEM((tm, 3 * D_MODEL), BF16),
                        pltpu.VMEM((tm, RET_V_W), BF16),
                        pltpu.VMEM((tm, D_MODEL), BF16),
                        pltpu.VMEM((tm, D_MODEL), BF16),
                        pltpu.VMEM((tm, RET_V_W), BF16),
                        pltpu.VMEM((tm, RET_V_W), BF16),
                        pltpu.VMEM((RET_HEADS, RET_QK_HEAD, RET_V_HEAD), F32),
                        pltpu.VMEM((POOL_HALO, D_MODEL), BF16)],
        compiler_params=pltpu.CompilerParams(
            dimension_semantics=("arbitrary", "arbitrary"),
            vmem_limit_bytes=VMEM_LIMIT_BYTES),
        name="token_mixer",
    )(x3, g_mix, w_in_bf, cos, sin, dmask, q_dec, k_dec, w_pool_bf, pool_scale,
      w_ret_o_bf, w_out_bf, g_ffn, wr_hi, wr_lo, b_router)


def _window_sum(ext, w, tm):
    cur = ext
    span = 1
    while span < w:
        cur = cur[span:, :] + cur[:-span, :]
        span *= 2
    start = POOL_HALO + 1 - w
    return cur[start:start + tm, :]


def _mix_route(pos0, u_bf, halo_bf, gate_a, gate_b, o_gated, x,
               wpool_ref, pscale_ref, wreto_ref, wout_ref, gffn_ref,
               wr_hi_ref, wr_lo_ref, br_ref,
               x1_ref, h2_ref, meta_ref, counts_ref):
    tm = x.shape[0]

    u = u_bf.astype(F32)
    ext = jnp.concatenate([halo_bf.astype(F32), u], axis=0)
    pos = (pos0 + lax.broadcasted_iota(jnp.int32, (tm, 1), 0)).astype(F32)
    pooled_out = []
    for g, w in enumerate(POOL_WINDOWS):
        cols = slice(g * POOL_GROUP_W, (g + 1) * POOL_GROUP_W)
        ws = _window_sum(ext[:, cols], w, tm)
        count = jnp.minimum(pos + 1.0, float(w))
        pooled = ws / count - u[:, cols]
        pooled_out.append(jnp.dot(pooled.astype(BF16), wpool_ref[g],
                                  preferred_element_type=F32))
    y_pool = jnp.concatenate(pooled_out, axis=-1) * pscale_ref[...]

    y_ret = jnp.dot(o_gated, wreto_ref[...], preferred_element_type=F32)
    merged = (jax.nn.sigmoid(gate_a.astype(F32)) * y_pool
              + jax.nn.sigmoid(gate_b.astype(F32)) * y_ret)
    x1 = x + jnp.dot(merged.astype(BF16), wout_ref[...], preferred_element_type=F32)
    x1_ref[...] = x1
    h2 = _rms(x1, gffn_ref[...])
    h2_ref[...] = h2.astype(BF16)

    h_hi = h2.astype(BF16)
    h_lo = (h2 - h_hi.astype(F32)).astype(BF16)
    logits = (jnp.dot(h_hi, wr_hi_ref[...], preferred_element_type=F32)
              + jnp.dot(h_lo, wr_hi_ref[...], preferred_element_type=F32)
              + jnp.dot(h_hi, wr_lo_ref[...], preferred_element_type=F32)
              + br_ref[...])
    logits_t = logits.T[:N_EXPERTS, :]

    sub = lax.broadcasted_iota(jnp.int32, (N_EXPERTS, tm), 0)
    work = logits_t
    vals, idxs, hots = [], [], []
    for _ in range(TOP_K):
        m = jnp.max(work, axis=0, keepdims=True)
        idx = jnp.min(jnp.where(work == m, sub, N_EXPERTS), axis=0, keepdims=True)
        hot = sub == idx
        vals.append(m)
        idxs.append(idx)
        hots.append(hot)
        work = jnp.where(hot, -jnp.inf, work)
    exps = [jnp.exp(v - vals[0]) for v in vals]
    denom = exps[0] + exps[1] + exps[2] + exps[3]
    gates = [e / denom for e in exps]

    sel = (jnp.where(hots[0], 1.0, 0.0) + jnp.where(hots[1], 1.0, 0.0)
           + jnp.where(hots[2], 1.0, 0.0) + jnp.where(hots[3], 1.0, 0.0))
    row = lax.broadcasted_iota(jnp.int32, (tm, tm), 0)
    col = lax.broadcasted_iota(jnp.int32, (tm, tm), 1)
    earlier = jnp.where(row < col, 1.0, 0.0).astype(BF16)
    before = jnp.dot(sel.astype(BF16), earlier, preferred_element_type=F32)
    counts_ref[...] = jnp.sum(sel, axis=1, keepdims=True)

    msub = lax.broadcasted_iota(jnp.int32, (META_ROWS, tm), 0)
    meta = jnp.zeros((META_ROWS, tm), F32)
    for r in range(TOP_K):
        rank_r = jnp.sum(jnp.where(hots[r], before, 0.0), axis=0, keepdims=True)
        meta = jnp.where(msub == META_EXPERT + r, idxs[r].astype(F32), meta)
        meta = jnp.where(msub == META_RANK + r, rank_r, meta)
        meta = jnp.where(msub == META_GATE + r, gates[r], meta)
    meta_ref[...] = meta


def _local_positions(meta, lstart):
    tm = meta.shape[0]
    lane_e = lax.broadcasted_iota(jnp.int32, (tm, N_EXPERTS), 1)
    out = []
    for r in range(TOP_K):
        e_r = meta[:, META_EXPERT + r:META_EXPERT + r + 1].astype(jnp.int32)
        first = jnp.sum(jnp.where(lane_e == e_r, lstart, 0.0), axis=-1, keepdims=True)
        out.append((meta[:, META_RANK + r:META_RANK + r + 1] + first).astype(jnp.int32))
    return out


def _local_rows(tm):
    return TOP_K * tm + FILLER_ROWS


def _for_each_strip(count, max_rows, start_copy):
    top = 1 << (max(max_rows, FILLER_ROWS).bit_length() - 1)
    pieces = [top >> k for k in range(top.bit_length()) if (top >> k) >= STRIP_ALIGN]
    rare_from = 2 * max_rows * TOP_K // N_EXPERTS
    rare = [p for p in pieces if p >= rare_from]
    rare_bits = sum(rare)

    def cover(sizes, off):
        for piece in sizes:
            take = count & piece

            @pl.when(take != 0)
            def _(off=off, piece=piece):
                start_copy(off, piece)

            off = off + take

    @pl.when((count & rare_bits) != 0)
    def _():
        cover(rare, jnp.int32(0))

    cover([p for p in pieces if p < rare_from], count & rare_bits)


def _dispatch_kernel(blk_rows, base_ref, lstart_ref, cnt_ref, zflag_ref,
                     h_ref, meta_ref, lstart_col_ref, xs_ref,
                     sorted_ref, zero_ref, sems, zsem, tail_sem):
    i = pl.program_id(0)
    tm = h_ref.shape[0]
    nrows = _local_rows(tm)
    slot = lax.rem(i, DISPATCH_SLOTS)

    n_zero_blocks = xs_ref.shape[0] // blk_rows

    def zcopy(b, sem):
        start = pl.multiple_of(b * blk_rows, blk_rows)
        return pltpu.make_async_copy(zero_ref, xs_ref.at[pl.ds(start, blk_rows), :], sem)

    def for_flagged(flag, sem, act):
        def body(b, c):
            @pl.when(zflag_ref[b] == flag)
            def _():
                act(zcopy(b, sem))
            return c
        lax.fori_loop(0, n_zero_blocks, body, 0)

    @pl.when(i == 0)
    def _():
        zero_ref[...] = jnp.zeros_like(zero_ref)
        for_flagged(ZERO_FIRST, zsem, lambda cp: cp.start())
        for_flagged(ZERO_ANYTIME, tail_sem, lambda cp: cp.start())
        for_flagged(ZERO_FIRST, zsem, lambda cp: cp.wait())

    meta_t = meta_ref[...]
    sub_e = lax.broadcasted_iota(jnp.int32, (N_EXPERTS, tm), 0)
    pos = []
    for r in range(TOP_K):
        e_r = meta_t[META_EXPERT + r:META_EXPERT + r + 1, :].astype(jnp.int32)
        first = jnp.sum(jnp.where(sub_e == e_r, lstart_col_ref[...], 0.0), axis=0, keepdims=True)
        pos.append((meta_t[META_RANK + r:META_RANK + r + 1, :] + first).astype(jnp.int16))
    jr = lax.broadcasted_iota(jnp.int16, (nrows, tm), 0)
    one, zero = jnp.ones((), BF16), jnp.zeros((), BF16)
    onehot = jnp.where(jr == pos[0], one, jnp.where(jr == pos[1], one,
             jnp.where(jr == pos[2], one, jnp.where(jr == pos[3], one, zero))))
    srt = jnp.dot(onehot, h_ref[...], preferred_element_type=F32)
    sorted_ref[slot] = srt.astype(ROW_DTYPE)

    def per_segment(e, c):
        k = i * N_SEG + e
        src0 = lstart_ref[k]
        dst0 = base_ref[k]

        def start_copy(off, rows):
            src = pl.multiple_of(src0 + off, STRIP_ALIGN)
            dst = pl.multiple_of(dst0 + off, STRIP_ALIGN)
            pltpu.make_async_copy(sorted_ref.at[slot, pl.ds(src, rows), :],
                                  xs_ref.at[pl.ds(dst, rows), :],
                                  sems.at[slot]).start()

        _for_each_strip(cnt_ref[k], tm, start_copy)
        return c

    lax.fori_loop(0, N_SEG, per_segment, 0)

    def wait_tile(s):
        pltpu.make_async_copy(sorted_ref.at[s], xs_ref.at[pl.ds(0, nrows), :],
                              sems.at[s]).wait()

    @pl.when(i >= DISPATCH_SLOTS - 1)
    def _():
        wait_tile(lax.rem(i + 1, DISPATCH_SLOTS))

    @pl.when(i == pl.num_programs(0) - 1)
    def _():
        for age in range(DISPATCH_SLOTS - 2, -1, -1):
            @pl.when(i >= age)
            def _(age=age):
                wait_tile(lax.rem(i - age, DISPATCH_SLOTS))
        for_flagged(ZERO_ANYTIME, tail_sem, lambda cp: cp.wait())


def _dispatch(h2, meta, tables, n_buf, blk_rows, tm):
    t = h2.shape[0]
    grid_spec = pltpu.PrefetchScalarGridSpec(
        num_scalar_prefetch=4,
        grid=(t // tm,),
        in_specs=[
            pl.BlockSpec((tm, D_MODEL), lambda i, *_: (i, 0)),
            pl.BlockSpec((None, META_ROWS, tm), lambda i, *_: (i, 0, 0)),
            pl.BlockSpec((None, N_EXPERTS, 1), lambda i, *_: (i, 0, 0)),
        ],
        out_specs=pl.BlockSpec(memory_space=pl.ANY),
        scratch_shapes=[pltpu.VMEM((DISPATCH_SLOTS, _local_rows(tm), D_MODEL), ROW_DTYPE),
                        pltpu.VMEM((blk_rows, D_MODEL), ROW_DTYPE),
                        pltpu.SemaphoreType.DMA((DISPATCH_SLOTS,)),
                        pltpu.SemaphoreType.DMA(()),
                        pltpu.SemaphoreType.DMA(())],
    )
    return pl.pallas_call(
        functools.partial(_dispatch_kernel, blk_rows),
        out_shape=jax.ShapeDtypeStruct((n_buf + SPARE_ROWS, D_MODEL), ROW_DTYPE),
        grid_spec=grid_spec,
        compiler_params=pltpu.CompilerParams(
            dimension_semantics=("arbitrary",),
            vmem_limit_bytes=VMEM_LIMIT_BYTES,
            has_side_effects=True),
        name="dispatch",
    )(tables["base_out"], tables["lstart"], tables["cnt"], tables["zflag"],
      h2, meta, tables["lstart_cols"])


def _experts_kernel(be_ref, nu_ref, x_ref, wgu_ref, bgu_ref, wd_ref, bd_ref,
                    y_ref, wgu_bf, wd_bf):
    i = pl.program_id(0)

    @pl.when(i < nu_ref[0])
    def _():
        prev = be_ref[jnp.maximum(i - 1, 0)]

        @pl.when((i == 0) | (be_ref[i] != prev))
        def _():
            wgu_bf[...] = wgu_ref[...].astype(BF16)
            wd_bf[...] = wd_ref[...].astype(BF16)

        gu = jnp.dot(x_ref[...].astype(BF16), wgu_bf[...],
                     preferred_element_type=F32) + bgu_ref[...]
        glu = jnp.minimum(gu[:, :D_FF], SWIGLU_LIMIT)
        lin = jnp.clip(gu[:, D_FF:], -SWIGLU_LIMIT, SWIGLU_LIMIT)
        act = glu * jax.nn.sigmoid(SWIGLU_ALPHA * glu) * (lin + 1.0)
        y_ref[...] = (jnp.dot(act.astype(BF16), wd_bf[...],
                              preferred_element_type=F32) + bd_ref[...]).astype(ROW_DTYPE)

    @pl.when(i >= nu_ref[0])
    def _():
        y_ref[...] = jnp.zeros_like(y_ref)


def _experts(xs, n_buf, block_e, n_used, w_gate_up, b_gate_up, w_down, b_down, blk_rows):
    used = lambda i, be, nu: jnp.minimum(i, nu[0] - 1)
    grid_spec = pltpu.PrefetchScalarGridSpec(
        num_scalar_prefetch=2,
        grid=(n_buf // blk_rows,),
        in_specs=[
            pl.BlockSpec((blk_rows, D_MODEL), lambda i, be, nu: (used(i, be, nu), 0)),
            pl.BlockSpec((None, D_MODEL, 2 * D_FF), lambda i, be, nu: (be[i], 0, 0)),
            pl.BlockSpec((None, 1, 2 * D_FF), lambda i, be, nu: (be[i], 0, 0)),
            pl.BlockSpec((None, D_FF, D_MODEL), lambda i, be, nu: (be[i], 0, 0)),
            pl.BlockSpec((None, 1, D_MODEL), lambda i, be, nu: (be[i], 0, 0)),
        ],
        out_specs=pl.BlockSpec((blk_rows, D_MODEL), lambda i, be, nu: (i, 0)),
        scratch_shapes=[pltpu.VMEM((D_MODEL, 2 * D_FF), BF16),
                        pltpu.VMEM((D_FF, D_MODEL), BF16)],
    )
    return pl.pallas_call(
        _experts_kernel,
        out_shape=jax.ShapeDtypeStruct((n_buf, D_MODEL), ROW_DTYPE),
        grid_spec=grid_spec,
        compiler_params=pltpu.CompilerParams(
            dimension_semantics=("arbitrary",),
            vmem_limit_bytes=VMEM_LIMIT_BYTES),
        name="experts",
    )(block_e, n_used, xs, w_gate_up, b_gate_up[:, None, :], w_down, b_down[:, None, :])


def _unsort_weights(meta_t, lstart_row, nrows):
    tm = meta_t.shape[1]
    meta = jnp.concatenate(
        [meta_t, jnp.zeros((LANES - META_ROWS, tm), F32)], axis=0).T
    pos = _local_positions(meta, lstart_row)
    jl = lax.broadcasted_iota(jnp.int16, (tm, nrows), 1)
    pos = [p.astype(jnp.int16) for p in pos]
    gate = [meta[:, META_GATE + r:META_GATE + r + 1].astype(BF16) for r in range(TOP_K)]
    zero = jnp.zeros((), BF16)
    return jnp.where(jl == pos[0], gate[0], jnp.where(jl == pos[1], gate[1],
           jnp.where(jl == pos[2], gate[2], jnp.where(jl == pos[3], gate[3], zero))))


def _combine_kernel(base_ref, lstart_ref, cnt_ref, meta_ref, lstart_row_ref,
                    meta_next_ref, lstart_row_next_ref, x1_ref,
                    p_ref, ys_ref, gple_ref, wpg_ref, wple_ref, gfin_ref, out_ref,
                    rows_ref, unsort_ref, sems):
    i = pl.program_id(0)
    n = pl.num_programs(0)
    tm = x1_ref.shape[0]
    nrows = _local_rows(tm)
    slot = lax.rem(i, 2)

    def fetch_tile(tile, s):
        def per_segment(e, c):
            k = tile * N_SEG + e
            src0 = base_ref[k]
            dst0 = lstart_ref[k]

            def start_copy(off, rows):
                src = pl.multiple_of(src0 + off, STRIP_ALIGN)
                dst = pl.multiple_of(dst0 + off, STRIP_ALIGN)
                pltpu.make_async_copy(ys_ref.at[pl.ds(src, rows), :],
                                      rows_ref.at[s, pl.ds(dst, rows), :],
                                      sems.at[s]).start()

            _for_each_strip(cnt_ref[k], tm, start_copy)
            return c

        lax.fori_loop(0, N_SEG, per_segment, 0)

    @pl.when(i == 0)
    def _():
        fetch_tile(0, 0)
        unsort_ref[0] = _unsort_weights(meta_ref[...], lstart_row_ref[...], nrows)

    @pl.when(i + 1 < n)
    def _():
        fetch_tile(i + 1, 1 - slot)

    pltpu.make_async_copy(ys_ref.at[pl.ds(0, nrows), :], rows_ref.at[slot],
                          sems.at[slot]).wait()
    moe = jnp.dot(unsort_ref[slot], rows_ref[slot].astype(BF16),
                  preferred_element_type=F32)
    x2 = x1_ref[...] + moe
    u = _rms(x2, gple_ref[...])
    gate_ple = jax.nn.sigmoid(jnp.dot(u.astype(BF16), wpg_ref[...],
                                      preferred_element_type=F32))
    emb = jnp.dot(p_ref[...].astype(BF16), wple_ref[...], preferred_element_type=F32)
    x3 = x2 + gate_ple * emb
    out_ref[...] = _rms(x3, gfin_ref[...])

    unsort_ref[1 - slot] = _unsort_weights(meta_next_ref[...], lstart_row_next_ref[...], nrows)


def _combine(meta, tables, x1, p2, ys, g_ple, w_ple_gate_bf, w_ple_bf, g_final, tm):
    t = x1.shape[0]
    full = lambda *shape: pl.BlockSpec(shape, lambda i, *_: (0,) * len(shape))
    last = t // tm - 1
    nxt = lambda i: jnp.minimum(i + 1, last)
    grid_spec = pltpu.PrefetchScalarGridSpec(
        num_scalar_prefetch=3,
        grid=(t // tm,),
        in_specs=[
            pl.BlockSpec((None, META_ROWS, tm), lambda i, *_: (i, 0, 0)),
            pl.BlockSpec((None, 1, N_EXPERTS), lambda i, *_: (i, 0, 0)),
            pl.BlockSpec((None, META_ROWS, tm), lambda i, *_: (nxt(i), 0, 0)),
            pl.BlockSpec((None, 1, N_EXPERTS), lambda i, *_: (nxt(i), 0, 0)),
            pl.BlockSpec((tm, D_MODEL), lambda i, *_: (i, 0)),
            pl.BlockSpec((tm, PLE_DIM), lambda i, *_: (i, 0)),
            pl.BlockSpec(memory_space=pl.ANY),
            full(1, D_MODEL),
            full(D_MODEL, D_MODEL),
            full(PLE_DIM, D_MODEL),
            full(1, D_MODEL),
        ],
        out_specs=pl.BlockSpec((tm, D_MODEL), lambda i, *_: (i, 0)),
        scratch_shapes=[pltpu.VMEM((2, _local_rows(tm), D_MODEL), ROW_DTYPE),
                        pltpu.VMEM((2, tm, _local_rows(tm)), BF16),
                        pltpu.SemaphoreType.DMA((2,))],
    )
    return pl.pallas_call(
        _combine_kernel,
        out_shape=jax.ShapeDtypeStruct((t, D_MODEL), F32),
        grid_spec=grid_spec,
        compiler_params=pltpu.CompilerParams(
            dimension_semantics=("arbitrary",),
            vmem_limit_bytes=VMEM_LIMIT_BYTES),
        name="combine",
    )(tables["base_in"], tables["lstart"], tables["cnt"], meta, tables["lstart_rows"],
      meta, tables["lstart_rows"], x1, p2, ys, g_ple, w_ple_gate_bf, w_ple_bf, g_final)


def _tiles(seq_len):
    return dict(
        proj_rows=min(512, seq_len),
        ret_block=min(256, seq_len),
        route_rows=min(512, seq_len),
        expert_rows=512,
    )


def _slot_tables(tile_counts, blk, n_buf, nloc):
    cnt = tile_counts[:, :, 0].astype(jnp.int32)
    cnt = ((cnt + STRIP_ALIGN - 1) // STRIP_ALIGN) * STRIP_ALIGN
    seg_rows = jnp.sum(cnt, axis=1)
    counts = jnp.sum(cnt, axis=0)
    padded = ((counts + blk - 1) // blk) * blk
    ends_pad = jnp.cumsum(padded)
    starts_pad = ends_pad - padded
    tile_before = jnp.cumsum(cnt, axis=0) - cnt
    lstart = jnp.cumsum(cnt, axis=1) - cnt
    base = starts_pad[None, :] + tile_before
    with_filler = lambda a, col: jnp.concatenate([a, col[:, None]], axis=1).reshape(-1)
    blk_start = jnp.arange((n_buf + SPARE_ROWS) // blk, dtype=jnp.int32) * blk
    last_of_expert = jnp.any((padded > 0)[None, :]
                             & (blk_start[:, None] == (ends_pad - blk)[None, :]), axis=1)
    zflag = jnp.where(last_of_expert | (blk_start >= n_buf), ZERO_FIRST,
                      jnp.where(blk_start >= ends_pad[-1], ZERO_ANYTIME, 0)).astype(jnp.int32)
    filler_dst = n_buf + (jnp.arange(cnt.shape[0], dtype=jnp.int32) % DISPATCH_SLOTS) * FILLER_ROWS
    return dict(
        cnt=with_filler(cnt, nloc - seg_rows),
        lstart=with_filler(lstart, seg_rows),
        base_out=with_filler(base, filler_dst),
        base_in=with_filler(base, jnp.zeros_like(seg_rows)),
        lstart_rows=lstart.astype(F32)[:, None, :],
        lstart_cols=lstart.astype(F32)[:, :, None],
        zflag=zflag, ends_pad=ends_pad)


def _layer(x2, p2, seq_len, g_mix, w_in, w_pool, pool_scale, w_ret_o, w_out, g_ffn,
           w_router, b_router, w_gate_up, b_gate_up, w_down, b_down, g_ple,
           w_ple_gate, w_ple, g_out):
    t = x2.shape[0]
    cfg = _tiles(seq_len)
    row = lambda a: a.reshape(1, -1)

    lane_pad = ((0, 0), (0, LANES - N_EXPERTS))
    wr_hi = w_router.astype(BF16)
    wr_lo = jnp.pad((w_router - wr_hi.astype(F32)).astype(BF16), lane_pad)
    wr_hi = jnp.pad(wr_hi, lane_pad)
    tm = cfg["route_rows"]
    x1, h2, meta, tile_counts = _token_mixer(
        x2.reshape(t // seq_len, seq_len, D_MODEL), row(g_mix), w_in.astype(BF16),
        w_pool.astype(BF16), row(pool_scale), w_ret_o.astype(BF16), w_out.astype(BF16),
        row(g_ffn), wr_hi, wr_lo, jnp.pad(row(b_router), lane_pad), tm, cfg["ret_block"])
    x1 = x1.reshape(t, D_MODEL)
    h2 = h2.reshape(t, D_MODEL)

    blk = cfg["expert_rows"]
    n_buf = t * TOP_K + (t // tm) * FILLER_ROWS + N_EXPERTS * blk
    n_blocks = n_buf // blk
    tables = _slot_tables(tile_counts, blk, n_buf, _local_rows(tm))
    blk_start = jnp.arange(n_blocks, dtype=jnp.int32) * blk
    block_e = jnp.minimum(
        jnp.sum((tables["ends_pad"][None, :] <= blk_start[:, None]).astype(jnp.int32), axis=1),
        N_EXPERTS - 1)
    n_used = tables["ends_pad"][-1:] // blk

    xs = _dispatch(h2, meta, tables, n_buf, blk, tm)
    ys = _experts(xs, n_buf, block_e, n_used, w_gate_up, b_gate_up, w_down, b_down, blk)
    return _combine(meta, tables, x1, p2, ys, row(g_ple), w_ple_gate.astype(BF16),
                    w_ple.astype(BF16), row(g_out), tm)


def kernel(x, p, g_mix, w_in, w_pool, pool_scale, w_ret_o, w_out, g_ffn, w_router,
           b_router, w_gate_up, b_gate_up, w_down, b_down, g_ple, w_ple_gate, w_ple,
           g_final):
    b, s, d = x.shape
    depth = p.shape[0]
    assert depth == 1 and d == D_MODEL
    x2 = x.reshape(b * s, d)
    out = _layer(x2, p[0].reshape(b * s, PLE_DIM), s, g_mix[0], w_in[0], w_pool[0],
                 pool_scale[0], w_ret_o[0], w_out[0], g_ffn[0], w_router[0],
                 b_router[0], w_gate_up[0], b_gate_up[0], w_down[0], b_down[0],
                 g_ple[0], w_ple_gate[0], w_ple[0], g_final)
    return out.reshape(b, s, d)
```

```python
import functools

import numpy as np
import jax
import jax.numpy as jnp
from jax import lax
from jax.experimental import pallas as pl
from jax.experimental.pallas import tpu as pltpu

F32 = jnp.float32
BF16 = jnp.bfloat16

D_MODEL = 1024
EPS = 1e-6
CHUNK = 64
PLE_DIM = 256
POOL_WINDOWS = (2, 4, 8, 16)
POOL_GROUP_W = D_MODEL // len(POOL_WINDOWS)
POOL_HALO = 16
RET_HEADS = 4
RET_QK_HEAD = 256
RET_V_HEAD = 512
RET_V_W = RET_HEADS * RET_V_HEAD
ROPE_BASE = 10000.0
IN_W = 9 * D_MODEL
N_EXPERTS = 32
TOP_K = 4
D_FF = D_MODEL
SWIGLU_ALPHA = 1.702
SWIGLU_LIMIT = 7.0

VMEM_LIMIT_BYTES = 56 * 1024 * 1024
LANES = 128
META_ROWS = 16
META_EXPERT, META_RANK, META_GATE = 0, TOP_K, 2 * TOP_K
ROW_DTYPE = F32
SUBLANES = 8
STRIP_ALIGN = SUBLANES * 4 // jnp.dtype(ROW_DTYPE).itemsize
FILLER_ROWS = N_EXPERTS * STRIP_ALIGN
N_SEG = N_EXPERTS + 1
ZERO_FIRST, ZERO_ANYTIME = 1, 2
DISPATCH_SLOTS = 3
SPARE_ROWS = 4 * FILLER_ROWS


def _rms(x, g):
    return x * lax.rsqrt(jnp.mean(x * x, axis=-1, keepdims=True) + EPS) * g


COL_U, COL_Q, COL_K = 0, 1, 2
COL_V, COL_G = (3, 4), (5, 6)
COL_GATE_A, COL_GATE_B = 7, 8
SIDE_U, SIDE_GATE_A, SIDE_GATE_B = 0, 1, 2


def _rotary_heads(y, cos, sin):
    half = RET_QK_HEAD // 2
    parts = []
    for h in range(RET_HEADS):
        x1 = y[:, h * RET_QK_HEAD:h * RET_QK_HEAD + half]
        x2 = y[:, h * RET_QK_HEAD + half:(h + 1) * RET_QK_HEAD]
        parts += [x1 * cos - x2 * sin, x2 * cos + x1 * sin]
    return jnp.concatenate(parts, axis=-1)


def _head_decay_logs():
    return [float(np.log(1.0 - 2.0 ** (-5.0 - h))) for h in range(RET_HEADS)]


def _token_mixer_kernel(blk_decay, ret_blk, x_ref, g_ref, w_ref, cos_ref, sin_ref,
                        d_ref, qd_ref, kd_ref,
                        wpool_ref, pscale_ref, wreto_ref, wout_ref, gffn_ref,
                        wr_hi_ref, wr_lo_ref, br_ref,
                        x1_ref, h2_ref, meta_ref, counts_ref,
                        side_ref, o_ref, q_s, k_s, v_s, g_s, state_ref, halo_ref):
    @pl.when(pl.program_id(1) == 0)
    def _():
        state_ref[...] = jnp.zeros_like(state_ref)
        halo_ref[...] = jnp.zeros_like(halo_ref)

    tm = x_ref.shape[0]
    h_in = _rms(x_ref[...], g_ref[...]).astype(BF16)

    def proj(c):
        return jnp.dot(h_in, w_ref[:, c * D_MODEL:(c + 1) * D_MODEL],
                       preferred_element_type=F32)

    def chunk(n):
        return slice(n * D_MODEL, (n + 1) * D_MODEL)

    side_ref[:, chunk(SIDE_U)] = proj(COL_U).astype(BF16)
    q_s[...] = _rotary_heads(proj(COL_Q), cos_ref[...], sin_ref[...]).astype(BF16)
    k_s[...] = (_rotary_heads(proj(COL_K), cos_ref[...], sin_ref[...])
                * (RET_QK_HEAD ** -0.5)).astype(BF16)
    for n, c in enumerate(COL_V):
        v_s[:, chunk(n)] = proj(c).astype(BF16)
    for n, c in enumerate(COL_G):
        y = proj(c)
        g_s[:, chunk(n)] = (y * jax.nn.sigmoid(y)).astype(BF16)
    side_ref[:, chunk(SIDE_GATE_A)] = proj(COL_GATE_A).astype(BF16)
    side_ref[:, chunk(SIDE_GATE_B)] = proj(COL_GATE_B).astype(BF16)

    for b in range(tm // ret_blk):
        rows = slice(b * ret_blk, (b + 1) * ret_blk)
        for h in range(RET_HEADS):
            qk_cols = slice(h * RET_QK_HEAD, (h + 1) * RET_QK_HEAD)
            v_cols = slice(h * RET_V_HEAD, (h + 1) * RET_V_HEAD)
            q = q_s[rows, qk_cols]
            k = k_s[rows, qk_cols]
            v = v_s[rows, v_cols]
            scores = lax.dot_general(q, k, (((1,), (1,)), ((), ())),
                                     preferred_element_type=F32) * d_ref[h]
            state = state_ref[h]
            o = jnp.dot(scores.astype(BF16), v, preferred_element_type=F32)
            o = o + jnp.dot((q.astype(F32) * qd_ref[h]).astype(BF16), state.astype(BF16),
                            preferred_element_type=F32)
            k_dec = (k.astype(F32) * kd_ref[h]).astype(BF16)
            state_ref[h] = state * blk_decay[h] + lax.dot_general(
                k_dec, v, (((0,), (0,)), ((), ())), preferred_element_type=F32)
            o = o * lax.rsqrt(jnp.mean(o * o, axis=-1, keepdims=True) + EPS)
            o_ref[rows, v_cols] = (o * g_s[rows, v_cols].astype(F32)).astype(BF16)

    u_bf = side_ref[:, chunk(SIDE_U)]
    _mix_route(pl.program_id(1) * tm, u_bf, halo_ref[...], side_ref[:, chunk(SIDE_GATE_A)],
               side_ref[:, chunk(SIDE_GATE_B)], o_ref[...], x_ref[...],
               wpool_ref, pscale_ref, wreto_ref, wout_ref, gffn_ref,
               wr_hi_ref, wr_lo_ref, br_ref, x1_ref, h2_ref, meta_ref, counts_ref)
    halo_ref[...] = u_bf[tm - POOL_HALO:, :]


def _rotary_tables(s):
    half = RET_QK_HEAD // 2
    pos = jnp.arange(s, dtype=F32)
    inv = ROPE_BASE ** (-jnp.linspace(0.0, 1.0, half, dtype=F32))
    ang = pos[:, None] * inv[None, :]
    return jnp.cos(ang), jnp.sin(ang)


def _retention_tables(blk):
    log_g = jnp.asarray(_head_decay_logs(), F32)
    idx = jnp.arange(blk, dtype=F32)
    diff = idx[:, None] - idx[None, :]
    chunk = jnp.arange(blk, dtype=jnp.int32) // CHUNK
    visible = chunk[None, :] <= chunk[:, None]
    dmask = jnp.where(visible[None], jnp.exp(log_g[:, None, None] * jnp.abs(diff)[None]), 0.0)
    q_dec = jnp.exp(log_g[:, None] * (idx + 1.0))[:, :, None]
    k_dec = jnp.exp(log_g[:, None] * (blk - 1.0 - idx))[:, :, None]
    q_dec = jnp.broadcast_to(q_dec, (RET_HEADS, blk, RET_QK_HEAD))
    k_dec = jnp.broadcast_to(k_dec, (RET_HEADS, blk, RET_QK_HEAD))
    return dmask.astype(F32), q_dec, k_dec


def _token_mixer(x3, g_mix, w_in_bf, w_pool_bf, pool_scale, w_ret_o_bf, w_out_bf,
                 g_ffn, wr_hi, wr_lo, b_router, tm, ret_blk):
    b, s, _ = x3.shape
    half = RET_QK_HEAD // 2
    tiles_per_seq = s // tm
    n_tiles = b * tiles_per_seq
    cos, sin = _rotary_tables(s)
    dmask, q_dec, k_dec = _retention_tables(ret_blk)
    blk_decay = [float(np.exp(lg * ret_blk)) for lg in _head_decay_logs()]
    once = lambda *shape: pl.BlockSpec(shape, lambda bi, j: (0,) * len(shape),
                                       pipeline_mode=pl.Buffered(1))
    tile = lambda bi, j: bi * tiles_per_seq + j
    return pl.pallas_call(
        functools.partial(_token_mixer_kernel, blk_decay, ret_blk),
        out_shape=(
            jax.ShapeDtypeStruct((b, s, D_MODEL), F32),
            jax.ShapeDtypeStruct((b, s, D_MODEL), BF16),
            jax.ShapeDtypeStruct((n_tiles, META_ROWS, tm), F32),
            jax.ShapeDtypeStruct((n_tiles, N_EXPERTS, 1), F32),
        ),
        grid=(b, tiles_per_seq),
        in_specs=[
            pl.BlockSpec((None, tm, D_MODEL), lambda bi, j: (bi, j, 0)),
            pl.BlockSpec((1, D_MODEL), lambda bi, j: (0, 0)),
            once(D_MODEL, IN_W),
            pl.BlockSpec((tm, half), lambda bi, j: (j, 0)),
            pl.BlockSpec((tm, half), lambda bi, j: (j, 0)),
            once(RET_HEADS, ret_blk, ret_blk),
            once(RET_HEADS, ret_blk, RET_QK_HEAD),
            once(RET_HEADS, ret_blk, RET_QK_HEAD),
            once(len(POOL_WINDOWS), POOL_GROUP_W, POOL_GROUP_W),
            once(1, D_MODEL),
            once(RET_V_W, D_MODEL),
            once(D_MODEL, D_MODEL),
            once(1, D_MODEL),
            once(D_MODEL, LANES),
            once(D_MODEL, LANES),
            once(1, LANES),
        ],
        out_specs=(
            pl.BlockSpec((None, tm, D_MODEL), lambda bi, j: (bi, j, 0)),
            pl.BlockSpec((None, tm, D_MODEL), lambda bi, j: (bi, j, 0)),
            pl.BlockSpec((None, META_ROWS, tm), lambda bi, j: (tile(bi, j), 0, 0)),
            pl.BlockSpec((None, N_EXPERTS, 1), lambda bi, j: (tile(bi, j), 0, 0)),
        ),
        scratch_shapes=[pltpu.VMEM((tm, 3 * D_MODEL), BF16),
                        pltpu.VMEM((tm, RET_V_W), BF16),
                        pltpu.VMEM((tm, D_MODEL), BF16),
                        pltpu.VMEM((tm, D_MODEL), BF16),
                        pltpu.VMEM((tm, RET_V_W), BF16),
                        pltpu.VMEM((tm, RET_V_W), BF16),
                        pltpu.VMEM((RET_HEADS, RET_QK_HEAD, RET_V_HEAD), F32),
                        pltpu.VMEM((POOL_HALO, D_MODEL), BF16)],
        compiler_params=pltpu.CompilerParams(
            dimension_semantics=("arbitrary", "arbitrary"),
            vmem_limit_bytes=VMEM_LIMIT_BYTES),
        name="token_mixer",
    )(x3, g_mix, w_in_bf, cos, sin, dmask, q_dec, k_dec, w_pool_bf, pool_scale,
      w_ret_o_bf, w_out_bf, g_ffn, wr_hi, wr_lo, b_router)


def _window_sum(ext, w, tm):
    cur = ext
    span = 1
    while span < w:
        cur = cur[span:, :] + cur[:-span, :]
        span *= 2
    start = POOL_HALO + 1 - w
    return cur[start:start + tm, :]


def _mix_route(pos0, u_bf, halo_bf, gate_a, gate_b, o_gated, x,
               wpool_ref, pscale_ref, wreto_ref, wout_ref, gffn_ref,
               wr_hi_ref, wr_lo_ref, br_ref,
               x1_ref, h2_ref, meta_ref, counts_ref):
    tm = x.shape[0]

    u = u_bf.astype(F32)
    ext = jnp.concatenate([halo_bf.astype(F32), u], axis=0)
    pos = (pos0 + lax.broadcasted_iota(jnp.int32, (tm, 1), 0)).astype(F32)
    pooled_out = []
    for g, w in enumerate(POOL_WINDOWS):
        cols = slice(g * POOL_GROUP_W, (g + 1) * POOL_GROUP_W)
        ws = _window_sum(ext[:, cols], w, tm)
        count = jnp.minimum(pos + 1.0, float(w))
        pooled = ws / count - u[:, cols]
        pooled_out.append(jnp.dot(pooled.astype(BF16), wpool_ref[g],
                                  preferred_element_type=F32))
    y_pool = jnp.concatenate(pooled_out, axis=-1) * pscale_ref[...]

    y_ret = jnp.dot(o_gated, wreto_ref[...], preferred_element_type=F32)
    merged = (jax.nn.sigmoid(gate_a.astype(F32)) * y_pool
              + jax.nn.sigmoid(gate_b.astype(F32)) * y_ret)
    x1 = x + jnp.dot(merged.astype(BF16), wout_ref[...], preferred_element_type=F32)
    x1_ref[...] = x1
    h2 = _rms(x1, gffn_ref[...])
    h2_ref[...] = h2.astype(BF16)

    h_hi = h2.astype(BF16)
    h_lo = (h2 - h_hi.astype(F32)).astype(BF16)
    logits = (jnp.dot(h_hi, wr_hi_ref[...], preferred_element_type=F32)
              + jnp.dot(h_lo, wr_hi_ref[...], preferred_element_type=F32)
              + jnp.dot(h_hi, wr_lo_ref[...], preferred_element_type=F32)
              + br_ref[...])
    logits_t = logits.T[:N_EXPERTS, :]

    sub = lax.broadcasted_iota(jnp.int32, (N_EXPERTS, tm), 0)
    work = logits_t
    vals, idxs, hots = [], [], []
    for _ in range(TOP_K):
        m = jnp.max(work, axis=0, keepdims=True)
        idx = jnp.min(jnp.where(work == m, sub, N_EXPERTS), axis=0, keepdims=True)
        hot = sub == idx
        vals.append(m)
        idxs.append(idx)
        hots.append(hot)
        work = jnp.where(hot, -jnp.inf, work)
    exps = [jnp.exp(v - vals[0]) for v in vals]
    denom = exps[0] + exps[1] + exps[2] + exps[3]
    gates = [e / denom for e in exps]

    sel = (jnp.where(hots[0], 1.0, 0.0) + jnp.where(hots[1], 1.0, 0.0)
           + jnp.where(hots[2], 1.0, 0.0) + jnp.where(hots[3], 1.0, 0.0))
    row = lax.broadcasted_iota(jnp.int32, (tm, tm), 0)
    col = lax.broadcasted_iota(jnp.int32, (tm, tm), 1)
    earlier = jnp.where(row < col, 1.0, 0.0).astype(BF16)
    before = jnp.dot(sel.astype(BF16), earlier, preferred_element_type=F32)
    counts_ref[...] = jnp.sum(sel, axis=1, keepdims=True)

    msub = lax.broadcasted_iota(jnp.int32, (META_ROWS, tm), 0)
    meta = jnp.zeros((META_ROWS, tm), F32)
    for r in range(TOP_K):
        rank_r = jnp.sum(jnp.where(hots[r], before, 0.0), axis=0, keepdims=True)
        meta = jnp.where(msub == META_EXPERT + r, idxs[r].astype(F32), meta)
        meta = jnp.where(msub == META_RANK + r, rank_r, meta)
        meta = jnp.where(msub == META_GATE + r, gates[r], meta)
    meta_ref[...] = meta


def _local_positions(meta, lstart):
    tm = meta.shape[0]
    lane_e = lax.broadcasted_iota(jnp.int32, (tm, N_EXPERTS), 1)
    out = []
    for r in range(TOP_K):
        e_r = meta[:, META_EXPERT + r:META_EXPERT + r + 1].astype(jnp.int32)
        first = jnp.sum(jnp.where(lane_e == e_r, lstart, 0.0), axis=-1, keepdims=True)
        out.append((meta[:, META_RANK + r:META_RANK + r + 1] + first).astype(jnp.int32))
    return out


def _local_rows(tm):
    return TOP_K * tm + FILLER_ROWS


def _for_each_strip(count, max_rows, start_copy):
    top = 1 << (max(max_rows, FILLER_ROWS).bit_length() - 1)
    pieces = [top >> k for k in range(top.bit_length()) if (top >> k) >= STRIP_ALIGN]
    rare_from = 2 * max_rows * TOP_K // N_EXPERTS
    rare = [p for p in pieces if p >= rare_from]
    rare_bits = sum(rare)

    def cover(sizes, off):
        for piece in sizes:
            take = count & piece

            @pl.when(take != 0)
            def _(off=off, piece=piece):
                start_copy(off, piece)

            off = off + take

    @pl.when((count & rare_bits) != 0)
    def _():
        cover(rare, jnp.int32(0))

    cover([p for p in pieces if p < rare_from], count & rare_bits)


def _dispatch_kernel(blk_rows, base_ref, lstart_ref, cnt_ref, zflag_ref,
                     h_ref, meta_ref, lstart_col_ref, xs_ref,
                     sorted_ref, zero_ref, sems, zsem, tail_sem):
    i = pl.program_id(0)
    tm = h_ref.shape[0]
    nrows = _local_rows(tm)
    slot = lax.rem(i, DISPATCH_SLOTS)

    n_zero_blocks = xs_ref.shape[0] // blk_rows

    def zcopy(b, sem):
        start = pl.multiple_of(b * blk_rows, blk_rows)
        return pltpu.make_async_copy(zero_ref, xs_ref.at[pl.ds(start, blk_rows), :], sem)

    def for_flagged(flag, sem, act):
        def body(b, c):
            @pl.when(zflag_ref[b] == flag)
            def _():
                act(zcopy(b, sem))
            return c
        lax.fori_loop(0, n_zero_blocks, body, 0)

    @pl.when(i == 0)
    def _():
        zero_ref[...] = jnp.zeros_like(zero_ref)
        for_flagged(ZERO_FIRST, zsem, lambda cp: cp.start())
        for_flagged(ZERO_ANYTIME, tail_sem, lambda cp: cp.start())
        for_flagged(ZERO_FIRST, zsem, lambda cp: cp.wait())

    meta_t = meta_ref[...]
    sub_e = lax.broadcasted_iota(jnp.int32, (N_EXPERTS, tm), 0)
    pos = []
    for r in range(TOP_K):
        e_r = meta_t[META_EXPERT + r:META_EXPERT + r + 1, :].astype(jnp.int32)
        first = jnp.sum(jnp.where(sub_e == e_r, lstart_col_ref[...], 0.0), axis=0, keepdims=True)
        pos.append((meta_t[META_RANK + r:META_RANK + r + 1, :] + first).astype(jnp.int16))
    jr = lax.broadcasted_iota(jnp.int16, (nrows, tm), 0)
    one, zero = jnp.ones((), BF16), jnp.zeros((), BF16)
    onehot = jnp.where(jr == pos[0], one, jnp.where(jr == pos[1], one,
             jnp.where(jr == pos[2], one, jnp.where(jr == pos[3], one, zero))))
    srt = jnp.dot(onehot, h_ref[...], preferred_element_type=F32)
    sorted_ref[slot] = srt.astype(ROW_DTYPE)

    def per_segment(e, c):
        k = i * N_SEG + e
        src0 = lstart_ref[k]
        dst0 = base_ref[k]

        def start_copy(off, rows):
            src = pl.multiple_of(src0 + off, STRIP_ALIGN)
            dst = pl.multiple_of(dst0 + off, STRIP_ALIGN)
            pltpu.make_async_copy(sorted_ref.at[slot, pl.ds(src, rows), :],
                                  xs_ref.at[pl.ds(dst, rows), :],
                                  sems.at[slot]).start()

        _for_each_strip(cnt_ref[k], tm, start_copy)
        return c

    lax.fori_loop(0, N_SEG, per_segment, 0)

    def wait_tile(s):
        pltpu.make_async_copy(sorted_ref.at[s], xs_ref.at[pl.ds(0, nrows), :],
                              sems.at[s]).wait()

    @pl.when(i >= DISPATCH_SLOTS - 1)
    def _():
        wait_tile(lax.rem(i + 1, DISPATCH_SLOTS))

    @pl.when(i == pl.num_programs(0) - 1)
    def _():
        for age in range(DISPATCH_SLOTS - 2, -1, -1):
            @pl.when(i >= age)
            def _(age=age):
                wait_tile(lax.rem(i - age, DISPATCH_SLOTS))
        for_flagged(ZERO_ANYTIME, tail_sem, lambda cp: cp.wait())


def _dispatch(h2, meta, tables, n_buf, blk_rows, tm):
    t = h2.shape[0]
    grid_spec = pltpu.PrefetchScalarGridSpec(
        num_scalar_prefetch=4,
        grid=(t // tm,),
        in_specs=[
            pl.BlockSpec((tm, D_MODEL), lambda i, *_: (i, 0)),
            pl.BlockSpec((None, META_ROWS, tm), lambda i, *_: (i, 0, 0)),
            pl.BlockSpec((None, N_EXPERTS, 1), lambda i, *_: (i, 0, 0)),
        ],
        out_specs=pl.BlockSpec(memory_space=pl.ANY),
        scratch_shapes=[pltpu.VMEM((DISPATCH_SLOTS, _local_rows(tm), D_MODEL), ROW_DTYPE),
                        pltpu.VMEM((blk_rows, D_MODEL), ROW_DTYPE),
                        pltpu.SemaphoreType.DMA((DISPATCH_SLOTS,)),
                        pltpu.SemaphoreType.DMA(()),
                        pltpu.SemaphoreType.DMA(())],
    )
    return pl.pallas_call(
        functools.partial(_dispatch_kernel, blk_rows),
        out_shape=jax.ShapeDtypeStruct((n_buf + SPARE_ROWS, D_MODEL), ROW_DTYPE),
        grid_spec=grid_spec,
        compiler_params=pltpu.CompilerParams(
            dimension_semantics=("arbitrary",),
            vmem_limit_bytes=VMEM_LIMIT_BYTES,
            has_side_effects=True),
        name="dispatch",
    )(tables["base_out"], tables["lstart"], tables["cnt"], tables["zflag"],
      h2, meta, tables["lstart_cols"])


def _experts_kernel(be_ref, nu_ref, next_ref, x_ref, wgu_hbm, bgu_ref, wd_hbm, bd_ref,
                    y_ref, wgu_stage, wd_stage, wgu_bf, wd_bf, sems):
    i = pl.program_id(0)

    def weight_copies(e):
        return (pltpu.make_async_copy(wgu_hbm.at[e], wgu_stage, sems.at[0]),
                pltpu.make_async_copy(wd_hbm.at[e], wd_stage, sems.at[1]))

    @pl.when(i < nu_ref[0])
    def _():
        e = be_ref[i]
        prev = be_ref[jnp.maximum(i - 1, 0)]

        @pl.when(i == 0)
        def _():
            for cp in weight_copies(e):
                cp.start()

        @pl.when((i == 0) | (e != prev))
        def _():
            for cp in weight_copies(e):
                cp.wait()
            wgu_bf[...] = wgu_stage[...].astype(BF16)
            wd_bf[...] = wd_stage[...].astype(BF16)
            nxt = next_ref[e]

            @pl.when(nxt != e)
            def _():
                for cp in weight_copies(nxt):
                    cp.start()

        gu = jnp.dot(x_ref[...].astype(BF16), wgu_bf[...],
                     preferred_element_type=F32) + bgu_ref[...]
        glu = jnp.minimum(gu[:, :D_FF], SWIGLU_LIMIT)
        lin = jnp.clip(gu[:, D_FF:], -SWIGLU_LIMIT, SWIGLU_LIMIT)
        act = glu * jax.nn.sigmoid(SWIGLU_ALPHA * glu) * (lin + 1.0)
        y_ref[...] = (jnp.dot(act.astype(BF16), wd_bf[...],
                              preferred_element_type=F32) + bd_ref[...]).astype(ROW_DTYPE)

    @pl.when(i >= nu_ref[0])
    def _():
        y_ref[...] = jnp.zeros_like(y_ref)


def _experts(xs, n_buf, block_e, n_used, next_expert, w_gate_up, b_gate_up, w_down, b_down,
             blk_rows):
    used = lambda i, be, nu, nx: jnp.minimum(i, nu[0] - 1)
    grid_spec = pltpu.PrefetchScalarGridSpec(
        num_scalar_prefetch=3,
        grid=(n_buf // blk_rows,),
        in_specs=[
            pl.BlockSpec((blk_rows, D_MODEL), lambda i, be, nu, nx: (used(i, be, nu, nx), 0)),
            pl.BlockSpec(memory_space=pl.ANY),
            pl.BlockSpec((None, 1, 2 * D_FF), lambda i, be, nu, nx: (be[i], 0, 0)),
            pl.BlockSpec(memory_space=pl.ANY),
            pl.BlockSpec((None, 1, D_MODEL), lambda i, be, nu, nx: (be[i], 0, 0)),
        ],
        out_specs=pl.BlockSpec((blk_rows, D_MODEL), lambda i, be, nu, nx: (i, 0)),
        scratch_shapes=[pltpu.VMEM((D_MODEL, 2 * D_FF), F32),
                        pltpu.VMEM((D_FF, D_MODEL), F32),
                        pltpu.VMEM((D_MODEL, 2 * D_FF), BF16),
                        pltpu.VMEM((D_FF, D_MODEL), BF16),
                        pltpu.SemaphoreType.DMA((2,))],
    )
    return pl.pallas_call(
        _experts_kernel,
        out_shape=jax.ShapeDtypeStruct((n_buf, D_MODEL), ROW_DTYPE),
        grid_spec=grid_spec,
        compiler_params=pltpu.CompilerParams(
            dimension_semantics=("arbitrary",),
            vmem_limit_bytes=VMEM_LIMIT_BYTES),
        name="experts",
    )(block_e, n_used, next_expert, xs, w_gate_up, b_gate_up[:, None, :], w_down,
      b_down[:, None, :])


def _unsort_weights(meta_t, lstart_row, nrows):
    tm = meta_t.shape[1]
    meta = jnp.concatenate(
        [meta_t, jnp.zeros((LANES - META_ROWS, tm), F32)], axis=0).T
    pos = _local_positions(meta, lstart_row)
    jl = lax.broadcasted_iota(jnp.int16, (tm, nrows), 1)
    pos = [p.astype(jnp.int16) for p in pos]
    gate = [meta[:, META_GATE + r:META_GATE + r + 1].astype(BF16) for r in range(TOP_K)]
    zero = jnp.zeros((), BF16)
    return jnp.where(jl == pos[0], gate[0], jnp.where(jl == pos[1], gate[1],
           jnp.where(jl == pos[2], gate[2], jnp.where(jl == pos[3], gate[3], zero))))


def _combine_kernel(base_ref, lstart_ref, cnt_ref, meta_ref, lstart_row_ref,
                    meta_next_ref, lstart_row_next_ref, x1_ref,
                    p_ref, ys_ref, gple_ref, wpg_ref, wple_ref, gfin_ref, out_ref,
                    rows_ref, unsort_ref, sems):
    i = pl.program_id(0)
    n = pl.num_programs(0)
    tm = x1_ref.shape[0]
    nrows = _local_rows(tm)
    slot = lax.rem(i, 2)

    def fetch_tile(tile, s):
        def per_segment(e, c):
            k = tile * N_SEG + e
            src0 = base_ref[k]
            dst0 = lstart_ref[k]

            def start_copy(off, rows):
                src = pl.multiple_of(src0 + off, STRIP_ALIGN)
                dst = pl.multiple_of(dst0 + off, STRIP_ALIGN)
                pltpu.make_async_copy(ys_ref.at[pl.ds(src, rows), :],
                                      rows_ref.at[s, pl.ds(dst, rows), :],
                                      sems.at[s]).start()

            _for_each_strip(cnt_ref[k], tm, start_copy)
            return c

        lax.fori_loop(0, N_SEG, per_segment, 0)

    @pl.when(i == 0)
    def _():
        fetch_tile(0, 0)
        unsort_ref[0] = _unsort_weights(meta_ref[...], lstart_row_ref[...], nrows)

    @pl.when(i + 1 < n)
    def _():
        fetch_tile(i + 1, 1 - slot)

    pltpu.make_async_copy(ys_ref.at[pl.ds(0, nrows), :], rows_ref.at[slot],
                          sems.at[slot]).wait()
    moe = jnp.dot(unsort_ref[slot], rows_ref[slot].astype(BF16),
                  preferred_element_type=F32)
    x2 = x1_ref[...] + moe
    u = _rms(x2, gple_ref[...])
    gate_ple = jax.nn.sigmoid(jnp.dot(u.astype(BF16), wpg_ref[...],
                                      preferred_element_type=F32))
    emb = jnp.dot(p_ref[...].astype(BF16), wple_ref[...], preferred_element_type=F32)
    x3 = x2 + gate_ple * emb
    out_ref[...] = _rms(x3, gfin_ref[...])

    unsort_ref[1 - slot] = _unsort_weights(meta_next_ref[...], lstart_row_next_ref[...], nrows)


def _combine(meta, tables, x1, p2, ys, g_ple, w_ple_gate_bf, w_ple_bf, g_final, tm):
    t = x1.shape[0]
    full = lambda *shape: pl.BlockSpec(shape, lambda i, *_: (0,) * len(shape))
    last = t // tm - 1
    nxt = lambda i: jnp.minimum(i + 1, last)
    grid_spec = pltpu.PrefetchScalarGridSpec(
        num_scalar_prefetch=3,
        grid=(t // tm,),
        in_specs=[
            pl.BlockSpec((None, META_ROWS, tm), lambda i, *_: (i, 0, 0)),
            pl.BlockSpec((None, 1, N_EXPERTS), lambda i, *_: (i, 0, 0)),
            pl.BlockSpec((None, META_ROWS, tm), lambda i, *_: (nxt(i), 0, 0)),
            pl.BlockSpec((None, 1, N_EXPERTS), lambda i, *_: (nxt(i), 0, 0)),
            pl.BlockSpec((tm, D_MODEL), lambda i, *_: (i, 0)),
            pl.BlockSpec((tm, PLE_DIM), lambda i, *_: (i, 0)),
            pl.BlockSpec(memory_space=pl.ANY),
            full(1, D_MODEL),
            full(D_MODEL, D_MODEL),
            full(PLE_DIM, D_MODEL),
            full(1, D_MODEL),
        ],
        out_specs=pl.BlockSpec((tm, D_MODEL), lambda i, *_: (i, 0)),
        scratch_shapes=[pltpu.VMEM((2, _local_rows(tm), D_MODEL), ROW_DTYPE),
                        pltpu.VMEM((2, tm, _local_rows(tm)), BF16),
                        pltpu.SemaphoreType.DMA((2,))],
    )
    return pl.pallas_call(
        _combine_kernel,
        out_shape=jax.ShapeDtypeStruct((t, D_MODEL), F32),
        grid_spec=grid_spec,
        compiler_params=pltpu.CompilerParams(
            dimension_semantics=("arbitrary",),
            vmem_limit_bytes=VMEM_LIMIT_BYTES),
        name="combine",
    )(tables["base_in"], tables["lstart"], tables["cnt"], meta, tables["lstart_rows"],
      meta, tables["lstart_rows"], x1, p2, ys, g_ple, w_ple_gate_bf, w_ple_bf, g_final)


def _tiles(seq_len):
    return dict(
        proj_rows=min(512, seq_len),
        ret_block=min(256, seq_len),
        route_rows=min(512, seq_len),
        expert_rows=512,
    )


def _slot_tables(tile_counts, blk, n_buf, nloc):
    cnt = tile_counts[:, :, 0].astype(jnp.int32)
    cnt = ((cnt + STRIP_ALIGN - 1) // STRIP_ALIGN) * STRIP_ALIGN
    seg_rows = jnp.sum(cnt, axis=1)
    counts = jnp.sum(cnt, axis=0)
    padded = ((counts + blk - 1) // blk) * blk
    ends_pad = jnp.cumsum(padded)
    starts_pad = ends_pad - padded
    tile_before = jnp.cumsum(cnt, axis=0) - cnt
    lstart = jnp.cumsum(cnt, axis=1) - cnt
    base = starts_pad[None, :] + tile_before
    with_filler = lambda a, col: jnp.concatenate([a, col[:, None]], axis=1).reshape(-1)
    blk_start = jnp.arange((n_buf + SPARE_ROWS) // blk, dtype=jnp.int32) * blk
    last_of_expert = jnp.any((padded > 0)[None, :]
                             & (blk_start[:, None] == (ends_pad - blk)[None, :]), axis=1)
    zflag = jnp.where(last_of_expert | (blk_start >= n_buf), ZERO_FIRST,
                      jnp.where(blk_start >= ends_pad[-1], ZERO_ANYTIME, 0)).astype(jnp.int32)
    filler_dst = n_buf + (jnp.arange(cnt.shape[0], dtype=jnp.int32) % DISPATCH_SLOTS) * FILLER_ROWS
    ids = jnp.arange(N_EXPERTS, dtype=jnp.int32)
    later = jnp.where((padded > 0)[None, :] & (ids[None, :] > ids[:, None]), ids[None, :], N_EXPERTS)
    next_expert = jnp.min(later, axis=1)
    next_expert = jnp.where(next_expert == N_EXPERTS, ids, next_expert).astype(jnp.int32)
    return dict(
        cnt=with_filler(cnt, nloc - seg_rows),
        lstart=with_filler(lstart, seg_rows),
        base_out=with_filler(base, filler_dst),
        base_in=with_filler(base, jnp.zeros_like(seg_rows)),
        lstart_rows=lstart.astype(F32)[:, None, :],
        lstart_cols=lstart.astype(F32)[:, :, None],
        zflag=zflag, ends_pad=ends_pad, next_expert=next_expert)


def _layer(x2, p2, seq_len, g_mix, w_in, w_pool, pool_scale, w_ret_o, w_out, g_ffn,
           w_router, b_router, w_gate_up, b_gate_up, w_down, b_down, g_ple,
           w_ple_gate, w_ple, g_out):
    t = x2.shape[0]
    cfg = _tiles(seq_len)
    row = lambda a: a.reshape(1, -1)

    lane_pad = ((0, 0), (0, LANES - N_EXPERTS))
    wr_hi = w_router.astype(BF16)
    wr_lo = jnp.pad((w_router - wr_hi.astype(F32)).astype(BF16), lane_pad)
    wr_hi = jnp.pad(wr_hi, lane_pad)
    tm = cfg["route_rows"]
    x1, h2, meta, tile_counts = _token_mixer(
        x2.reshape(t // seq_len, seq_len, D_MODEL), row(g_mix), w_in.astype(BF16),
        w_pool.astype(BF16), row(pool_scale), w_ret_o.astype(BF16), w_out.astype(BF16),
        row(g_ffn), wr_hi, wr_lo, jnp.pad(row(b_router), lane_pad), tm, cfg["ret_block"])
    x1 = x1.reshape(t, D_MODEL)
    h2 = h2.reshape(t, D_MODEL)

    blk = cfg["expert_rows"]
    n_buf = t * TOP_K + (t // tm) * FILLER_ROWS + N_EXPERTS * blk
    n_blocks = n_buf // blk
    tables = _slot_tables(tile_counts, blk, n_buf, _local_rows(tm))
    blk_start = jnp.arange(n_blocks, dtype=jnp.int32) * blk
    block_e = jnp.minimum(
        jnp.sum((tables["ends_pad"][None, :] <= blk_start[:, None]).astype(jnp.int32), axis=1),
        N_EXPERTS - 1)
    n_used = tables["ends_pad"][-1:] // blk

    xs = _dispatch(h2, meta, tables, n_buf, blk, tm)
    ys = _experts(xs, n_buf, block_e, n_used, tables["next_expert"], w_gate_up, b_gate_up,
                  w_down, b_down, blk)
    return _combine(meta, tables, x1, p2, ys, row(g_ple), w_ple_gate.astype(BF16),
                    w_ple.astype(BF16), row(g_out), tm)


def kernel(x, p, g_mix, w_in, w_pool, pool_scale, w_ret_o, w_out, g_ffn, w_router,
           b_router, w_gate_up, b_gate_up, w_down, b_down, g_ple, w_ple_gate, w_ple,
           g_final):
    b, s, d = x.shape
    depth = p.shape[0]
    assert depth == 1 and d == D_MODEL
    x2 = x.reshape(b * s, d)
    out = _layer(x2, p[0].reshape(b * s, PLE_DIM), s, g_mix[0], w_in[0], w_pool[0],
                 pool_scale[0], w_ret_o[0], w_out[0], g_ffn[0], w_router[0],
                 b_router[0], w_gate_up[0], b_gate_up[0], w_down[0], b_down[0],
                 g_ple[0], w_ple_gate[0], w_ple[0], g_final)
    return out.reshape(b, s, d)
```

```python
import functools

import numpy as np
import jax
import jax.numpy as jnp
from jax import lax
from jax.experimental import pallas as pl
from jax.experimental.pallas import tpu as pltpu

F32 = jnp.float32
BF16 = jnp.bfloat16

D_MODEL = 1024
EPS = 1e-6
CHUNK = 64
PLE_DIM = 256
POOL_WINDOWS = (2, 4, 8, 16)
POOL_GROUP_W = D_MODEL // len(POOL_WINDOWS)
POOL_HALO = 16
RET_HEADS = 4
RET_QK_HEAD = 256
RET_V_HEAD = 512
RET_V_W = RET_HEADS * RET_V_HEAD
ROPE_BASE = 10000.0
IN_W = 9 * D_MODEL
N_EXPERTS = 32
TOP_K = 4
D_FF = D_MODEL
SWIGLU_ALPHA = 1.702
SWIGLU_LIMIT = 7.0

VMEM_LIMIT_BYTES = 56 * 1024 * 1024
LANES = 128
META_ROWS = 16
META_EXPERT, META_RANK, META_GATE = 0, TOP_K, 2 * TOP_K
ROW_DTYPE = F32
SUBLANES = 8
STRIP_ALIGN = SUBLANES * 4 // jnp.dtype(ROW_DTYPE).itemsize
FILLER_ROWS = N_EXPERTS * STRIP_ALIGN
N_SEG = N_EXPERTS + 1
ZERO_FIRST, ZERO_ANYTIME = 1, 2
DISPATCH_SLOTS = 3
SPARE_ROWS = 4 * FILLER_ROWS


def _rms(x, g):
    return x * lax.rsqrt(jnp.mean(x * x, axis=-1, keepdims=True) + EPS) * g


COL_U, COL_Q, COL_K = 0, 1, 2
COL_V, COL_G = (3, 4), (5, 6)
COL_GATE_A, COL_GATE_B = 7, 8
SIDE_U, SIDE_GATE_A, SIDE_GATE_B = 0, 1, 2


def _rotary_heads(y, cos, sin):
    half = RET_QK_HEAD // 2
    parts = []
    for h in range(RET_HEADS):
        x1 = y[:, h * RET_QK_HEAD:h * RET_QK_HEAD + half]
        x2 = y[:, h * RET_QK_HEAD + half:(h + 1) * RET_QK_HEAD]
        parts += [x1 * cos - x2 * sin, x2 * cos + x1 * sin]
    return jnp.concatenate(parts, axis=-1)


def _head_decay_logs():
    return [float(np.log(1.0 - 2.0 ** (-5.0 - h))) for h in range(RET_HEADS)]


def _token_mixer_kernel(blk_decay, ret_blk, x_ref, g_ref, w_ref, cos_ref, sin_ref,
                        d_ref, qd_ref, kd_ref,
                        wpool_ref, pscale_ref, wreto_ref, wout_ref, gffn_ref,
                        wr_hi_ref, wr_lo_ref, br_ref,
                        x1_ref, h2_ref, meta_ref, counts_ref,
                        side_ref, o_ref, q_s, k_s, v_s, g_s, state_ref, halo_ref):
    @pl.when(pl.program_id(1) == 0)
    def _():
        state_ref[...] = jnp.zeros_like(state_ref)
        halo_ref[...] = jnp.zeros_like(halo_ref)

    tm = x_ref.shape[0]
    h_in = _rms(x_ref[...], g_ref[...]).astype(BF16)

    def proj(c):
        return jnp.dot(h_in, w_ref[:, c * D_MODEL:(c + 1) * D_MODEL],
                       preferred_element_type=F32)

    def chunk(n):
        return slice(n * D_MODEL, (n + 1) * D_MODEL)

    side_ref[:, chunk(SIDE_U)] = proj(COL_U).astype(BF16)
    q_s[...] = _rotary_heads(proj(COL_Q), cos_ref[...], sin_ref[...]).astype(BF16)
    k_s[...] = (_rotary_heads(proj(COL_K), cos_ref[...], sin_ref[...])
                * (RET_QK_HEAD ** -0.5)).astype(BF16)
    for n, c in enumerate(COL_V):
        v_s[:, chunk(n)] = proj(c).astype(BF16)
    for n, c in enumerate(COL_G):
        y = proj(c)
        g_s[:, chunk(n)] = (y * jax.nn.sigmoid(y)).astype(BF16)
    side_ref[:, chunk(SIDE_GATE_A)] = proj(COL_GATE_A).astype(BF16)
    side_ref[:, chunk(SIDE_GATE_B)] = proj(COL_GATE_B).astype(BF16)

    for b in range(tm // ret_blk):
        rows = slice(b * ret_blk, (b + 1) * ret_blk)
        for h in range(RET_HEADS):
            qk_cols = slice(h * RET_QK_HEAD, (h + 1) * RET_QK_HEAD)
            v_cols = slice(h * RET_V_HEAD, (h + 1) * RET_V_HEAD)
            q = q_s[rows, qk_cols]
            k = k_s[rows, qk_cols]
            v = v_s[rows, v_cols]
            scores = lax.dot_general(q, k, (((1,), (1,)), ((), ())),
                                     preferred_element_type=F32) * d_ref[h]
            state = state_ref[h]
            o = jnp.dot(scores.astype(BF16), v, preferred_element_type=F32)
            o = o + jnp.dot((q.astype(F32) * qd_ref[h]).astype(BF16), state.astype(BF16),
                            preferred_element_type=F32)
            k_dec = (k.astype(F32) * kd_ref[h]).astype(BF16)
            state_ref[h] = state * blk_decay[h] + lax.dot_general(
                k_dec, v, (((0,), (0,)), ((), ())), preferred_element_type=F32)
            o = o * lax.rsqrt(jnp.mean(o * o, axis=-1, keepdims=True) + EPS)
            o_ref[rows, v_cols] = (o * g_s[rows, v_cols].astype(F32)).astype(BF16)

    u_bf = side_ref[:, chunk(SIDE_U)]
    _mix_route(pl.program_id(1) * tm, u_bf, halo_ref[...], side_ref[:, chunk(SIDE_GATE_A)],
               side_ref[:, chunk(SIDE_GATE_B)], o_ref[...], x_ref[...],
               wpool_ref, pscale_ref, wreto_ref, wout_ref, gffn_ref,
               wr_hi_ref, wr_lo_ref, br_ref, x1_ref, h2_ref, meta_ref, counts_ref)
    halo_ref[...] = u_bf[tm - POOL_HALO:, :]


def _rotary_tables(s):
    half = RET_QK_HEAD // 2
    pos = jnp.arange(s, dtype=F32)
    inv = ROPE_BASE ** (-jnp.linspace(0.0, 1.0, half, dtype=F32))
    ang = pos[:, None] * inv[None, :]
    return jnp.cos(ang), jnp.sin(ang)


def _retention_tables(blk):
    log_g = jnp.asarray(_head_decay_logs(), F32)
    idx = jnp.arange(blk, dtype=F32)
    diff = idx[:, None] - idx[None, :]
    chunk = jnp.arange(blk, dtype=jnp.int32) // CHUNK
    visible = chunk[None, :] <= chunk[:, None]
    dmask = jnp.where(visible[None], jnp.exp(log_g[:, None, None] * jnp.abs(diff)[None]), 0.0)
    q_dec = jnp.exp(log_g[:, None] * (idx + 1.0))[:, :, None]
    k_dec = jnp.exp(log_g[:, None] * (blk - 1.0 - idx))[:, :, None]
    q_dec = jnp.broadcast_to(q_dec, (RET_HEADS, blk, RET_QK_HEAD))
    k_dec = jnp.broadcast_to(k_dec, (RET_HEADS, blk, RET_QK_HEAD))
    return dmask.astype(F32), q_dec, k_dec


def _token_mixer(x3, g_mix, w_in_bf, w_pool_bf, pool_scale, w_ret_o_bf, w_out_bf,
                 g_ffn, wr_hi, wr_lo, b_router, tm, ret_blk):
    b, s, _ = x3.shape
    half = RET_QK_HEAD // 2
    tiles_per_seq = s // tm
    n_tiles = b * tiles_per_seq
    cos, sin = _rotary_tables(s)
    dmask, q_dec, k_dec = _retention_tables(ret_blk)
    blk_decay = [float(np.exp(lg * ret_blk)) for lg in _head_decay_logs()]
    once = lambda *shape: pl.BlockSpec(shape, lambda bi, j: (0,) * len(shape),
                                       pipeline_mode=pl.Buffered(1))
    tile = lambda bi, j: bi * tiles_per_seq + j
    return pl.pallas_call(
        functools.partial(_token_mixer_kernel, blk_decay, ret_blk),
        out_shape=(
            jax.ShapeDtypeStruct((b, s, D_MODEL), F32),
            jax.ShapeDtypeStruct((b, s, D_MODEL), BF16),
            jax.ShapeDtypeStruct((n_tiles, META_ROWS, tm), F32),
            jax.ShapeDtypeStruct((n_tiles, N_EXPERTS, 1), F32),
        ),
        grid=(b, tiles_per_seq),
        in_specs=[
            pl.BlockSpec((None, tm, D_MODEL), lambda bi, j: (bi, j, 0)),
            pl.BlockSpec((1, D_MODEL), lambda bi, j: (0, 0)),
            once(D_MODEL, IN_W),
            pl.BlockSpec((tm, half), lambda bi, j: (j, 0)),
            pl.BlockSpec((tm, half), lambda bi, j: (j, 0)),
            once(RET_HEADS, ret_blk, ret_blk),
            once(RET_HEADS, ret_blk, RET_QK_HEAD),
            once(RET_HEADS, ret_blk, RET_QK_HEAD),
            once(len(POOL_WINDOWS), POOL_GROUP_W, POOL_GROUP_W),
            once(1, D_MODEL),
            once(RET_V_W, D_MODEL),
            once(D_MODEL, D_MODEL),
            once(1, D_MODEL),
            once(D_MODEL, LANES),
            once(D_MODEL, LANES),
            once(1, LANES),
        ],
        out_specs=(
            pl.BlockSpec((None, tm, D_MODEL), lambda bi, j: (bi, j, 0)),
            pl.BlockSpec((None, tm, D_MODEL), lambda bi, j: (bi, j, 0)),
            pl.BlockSpec((None, META_ROWS, tm), lambda bi, j: (tile(bi, j), 0, 0)),
            pl.BlockSpec((None, N_EXPERTS, 1), lambda bi, j: (tile(bi, j), 0, 0)),
        ),
        scratch_shapes=[pltpu.VMEM((tm, 3 * D_MODEL), BF16),
                        pltpu.VMEM((tm, RET_V_W), BF16),
                        pltpu.VMEM((tm, D_MODEL), BF16),
                        pltpu.VMEM((tm, D_MODEL), BF16),
                        pltpu.VMEM((tm, RET_V_W), BF16),
                        pltpu.VMEM((tm, RET_V_W), BF16),
                        pltpu.VMEM((RET_HEADS, RET_QK_HEAD, RET_V_HEAD), F32),
                        pltpu.VMEM((POOL_HALO, D_MODEL), BF16)],
        compiler_params=pltpu.CompilerParams(
            dimension_semantics=("arbitrary", "arbitrary"),
            vmem_limit_bytes=VMEM_LIMIT_BYTES),
        name="token_mixer",
    )(x3, g_mix, w_in_bf, cos, sin, dmask, q_dec, k_dec, w_pool_bf, pool_scale,
      w_ret_o_bf, w_out_bf, g_ffn, wr_hi, wr_lo, b_router)


def _window_sum(ext, w, tm):
    cur = ext
    span = 1
    while span < w:
        cur = cur[span:, :] + cur[:-span, :]
        span *= 2
    start = POOL_HALO + 1 - w
    return cur[start:start + tm, :]


def _mix_route(pos0, u_bf, halo_bf, gate_a, gate_b, o_gated, x,
               wpool_ref, pscale_ref, wreto_ref, wout_ref, gffn_ref,
               wr_hi_ref, wr_lo_ref, br_ref,
               x1_ref, h2_ref, meta_ref, counts_ref):
    tm = x.shape[0]

    u = u_bf.astype(F32)
    ext = jnp.concatenate([halo_bf.astype(F32), u], axis=0)
    pos = (pos0 + lax.broadcasted_iota(jnp.int32, (tm, 1), 0)).astype(F32)
    pooled_out = []
    for g, w in enumerate(POOL_WINDOWS):
        cols = slice(g * POOL_GROUP_W, (g + 1) * POOL_GROUP_W)
        ws = _window_sum(ext[:, cols], w, tm)
        count = jnp.minimum(pos + 1.0, float(w))
        pooled = ws / count - u[:, cols]
        pooled_out.append(jnp.dot(pooled.astype(BF16), wpool_ref[g],
                                  preferred_element_type=F32))
    y_pool = jnp.concatenate(pooled_out, axis=-1) * pscale_ref[...]

    y_ret = jnp.dot(o_gated, wreto_ref[...], preferred_element_type=F32)
    merged = (jax.nn.sigmoid(gate_a.astype(F32)) * y_pool
              + jax.nn.sigmoid(gate_b.astype(F32)) * y_ret)
    x1 = x + jnp.dot(merged.astype(BF16), wout_ref[...], preferred_element_type=F32)
    x1_ref[...] = x1
    h2 = _rms(x1, gffn_ref[...])
    h2_ref[...] = h2.astype(BF16)

    h_hi = h2.astype(BF16)
    h_lo = (h2 - h_hi.astype(F32)).astype(BF16)
    logits = (jnp.dot(h_hi, wr_hi_ref[...], preferred_element_type=F32)
              + jnp.dot(h_lo, wr_hi_ref[...], preferred_element_type=F32)
              + jnp.dot(h_hi, wr_lo_ref[...], preferred_element_type=F32)
              + br_ref[...])
    logits_t = logits.T[:N_EXPERTS, :]

    sub = lax.broadcasted_iota(jnp.int32, (N_EXPERTS, tm), 0)
    work = logits_t
    vals, idxs, hots = [], [], []
    for _ in range(TOP_K):
        m = jnp.max(work, axis=0, keepdims=True)
        idx = jnp.min(jnp.where(work == m, sub, N_EXPERTS), axis=0, keepdims=True)
        hot = sub == idx
        vals.append(m)
        idxs.append(idx)
        hots.append(hot)
        work = jnp.where(hot, -jnp.inf, work)
    exps = [jnp.exp(v - vals[0]) for v in vals]
    denom = exps[0] + exps[1] + exps[2] + exps[3]
    gates = [e / denom for e in exps]

    sel = (jnp.where(hots[0], 1.0, 0.0) + jnp.where(hots[1], 1.0, 0.0)
           + jnp.where(hots[2], 1.0, 0.0) + jnp.where(hots[3], 1.0, 0.0))
    row = lax.broadcasted_iota(jnp.int32, (tm, tm), 0)
    col = lax.broadcasted_iota(jnp.int32, (tm, tm), 1)
    earlier = jnp.where(row < col, 1.0, 0.0).astype(BF16)
    before = jnp.dot(sel.astype(BF16), earlier, preferred_element_type=F32)
    counts_ref[...] = jnp.sum(sel, axis=1, keepdims=True)

    msub = lax.broadcasted_iota(jnp.int32, (META_ROWS, tm), 0)
    meta = jnp.zeros((META_ROWS, tm), F32)
    for r in range(TOP_K):
        rank_r = jnp.sum(jnp.where(hots[r], before, 0.0), axis=0, keepdims=True)
        meta = jnp.where(msub == META_EXPERT + r, idxs[r].astype(F32), meta)
        meta = jnp.where(msub == META_RANK + r, rank_r, meta)
        meta = jnp.where(msub == META_GATE + r, gates[r], meta)
    meta_ref[...] = meta


def _local_positions(meta, lstart):
    tm = meta.shape[0]
    lane_e = lax.broadcasted_iota(jnp.int32, (tm, N_EXPERTS), 1)
    out = []
    for r in range(TOP_K):
        e_r = meta[:, META_EXPERT + r:META_EXPERT + r + 1].astype(jnp.int32)
        first = jnp.sum(jnp.where(lane_e == e_r, lstart, 0.0), axis=-1, keepdims=True)
        out.append((meta[:, META_RANK + r:META_RANK + r + 1] + first).astype(jnp.int32))
    return out


def _local_rows(tm):
    return TOP_K * tm + FILLER_ROWS


def _for_each_strip(count, max_rows, start_copy):
    top = 1 << (max(max_rows, FILLER_ROWS).bit_length() - 1)
    pieces = [top >> k for k in range(top.bit_length()) if (top >> k) >= STRIP_ALIGN]
    rare_from = 2 * max_rows * TOP_K // N_EXPERTS
    rare = [p for p in pieces if p >= rare_from]
    rare_bits = sum(rare)

    def cover(sizes, off):
        for piece in sizes:
            take = count & piece

            @pl.when(take != 0)
            def _(off=off, piece=piece):
                start_copy(off, piece)

            off = off + take

    @pl.when((count & rare_bits) != 0)
    def _():
        cover(rare, jnp.int32(0))

    cover([p for p in pieces if p < rare_from], count & rare_bits)


def _dispatch_kernel(blk_rows, base_ref, lstart_ref, cnt_ref, zflag_ref, pad_start_ref,
                     pad_rows_ref, h_ref, meta_ref, lstart_col_ref, xs_ref,
                     sorted_ref, zero_ref, sems, zsem, tail_sem):
    i = pl.program_id(0)
    tm = h_ref.shape[0]
    nrows = _local_rows(tm)
    slot = lax.rem(i, DISPATCH_SLOTS)

    n_zero_blocks = xs_ref.shape[0] // blk_rows

    def for_padding(act):
        def body(e, c):
            def piece(off, rows):
                start = pl.multiple_of(pad_start_ref[e] + off, STRIP_ALIGN)
                act(pltpu.make_async_copy(zero_ref.at[pl.ds(0, rows), :],
                                          xs_ref.at[pl.ds(start, rows), :], tail_sem))
            _for_each_strip(pad_rows_ref[e], blk_rows, piece)
            return c
        lax.fori_loop(0, N_EXPERTS, body, 0)

    def zcopy(b, sem):
        start = pl.multiple_of(b * blk_rows, blk_rows)
        return pltpu.make_async_copy(zero_ref, xs_ref.at[pl.ds(start, blk_rows), :], sem)

    def for_flagged(flag, sem, act):
        def body(b, c):
            @pl.when(zflag_ref[b] == flag)
            def _():
                act(zcopy(b, sem))
            return c
        lax.fori_loop(0, n_zero_blocks, body, 0)

    @pl.when(i == 0)
    def _():
        zero_ref[...] = jnp.zeros_like(zero_ref)
        for_flagged(ZERO_FIRST, zsem, lambda cp: cp.start())
        for_flagged(ZERO_ANYTIME, tail_sem, lambda cp: cp.start())
        for_padding(lambda cp: cp.start())
        for_flagged(ZERO_FIRST, zsem, lambda cp: cp.wait())

    meta_t = meta_ref[...]
    sub_e = lax.broadcasted_iota(jnp.int32, (N_EXPERTS, tm), 0)
    pos = []
    for r in range(TOP_K):
        e_r = meta_t[META_EXPERT + r:META_EXPERT + r + 1, :].astype(jnp.int32)
        first = jnp.sum(jnp.where(sub_e == e_r, lstart_col_ref[...], 0.0), axis=0, keepdims=True)
        pos.append((meta_t[META_RANK + r:META_RANK + r + 1, :] + first).astype(jnp.int16))
    jr = lax.broadcasted_iota(jnp.int16, (nrows, tm), 0)
    one, zero = jnp.ones((), BF16), jnp.zeros((), BF16)
    onehot = jnp.where(jr == pos[0], one, jnp.where(jr == pos[1], one,
             jnp.where(jr == pos[2], one, jnp.where(jr == pos[3], one, zero))))
    srt = jnp.dot(onehot, h_ref[...], preferred_element_type=F32)
    sorted_ref[slot] = srt.astype(ROW_DTYPE)

    def per_segment(e, c):
        k = i * N_SEG + e
        src0 = lstart_ref[k]
        dst0 = base_ref[k]

        def start_copy(off, rows):
            src = pl.multiple_of(src0 + off, STRIP_ALIGN)
            dst = pl.multiple_of(dst0 + off, STRIP_ALIGN)
            pltpu.make_async_copy(sorted_ref.at[slot, pl.ds(src, rows), :],
                                  xs_ref.at[pl.ds(dst, rows), :],
                                  sems.at[slot]).start()

        _for_each_strip(cnt_ref[k], tm, start_copy)
        return c

    lax.fori_loop(0, N_SEG, per_segment, 0)

    def wait_tile(s):
        pltpu.make_async_copy(sorted_ref.at[s], xs_ref.at[pl.ds(0, nrows), :],
                              sems.at[s]).wait()

    @pl.when(i >= DISPATCH_SLOTS - 1)
    def _():
        wait_tile(lax.rem(i + 1, DISPATCH_SLOTS))

    @pl.when(i == pl.num_programs(0) - 1)
    def _():
        for age in range(DISPATCH_SLOTS - 2, -1, -1):
            @pl.when(i >= age)
            def _(age=age):
                wait_tile(lax.rem(i - age, DISPATCH_SLOTS))
        for_flagged(ZERO_ANYTIME, tail_sem, lambda cp: cp.wait())
        for_padding(lambda cp: cp.wait())


def _dispatch(h2, meta, tables, n_buf, blk_rows, tm):
    t = h2.shape[0]
    grid_spec = pltpu.PrefetchScalarGridSpec(
        num_scalar_prefetch=6,
        grid=(t // tm,),
        in_specs=[
            pl.BlockSpec((tm, D_MODEL), lambda i, *_: (i, 0)),
            pl.BlockSpec((None, META_ROWS, tm), lambda i, *_: (i, 0, 0)),
            pl.BlockSpec((None, N_EXPERTS, 1), lambda i, *_: (i, 0, 0)),
        ],
        out_specs=pl.BlockSpec(memory_space=pl.ANY),
        scratch_shapes=[pltpu.VMEM((DISPATCH_SLOTS, _local_rows(tm), D_MODEL), ROW_DTYPE),
                        pltpu.VMEM((blk_rows, D_MODEL), ROW_DTYPE),
                        pltpu.SemaphoreType.DMA((DISPATCH_SLOTS,)),
                        pltpu.SemaphoreType.DMA(()),
                        pltpu.SemaphoreType.DMA(())],
    )
    return pl.pallas_call(
        functools.partial(_dispatch_kernel, blk_rows),
        out_shape=jax.ShapeDtypeStruct((n_buf + SPARE_ROWS, D_MODEL), ROW_DTYPE),
        grid_spec=grid_spec,
        compiler_params=pltpu.CompilerParams(
            dimension_semantics=("arbitrary",),
            vmem_limit_bytes=VMEM_LIMIT_BYTES,
            has_side_effects=True),
        name="dispatch",
    )(tables["base_out"], tables["lstart"], tables["cnt"], tables["zflag"],
      tables["pad_start"], tables["pad_rows"], h2, meta, tables["lstart_cols"])


def _experts_kernel(be_ref, nu_ref, next_ref, x_ref, wgu_hbm, bgu_ref, wd_hbm, bd_ref,
                    y_ref, wgu_stage, wd_stage, wgu_bf, wd_bf, sems):
    i = pl.program_id(0)

    def weight_copies(e):
        return (pltpu.make_async_copy(wgu_hbm.at[e], wgu_stage, sems.at[0]),
                pltpu.make_async_copy(wd_hbm.at[e], wd_stage, sems.at[1]))

    @pl.when(i < nu_ref[0])
    def _():
        e = be_ref[i]
        prev = be_ref[jnp.maximum(i - 1, 0)]

        @pl.when(i == 0)
        def _():
            for cp in weight_copies(e):
                cp.start()

        @pl.when((i == 0) | (e != prev))
        def _():
            for cp in weight_copies(e):
                cp.wait()
            wgu_bf[...] = wgu_stage[...].astype(BF16)
            wd_bf[...] = wd_stage[...].astype(BF16)
            nxt = next_ref[e]

            @pl.when(nxt != e)
            def _():
                for cp in weight_copies(nxt):
                    cp.start()

        gu = jnp.dot(x_ref[...].astype(BF16), wgu_bf[...],
                     preferred_element_type=F32) + bgu_ref[...]
        glu = jnp.minimum(gu[:, :D_FF], SWIGLU_LIMIT)
        lin = jnp.clip(gu[:, D_FF:], -SWIGLU_LIMIT, SWIGLU_LIMIT)
        act = glu * jax.nn.sigmoid(SWIGLU_ALPHA * glu) * (lin + 1.0)
        y_ref[...] = (jnp.dot(act.astype(BF16), wd_bf[...],
                              preferred_element_type=F32) + bd_ref[...]).astype(ROW_DTYPE)

    @pl.when(i >= nu_ref[0])
    def _():
        y_ref[...] = jnp.zeros_like(y_ref)


def _experts(xs, n_buf, block_e, n_used, next_expert, w_gate_up, b_gate_up, w_down, b_down,
             blk_rows):
    used = lambda i, be, nu, nx: jnp.minimum(i, nu[0] - 1)
    grid_spec = pltpu.PrefetchScalarGridSpec(
        num_scalar_prefetch=3,
        grid=(n_buf // blk_rows,),
        in_specs=[
            pl.BlockSpec((blk_rows, D_MODEL), lambda i, be, nu, nx: (used(i, be, nu, nx), 0)),
            pl.BlockSpec(memory_space=pl.ANY),
            pl.BlockSpec((None, 1, 2 * D_FF), lambda i, be, nu, nx: (be[i], 0, 0)),
            pl.BlockSpec(memory_space=pl.ANY),
            pl.BlockSpec((None, 1, D_MODEL), lambda i, be, nu, nx: (be[i], 0, 0)),
        ],
        out_specs=pl.BlockSpec((blk_rows, D_MODEL), lambda i, be, nu, nx: (i, 0)),
        scratch_shapes=[pltpu.VMEM((D_MODEL, 2 * D_FF), F32),
                        pltpu.VMEM((D_FF, D_MODEL), F32),
                        pltpu.VMEM((D_MODEL, 2 * D_FF), BF16),
                        pltpu.VMEM((D_FF, D_MODEL), BF16),
                        pltpu.SemaphoreType.DMA((2,))],
    )
    return pl.pallas_call(
        _experts_kernel,
        out_shape=jax.ShapeDtypeStruct((n_buf, D_MODEL), ROW_DTYPE),
        grid_spec=grid_spec,
        compiler_params=pltpu.CompilerParams(
            dimension_semantics=("arbitrary",),
            vmem_limit_bytes=VMEM_LIMIT_BYTES),
        name="experts",
    )(block_e, n_used, next_expert, xs, w_gate_up, b_gate_up[:, None, :], w_down,
      b_down[:, None, :])


def _unsort_weights(meta_t, lstart_row, nrows):
    tm = meta_t.shape[1]
    meta = jnp.concatenate(
        [meta_t, jnp.zeros((LANES - META_ROWS, tm), F32)], axis=0).T
    pos = _local_positions(meta, lstart_row)
    jl = lax.broadcasted_iota(jnp.int16, (tm, nrows), 1)
    pos = [p.astype(jnp.int16) for p in pos]
    gate = [meta[:, META_GATE + r:META_GATE + r + 1].astype(BF16) for r in range(TOP_K)]
    zero = jnp.zeros((), BF16)
    return jnp.where(jl == pos[0], gate[0], jnp.where(jl == pos[1], gate[1],
           jnp.where(jl == pos[2], gate[2], jnp.where(jl == pos[3], gate[3], zero))))


def _combine_kernel(base_ref, lstart_ref, cnt_ref, meta_ref, lstart_row_ref,
                    meta_next_ref, lstart_row_next_ref, x1_ref,
                    p_ref, ys_ref, gple_ref, wpg_ref, wple_ref, gfin_ref, out_ref,
                    rows_ref, unsort_ref, sems):
    i = pl.program_id(0)
    n = pl.num_programs(0)
    tm = x1_ref.shape[0]
    nrows = _local_rows(tm)
    slot = lax.rem(i, 2)

    def fetch_tile(tile, s):
        def per_segment(e, c):
            k = tile * N_SEG + e
            src0 = base_ref[k]
            dst0 = lstart_ref[k]

            def start_copy(off, rows):
                src = pl.multiple_of(src0 + off, STRIP_ALIGN)
                dst = pl.multiple_of(dst0 + off, STRIP_ALIGN)
                pltpu.make_async_copy(ys_ref.at[pl.ds(src, rows), :],
                                      rows_ref.at[s, pl.ds(dst, rows), :],
                                      sems.at[s]).start()

            _for_each_strip(cnt_ref[k], tm, start_copy)
            return c

        lax.fori_loop(0, N_SEG, per_segment, 0)

    @pl.when(i == 0)
    def _():
        fetch_tile(0, 0)
        unsort_ref[0] = _unsort_weights(meta_ref[...], lstart_row_ref[...], nrows)

    @pl.when(i + 1 < n)
    def _():
        fetch_tile(i + 1, 1 - slot)

    pltpu.make_async_copy(ys_ref.at[pl.ds(0, nrows), :], rows_ref.at[slot],
                          sems.at[slot]).wait()
    moe = jnp.dot(unsort_ref[slot], rows_ref[slot].astype(BF16),
                  preferred_element_type=F32)
    x2 = x1_ref[...] + moe
    u = _rms(x2, gple_ref[...])
    gate_ple = jax.nn.sigmoid(jnp.dot(u.astype(BF16), wpg_ref[...],
                                      preferred_element_type=F32))
    emb = jnp.dot(p_ref[...].astype(BF16), wple_ref[...], preferred_element_type=F32)
    x3 = x2 + gate_ple * emb
    out_ref[...] = _rms(x3, gfin_ref[...])

    unsort_ref[1 - slot] = _unsort_weights(meta_next_ref[...], lstart_row_next_ref[...], nrows)


def _combine(meta, tables, x1, p2, ys, g_ple, w_ple_gate_bf, w_ple_bf, g_final, tm):
    t = x1.shape[0]
    full = lambda *shape: pl.BlockSpec(shape, lambda i, *_: (0,) * len(shape))
    last = t // tm - 1
    nxt = lambda i: jnp.minimum(i + 1, last)
    grid_spec = pltpu.PrefetchScalarGridSpec(
        num_scalar_prefetch=3,
        grid=(t // tm,),
        in_specs=[
            pl.BlockSpec((None, META_ROWS, tm), lambda i, *_: (i, 0, 0)),
            pl.BlockSpec((None, 1, N_EXPERTS), lambda i, *_: (i, 0, 0)),
            pl.BlockSpec((None, META_ROWS, tm), lambda i, *_: (nxt(i), 0, 0)),
            pl.BlockSpec((None, 1, N_EXPERTS), lambda i, *_: (nxt(i), 0, 0)),
            pl.BlockSpec((tm, D_MODEL), lambda i, *_: (i, 0)),
            pl.BlockSpec((tm, PLE_DIM), lambda i, *_: (i, 0)),
            pl.BlockSpec(memory_space=pl.ANY),
            full(1, D_MODEL),
            full(D_MODEL, D_MODEL),
            full(PLE_DIM, D_MODEL),
            full(1, D_MODEL),
        ],
        out_specs=pl.BlockSpec((tm, D_MODEL), lambda i, *_: (i, 0)),
        scratch_shapes=[pltpu.VMEM((2, _local_rows(tm), D_MODEL), ROW_DTYPE),
                        pltpu.VMEM((2, tm, _local_rows(tm)), BF16),
                        pltpu.SemaphoreType.DMA((2,))],
    )
    return pl.pallas_call(
        _combine_kernel,
        out_shape=jax.ShapeDtypeStruct((t, D_MODEL), F32),
        grid_spec=grid_spec,
        compiler_params=pltpu.CompilerParams(
            dimension_semantics=("arbitrary",),
            vmem_limit_bytes=VMEM_LIMIT_BYTES),
        name="combine",
    )(tables["base_in"], tables["lstart"], tables["cnt"], meta, tables["lstart_rows"],
      meta, tables["lstart_rows"], x1, p2, ys, g_ple, w_ple_gate_bf, w_ple_bf, g_final)


def _tiles(seq_len):
    return dict(
        proj_rows=min(512, seq_len),
        ret_block=min(256, seq_len),
        route_rows=min(512, seq_len),
        expert_rows=512,
    )


def _slot_tables(tile_counts, blk, n_buf, nloc):
    cnt = tile_counts[:, :, 0].astype(jnp.int32)
    cnt = ((cnt + STRIP_ALIGN - 1) // STRIP_ALIGN) * STRIP_ALIGN
    seg_rows = jnp.sum(cnt, axis=1)
    counts = jnp.sum(cnt, axis=0)
    padded = ((counts + blk - 1) // blk) * blk
    ends_pad = jnp.cumsum(padded)
    starts_pad = ends_pad - padded
    tile_before = jnp.cumsum(cnt, axis=0) - cnt
    lstart = jnp.cumsum(cnt, axis=1) - cnt
    base = starts_pad[None, :] + tile_before
    with_filler = lambda a, col: jnp.concatenate([a, col[:, None]], axis=1).reshape(-1)
    blk_start = jnp.arange((n_buf + SPARE_ROWS) // blk, dtype=jnp.int32) * blk
    zflag = jnp.where(blk_start >= n_buf, ZERO_FIRST,
                      jnp.where(blk_start >= ends_pad[-1], ZERO_ANYTIME, 0)).astype(jnp.int32)
    filler_dst = n_buf + (jnp.arange(cnt.shape[0], dtype=jnp.int32) % DISPATCH_SLOTS) * FILLER_ROWS
    ids = jnp.arange(N_EXPERTS, dtype=jnp.int32)
    later = jnp.where((padded > 0)[None, :] & (ids[None, :] > ids[:, None]), ids[None, :], N_EXPERTS)
    next_expert = jnp.min(later, axis=1)
    next_expert = jnp.where(next_expert == N_EXPERTS, ids, next_expert).astype(jnp.int32)
    return dict(
        cnt=with_filler(cnt, nloc - seg_rows),
        lstart=with_filler(lstart, seg_rows),
        base_out=with_filler(base, filler_dst),
        base_in=with_filler(base, jnp.zeros_like(seg_rows)),
        lstart_rows=lstart.astype(F32)[:, None, :],
        lstart_cols=lstart.astype(F32)[:, :, None],
        zflag=zflag, pad_start=starts_pad + counts, pad_rows=padded - counts,
        ends_pad=ends_pad, next_expert=next_expert)


def _layer(x2, p2, seq_len, g_mix, w_in, w_pool, pool_scale, w_ret_o, w_out, g_ffn,
           w_router, b_router, w_gate_up, b_gate_up, w_down, b_down, g_ple,
           w_ple_gate, w_ple, g_out):
    t = x2.shape[0]
    cfg = _tiles(seq_len)
    row = lambda a: a.reshape(1, -1)

    lane_pad = ((0, 0), (0, LANES - N_EXPERTS))
    wr_hi = w_router.astype(BF16)
    wr_lo = jnp.pad((w_router - wr_hi.astype(F32)).astype(BF16), lane_pad)
    wr_hi = jnp.pad(wr_hi, lane_pad)
    tm = cfg["route_rows"]
    x1, h2, meta, tile_counts = _token_mixer(
        x2.reshape(t // seq_len, seq_len, D_MODEL), row(g_mix), w_in.astype(BF16),
        w_pool.astype(BF16), row(pool_scale), w_ret_o.astype(BF16), w_out.astype(BF16),
        row(g_ffn), wr_hi, wr_lo, jnp.pad(row(b_router), lane_pad), tm, cfg["ret_block"])
    x1 = x1.reshape(t, D_MODEL)
    h2 = h2.reshape(t, D_MODEL)

    blk = cfg["expert_rows"]
    n_buf = t * TOP_K + (t // tm) * FILLER_ROWS + N_EXPERTS * blk
    n_blocks = n_buf // blk
    tables = _slot_tables(tile_counts, blk, n_buf, _local_rows(tm))
    blk_start = jnp.arange(n_blocks, dtype=jnp.int32) * blk
    block_e = jnp.minimum(
        jnp.sum((tables["ends_pad"][None, :] <= blk_start[:, None]).astype(jnp.int32), axis=1),
        N_EXPERTS - 1)
    n_used = tables["ends_pad"][-1:] // blk

    xs = _dispatch(h2, meta, tables, n_buf, blk, tm)
    ys = _experts(xs, n_buf, block_e, n_used, tables["next_expert"], w_gate_up, b_gate_up,
                  w_down, b_down, blk)
    return _combine(meta, tables, x1, p2, ys, row(g_ple), w_ple_gate.astype(BF16),
                    w_ple.astype(BF16), row(g_out), tm)


def kernel(x, p, g_mix, w_in, w_pool, pool_scale, w_ret_o, w_out, g_ffn, w_router,
           b_router, w_gate_up, b_gate_up, w_down, b_down, g_ple, w_ple_gate, w_ple,
           g_final):
    b, s, d = x.shape
    depth = p.shape[0]
    assert depth == 1 and d == D_MODEL
    x2 = x.reshape(b * s, d)
    out = _layer(x2, p[0].reshape(b * s, PLE_DIM), s, g_mix[0], w_in[0], w_pool[0],
                 pool_scale[0], w_ret_o[0], w_out[0], g_ffn[0], w_router[0],
                 b_router[0], w_gate_up[0], b_gate_up[0], w_down[0], b_down[0],
                 g_ple[0], w_ple_gate[0], w_ple[0], g_final)
    return out.reshape(b, s, d)
```

```python
import functools

import numpy as np
import jax
import jax.numpy as jnp
from jax import lax
from jax.experimental import pallas as pl
from jax.experimental.pallas import tpu as pltpu

F32 = jnp.float32
BF16 = jnp.bfloat16

D_MODEL = 1024
EPS = 1e-6
CHUNK = 64
PLE_DIM = 256
POOL_WINDOWS = (2, 4, 8, 16)
POOL_GROUP_W = D_MODEL // len(POOL_WINDOWS)
POOL_HALO = 16
RET_HEADS = 4
RET_QK_HEAD = 256
RET_V_HEAD = 512
RET_V_W = RET_HEADS * RET_V_HEAD
ROPE_BASE = 10000.0
IN_W = 9 * D_MODEL
N_EXPERTS = 32
TOP_K = 4
D_FF = D_MODEL
SWIGLU_ALPHA = 1.702
SWIGLU_LIMIT = 7.0

VMEM_LIMIT_BYTES = 56 * 1024 * 1024
LANES = 128
META_ROWS = 16
META_EXPERT, META_RANK, META_GATE = 0, TOP_K, 2 * TOP_K
ROW_DTYPE = F32
SUBLANES = 8
STRIP_ALIGN = SUBLANES * 4 // jnp.dtype(ROW_DTYPE).itemsize
FILLER_ROWS = N_EXPERTS * STRIP_ALIGN
N_SEG = N_EXPERTS + 1
ZERO_FIRST, ZERO_ANYTIME = 1, 2
DISPATCH_SLOTS = 3
SPARE_ROWS = 4 * FILLER_ROWS


def _rms(x, g):
    return x * lax.rsqrt(jnp.mean(x * x, axis=-1, keepdims=True) + EPS) * g


COL_U, COL_Q, COL_K = 0, 1, 2
COL_V, COL_G = (3, 4), (5, 6)
COL_GATE_A, COL_GATE_B = 7, 8
SIDE_U, SIDE_GATE_A, SIDE_GATE_B = 0, 1, 2


def _rotary_heads(y, cos, sin):
    half = RET_QK_HEAD // 2
    parts = []
    for h in range(RET_HEADS):
        x1 = y[:, h * RET_QK_HEAD:h * RET_QK_HEAD + half]
        x2 = y[:, h * RET_QK_HEAD + half:(h + 1) * RET_QK_HEAD]
        parts += [x1 * cos - x2 * sin, x2 * cos + x1 * sin]
    return jnp.concatenate(parts, axis=-1)


def _head_decay_logs():
    return [float(np.log(1.0 - 2.0 ** (-5.0 - h))) for h in range(RET_HEADS)]


def _token_mixer_kernel(blk_decay, ret_blk, x_ref, g_ref, w_ref, cos_ref, sin_ref,
                        d_ref, qd_ref, kd_ref,
                        wpool_ref, pscale_ref, wreto_ref, wout_ref, gffn_ref,
                        wr_ref, br_ref,
                        x1_ref, h2_ref, meta_ref, counts_ref,
                        side_ref, o_ref, q_s, k_s, v_s, g_s, state_ref, halo_ref):
    @pl.when(pl.program_id(1) == 0)
    def _():
        state_ref[...] = jnp.zeros_like(state_ref)
        halo_ref[...] = jnp.zeros_like(halo_ref)

    tm = x_ref.shape[0]
    h_in = _rms(x_ref[...], g_ref[...]).astype(BF16)

    def proj(c):
        return jnp.dot(h_in, w_ref[:, c * D_MODEL:(c + 1) * D_MODEL],
                       preferred_element_type=F32)

    def chunk(n):
        return slice(n * D_MODEL, (n + 1) * D_MODEL)

    side_ref[:, chunk(SIDE_U)] = proj(COL_U).astype(BF16)
    q_s[...] = _rotary_heads(proj(COL_Q), cos_ref[...], sin_ref[...]).astype(BF16)
    k_s[...] = (_rotary_heads(proj(COL_K), cos_ref[...], sin_ref[...])
                * (RET_QK_HEAD ** -0.5)).astype(BF16)
    for n, c in enumerate(COL_V):
        v_s[:, chunk(n)] = proj(c).astype(BF16)
    for n, c in enumerate(COL_G):
        y = proj(c)
        g_s[:, chunk(n)] = (y * jax.nn.sigmoid(y)).astype(BF16)
    side_ref[:, chunk(SIDE_GATE_A)] = proj(COL_GATE_A).astype(BF16)
    side_ref[:, chunk(SIDE_GATE_B)] = proj(COL_GATE_B).astype(BF16)

    for b in range(tm // ret_blk):
        rows = slice(b * ret_blk, (b + 1) * ret_blk)
        for h in range(RET_HEADS):
            qk_cols = slice(h * RET_QK_HEAD, (h + 1) * RET_QK_HEAD)
            v_cols = slice(h * RET_V_HEAD, (h + 1) * RET_V_HEAD)
            q = q_s[rows, qk_cols]
            k = k_s[rows, qk_cols]
            v = v_s[rows, v_cols]
            scores = lax.dot_general(q, k, (((1,), (1,)), ((), ())),
                                     preferred_element_type=F32) * d_ref[h]
            state = state_ref[h]
            o = jnp.dot(scores.astype(BF16), v, preferred_element_type=F32)
            o = o + jnp.dot((q.astype(F32) * qd_ref[h]).astype(BF16), state.astype(BF16),
                            preferred_element_type=F32)
            k_dec = (k.astype(F32) * kd_ref[h]).astype(BF16)
            state_ref[h] = state * blk_decay[h] + lax.dot_general(
                k_dec, v, (((0,), (0,)), ((), ())), preferred_element_type=F32)
            o = o * lax.rsqrt(jnp.mean(o * o, axis=-1, keepdims=True) + EPS)
            o_ref[rows, v_cols] = (o * g_s[rows, v_cols].astype(F32)).astype(BF16)

    u_bf = side_ref[:, chunk(SIDE_U)]
    _mix_route(pl.program_id(1) * tm, u_bf, halo_ref[...], side_ref[:, chunk(SIDE_GATE_A)],
               side_ref[:, chunk(SIDE_GATE_B)], o_ref[...], x_ref[...],
               wpool_ref, pscale_ref, wreto_ref, wout_ref, gffn_ref,
               wr_ref, br_ref, x1_ref, h2_ref, meta_ref, counts_ref)
    halo_ref[...] = u_bf[tm - POOL_HALO:, :]


def _rotary_tables(s):
    half = RET_QK_HEAD // 2
    pos = jnp.arange(s, dtype=F32)
    inv = ROPE_BASE ** (-jnp.linspace(0.0, 1.0, half, dtype=F32))
    ang = pos[:, None] * inv[None, :]
    return jnp.cos(ang), jnp.sin(ang)


def _retention_tables(blk):
    log_g = jnp.asarray(_head_decay_logs(), F32)
    idx = jnp.arange(blk, dtype=F32)
    diff = idx[:, None] - idx[None, :]
    chunk = jnp.arange(blk, dtype=jnp.int32) // CHUNK
    visible = chunk[None, :] <= chunk[:, None]
    dmask = jnp.where(visible[None], jnp.exp(log_g[:, None, None] * jnp.abs(diff)[None]), 0.0)
    q_dec = jnp.exp(log_g[:, None] * (idx + 1.0))[:, :, None]
    k_dec = jnp.exp(log_g[:, None] * (blk - 1.0 - idx))[:, :, None]
    q_dec = jnp.broadcast_to(q_dec, (RET_HEADS, blk, RET_QK_HEAD))
    k_dec = jnp.broadcast_to(k_dec, (RET_HEADS, blk, RET_QK_HEAD))
    return dmask.astype(F32), q_dec, k_dec


def _token_mixer(x3, g_mix, w_in_bf, w_pool_bf, pool_scale, w_ret_o_bf, w_out_bf,
                 g_ffn, wr_split, b_router, tm, ret_blk):
    b, s, _ = x3.shape
    half = RET_QK_HEAD // 2
    tiles_per_seq = s // tm
    n_tiles = b * tiles_per_seq
    cos, sin = _rotary_tables(s)
    dmask, q_dec, k_dec = _retention_tables(ret_blk)
    blk_decay = [float(np.exp(lg * ret_blk)) for lg in _head_decay_logs()]
    once = lambda *shape: pl.BlockSpec(shape, lambda bi, j: (0,) * len(shape),
                                       pipeline_mode=pl.Buffered(1))
    tile = lambda bi, j: bi * tiles_per_seq + j
    return pl.pallas_call(
        functools.partial(_token_mixer_kernel, blk_decay, ret_blk),
        out_shape=(
            jax.ShapeDtypeStruct((b, s, D_MODEL), F32),
            jax.ShapeDtypeStruct((b, s, D_MODEL), BF16),
            jax.ShapeDtypeStruct((n_tiles, META_ROWS, tm), F32),
            jax.ShapeDtypeStruct((n_tiles, N_EXPERTS, 1), F32),
        ),
        grid=(b, tiles_per_seq),
        in_specs=[
            pl.BlockSpec((None, tm, D_MODEL), lambda bi, j: (bi, j, 0)),
            pl.BlockSpec((1, D_MODEL), lambda bi, j: (0, 0)),
            once(D_MODEL, IN_W),
            pl.BlockSpec((tm, half), lambda bi, j: (j, 0)),
            pl.BlockSpec((tm, half), lambda bi, j: (j, 0)),
            once(RET_HEADS, ret_blk, ret_blk),
            once(RET_HEADS, ret_blk, RET_QK_HEAD),
            once(RET_HEADS, ret_blk, RET_QK_HEAD),
            once(len(POOL_WINDOWS), POOL_GROUP_W, POOL_GROUP_W),
            once(1, D_MODEL),
            once(RET_V_W, D_MODEL),
            once(D_MODEL, D_MODEL),
            once(1, D_MODEL),
            once(D_MODEL, LANES),
            once(1, LANES),
        ],
        out_specs=(
            pl.BlockSpec((None, tm, D_MODEL), lambda bi, j: (bi, j, 0)),
            pl.BlockSpec((None, tm, D_MODEL), lambda bi, j: (bi, j, 0)),
            pl.BlockSpec((None, META_ROWS, tm), lambda bi, j: (tile(bi, j), 0, 0)),
            pl.BlockSpec((None, N_EXPERTS, 1), lambda bi, j: (tile(bi, j), 0, 0)),
        ),
        scratch_shapes=[pltpu.VMEM((tm, 3 * D_MODEL), BF16),
                        pltpu.VMEM((tm, RET_V_W), BF16),
                        pltpu.VMEM((tm, D_MODEL), BF16),
                        pltpu.VMEM((tm, D_MODEL), BF16),
                        pltpu.VMEM((tm, RET_V_W), BF16),
                        pltpu.VMEM((tm, RET_V_W), BF16),
                        pltpu.VMEM((RET_HEADS, RET_QK_HEAD, RET_V_HEAD), F32),
                        pltpu.VMEM((POOL_HALO, D_MODEL), BF16)],
        compiler_params=pltpu.CompilerParams(
            dimension_semantics=("arbitrary", "arbitrary"),
            vmem_limit_bytes=VMEM_LIMIT_BYTES),
        name="token_mixer",
    )(x3, g_mix, w_in_bf, cos, sin, dmask, q_dec, k_dec, w_pool_bf, pool_scale,
      w_ret_o_bf, w_out_bf, g_ffn, wr_split, b_router)


def _window_sum(ext, w, tm):
    cur = ext
    span = 1
    while span < w:
        cur = cur[span:, :] + cur[:-span, :]
        span *= 2
    start = POOL_HALO + 1 - w
    return cur[start:start + tm, :]


def _mix_route(pos0, u_bf, halo_bf, gate_a, gate_b, o_gated, x,
               wpool_ref, pscale_ref, wreto_ref, wout_ref, gffn_ref,
               wr_ref, br_ref,
               x1_ref, h2_ref, meta_ref, counts_ref):
    tm = x.shape[0]

    u = u_bf.astype(F32)
    ext = jnp.concatenate([halo_bf.astype(F32), u], axis=0)
    pos = (pos0 + lax.broadcasted_iota(jnp.int32, (tm, 1), 0)).astype(F32)
    pooled_out = []
    for g, w in enumerate(POOL_WINDOWS):
        cols = slice(g * POOL_GROUP_W, (g + 1) * POOL_GROUP_W)
        ws = _window_sum(ext[:, cols], w, tm)
        count = jnp.minimum(pos + 1.0, float(w))
        pooled = ws / count - u[:, cols]
        pooled_out.append(jnp.dot(pooled.astype(BF16), wpool_ref[g],
                                  preferred_element_type=F32))
    y_pool = jnp.concatenate(pooled_out, axis=-1) * pscale_ref[...]

    y_ret = jnp.dot(o_gated, wreto_ref[...], preferred_element_type=F32)
    merged = (jax.nn.sigmoid(gate_a.astype(F32)) * y_pool
              + jax.nn.sigmoid(gate_b.astype(F32)) * y_ret)
    x1 = x + jnp.dot(merged.astype(BF16), wout_ref[...], preferred_element_type=F32)
    x1_ref[...] = x1
    h2 = _rms(x1, gffn_ref[...])
    h2_ref[...] = h2.astype(BF16)

    h_hi = h2.astype(BF16)
    h_lo = (h2 - h_hi.astype(F32)).astype(BF16)
    parts = (jnp.dot(h_hi, wr_ref[...], preferred_element_type=F32)
             + jnp.dot(h_lo, wr_ref[...], preferred_element_type=F32)
             + br_ref[...])
    parts_t = parts.T
    logits_t = parts_t[:N_EXPERTS, :] + parts_t[N_EXPERTS:2 * N_EXPERTS, :]

    sub = lax.broadcasted_iota(jnp.int32, (N_EXPERTS, tm), 0)
    work = logits_t
    vals, idxs, hots = [], [], []
    for _ in range(TOP_K):
        m = jnp.max(work, axis=0, keepdims=True)
        idx = jnp.min(jnp.where(work == m, sub, N_EXPERTS), axis=0, keepdims=True)
        hot = sub == idx
        vals.append(m)
        idxs.append(idx)
        hots.append(hot)
        work = jnp.where(hot, -jnp.inf, work)
    exps = [jnp.exp(v - vals[0]) for v in vals]
    denom = exps[0] + exps[1] + exps[2] + exps[3]
    gates = [e / denom for e in exps]

    sel = (jnp.where(hots[0], 1.0, 0.0) + jnp.where(hots[1], 1.0, 0.0)
           + jnp.where(hots[2], 1.0, 0.0) + jnp.where(hots[3], 1.0, 0.0))
    row = lax.broadcasted_iota(jnp.int32, (tm, tm), 0)
    col = lax.broadcasted_iota(jnp.int32, (tm, tm), 1)
    earlier = jnp.where(row < col, 1.0, 0.0).astype(BF16)
    before = jnp.dot(sel.astype(BF16), earlier, preferred_element_type=F32)
    counts_ref[...] = jnp.sum(sel, axis=1, keepdims=True)

    msub = lax.broadcasted_iota(jnp.int32, (META_ROWS, tm), 0)
    meta = jnp.zeros((META_ROWS, tm), F32)
    for r in range(TOP_K):
        rank_r = jnp.sum(jnp.where(hots[r], before, 0.0), axis=0, keepdims=True)
        meta = jnp.where(msub == META_EXPERT + r, idxs[r].astype(F32), meta)
        meta = jnp.where(msub == META_RANK + r, rank_r, meta)
        meta = jnp.where(msub == META_GATE + r, gates[r], meta)
    meta_ref[...] = meta


def _local_positions(meta, lstart):
    tm = meta.shape[0]
    lane_e = lax.broadcasted_iota(jnp.int32, (tm, N_EXPERTS), 1)
    out = []
    for r in range(TOP_K):
        e_r = meta[:, META_EXPERT + r:META_EXPERT + r + 1].astype(jnp.int32)
        first = jnp.sum(jnp.where(lane_e == e_r, lstart, 0.0), axis=-1, keepdims=True)
        out.append((meta[:, META_RANK + r:META_RANK + r + 1] + first).astype(jnp.int32))
    return out


def _local_rows(tm):
    return TOP_K * tm + FILLER_ROWS


def _for_each_strip(count, max_rows, start_copy):
    top = 1 << (max(max_rows, FILLER_ROWS).bit_length() - 1)
    pieces = [top >> k for k in range(top.bit_length()) if (top >> k) >= STRIP_ALIGN]
    rare_from = 2 * max_rows * TOP_K // N_EXPERTS
    rare = [p for p in pieces if p >= rare_from]
    rare_bits = sum(rare)

    def cover(sizes, off):
        for piece in sizes:
            take = count & piece

            @pl.when(take != 0)
            def _(off=off, piece=piece):
                start_copy(off, piece)

            off = off + take

    @pl.when((count & rare_bits) != 0)
    def _():
        cover(rare, jnp.int32(0))

    cover([p for p in pieces if p < rare_from], count & rare_bits)


def _dispatch_kernel(blk_rows, base_ref, lstart_ref, cnt_ref, zflag_ref,
                     h_ref, meta_ref, lstart_col_ref, xs_ref,
                     sorted_ref, zero_ref, sems, zsem, tail_sem):
    i = pl.program_id(0)
    tm = h_ref.shape[0]
    nrows = _local_rows(tm)
    slot = lax.rem(i, DISPATCH_SLOTS)

    n_zero_blocks = xs_ref.shape[0] // blk_rows

    def zcopy(b, sem):
        start = pl.multiple_of(b * blk_rows, blk_rows)
        return pltpu.make_async_copy(zero_ref, xs_ref.at[pl.ds(start, blk_rows), :], sem)

    def for_flagged(flag, sem, act):
        def body(b, c):
            @pl.when(zflag_ref[b] == flag)
            def _():
                act(zcopy(b, sem))
            return c
        lax.fori_loop(0, n_zero_blocks, body, 0)

    @pl.when(i == 0)
    def _():
        zero_ref[...] = jnp.zeros_like(zero_ref)
        for_flagged(ZERO_FIRST, zsem, lambda cp: cp.start())
        for_flagged(ZERO_ANYTIME, tail_sem, lambda cp: cp.start())
        for_flagged(ZERO_FIRST, zsem, lambda cp: cp.wait())

    meta_t = meta_ref[...]
    sub_e = lax.broadcasted_iota(jnp.int32, (N_EXPERTS, tm), 0)
    pos = []
    for r in range(TOP_K):
        e_r = meta_t[META_EXPERT + r:META_EXPERT + r + 1, :].astype(jnp.int32)
        first = jnp.sum(jnp.where(sub_e == e_r, lstart_col_ref[...], 0.0), axis=0, keepdims=True)
        pos.append((meta_t[META_RANK + r:META_RANK + r + 1, :] + first).astype(jnp.int16))
    jr = lax.broadcasted_iota(jnp.int16, (nrows, tm), 0)
    one, zero = jnp.ones((), BF16), jnp.zeros((), BF16)
    onehot = jnp.where(jr == pos[0], one, jnp.where(jr == pos[1], one,
             jnp.where(jr == pos[2], one, jnp.where(jr == pos[3], one, zero))))
    srt = jnp.dot(onehot, h_ref[...], preferred_element_type=F32)
    sorted_ref[slot] = srt.astype(ROW_DTYPE)

    def per_segment(e, c):
        k = i * N_SEG + e
        src0 = lstart_ref[k]
        dst0 = base_ref[k]

        def start_copy(off, rows):
            src = pl.multiple_of(src0 + off, STRIP_ALIGN)
            dst = pl.multiple_of(dst0 + off, STRIP_ALIGN)
            pltpu.make_async_copy(sorted_ref.at[slot, pl.ds(src, rows), :],
                                  xs_ref.at[pl.ds(dst, rows), :],
                                  sems.at[slot]).start()

        _for_each_strip(cnt_ref[k], tm, start_copy)
        return c

    lax.fori_loop(0, N_SEG, per_segment, 0)

    def wait_tile(s):
        pltpu.make_async_copy(sorted_ref.at[s], xs_ref.at[pl.ds(0, nrows), :],
                              sems.at[s]).wait()

    @pl.when(i >= DISPATCH_SLOTS - 1)
    def _():
        wait_tile(lax.rem(i + 1, DISPATCH_SLOTS))

    @pl.when(i == pl.num_programs(0) - 1)
    def _():
        for age in range(DISPATCH_SLOTS - 2, -1, -1):
            @pl.when(i >= age)
            def _(age=age):
                wait_tile(lax.rem(i - age, DISPATCH_SLOTS))
        for_flagged(ZERO_ANYTIME, tail_sem, lambda cp: cp.wait())


def _dispatch(h2, meta, tables, n_buf, blk_rows, tm):
    t = h2.shape[0]
    grid_spec = pltpu.PrefetchScalarGridSpec(
        num_scalar_prefetch=4,
        grid=(t // tm,),
        in_specs=[
            pl.BlockSpec((tm, D_MODEL), lambda i, *_: (i, 0)),
            pl.BlockSpec((None, META_ROWS, tm), lambda i, *_: (i, 0, 0)),
            pl.BlockSpec((None, N_EXPERTS, 1), lambda i, *_: (i, 0, 0)),
        ],
        out_specs=pl.BlockSpec(memory_space=pl.ANY),
        scratch_shapes=[pltpu.VMEM((DISPATCH_SLOTS, _local_rows(tm), D_MODEL), ROW_DTYPE),
                        pltpu.VMEM((blk_rows, D_MODEL), ROW_DTYPE),
                        pltpu.SemaphoreType.DMA((DISPATCH_SLOTS,)),
                        pltpu.SemaphoreType.DMA(()),
                        pltpu.SemaphoreType.DMA(())],
    )
    return pl.pallas_call(
        functools.partial(_dispatch_kernel, blk_rows),
        out_shape=jax.ShapeDtypeStruct((n_buf + SPARE_ROWS, D_MODEL), ROW_DTYPE),
        grid_spec=grid_spec,
        compiler_params=pltpu.CompilerParams(
            dimension_semantics=("arbitrary",),
            vmem_limit_bytes=VMEM_LIMIT_BYTES,
            has_side_effects=True),
        name="dispatch",
    )(tables["base_out"], tables["lstart"], tables["cnt"], tables["zflag"],
      h2, meta, tables["lstart_cols"])


def _experts_kernel(be_ref, nu_ref, next_ref, x_ref, wgu_hbm, bgu_ref, wd_hbm, bd_ref,
                    y_ref, wgu_stage, wd_stage, wgu_bf, wd_bf, sems):
    i = pl.program_id(0)

    def weight_copies(e):
        return (pltpu.make_async_copy(wgu_hbm.at[e], wgu_stage, sems.at[0]),
                pltpu.make_async_copy(wd_hbm.at[e], wd_stage, sems.at[1]))

    @pl.when(i < nu_ref[0])
    def _():
        e = be_ref[i]
        prev = be_ref[jnp.maximum(i - 1, 0)]

        @pl.when(i == 0)
        def _():
            for cp in weight_copies(e):
                cp.start()

        @pl.when((i == 0) | (e != prev))
        def _():
            for cp in weight_copies(e):
                cp.wait()
            wgu_bf[...] = wgu_stage[...].astype(BF16)
            wd_bf[...] = wd_stage[...].astype(BF16)
            nxt = next_ref[e]

            @pl.when(nxt != e)
            def _():
                for cp in weight_copies(nxt):
                    cp.start()

        gu = jnp.dot(x_ref[...].astype(BF16), wgu_bf[...],
                     preferred_element_type=F32) + bgu_ref[...]
        glu = jnp.minimum(gu[:, :D_FF], SWIGLU_LIMIT)
        lin = jnp.clip(gu[:, D_FF:], -SWIGLU_LIMIT, SWIGLU_LIMIT)
        act = glu * jax.nn.sigmoid(SWIGLU_ALPHA * glu) * (lin + 1.0)
        y_ref[...] = (jnp.dot(act.astype(BF16), wd_bf[...],
                              preferred_element_type=F32) + bd_ref[...]).astype(ROW_DTYPE)

    @pl.when(i >= nu_ref[0])
    def _():
        y_ref[...] = jnp.zeros_like(y_ref)


def _experts(xs, n_buf, block_e, n_used, next_expert, w_gate_up, b_gate_up, w_down, b_down,
             blk_rows):
    used = lambda i, be, nu, nx: jnp.minimum(i, nu[0] - 1)
    grid_spec = pltpu.PrefetchScalarGridSpec(
        num_scalar_prefetch=3,
        grid=(n_buf // blk_rows,),
        in_specs=[
            pl.BlockSpec((blk_rows, D_MODEL), lambda i, be, nu, nx: (used(i, be, nu, nx), 0)),
            pl.BlockSpec(memory_space=pl.ANY),
            pl.BlockSpec((None, 1, 2 * D_FF), lambda i, be, nu, nx: (be[i], 0, 0)),
            pl.BlockSpec(memory_space=pl.ANY),
            pl.BlockSpec((None, 1, D_MODEL), lambda i, be, nu, nx: (be[i], 0, 0)),
        ],
        out_specs=pl.BlockSpec((blk_rows, D_MODEL), lambda i, be, nu, nx: (i, 0)),
        scratch_shapes=[pltpu.VMEM((D_MODEL, 2 * D_FF), F32),
                        pltpu.VMEM((D_FF, D_MODEL), F32),
                        pltpu.VMEM((D_MODEL, 2 * D_FF), BF16),
                        pltpu.VMEM((D_FF, D_MODEL), BF16),
                        pltpu.SemaphoreType.DMA((2,))],
    )
    return pl.pallas_call(
        _experts_kernel,
        out_shape=jax.ShapeDtypeStruct((n_buf, D_MODEL), ROW_DTYPE),
        grid_spec=grid_spec,
        compiler_params=pltpu.CompilerParams(
            dimension_semantics=("arbitrary",),
            vmem_limit_bytes=VMEM_LIMIT_BYTES),
        name="experts",
    )(block_e, n_used, next_expert, xs, w_gate_up, b_gate_up[:, None, :], w_down,
      b_down[:, None, :])


def _unsort_weights(meta_t, lstart_row, nrows):
    tm = meta_t.shape[1]
    meta = jnp.concatenate(
        [meta_t, jnp.zeros((LANES - META_ROWS, tm), F32)], axis=0).T
    pos = _local_positions(meta, lstart_row)
    jl = lax.broadcasted_iota(jnp.int16, (tm, nrows), 1)
    pos = [p.astype(jnp.int16) for p in pos]
    gate = [meta[:, META_GATE + r:META_GATE + r + 1].astype(BF16) for r in range(TOP_K)]
    zero = jnp.zeros((), BF16)
    return jnp.where(jl == pos[0], gate[0], jnp.where(jl == pos[1], gate[1],
           jnp.where(jl == pos[2], gate[2], jnp.where(jl == pos[3], gate[3], zero))))


def _combine_kernel(base_ref, lstart_ref, cnt_ref, meta_ref, lstart_row_ref,
                    meta_next_ref, lstart_row_next_ref, x1_ref,
                    p_ref, ys_ref, gple_ref, wpg_ref, wple_ref, gfin_ref, out_ref,
                    rows_ref, unsort_ref, sems):
    i = pl.program_id(0)
    n = pl.num_programs(0)
    tm = x1_ref.shape[0]
    nrows = _local_rows(tm)
    slot = lax.rem(i, 2)

    def fetch_tile(tile, s):
        def per_segment(e, c):
            k = tile * N_SEG + e
            src0 = base_ref[k]
            dst0 = lstart_ref[k]

            def start_copy(off, rows):
                src = pl.multiple_of(src0 + off, STRIP_ALIGN)
                dst = pl.multiple_of(dst0 + off, STRIP_ALIGN)
                pltpu.make_async_copy(ys_ref.at[pl.ds(src, rows), :],
                                      rows_ref.at[s, pl.ds(dst, rows), :],
                                      sems.at[s]).start()

            _for_each_strip(cnt_ref[k], tm, start_copy)
            return c

        lax.fori_loop(0, N_SEG, per_segment, 0)

    @pl.when(i == 0)
    def _():
        fetch_tile(0, 0)
        unsort_ref[0] = _unsort_weights(meta_ref[...], lstart_row_ref[...], nrows)

    @pl.when(i + 1 < n)
    def _():
        fetch_tile(i + 1, 1 - slot)

    pltpu.make_async_copy(ys_ref.at[pl.ds(0, nrows), :], rows_ref.at[slot],
                          sems.at[slot]).wait()
    moe = jnp.dot(unsort_ref[slot], rows_ref[slot].astype(BF16),
                  preferred_element_type=F32)
    x2 = x1_ref[...] + moe
    u = _rms(x2, gple_ref[...])
    gate_ple = jax.nn.sigmoid(jnp.dot(u.astype(BF16), wpg_ref[...],
                                      preferred_element_type=F32))
    emb = jnp.dot(p_ref[...].astype(BF16), wple_ref[...], preferred_element_type=F32)
    x3 = x2 + gate_ple * emb
    out_ref[...] = _rms(x3, gfin_ref[...])

    unsort_ref[1 - slot] = _unsort_weights(meta_next_ref[...], lstart_row_next_ref[...], nrows)


def _combine(meta, tables, x1, p2, ys, g_ple, w_ple_gate_bf, w_ple_bf, g_final, tm):
    t = x1.shape[0]
    full = lambda *shape: pl.BlockSpec(shape, lambda i, *_: (0,) * len(shape))
    last = t // tm - 1
    nxt = lambda i: jnp.minimum(i + 1, last)
    grid_spec = pltpu.PrefetchScalarGridSpec(
        num_scalar_prefetch=3,
        grid=(t // tm,),
        in_specs=[
            pl.BlockSpec((None, META_ROWS, tm), lambda i, *_: (i, 0, 0)),
            pl.BlockSpec((None, 1, N_EXPERTS), lambda i, *_: (i, 0, 0)),
            pl.BlockSpec((None, META_ROWS, tm), lambda i, *_: (nxt(i), 0, 0)),
            pl.BlockSpec((None, 1, N_EXPERTS), lambda i, *_: (nxt(i), 0, 0)),
            pl.BlockSpec((tm, D_MODEL), lambda i, *_: (i, 0)),
            pl.BlockSpec((tm, PLE_DIM), lambda i, *_: (i, 0)),
            pl.BlockSpec(memory_space=pl.ANY),
            full(1, D_MODEL),
            full(D_MODEL, D_MODEL),
            full(PLE_DIM, D_MODEL),
            full(1, D_MODEL),
        ],
        out_specs=pl.BlockSpec((tm, D_MODEL), lambda i, *_: (i, 0)),
        scratch_shapes=[pltpu.VMEM((2, _local_rows(tm), D_MODEL), ROW_DTYPE),
                        pltpu.VMEM((2, tm, _local_rows(tm)), BF16),
                        pltpu.SemaphoreType.DMA((2,))],
    )
    return pl.pallas_call(
        _combine_kernel,
        out_shape=jax.ShapeDtypeStruct((t, D_MODEL), F32),
        grid_spec=grid_spec,
        compiler_params=pltpu.CompilerParams(
            dimension_semantics=("arbitrary",),
            vmem_limit_bytes=VMEM_LIMIT_BYTES),
        name="combine",
    )(tables["base_in"], tables["lstart"], tables["cnt"], meta, tables["lstart_rows"],
      meta, tables["lstart_rows"], x1, p2, ys, g_ple, w_ple_gate_bf, w_ple_bf, g_final)


def _tiles(seq_len):
    return dict(
        proj_rows=min(512, seq_len),
        ret_block=min(256, seq_len),
        route_rows=min(512, seq_len),
        expert_rows=512,
    )


def _slot_tables(tile_counts, blk, n_buf, nloc):
    cnt = tile_counts[:, :, 0].astype(jnp.int32)
    cnt = ((cnt + STRIP_ALIGN - 1) // STRIP_ALIGN) * STRIP_ALIGN
    seg_rows = jnp.sum(cnt, axis=1)
    counts = jnp.sum(cnt, axis=0)
    padded = ((counts + blk - 1) // blk) * blk
    ends_pad = jnp.cumsum(padded)
    starts_pad = ends_pad - padded
    tile_before = jnp.cumsum(cnt, axis=0) - cnt
    lstart = jnp.cumsum(cnt, axis=1) - cnt
    base = starts_pad[None, :] + tile_before
    with_filler = lambda a, col: jnp.concatenate([a, col[:, None]], axis=1).reshape(-1)
    blk_start = jnp.arange((n_buf + SPARE_ROWS) // blk, dtype=jnp.int32) * blk
    last_of_expert = jnp.any((padded > 0)[None, :]
                             & (blk_start[:, None] == (ends_pad - blk)[None, :]), axis=1)
    zflag = jnp.where(last_of_expert | (blk_start >= n_buf), ZERO_FIRST,
                      jnp.where(blk_start >= ends_pad[-1], ZERO_ANYTIME, 0)).astype(jnp.int32)
    filler_dst = n_buf + (jnp.arange(cnt.shape[0], dtype=jnp.int32) % DISPATCH_SLOTS) * FILLER_ROWS
    ids = jnp.arange(N_EXPERTS, dtype=jnp.int32)
    later = jnp.where((padded > 0)[None, :] & (ids[None, :] > ids[:, None]), ids[None, :], N_EXPERTS)
    next_expert = jnp.min(later, axis=1)
    next_expert = jnp.where(next_expert == N_EXPERTS, ids, next_expert).astype(jnp.int32)
    return dict(
        cnt=with_filler(cnt, nloc - seg_rows),
        lstart=with_filler(lstart, seg_rows),
        base_out=with_filler(base, filler_dst),
        base_in=with_filler(base, jnp.zeros_like(seg_rows)),
        lstart_rows=lstart.astype(F32)[:, None, :],
        lstart_cols=lstart.astype(F32)[:, :, None],
        zflag=zflag, ends_pad=ends_pad, next_expert=next_expert)


def _layer(x2, p2, seq_len, g_mix, w_in, w_pool, pool_scale, w_ret_o, w_out, g_ffn,
           w_router, b_router, w_gate_up, b_gate_up, w_down, b_down, g_ple,
           w_ple_gate, w_ple, g_out):
    t = x2.shape[0]
    cfg = _tiles(seq_len)
    row = lambda a: a.reshape(1, -1)

    wr_hi = w_router.astype(BF16)
    wr_lo = (w_router - wr_hi.astype(F32)).astype(BF16)
    wr_split = jnp.pad(jnp.concatenate([wr_hi, wr_lo], axis=1),
                       ((0, 0), (0, LANES - 2 * N_EXPERTS)))
    lane_pad = ((0, 0), (0, LANES - N_EXPERTS))
    tm = cfg["route_rows"]
    x1, h2, meta, tile_counts = _token_mixer(
        x2.reshape(t // seq_len, seq_len, D_MODEL), row(g_mix), w_in.astype(BF16),
        w_pool.astype(BF16), row(pool_scale), w_ret_o.astype(BF16), w_out.astype(BF16),
        row(g_ffn), wr_split, jnp.pad(row(b_router), lane_pad), tm, cfg["ret_block"])
    x1 = x1.reshape(t, D_MODEL)
    h2 = h2.reshape(t, D_MODEL)

    blk = cfg["expert_rows"]
    n_buf = t * TOP_K + (t // tm) * FILLER_ROWS + N_EXPERTS * blk
    n_blocks = n_buf // blk
    tables = _slot_tables(tile_counts, blk, n_buf, _local_rows(tm))
    blk_start = jnp.arange(n_blocks, dtype=jnp.int32) * blk
    block_e = jnp.minimum(
        jnp.sum((tables["ends_pad"][None, :] <= blk_start[:, None]).astype(jnp.int32), axis=1),
        N_EXPERTS - 1)
    n_used = tables["ends_pad"][-1:] // blk

    xs = _dispatch(h2, meta, tables, n_buf, blk, tm)
    ys = _experts(xs, n_buf, block_e, n_used, tables["next_expert"], w_gate_up, b_gate_up,
                  w_down, b_down, blk)
    return _combine(meta, tables, x1, p2, ys, row(g_ple), w_ple_gate.astype(BF16),
                    w_ple.astype(BF16), row(g_out), tm)


def kernel(x, p, g_mix, w_in, w_pool, pool_scale, w_ret_o, w_out, g_ffn, w_router,
           b_router, w_gate_up, b_gate_up, w_down, b_down, g_ple, w_ple_gate, w_ple,
           g_final):
    b, s, d = x.shape
    depth = p.shape[0]
    assert depth == 1 and d == D_MODEL
    x2 = x.reshape(b * s, d)
    out = _layer(x2, p[0].reshape(b * s, PLE_DIM), s, g_mix[0], w_in[0], w_pool[0],
                 pool_scale[0], w_ret_o[0], w_out[0], g_ffn[0], w_router[0],
                 b_router[0], w_gate_up[0], b_gate_up[0], w_down[0], b_down[0],
                 g_ple[0], w_ple_gate[0], w_ple[0], g_final)
    return out.reshape(b, s, d)
```

```python
import functools

import numpy as np
import jax
import jax.numpy as jnp
from jax import lax
from jax.experimental import pallas as pl
from jax.experimental.pallas import tpu as pltpu

F32 = jnp.float32
BF16 = jnp.bfloat16

D_MODEL = 1024
EPS = 1e-6
CHUNK = 64
PLE_DIM = 256
POOL_WINDOWS = (2, 4, 8, 16)
POOL_GROUP_W = D_MODEL // len(POOL_WINDOWS)
POOL_HALO = 16
RET_HEADS = 4
RET_QK_HEAD = 256
RET_V_HEAD = 512
RET_V_W = RET_HEADS * RET_V_HEAD
ROPE_BASE = 10000.0
IN_W = 9 * D_MODEL
N_EXPERTS = 32
TOP_K = 4
D_FF = D_MODEL
SWIGLU_ALPHA = 1.702
SWIGLU_LIMIT = 7.0

VMEM_LIMIT_BYTES = 56 * 1024 * 1024
LANES = 128
META_ROWS = 16
META_EXPERT, META_RANK, META_GATE = 0, TOP_K, 2 * TOP_K
ROW_DTYPE = F32
SUBLANES = 8
STRIP_ALIGN = SUBLANES * 4 // jnp.dtype(ROW_DTYPE).itemsize
FILLER_ROWS = N_EXPERTS * STRIP_ALIGN
N_SEG = N_EXPERTS + 1
ZERO_FIRST, ZERO_ANYTIME = 1, 2
DISPATCH_SLOTS = 3
SPARE_ROWS = 4 * FILLER_ROWS


def _rms(x, g):
    return x * lax.rsqrt(jnp.mean(x * x, axis=-1, keepdims=True) + EPS) * g


COL_U, COL_Q, COL_K = 0, 1, 2
COL_V, COL_G = (3, 4), (5, 6)
COL_GATE_A, COL_GATE_B = 7, 8
SIDE_U, SIDE_GATE_A, SIDE_GATE_B = 0, 1, 2


def _rotary_heads(y, cos, sin):
    half = RET_QK_HEAD // 2
    parts = []
    for h in range(RET_HEADS):
        x1 = y[:, h * RET_QK_HEAD:h * RET_QK_HEAD + half]
        x2 = y[:, h * RET_QK_HEAD + half:(h + 1) * RET_QK_HEAD]
        parts += [x1 * cos - x2 * sin, x2 * cos + x1 * sin]
    return jnp.concatenate(parts, axis=-1)


def _head_decay_logs():
    return [float(np.log(1.0 - 2.0 ** (-5.0 - h))) for h in range(RET_HEADS)]


def _token_mixer_kernel(blk_decay, ret_blk, x_ref, g_ref, w_ref, cos_ref, sin_ref,
                        d_ref, qd_ref, kd_ref,
                        wpool_ref, pscale_ref, wreto_ref, wout_ref, gffn_ref,
                        wr_ref, br_ref,
                        x1_ref, h2_ref, meta_ref, counts_ref,
                        side_ref, o_ref, q_s, k_s, v_s, g_s, state_ref, halo_ref):
    @pl.when(pl.program_id(1) == 0)
    def _():
        state_ref[...] = jnp.zeros_like(state_ref)
        halo_ref[...] = jnp.zeros_like(halo_ref)

    tm = x_ref.shape[0]
    h_in = _rms(x_ref[...], g_ref[...]).astype(BF16)

    def proj(c):
        return jnp.dot(h_in, w_ref[:, c * D_MODEL:(c + 1) * D_MODEL],
                       preferred_element_type=F32)

    def chunk(n):
        return slice(n * D_MODEL, (n + 1) * D_MODEL)

    side_ref[:, chunk(SIDE_U)] = proj(COL_U).astype(BF16)
    q_s[...] = _rotary_heads(proj(COL_Q), cos_ref[...], sin_ref[...]).astype(BF16)
    k_s[...] = (_rotary_heads(proj(COL_K), cos_ref[...], sin_ref[...])
                * (RET_QK_HEAD ** -0.5)).astype(BF16)
    for n, c in enumerate(COL_V):
        v_s[:, chunk(n)] = proj(c).astype(BF16)
    for n, c in enumerate(COL_G):
        y = proj(c)
        g_s[:, chunk(n)] = (y * jax.nn.sigmoid(y)).astype(BF16)
    side_ref[:, chunk(SIDE_GATE_A)] = proj(COL_GATE_A).astype(BF16)
    side_ref[:, chunk(SIDE_GATE_B)] = proj(COL_GATE_B).astype(BF16)

    for b in range(tm // ret_blk):
        rows = slice(b * ret_blk, (b + 1) * ret_blk)
        for h in range(RET_HEADS):
            qk_cols = slice(h * RET_QK_HEAD, (h + 1) * RET_QK_HEAD)
            v_cols = slice(h * RET_V_HEAD, (h + 1) * RET_V_HEAD)
            q = q_s[rows, qk_cols]
            k = k_s[rows, qk_cols]
            v = v_s[rows, v_cols]
            scores = lax.dot_general(q, k, (((1,), (1,)), ((), ())),
                                     preferred_element_type=F32) * d_ref[h]
            state = state_ref[h]
            o = jnp.dot(scores.astype(BF16), v, preferred_element_type=F32)
            o = o + jnp.dot((q.astype(F32) * qd_ref[h]).astype(BF16), state.astype(BF16),
                            preferred_element_type=F32)
            k_dec = (k.astype(F32) * kd_ref[h]).astype(BF16)
            state_ref[h] = state * blk_decay[h] + lax.dot_general(
                k_dec, v, (((0,), (0,)), ((), ())), preferred_element_type=F32)
            o = o * lax.rsqrt(jnp.mean(o * o, axis=-1, keepdims=True) + EPS)
            o_ref[rows, v_cols] = (o * g_s[rows, v_cols].astype(F32)).astype(BF16)

    u_bf = side_ref[:, chunk(SIDE_U)]
    _mix_route(pl.program_id(1) * tm, u_bf, halo_ref[...], side_ref[:, chunk(SIDE_GATE_A)],
               side_ref[:, chunk(SIDE_GATE_B)], o_ref[...], x_ref[...],
               wpool_ref, pscale_ref, wreto_ref, wout_ref, gffn_ref,
               wr_ref, br_ref, x1_ref, h2_ref, meta_ref, counts_ref)
    halo_ref[...] = u_bf[tm - POOL_HALO:, :]


def _rotary_tables(s):
    half = RET_QK_HEAD // 2
    pos = jnp.arange(s, dtype=F32)
    inv = ROPE_BASE ** (-jnp.linspace(0.0, 1.0, half, dtype=F32))
    ang = pos[:, None] * inv[None, :]
    return jnp.cos(ang), jnp.sin(ang)


def _retention_tables(blk):
    log_g = jnp.asarray(_head_decay_logs(), F32)
    idx = jnp.arange(blk, dtype=F32)
    diff = idx[:, None] - idx[None, :]
    chunk = jnp.arange(blk, dtype=jnp.int32) // CHUNK
    visible = chunk[None, :] <= chunk[:, None]
    dmask = jnp.where(visible[None], jnp.exp(log_g[:, None, None] * jnp.abs(diff)[None]), 0.0)
    q_dec = jnp.exp(log_g[:, None] * (idx + 1.0))[:, :, None]
    k_dec = jnp.exp(log_g[:, None] * (blk - 1.0 - idx))[:, :, None]
    q_dec = jnp.broadcast_to(q_dec, (RET_HEADS, blk, RET_QK_HEAD))
    k_dec = jnp.broadcast_to(k_dec, (RET_HEADS, blk, RET_QK_HEAD))
    return dmask.astype(F32), q_dec, k_dec


def _token_mixer(x3, g_mix, w_in_bf, w_pool_bf, pool_scale, w_ret_o_bf, w_out_bf,
                 g_ffn, wr_split, b_router, tm, ret_blk):
    b, s, _ = x3.shape
    half = RET_QK_HEAD // 2
    tiles_per_seq = s // tm
    n_tiles = b * tiles_per_seq
    cos, sin = _rotary_tables(s)
    dmask, q_dec, k_dec = _retention_tables(ret_blk)
    blk_decay = [float(np.exp(lg * ret_blk)) for lg in _head_decay_logs()]
    once = lambda *shape: pl.BlockSpec(shape, lambda bi, j: (0,) * len(shape),
                                       pipeline_mode=pl.Buffered(1))
    tile = lambda bi, j: bi * tiles_per_seq + j
    return pl.pallas_call(
        functools.partial(_token_mixer_kernel, blk_decay, ret_blk),
        out_shape=(
            jax.ShapeDtypeStruct((b, s, D_MODEL), F32),
            jax.ShapeDtypeStruct((b, s, D_MODEL), BF16),
            jax.ShapeDtypeStruct((n_tiles, META_ROWS, tm), F32),
            jax.ShapeDtypeStruct((n_tiles, N_EXPERTS, 1), F32),
        ),
        grid=(b, tiles_per_seq),
        in_specs=[
            pl.BlockSpec((None, tm, D_MODEL), lambda bi, j: (bi, j, 0)),
            pl.BlockSpec((1, D_MODEL), lambda bi, j: (0, 0)),
            once(D_MODEL, IN_W),
            pl.BlockSpec((tm, half), lambda bi, j: (j, 0)),
            pl.BlockSpec((tm, half), lambda bi, j: (j, 0)),
            once(RET_HEADS, ret_blk, ret_blk),
            once(RET_HEADS, ret_blk, RET_QK_HEAD),
            once(RET_HEADS, ret_blk, RET_QK_HEAD),
            once(len(POOL_WINDOWS), POOL_GROUP_W, POOL_GROUP_W),
            once(1, D_MODEL),
            once(RET_V_W, D_MODEL),
            once(D_MODEL, D_MODEL),
            once(1, D_MODEL),
            once(D_MODEL, LANES),
            once(1, LANES),
        ],
        out_specs=(
            pl.BlockSpec((None, tm, D_MODEL), lambda bi, j: (bi, j, 0)),
            pl.BlockSpec((None, tm, D_MODEL), lambda bi, j: (bi, j, 0)),
            pl.BlockSpec((None, META_ROWS, tm), lambda bi, j: (tile(bi, j), 0, 0)),
            pl.BlockSpec((None, N_EXPERTS, 1), lambda bi, j: (tile(bi, j), 0, 0)),
        ),
        scratch_shapes=[pltpu.VMEM((tm, 3 * D_MODEL), BF16),
                        pltpu.VMEM((tm, RET_V_W), BF16),
                        pltpu.VMEM((tm, D_MODEL), BF16),
                        pltpu.VMEM((tm, D_MODEL), BF16),
                        pltpu.VMEM((tm, RET_V_W), BF16),
                        pltpu.VMEM((tm, RET_V_W), BF16),
                        pltpu.VMEM((RET_HEADS, RET_QK_HEAD, RET_V_HEAD), F32),
                        pltpu.VMEM((POOL_HALO, D_MODEL), BF16)],
        compiler_params=pltpu.CompilerParams(
            dimension_semantics=("arbitrary", "arbitrary"),
            vmem_limit_bytes=VMEM_LIMIT_BYTES),
        name="token_mixer",
    )(x3, g_mix, w_in_bf, cos, sin, dmask, q_dec, k_dec, w_pool_bf, pool_scale,
      w_ret_o_bf, w_out_bf, g_ffn, wr_split, b_router)


def _window_sum(ext, w, tm):
    cur = ext
    span = 1
    while span < w:
        cur = cur[span:, :] + cur[:-span, :]
        span *= 2
    start = POOL_HALO + 1 - w
    return cur[start:start + tm, :]


def _mix_route(pos0, u_bf, halo_bf, gate_a, gate_b, o_gated, x,
               wpool_ref, pscale_ref, wreto_ref, wout_ref, gffn_ref,
               wr_ref, br_ref,
               x1_ref, h2_ref, meta_ref, counts_ref):
    tm = x.shape[0]

    u = u_bf.astype(F32)
    ext = jnp.concatenate([halo_bf.astype(F32), u], axis=0)
    pos = (pos0 + lax.broadcasted_iota(jnp.int32, (tm, 1), 0)).astype(F32)
    pooled_out = []
    for g, w in enumerate(POOL_WINDOWS):
        cols = slice(g * POOL_GROUP_W, (g + 1) * POOL_GROUP_W)
        ws = _window_sum(ext[:, cols], w, tm)
        count = jnp.minimum(pos + 1.0, float(w))
        pooled = ws / count - u[:, cols]
        pooled_out.append(jnp.dot(pooled.astype(BF16), wpool_ref[g],
                                  preferred_element_type=F32))
    y_pool = jnp.concatenate(pooled_out, axis=-1) * pscale_ref[...]

    y_ret = jnp.dot(o_gated, wreto_ref[...], preferred_element_type=F32)
    merged = (jax.nn.sigmoid(gate_a.astype(F32)) * y_pool
              + jax.nn.sigmoid(gate_b.astype(F32)) * y_ret)
    x1 = x + jnp.dot(merged.astype(BF16), wout_ref[...], preferred_element_type=F32)
    x1_ref[...] = x1
    h2 = _rms(x1, gffn_ref[...])
    h2_ref[...] = h2.astype(BF16)

    h_hi = h2.astype(BF16)
    h_lo = (h2 - h_hi.astype(F32)).astype(BF16)
    parts = (jnp.dot(h_hi, wr_ref[...], preferred_element_type=F32)
             + jnp.dot(h_lo, wr_ref[...], preferred_element_type=F32)
             + br_ref[...])
    parts_t = parts.T
    logits_t = parts_t[:N_EXPERTS, :] + parts_t[N_EXPERTS:2 * N_EXPERTS, :]

    sub = lax.broadcasted_iota(jnp.int32, (N_EXPERTS, tm), 0)
    work = logits_t
    vals, idxs, hots = [], [], []
    for _ in range(TOP_K):
        m = jnp.max(work, axis=0, keepdims=True)
        idx = jnp.min(jnp.where(work == m, sub, N_EXPERTS), axis=0, keepdims=True)
        hot = sub == idx
        vals.append(m)
        idxs.append(idx)
        hots.append(hot)
        work = jnp.where(hot, -jnp.inf, work)
    exps = [jnp.exp(v - vals[0]) for v in vals]
    denom = exps[0] + exps[1] + exps[2] + exps[3]
    gates = [e / denom for e in exps]

    sel = (jnp.where(hots[0], 1.0, 0.0) + jnp.where(hots[1], 1.0, 0.0)
           + jnp.where(hots[2], 1.0, 0.0) + jnp.where(hots[3], 1.0, 0.0))
    row = lax.broadcasted_iota(jnp.int32, (tm, tm), 0)
    col = lax.broadcasted_iota(jnp.int32, (tm, tm), 1)
    earlier = jnp.where(row < col, 1.0, 0.0).astype(BF16)
    before = jnp.dot(sel.astype(BF16), earlier, preferred_element_type=F32)
    counts_ref[...] = jnp.sum(sel, axis=1, keepdims=True)

    msub = lax.broadcasted_iota(jnp.int32, (META_ROWS, tm), 0)
    meta = jnp.zeros((META_ROWS, tm), F32)
    for r in range(TOP_K):
        rank_r = jnp.sum(jnp.where(hots[r], before, 0.0), axis=0, keepdims=True)
        meta = jnp.where(msub == META_EXPERT + r, idxs[r].astype(F32), meta)
        meta = jnp.where(msub == META_RANK + r, rank_r, meta)
        meta = jnp.where(msub == META_GATE + r, gates[r], meta)
    meta_ref[...] = meta


def _local_positions(meta, lstart):
    tm = meta.shape[0]
    lane_e = lax.broadcasted_iota(jnp.int32, (tm, N_EXPERTS), 1)
    out = []
    for r in range(TOP_K):
        e_r = meta[:, META_EXPERT + r:META_EXPERT + r + 1].astype(jnp.int32)
        first = jnp.sum(jnp.where(lane_e == e_r, lstart, 0.0), axis=-1, keepdims=True)
        out.append((meta[:, META_RANK + r:META_RANK + r + 1] + first).astype(jnp.int32))
    return out


def _local_rows(tm):
    return TOP_K * tm + FILLER_ROWS


def _for_each_strip(count, max_rows, start_copy):
    top = 1 << (max(max_rows, FILLER_ROWS).bit_length() - 1)
    pieces = [top >> k for k in range(top.bit_length()) if (top >> k) >= STRIP_ALIGN]
    rare_from = 2 * max_rows * TOP_K // N_EXPERTS
    rare = [p for p in pieces if p >= rare_from]
    rare_bits = sum(rare)

    def cover(sizes, off):
        for piece in sizes:
            take = count & piece

            @pl.when(take != 0)
            def _(off=off, piece=piece):
                start_copy(off, piece)

            off = off + take

    @pl.when((count & rare_bits) != 0)
    def _():
        cover(rare, jnp.int32(0))

    cover([p for p in pieces if p < rare_from], count & rare_bits)


def _dispatch_kernel(blk_rows, base_ref, lstart_ref, cnt_ref, zflag_ref,
                     h_ref, meta_ref, lstart_col_ref, xs_ref,
                     sorted_ref, zero_ref, sems, zsem, tail_sem):
    i = pl.program_id(0)
    tm = h_ref.shape[0]
    nrows = _local_rows(tm)
    slot = lax.rem(i, DISPATCH_SLOTS)

    n_zero_blocks = xs_ref.shape[0] // blk_rows

    def zcopy(b, sem):
        start = pl.multiple_of(b * blk_rows, blk_rows)
        return pltpu.make_async_copy(zero_ref, xs_ref.at[pl.ds(start, blk_rows), :], sem)

    def for_flagged(flag, sem, act):
        def body(b, c):
            @pl.when(zflag_ref[b] == flag)
            def _():
                act(zcopy(b, sem))
            return c
        lax.fori_loop(0, n_zero_blocks, body, 0)

    @pl.when(i == 0)
    def _():
        zero_ref[...] = jnp.zeros_like(zero_ref)
        for_flagged(ZERO_FIRST, zsem, lambda cp: cp.start())
        for_flagged(ZERO_ANYTIME, tail_sem, lambda cp: cp.start())
        for_flagged(ZERO_FIRST, zsem, lambda cp: cp.wait())

    meta_t = meta_ref[...]
    sub_e = lax.broadcasted_iota(jnp.int32, (N_EXPERTS, tm), 0)
    pos = []
    for r in range(TOP_K):
        e_r = meta_t[META_EXPERT + r:META_EXPERT + r + 1, :].astype(jnp.int32)
        first = jnp.sum(jnp.where(sub_e == e_r, lstart_col_ref[...], 0.0), axis=0, keepdims=True)
        pos.append((meta_t[META_RANK + r:META_RANK + r + 1, :] + first).astype(jnp.int16))
    jr = lax.broadcasted_iota(jnp.int16, (nrows, tm), 0)
    one, zero = jnp.ones((), BF16), jnp.zeros((), BF16)
    onehot = jnp.where(jr == pos[0], one, jnp.where(jr == pos[1], one,
             jnp.where(jr == pos[2], one, jnp.where(jr == pos[3], one, zero))))
    srt = jnp.dot(onehot, h_ref[...], preferred_element_type=F32)
    sorted_ref[slot] = srt.astype(ROW_DTYPE)

    def per_segment(e, c):
        k = i * N_SEG + e
        src0 = lstart_ref[k]
        dst0 = base_ref[k]

        def start_copy(off, rows):
            src = pl.multiple_of(src0 + off, STRIP_ALIGN)
            dst = pl.multiple_of(dst0 + off, STRIP_ALIGN)
            pltpu.make_async_copy(sorted_ref.at[slot, pl.ds(src, rows), :],
                                  xs_ref.at[pl.ds(dst, rows), :],
                                  sems.at[slot]).start()

        _for_each_strip(cnt_ref[k], tm, start_copy)
        return c

    lax.fori_loop(0, N_SEG, per_segment, 0)

    def wait_tile(s):
        pltpu.make_async_copy(sorted_ref.at[s], xs_ref.at[pl.ds(0, nrows), :],
                              sems.at[s]).wait()

    @pl.when(i >= DISPATCH_SLOTS - 1)
    def _():
        wait_tile(lax.rem(i + 1, DISPATCH_SLOTS))

    @pl.when(i == pl.num_programs(0) - 1)
    def _():
        for age in range(DISPATCH_SLOTS - 2, -1, -1):
            @pl.when(i >= age)
            def _(age=age):
                wait_tile(lax.rem(i - age, DISPATCH_SLOTS))
        for_flagged(ZERO_ANYTIME, tail_sem, lambda cp: cp.wait())


def _dispatch(h2, meta, tables, n_buf, blk_rows, tm):
    t = h2.shape[0]
    grid_spec = pltpu.PrefetchScalarGridSpec(
        num_scalar_prefetch=4,
        grid=(t // tm,),
        in_specs=[
            pl.BlockSpec((tm, D_MODEL), lambda i, *_: (i, 0)),
            pl.BlockSpec((None, META_ROWS, tm), lambda i, *_: (i, 0, 0)),
            pl.BlockSpec((None, N_EXPERTS, 1), lambda i, *_: (i, 0, 0)),
        ],
        out_specs=pl.BlockSpec(memory_space=pl.ANY),
        scratch_shapes=[pltpu.VMEM((DISPATCH_SLOTS, _local_rows(tm), D_MODEL), ROW_DTYPE),
                        pltpu.VMEM((blk_rows, D_MODEL), ROW_DTYPE),
                        pltpu.SemaphoreType.DMA((DISPATCH_SLOTS,)),
                        pltpu.SemaphoreType.DMA(()),
                        pltpu.SemaphoreType.DMA(())],
    )
    return pl.pallas_call(
        functools.partial(_dispatch_kernel, blk_rows),
        out_shape=jax.ShapeDtypeStruct((n_buf + SPARE_ROWS, D_MODEL), ROW_DTYPE),
        grid_spec=grid_spec,
        compiler_params=pltpu.CompilerParams(
            dimension_semantics=("arbitrary",),
            vmem_limit_bytes=VMEM_LIMIT_BYTES,
            has_side_effects=True),
        name="dispatch",
    )(tables["base_out"], tables["lstart"], tables["cnt"], tables["zflag"],
      h2, meta, tables["lstart_cols"])


def _experts_kernel(be_ref, nu_ref, next_ref, x_ref, wgu_hbm, bgu_ref, wd_hbm, bd_ref,
                    y_ref, wgu_stage, wd_stage, wgu_bf, wd_bf, sems):
    i = pl.program_id(0)

    def weight_copies(e):
        return (pltpu.make_async_copy(wgu_hbm.at[e], wgu_stage, sems.at[0]),
                pltpu.make_async_copy(wd_hbm.at[e], wd_stage, sems.at[1]))

    @pl.when(i < nu_ref[0])
    def _():
        e = be_ref[i]
        prev = be_ref[jnp.maximum(i - 1, 0)]

        @pl.when(i == 0)
        def _():
            for cp in weight_copies(e):
                cp.start()

        @pl.when((i == 0) | (e != prev))
        def _():
            for cp in weight_copies(e):
                cp.wait()
            wgu_bf[...] = wgu_stage[...].astype(BF16)
            wd_bf[...] = wd_stage[...].astype(BF16)
            nxt = next_ref[e]

            @pl.when(nxt != e)
            def _():
                for cp in weight_copies(nxt):
                    cp.start()

        gu = jnp.dot(x_ref[...].astype(BF16), wgu_bf[...],
                     preferred_element_type=F32) + bgu_ref[...]
        glu = jnp.minimum(gu[:, :D_FF], SWIGLU_LIMIT)
        lin = jnp.clip(gu[:, D_FF:], -SWIGLU_LIMIT, SWIGLU_LIMIT)
        act = glu * jax.nn.sigmoid(SWIGLU_ALPHA * glu) * (lin + 1.0)
        y_ref[...] = (jnp.dot(act.astype(BF16), wd_bf[...],
                              preferred_element_type=F32) + bd_ref[...]).astype(ROW_DTYPE)

    @pl.when(i >= nu_ref[0])
    def _():
        y_ref[...] = jnp.zeros_like(y_ref)


def _experts(xs, n_buf, block_e, n_used, next_expert, w_gate_up, b_gate_up, w_down, b_down,
             blk_rows):
    used = lambda i, be, nu, nx: jnp.minimum(i, nu[0] - 1)
    grid_spec = pltpu.PrefetchScalarGridSpec(
        num_scalar_prefetch=3,
        grid=(n_buf // blk_rows,),
        in_specs=[
            pl.BlockSpec((blk_rows, D_MODEL), lambda i, be, nu, nx: (used(i, be, nu, nx), 0)),
            pl.BlockSpec(memory_space=pl.ANY),
            pl.BlockSpec((None, 1, 2 * D_FF), lambda i, be, nu, nx: (be[i], 0, 0)),
            pl.BlockSpec(memory_space=pl.ANY),
            pl.BlockSpec((None, 1, D_MODEL), lambda i, be, nu, nx: (be[i], 0, 0)),
        ],
        out_specs=pl.BlockSpec((blk_rows, D_MODEL), lambda i, be, nu, nx: (i, 0)),
        scratch_shapes=[pltpu.VMEM((D_MODEL, 2 * D_FF), F32),
                        pltpu.VMEM((D_FF, D_MODEL), F32),
                        pltpu.VMEM((D_MODEL, 2 * D_FF), BF16),
                        pltpu.VMEM((D_FF, D_MODEL), BF16),
                        pltpu.SemaphoreType.DMA((2,))],
    )
    return pl.pallas_call(
        _experts_kernel,
        out_shape=jax.ShapeDtypeStruct((n_buf, D_MODEL), ROW_DTYPE),
        grid_spec=grid_spec,
        compiler_params=pltpu.CompilerParams(
            dimension_semantics=("arbitrary",),
            vmem_limit_bytes=VMEM_LIMIT_BYTES),
        name="experts",
    )(block_e, n_used, next_expert, xs, w_gate_up, b_gate_up[:, None, :], w_down,
      b_down[:, None, :])


def _unsort_weights(meta_t, lstart_row, nrows):
    tm = meta_t.shape[1]
    meta = jnp.concatenate(
        [meta_t, jnp.zeros((LANES - META_ROWS, tm), F32)], axis=0).T
    pos = _local_positions(meta, lstart_row)
    jl = lax.broadcasted_iota(jnp.int16, (tm, nrows), 1)
    pos = [p.astype(jnp.int16) for p in pos]
    gate = [meta[:, META_GATE + r:META_GATE + r + 1].astype(BF16) for r in range(TOP_K)]
    zero = jnp.zeros((), BF16)
    return jnp.where(jl == pos[0], gate[0], jnp.where(jl == pos[1], gate[1],
           jnp.where(jl == pos[2], gate[2], jnp.where(jl == pos[3], gate[3], zero))))


def _combine_kernel(base_ref, lstart_ref, cnt_ref, meta_ref, lstart_row_ref,
                    meta_next_ref, lstart_row_next_ref, x1_ref,
                    p_ref, ys_ref, gple_ref, wpg_ref, wple_ref, gfin_ref, out_ref,
                    rows_ref, unsort_ref, sems):
    i = pl.program_id(0)
    n = pl.num_programs(0)
    tm = x1_ref.shape[0]
    nrows = _local_rows(tm)
    slot = lax.rem(i, 2)

    def fetch_tile(tile, s):
        def per_segment(e, c):
            k = tile * N_SEG + e
            src0 = base_ref[k]
            dst0 = lstart_ref[k]

            def start_copy(off, rows):
                src = pl.multiple_of(src0 + off, STRIP_ALIGN)
                dst = pl.multiple_of(dst0 + off, STRIP_ALIGN)
                pltpu.make_async_copy(ys_ref.at[pl.ds(src, rows), :],
                                      rows_ref.at[s, pl.ds(dst, rows), :],
                                      sems.at[s]).start()

            _for_each_strip(cnt_ref[k], tm, start_copy)
            return c

        lax.fori_loop(0, N_SEG, per_segment, 0)

    @pl.when(i == 0)
    def _():
        fetch_tile(0, 0)
        unsort_ref[0] = _unsort_weights(meta_ref[...], lstart_row_ref[...], nrows)

    @pl.when(i + 1 < n)
    def _():
        fetch_tile(i + 1, 1 - slot)

    pltpu.make_async_copy(ys_ref.at[pl.ds(0, nrows), :], rows_ref.at[slot],
                          sems.at[slot]).wait()
    moe = jnp.dot(unsort_ref[slot], rows_ref[slot].astype(BF16),
                  preferred_element_type=F32)
    x2 = x1_ref[...] + moe
    u = _rms(x2, gple_ref[...])
    gate_ple = jax.nn.sigmoid(jnp.dot(u.astype(BF16), wpg_ref[...],
                                      preferred_element_type=F32))
    emb = jnp.dot(p_ref[...].astype(BF16), wple_ref[...], preferred_element_type=F32)
    x3 = x2 + gate_ple * emb
    out_ref[...] = _rms(x3, gfin_ref[...])

    unsort_ref[1 - slot] = _unsort_weights(meta_next_ref[...], lstart_row_next_ref[...], nrows)


def _combine(meta, tables, x1, p2, ys, g_ple, w_ple_gate_bf, w_ple_bf, g_final, tm):
    t = x1.shape[0]
    full = lambda *shape: pl.BlockSpec(shape, lambda i, *_: (0,) * len(shape))
    last = t // tm - 1
    nxt = lambda i: jnp.minimum(i + 1, last)
    grid_spec = pltpu.PrefetchScalarGridSpec(
        num_scalar_prefetch=3,
        grid=(t // tm,),
        in_specs=[
            pl.BlockSpec((None, META_ROWS, tm), lambda i, *_: (i, 0, 0)),
            pl.BlockSpec((None, 1, N_EXPERTS), lambda i, *_: (i, 0, 0)),
            pl.BlockSpec((None, META_ROWS, tm), lambda i, *_: (nxt(i), 0, 0)),
            pl.BlockSpec((None, 1, N_EXPERTS), lambda i, *_: (nxt(i), 0, 0)),
            pl.BlockSpec((tm, D_MODEL), lambda i, *_: (i, 0)),
            pl.BlockSpec((tm, PLE_DIM), lambda i, *_: (i, 0)),
            pl.BlockSpec(memory_space=pl.ANY),
            full(1, D_MODEL),
            full(D_MODEL, D_MODEL),
            full(PLE_DIM, D_MODEL),
            full(1, D_MODEL),
        ],
        out_specs=pl.BlockSpec((tm, D_MODEL), lambda i, *_: (i, 0)),
        scratch_shapes=[pltpu.VMEM((2, _local_rows(tm), D_MODEL), ROW_DTYPE),
                        pltpu.VMEM((2, tm, _local_rows(tm)), BF16),
                        pltpu.SemaphoreType.DMA((2,))],
    )
    return pl.pallas_call(
        _combine_kernel,
        out_shape=jax.ShapeDtypeStruct((t, D_MODEL), F32),
        grid_spec=grid_spec,
        compiler_params=pltpu.CompilerParams(
            dimension_semantics=("arbitrary",),
            vmem_limit_bytes=VMEM_LIMIT_BYTES),
        name="combine",
    )(tables["base_in"], tables["lstart"], tables["cnt"], meta, tables["lstart_rows"],
      meta, tables["lstart_rows"], x1, p2, ys, g_ple, w_ple_gate_bf, w_ple_bf, g_final)


def _tiles(seq_len):
    return dict(
        ret_block=min(256, seq_len),
        route_rows=min(512, seq_len),
        expert_rows=512,
    )


def _slot_tables(tile_counts, blk, n_buf, nloc):
    cnt = tile_counts[:, :, 0].astype(jnp.int32)
    cnt = ((cnt + STRIP_ALIGN - 1) // STRIP_ALIGN) * STRIP_ALIGN
    seg_rows = jnp.sum(cnt, axis=1)
    counts = jnp.sum(cnt, axis=0)
    padded = ((counts + blk - 1) // blk) * blk
    ends_pad = jnp.cumsum(padded)
    starts_pad = ends_pad - padded
    tile_before = jnp.cumsum(cnt, axis=0) - cnt
    lstart = jnp.cumsum(cnt, axis=1) - cnt
    base = starts_pad[None, :] + tile_before
    with_filler = lambda a, col: jnp.concatenate([a, col[:, None]], axis=1).reshape(-1)
    blk_start = jnp.arange((n_buf + SPARE_ROWS) // blk, dtype=jnp.int32) * blk
    last_of_expert = jnp.any((padded > 0)[None, :]
                             & (blk_start[:, None] == (ends_pad - blk)[None, :]), axis=1)
    zflag = jnp.where(last_of_expert | (blk_start >= n_buf), ZERO_FIRST,
                      jnp.where(blk_start >= ends_pad[-1], ZERO_ANYTIME, 0)).astype(jnp.int32)
    filler_dst = n_buf + (jnp.arange(cnt.shape[0], dtype=jnp.int32) % DISPATCH_SLOTS) * FILLER_ROWS
    ids = jnp.arange(N_EXPERTS, dtype=jnp.int32)
    later = jnp.where((padded > 0)[None, :] & (ids[None, :] > ids[:, None]), ids[None, :], N_EXPERTS)
    next_expert = jnp.min(later, axis=1)
    next_expert = jnp.where(next_expert == N_EXPERTS, ids, next_expert).astype(jnp.int32)
    return dict(
        cnt=with_filler(cnt, nloc - seg_rows),
        lstart=with_filler(lstart, seg_rows),
        base_out=with_filler(base, filler_dst),
        base_in=with_filler(base, jnp.zeros_like(seg_rows)),
        lstart_rows=lstart.astype(F32)[:, None, :],
        lstart_cols=lstart.astype(F32)[:, :, None],
        zflag=zflag, ends_pad=ends_pad, next_expert=next_expert)


def _layer(x2, p2, seq_len, g_mix, w_in, w_pool, pool_scale, w_ret_o, w_out, g_ffn,
           w_router, b_router, w_gate_up, b_gate_up, w_down, b_down, g_ple,
           w_ple_gate, w_ple, g_out):
    t = x2.shape[0]
    cfg = _tiles(seq_len)
    row = lambda a: a.reshape(1, -1)

    wr_hi = w_router.astype(BF16)
    wr_lo = (w_router - wr_hi.astype(F32)).astype(BF16)
    wr_split = jnp.pad(jnp.concatenate([wr_hi, wr_lo], axis=1),
                       ((0, 0), (0, LANES - 2 * N_EXPERTS)))
    lane_pad = ((0, 0), (0, LANES - N_EXPERTS))
    tm = cfg["route_rows"]
    x1, h2, meta, tile_counts = _token_mixer(
        x2.reshape(t // seq_len, seq_len, D_MODEL), row(g_mix), w_in.astype(BF16),
        w_pool.astype(BF16), row(pool_scale), w_ret_o.astype(BF16), w_out.astype(BF16),
        row(g_ffn), wr_split, jnp.pad(row(b_router), lane_pad), tm, cfg["ret_block"])
    x1 = x1.reshape(t, D_MODEL)
    h2 = h2.reshape(t, D_MODEL)

    blk = cfg["expert_rows"]
    n_buf = t * TOP_K + (t // tm) * FILLER_ROWS + N_EXPERTS * blk
    n_blocks = n_buf // blk
    tables = _slot_tables(tile_counts, blk, n_buf, _local_rows(tm))
    blk_start = jnp.arange(n_blocks, dtype=jnp.int32) * blk
    block_e = jnp.minimum(
        jnp.sum((tables["ends_pad"][None, :] <= blk_start[:, None]).astype(jnp.int32), axis=1),
        N_EXPERTS - 1)
    n_used = tables["ends_pad"][-1:] // blk

    xs = _dispatch(h2, meta, tables, n_buf, blk, tm)
    ys = _experts(xs, n_buf, block_e, n_used, tables["next_expert"], w_gate_up, b_gate_up,
                  w_down, b_down, blk)
    return _combine(meta, tables, x1, p2, ys, row(g_ple), w_ple_gate.astype(BF16),
                    w_ple.astype(BF16), row(g_out), tm)


def kernel(x, p, g_mix, w_in, w_pool, pool_scale, w_ret_o, w_out, g_ffn, w_router,
           b_router, w_gate_up, b_gate_up, w_down, b_down, g_ple, w_ple_gate, w_ple,
           g_final):
    b, s, d = x.shape
    depth = p.shape[0]
    assert depth == 1 and d == D_MODEL
    x2 = x.reshape(b * s, d)
    out = _layer(x2, p[0].reshape(b * s, PLE_DIM), s, g_mix[0], w_in[0], w_pool[0],
                 pool_scale[0], w_ret_o[0], w_out[0], g_ffn[0], w_router[0],
                 b_router[0], w_gate_up[0], b_gate_up[0], w_down[0], b_down[0],
                 g_ple[0], w_ple_gate[0], w_ple[0], g_final)
    return out.reshape(b, s, d)
```

```python
import functools

import numpy as np
import jax
import jax.numpy as jnp
from jax import lax
from jax.experimental import pallas as pl
from jax.experimental.pallas import tpu as pltpu

F32 = jnp.float32
BF16 = jnp.bfloat16

D_MODEL = 1024
EPS = 1e-6
CHUNK = 64
PLE_DIM = 256
POOL_WINDOWS = (2, 4, 8, 16)
POOL_GROUP_W = D_MODEL // len(POOL_WINDOWS)
POOL_HALO = 16
RET_HEADS = 4
RET_QK_HEAD = 256
RET_V_HEAD = 512
RET_V_W = RET_HEADS * RET_V_HEAD
ROPE_BASE = 10000.0
IN_W = 9 * D_MODEL
N_EXPERTS = 32
TOP_K = 4
D_FF = D_MODEL
SWIGLU_ALPHA = 1.702
SWIGLU_LIMIT = 7.0

VMEM_LIMIT_BYTES = 56 * 1024 * 1024
LANES = 128
META_ROWS = 16
META_EXPERT, META_RANK, META_GATE = 0, TOP_K, 2 * TOP_K
ROW_DTYPE = F32
SUBLANES = 8
STRIP_ALIGN = SUBLANES * 4 // jnp.dtype(ROW_DTYPE).itemsize
FILLER_ROWS = N_EXPERTS * STRIP_ALIGN
N_SEG = N_EXPERTS + 1
ZERO_FIRST, ZERO_ANYTIME = 1, 2
DISPATCH_SLOTS = 3
SPARE_ROWS = 4 * FILLER_ROWS


def _rms(x, g):
    return x * lax.rsqrt(jnp.mean(x * x, axis=-1, keepdims=True) + EPS) * g


COL_U, COL_Q, COL_K = 0, 1, 2
COL_V, COL_G = (3, 4), (5, 6)
COL_GATE_A, COL_GATE_B = 7, 8
SIDE_U, SIDE_GATE_A, SIDE_GATE_B = 0, 1, 2


def _rotary_heads(y, cos, sin):
    half = RET_QK_HEAD // 2
    parts = []
    for h in range(RET_HEADS):
        x1 = y[:, h * RET_QK_HEAD:h * RET_QK_HEAD + half]
        x2 = y[:, h * RET_QK_HEAD + half:(h + 1) * RET_QK_HEAD]
        parts += [x1 * cos - x2 * sin, x2 * cos + x1 * sin]
    return jnp.concatenate(parts, axis=-1)


def _head_decay_logs():
    return [float(np.log(1.0 - 2.0 ** (-5.0 - h))) for h in range(RET_HEADS)]


def _token_mixer_kernel(blk_decay, ret_blk, x_ref, g_ref, w_ref, cos_ref, sin_ref,
                        d_ref, qd_ref, kd_ref,
                        wpool_ref, pscale_ref, wreto_ref, wout_ref, gffn_ref,
                        wr_ref, br_ref,
                        x1_ref, h2_ref, meta_ref, counts_ref,
                        side_ref, o_ref, q_s, k_s, v_s, g_s, state_ref, halo_ref):
    @pl.when(pl.program_id(1) == 0)
    def _():
        state_ref[...] = jnp.zeros_like(state_ref)
        halo_ref[...] = jnp.zeros_like(halo_ref)

    tm = x_ref.shape[0]
    h_in = _rms(x_ref[...], g_ref[...]).astype(BF16)

    def proj(c):
        return jnp.dot(h_in, w_ref[:, c * D_MODEL:(c + 1) * D_MODEL],
                       preferred_element_type=F32)

    def chunk(n):
        return slice(n * D_MODEL, (n + 1) * D_MODEL)

    side_ref[:, chunk(SIDE_U)] = proj(COL_U).astype(BF16)
    q_s[...] = _rotary_heads(proj(COL_Q), cos_ref[...], sin_ref[...]).astype(BF16)
    k_s[...] = (_rotary_heads(proj(COL_K), cos_ref[...], sin_ref[...])
                * (RET_QK_HEAD ** -0.5)).astype(BF16)
    for n, c in enumerate(COL_V):
        v_s[:, chunk(n)] = proj(c).astype(BF16)
    for n, c in enumerate(COL_G):
        y = proj(c)
        g_s[:, chunk(n)] = (y * jax.nn.sigmoid(y)).astype(BF16)
    side_ref[:, chunk(SIDE_GATE_A)] = proj(COL_GATE_A).astype(BF16)
    side_ref[:, chunk(SIDE_GATE_B)] = proj(COL_GATE_B).astype(BF16)

    for b in range(tm // ret_blk):
        rows = slice(b * ret_blk, (b + 1) * ret_blk)
        for h in range(RET_HEADS):
            qk_cols = slice(h * RET_QK_HEAD, (h + 1) * RET_QK_HEAD)
            v_cols = slice(h * RET_V_HEAD, (h + 1) * RET_V_HEAD)
            q = q_s[rows, qk_cols]
            k = k_s[rows, qk_cols]
            v = v_s[rows, v_cols]
            scores = lax.dot_general(q, k, (((1,), (1,)), ((), ())),
                                     preferred_element_type=F32) * d_ref[h]
            state = state_ref[h]
            o = jnp.dot(scores.astype(BF16), v, preferred_element_type=F32)
            o = o + jnp.dot((q.astype(F32) * qd_ref[h]).astype(BF16), state.astype(BF16),
                            preferred_element_type=F32)
            k_dec = (k.astype(F32) * kd_ref[h]).astype(BF16)
            state_ref[h] = state * blk_decay[h] + lax.dot_general(
                k_dec, v, (((0,), (0,)), ((), ())), preferred_element_type=F32)
            o = o * lax.rsqrt(jnp.mean(o * o, axis=-1, keepdims=True) + EPS)
            o_ref[rows, v_cols] = (o * g_s[rows, v_cols].astype(F32)).astype(BF16)

    u_bf = side_ref[:, chunk(SIDE_U)]
    _mix_route(pl.program_id(1) * tm, u_bf, halo_ref[...], side_ref[:, chunk(SIDE_GATE_A)],
               side_ref[:, chunk(SIDE_GATE_B)], o_ref[...], x_ref[...],
               wpool_ref, pscale_ref, wreto_ref, wout_ref, gffn_ref,
               wr_ref, br_ref, x1_ref, h2_ref, meta_ref, counts_ref)
    halo_ref[...] = u_bf[tm - POOL_HALO:, :]


def _rotary_tables(s):
    half = RET_QK_HEAD // 2
    pos = jnp.arange(s, dtype=F32)
    inv = ROPE_BASE ** (-jnp.linspace(0.0, 1.0, half, dtype=F32))
    ang = pos[:, None] * inv[None, :]
    return jnp.cos(ang), jnp.sin(ang)


def _retention_tables(blk):
    log_g = jnp.asarray(_head_decay_logs(), F32)
    idx = jnp.arange(blk, dtype=F32)
    diff = idx[:, None] - idx[None, :]
    chunk = jnp.arange(blk, dtype=jnp.int32) // CHUNK
    visible = chunk[None, :] <= chunk[:, None]
    dmask = jnp.where(visible[None], jnp.exp(log_g[:, None, None] * jnp.abs(diff)[None]), 0.0)
    q_dec = jnp.exp(log_g[:, None] * (idx + 1.0))[:, :, None]
    k_dec = jnp.exp(log_g[:, None] * (blk - 1.0 - idx))[:, :, None]
    q_dec = jnp.broadcast_to(q_dec, (RET_HEADS, blk, RET_QK_HEAD))
    k_dec = jnp.broadcast_to(k_dec, (RET_HEADS, blk, RET_QK_HEAD))
    return dmask.astype(F32), q_dec, k_dec


def _token_mixer(x3, g_mix, w_in_bf, w_pool_bf, pool_scale, w_ret_o_bf, w_out_bf,
                 g_ffn, wr_split, b_router, tm, ret_blk):
    b, s, _ = x3.shape
    half = RET_QK_HEAD // 2
    tiles_per_seq = s // tm
    n_tiles = b * tiles_per_seq
    cos, sin = _rotary_tables(s)
    dmask, q_dec, k_dec = _retention_tables(ret_blk)
    blk_decay = [float(np.exp(lg * ret_blk)) for lg in _head_decay_logs()]
    once = lambda *shape: pl.BlockSpec(shape, lambda bi, j: (0,) * len(shape),
                                       pipeline_mode=pl.Buffered(1))
    tile = lambda bi, j: bi * tiles_per_seq + j
    return pl.pallas_call(
        functools.partial(_token_mixer_kernel, blk_decay, ret_blk),
        out_shape=(
            jax.ShapeDtypeStruct((b, s, D_MODEL), F32),
            jax.ShapeDtypeStruct((b, s, D_MODEL), BF16),
            jax.ShapeDtypeStruct((n_tiles, META_ROWS, tm), F32),
            jax.ShapeDtypeStruct((n_tiles, N_EXPERTS, 1), F32),
        ),
        grid=(b, tiles_per_seq),
        in_specs=[
            pl.BlockSpec((None, tm, D_MODEL), lambda bi, j: (bi, j, 0)),
            pl.BlockSpec((1, D_MODEL), lambda bi, j: (0, 0)),
            once(D_MODEL, IN_W),
            pl.BlockSpec((tm, half), lambda bi, j: (j, 0)),
            pl.BlockSpec((tm, half), lambda bi, j: (j, 0)),
            once(RET_HEADS, ret_blk, ret_blk),
            once(RET_HEADS, ret_blk, RET_QK_HEAD),
            once(RET_HEADS, ret_blk, RET_QK_HEAD),
            once(len(POOL_WINDOWS), POOL_GROUP_W, POOL_GROUP_W),
            once(1, D_MODEL),
            once(RET_V_W, D_MODEL),
            once(D_MODEL, D_MODEL),
            once(1, D_MODEL),
            once(D_MODEL, LANES),
            once(1, LANES),
        ],
        out_specs=(
            pl.BlockSpec((None, tm, D_MODEL), lambda bi, j: (bi, j, 0)),
            pl.BlockSpec((None, tm, D_MODEL), lambda bi, j: (bi, j, 0)),
            pl.BlockSpec((None, META_ROWS, tm), lambda bi, j: (tile(bi, j), 0, 0)),
            pl.BlockSpec((None, N_EXPERTS, 1), lambda bi, j: (tile(bi, j), 0, 0)),
        ),
        scratch_shapes=[pltpu.VMEM((tm, 3 * D_MODEL), BF16),
                        pltpu.VMEM((tm, RET_V_W), BF16),
                        pltpu.VMEM((tm, D_MODEL), BF16),
                        pltpu.VMEM((tm, D_MODEL), BF16),
                        pltpu.VMEM((tm, RET_V_W), BF16),
                        pltpu.VMEM((tm, RET_V_W), BF16),
                        pltpu.VMEM((RET_HEADS, RET_QK_HEAD, RET_V_HEAD), F32),
                        pltpu.VMEM((POOL_HALO, D_MODEL), BF16)],
        compiler_params=pltpu.CompilerParams(
            dimension_semantics=("arbitrary", "arbitrary"),
            vmem_limit_bytes=VMEM_LIMIT_BYTES),
        name="token_mixer",
    )(x3, g_mix, w_in_bf, cos, sin, dmask, q_dec, k_dec, w_pool_bf, pool_scale,
      w_ret_o_bf, w_out_bf, g_ffn, wr_split, b_router)


def _window_sum(ext, w, tm):
    cur = ext
    span = 1
    while span < w:
        cur = cur[span:, :] + cur[:-span, :]
        span *= 2
    start = POOL_HALO + 1 - w
    return cur[start:start + tm, :]


def _mix_route(pos0, u_bf, halo_bf, gate_a, gate_b, o_gated, x,
               wpool_ref, pscale_ref, wreto_ref, wout_ref, gffn_ref,
               wr_ref, br_ref,
               x1_ref, h2_ref, meta_ref, counts_ref):
    tm = x.shape[0]

    u = u_bf.astype(F32)
    ext = jnp.concatenate([halo_bf.astype(F32), u], axis=0)
    pos = (pos0 + lax.broadcasted_iota(jnp.int32, (tm, 1), 0)).astype(F32)
    pooled_out = []
    for g, w in enumerate(POOL_WINDOWS):
        cols = slice(g * POOL_GROUP_W, (g + 1) * POOL_GROUP_W)
        ws = _window_sum(ext[:, cols], w, tm)
        count = jnp.minimum(pos + 1.0, float(w))
        pooled = ws / count - u[:, cols]
        pooled_out.append(jnp.dot(pooled.astype(BF16), wpool_ref[g],
                                  preferred_element_type=F32))
    y_pool = jnp.concatenate(pooled_out, axis=-1) * pscale_ref[...]

    y_ret = jnp.dot(o_gated, wreto_ref[...], preferred_element_type=F32)
    merged = (jax.nn.sigmoid(gate_a.astype(F32)) * y_pool
              + jax.nn.sigmoid(gate_b.astype(F32)) * y_ret)
    x1 = x + jnp.dot(merged.astype(BF16), wout_ref[...], preferred_element_type=F32)
    x1_ref[...] = x1
    h2 = _rms(x1, gffn_ref[...])
    h2_ref[...] = h2.astype(BF16)

    h_hi = h2.astype(BF16)
    h_lo = (h2 - h_hi.astype(F32)).astype(BF16)
    parts = (jnp.dot(h_hi, wr_ref[...], preferred_element_type=F32)
             + jnp.dot(h_lo, wr_ref[...], preferred_element_type=F32)
             + br_ref[...])
    parts_t = parts.T
    logits_t = parts_t[:N_EXPERTS, :] + parts_t[N_EXPERTS:2 * N_EXPERTS, :]

    sub = lax.broadcasted_iota(jnp.int32, (N_EXPERTS, tm), 0)
    work = logits_t
    vals, idxs, hots = [], [], []
    for _ in range(TOP_K):
        m = jnp.max(work, axis=0, keepdims=True)
        idx = jnp.min(jnp.where(work == m, sub, N_EXPERTS), axis=0, keepdims=True)
        hot = sub == idx
        vals.append(m)
        idxs.append(idx)
        hots.append(hot)
        work = jnp.where(hot, -jnp.inf, work)
    exps = [jnp.exp(v - vals[0]) for v in vals]
    denom = exps[0] + exps[1] + exps[2] + exps[3]
    gates = [e / denom for e in exps]

    sel = (jnp.where(hots[0], 1.0, 0.0) + jnp.where(hots[1], 1.0, 0.0)
           + jnp.where(hots[2], 1.0, 0.0) + jnp.where(hots[3], 1.0, 0.0))
    row = lax.broadcasted_iota(jnp.int32, (tm, tm), 0)
    col = lax.broadcasted_iota(jnp.int32, (tm, tm), 1)
    earlier = jnp.where(row < col, 1.0, 0.0).astype(BF16)
    before = jnp.dot(sel.astype(BF16), earlier, preferred_element_type=F32)
    counts_ref[...] = jnp.sum(sel, axis=1, keepdims=True)

    msub = lax.broadcasted_iota(jnp.int32, (META_ROWS, tm), 0)
    meta = jnp.zeros((META_ROWS, tm), F32)
    for r in range(TOP_K):
        rank_r = jnp.sum(jnp.where(hots[r], before, 0.0), axis=0, keepdims=True)
        meta = jnp.where(msub == META_EXPERT + r, idxs[r].astype(F32), meta)
        meta = jnp.where(msub == META_RANK + r, rank_r, meta)
        meta = jnp.where(msub == META_GATE + r, gates[r], meta)
    meta_ref[...] = meta


def _local_positions(meta, lstart):
    tm = meta.shape[0]
    lane_e = lax.broadcasted_iota(jnp.int32, (tm, N_EXPERTS), 1)
    out = []
    for r in range(TOP_K):
        e_r = meta[:, META_EXPERT + r:META_EXPERT + r + 1].astype(jnp.int32)
        first = jnp.sum(jnp.where(lane_e == e_r, lstart, 0.0), axis=-1, keepdims=True)
        out.append((meta[:, META_RANK + r:META_RANK + r + 1] + first).astype(jnp.int32))
    return out


def _local_rows(tm):
    return TOP_K * tm + FILLER_ROWS


def _for_each_strip(count, max_rows, start_copy):
    top = 1 << (max(max_rows, FILLER_ROWS).bit_length() - 1)
    pieces = [top >> k for k in range(top.bit_length()) if (top >> k) >= STRIP_ALIGN]
    rare_from = 2 * max_rows * TOP_K // N_EXPERTS
    rare = [p for p in pieces if p >= rare_from]
    rare_bits = sum(rare)

    def cover(sizes, off):
        for piece in sizes:
            take = count & piece

            @pl.when(take != 0)
            def _(off=off, piece=piece):
                start_copy(off, piece)

            off = off + take

    @pl.when((count & rare_bits) != 0)
    def _():
        cover(rare, jnp.int32(0))

    cover([p for p in pieces if p < rare_from], count & rare_bits)


def _dispatch_kernel(blk_rows, base_ref, lstart_ref, cnt_ref, zflag_ref,
                     h_ref, meta_ref, lstart_col_ref, xs_ref,
                     sorted_ref, zero_ref, sems, zsem, tail_sem):
    i = pl.program_id(0)
    tm = h_ref.shape[0]
    nrows = _local_rows(tm)
    slot = lax.rem(i, DISPATCH_SLOTS)

    n_zero_blocks = xs_ref.shape[0] // blk_rows

    def zcopy(b, sem):
        start = pl.multiple_of(b * blk_rows, blk_rows)
        return pltpu.make_async_copy(zero_ref, xs_ref.at[pl.ds(start, blk_rows), :], sem)

    def for_flagged(flag, sem, act):
        def body(b, c):
            @pl.when(zflag_ref[b] == flag)
            def _():
                act(zcopy(b, sem))
            return c
        lax.fori_loop(0, n_zero_blocks, body, 0)

    @pl.when(i == 0)
    def _():
        zero_ref[...] = jnp.zeros_like(zero_ref)
        for_flagged(ZERO_FIRST, zsem, lambda cp: cp.start())
        for_flagged(ZERO_ANYTIME, tail_sem, lambda cp: cp.start())
        for_flagged(ZERO_FIRST, zsem, lambda cp: cp.wait())

    meta_t = meta_ref[...]
    sub_e = lax.broadcasted_iota(jnp.int32, (N_EXPERTS, tm), 0)
    pos = []
    for r in range(TOP_K):
        e_r = meta_t[META_EXPERT + r:META_EXPERT + r + 1, :].astype(jnp.int32)
        first = jnp.sum(jnp.where(sub_e == e_r, lstart_col_ref[...], 0.0), axis=0, keepdims=True)
        pos.append((meta_t[META_RANK + r:META_RANK + r + 1, :] + first).astype(jnp.int16))
    jr = lax.broadcasted_iota(jnp.int16, (nrows, tm), 0)
    one, zero = jnp.ones((), BF16), jnp.zeros((), BF16)
    onehot = jnp.where(jr == pos[0], one, jnp.where(jr == pos[1], one,
             jnp.where(jr == pos[2], one, jnp.where(jr == pos[3], one, zero))))
    srt = jnp.dot(onehot, h_ref[...], preferred_element_type=F32)
    sorted_ref[slot] = srt.astype(ROW_DTYPE)

    def per_segment(e, c):
        k = i * N_SEG + e
        src0 = lstart_ref[k]
        dst0 = base_ref[k]

        def start_copy(off, rows):
            src = pl.multiple_of(src0 + off, STRIP_ALIGN)
            dst = pl.multiple_of(dst0 + off, STRIP_ALIGN)
            pltpu.make_async_copy(sorted_ref.at[slot, pl.ds(src, rows), :],
                                  xs_ref.at[pl.ds(dst, rows), :],
                                  sems.at[slot]).start()

        _for_each_strip(cnt_ref[k], tm, start_copy)
        return c

    lax.fori_loop(0, N_SEG, per_segment, 0)

    def wait_tile(s):
        pltpu.make_async_copy(sorted_ref.at[s], xs_ref.at[pl.ds(0, nrows), :],
                              sems.at[s]).wait()

    @pl.when(i >= DISPATCH_SLOTS - 1)
    def _():
        wait_tile(lax.rem(i + 1, DISPATCH_SLOTS))

    @pl.when(i == pl.num_programs(0) - 1)
    def _():
        for age in range(DISPATCH_SLOTS - 2, -1, -1):
            @pl.when(i >= age)
            def _(age=age):
                wait_tile(lax.rem(i - age, DISPATCH_SLOTS))
        for_flagged(ZERO_ANYTIME, tail_sem, lambda cp: cp.wait())


def _dispatch(h2, meta, tables, n_buf, blk_rows, tm):
    t = h2.shape[0]
    grid_spec = pltpu.PrefetchScalarGridSpec(
        num_scalar_prefetch=4,
        grid=(t // tm,),
        in_specs=[
            pl.BlockSpec((tm, D_MODEL), lambda i, *_: (i, 0)),
            pl.BlockSpec((None, META_ROWS, tm), lambda i, *_: (i, 0, 0)),
            pl.BlockSpec((None, N_EXPERTS, 1), lambda i, *_: (i, 0, 0)),
        ],
        out_specs=pl.BlockSpec(memory_space=pl.ANY),
        scratch_shapes=[pltpu.VMEM((DISPATCH_SLOTS, _local_rows(tm), D_MODEL), ROW_DTYPE),
                        pltpu.VMEM((blk_rows, D_MODEL), ROW_DTYPE),
                        pltpu.SemaphoreType.DMA((DISPATCH_SLOTS,)),
                        pltpu.SemaphoreType.DMA(()),
                        pltpu.SemaphoreType.DMA(())],
    )
    return pl.pallas_call(
        functools.partial(_dispatch_kernel, blk_rows),
        out_shape=jax.ShapeDtypeStruct((n_buf + SPARE_ROWS, D_MODEL), ROW_DTYPE),
        grid_spec=grid_spec,
        compiler_params=pltpu.CompilerParams(
            dimension_semantics=("arbitrary",),
            vmem_limit_bytes=VMEM_LIMIT_BYTES,
            has_side_effects=True),
        name="dispatch",
    )(tables["base_out"], tables["lstart"], tables["cnt"], tables["zflag"],
      h2, meta, tables["lstart_cols"])


def _experts_kernel(be_ref, nu_ref, next_ref, x_ref, wgu_hbm, bgu_ref, wd_hbm, bd_ref,
                    y_ref, wgu_stage, wd_stage, wgu_bf, wd_bf, sems):
    i = pl.program_id(0)

    def weight_copies(e):
        return (pltpu.make_async_copy(wgu_hbm.at[e], wgu_stage, sems.at[0]),
                pltpu.make_async_copy(wd_hbm.at[e], wd_stage, sems.at[1]))

    @pl.when(i < nu_ref[0])
    def _():
        e = be_ref[i]
        prev = be_ref[jnp.maximum(i - 1, 0)]

        @pl.when(i == 0)
        def _():
            for cp in weight_copies(e):
                cp.start()

        @pl.when((i == 0) | (e != prev))
        def _():
            for cp in weight_copies(e):
                cp.wait()
            wgu_bf[...] = wgu_stage[...].astype(BF16)
            wd_bf[...] = wd_stage[...].astype(BF16)
            nxt = next_ref[e]

            @pl.when(nxt != e)
            def _():
                for cp in weight_copies(nxt):
                    cp.start()

        gu = jnp.dot(x_ref[...].astype(BF16), wgu_bf[...],
                     preferred_element_type=F32) + bgu_ref[...]
        glu = jnp.minimum(gu[:, :D_FF], SWIGLU_LIMIT)
        lin = jnp.clip(gu[:, D_FF:], -SWIGLU_LIMIT, SWIGLU_LIMIT)
        act = glu * jax.nn.sigmoid(SWIGLU_ALPHA * glu) * (lin + 1.0)
        y_ref[...] = (jnp.dot(act.astype(BF16), wd_bf[...],
                              preferred_element_type=F32) + bd_ref[...]).astype(ROW_DTYPE)


def _experts(xs, n_buf, block_e, n_used, next_expert, w_gate_up, b_gate_up, w_down, b_down,
             blk_rows):
    used = lambda i, be, nu, nx: jnp.minimum(i, nu[0] - 1)
    grid_spec = pltpu.PrefetchScalarGridSpec(
        num_scalar_prefetch=3,
        grid=(n_buf // blk_rows,),
        in_specs=[
            pl.BlockSpec((blk_rows, D_MODEL), lambda i, be, nu, nx: (used(i, be, nu, nx), 0)),
            pl.BlockSpec(memory_space=pl.ANY),
            pl.BlockSpec((None, 1, 2 * D_FF), lambda i, be, nu, nx: (be[i], 0, 0)),
            pl.BlockSpec(memory_space=pl.ANY),
            pl.BlockSpec((None, 1, D_MODEL), lambda i, be, nu, nx: (be[i], 0, 0)),
        ],
        out_specs=pl.BlockSpec((blk_rows, D_MODEL),
                               lambda i, be, nu, nx: (used(i, be, nu, nx), 0)),
        scratch_shapes=[pltpu.VMEM((D_MODEL, 2 * D_FF), F32),
                        pltpu.VMEM((D_FF, D_MODEL), F32),
                        pltpu.VMEM((D_MODEL, 2 * D_FF), BF16),
                        pltpu.VMEM((D_FF, D_MODEL), BF16),
                        pltpu.SemaphoreType.DMA((2,))],
    )
    return pl.pallas_call(
        _experts_kernel,
        out_shape=jax.ShapeDtypeStruct(xs.shape, xs.dtype),
        input_output_aliases={3: 0},
        grid_spec=grid_spec,
        compiler_params=pltpu.CompilerParams(
            dimension_semantics=("arbitrary",),
            vmem_limit_bytes=VMEM_LIMIT_BYTES),
        name="experts",
    )(block_e, n_used, next_expert, xs, w_gate_up, b_gate_up[:, None, :], w_down,
      b_down[:, None, :])


def _unsort_weights(meta_t, lstart_row, nrows):
    tm = meta_t.shape[1]
    meta = jnp.concatenate(
        [meta_t, jnp.zeros((LANES - META_ROWS, tm), F32)], axis=0).T
    pos = _local_positions(meta, lstart_row)
    jl = lax.broadcasted_iota(jnp.int16, (tm, nrows), 1)
    pos = [p.astype(jnp.int16) for p in pos]
    gate = [meta[:, META_GATE + r:META_GATE + r + 1].astype(BF16) for r in range(TOP_K)]
    zero = jnp.zeros((), BF16)
    return jnp.where(jl == pos[0], gate[0], jnp.where(jl == pos[1], gate[1],
           jnp.where(jl == pos[2], gate[2], jnp.where(jl == pos[3], gate[3], zero))))


def _combine_kernel(base_ref, lstart_ref, cnt_ref, meta_ref, lstart_row_ref,
                    meta_next_ref, lstart_row_next_ref, x1_ref,
                    p_ref, ys_ref, gple_ref, wpg_ref, wple_ref, gfin_ref, out_ref,
                    rows_ref, unsort_ref, sems):
    i = pl.program_id(0)
    n = pl.num_programs(0)
    tm = x1_ref.shape[0]
    nrows = _local_rows(tm)
    slot = lax.rem(i, 2)

    def fetch_tile(tile, s):
        def per_segment(e, c):
            k = tile * N_SEG + e
            src0 = base_ref[k]
            dst0 = lstart_ref[k]

            def start_copy(off, rows):
                src = pl.multiple_of(src0 + off, STRIP_ALIGN)
                dst = pl.multiple_of(dst0 + off, STRIP_ALIGN)
                pltpu.make_async_copy(ys_ref.at[pl.ds(src, rows), :],
                                      rows_ref.at[s, pl.ds(dst, rows), :],
                                      sems.at[s]).start()

            _for_each_strip(cnt_ref[k], tm, start_copy)
            return c

        lax.fori_loop(0, N_SEG, per_segment, 0)

    @pl.when(i == 0)
    def _():
        fetch_tile(0, 0)
        unsort_ref[0] = _unsort_weights(meta_ref[...], lstart_row_ref[...], nrows)

    @pl.when(i + 1 < n)
    def _():
        fetch_tile(i + 1, 1 - slot)

    pltpu.make_async_copy(ys_ref.at[pl.ds(0, nrows), :], rows_ref.at[slot],
                          sems.at[slot]).wait()
    moe = jnp.dot(unsort_ref[slot], rows_ref[slot].astype(BF16),
                  preferred_element_type=F32)
    x2 = x1_ref[...] + moe
    u = _rms(x2, gple_ref[...])
    gate_ple = jax.nn.sigmoid(jnp.dot(u.astype(BF16), wpg_ref[...],
                                      preferred_element_type=F32))
    emb = jnp.dot(p_ref[...].astype(BF16), wple_ref[...], preferred_element_type=F32)
    x3 = x2 + gate_ple * emb
    out_ref[...] = _rms(x3, gfin_ref[...])

    unsort_ref[1 - slot] = _unsort_weights(meta_next_ref[...], lstart_row_next_ref[...], nrows)


def _combine(meta, tables, x1, p2, ys, g_ple, w_ple_gate_bf, w_ple_bf, g_final, tm):
    t = x1.shape[0]
    full = lambda *shape: pl.BlockSpec(shape, lambda i, *_: (0,) * len(shape))
    last = t // tm - 1
    nxt = lambda i: jnp.minimum(i + 1, last)
    grid_spec = pltpu.PrefetchScalarGridSpec(
        num_scalar_prefetch=3,
        grid=(t // tm,),
        in_specs=[
            pl.BlockSpec((None, META_ROWS, tm), lambda i, *_: (i, 0, 0)),
            pl.BlockSpec((None, 1, N_EXPERTS), lambda i, *_: (i, 0, 0)),
            pl.BlockSpec((None, META_ROWS, tm), lambda i, *_: (nxt(i), 0, 0)),
            pl.BlockSpec((None, 1, N_EXPERTS), lambda i, *_: (nxt(i), 0, 0)),
            pl.BlockSpec((tm, D_MODEL), lambda i, *_: (i, 0)),
            pl.BlockSpec((tm, PLE_DIM), lambda i, *_: (i, 0)),
            pl.BlockSpec(memory_space=pl.ANY),
            full(1, D_MODEL),
            full(D_MODEL, D_MODEL),
            full(PLE_DIM, D_MODEL),
            full(1, D_MODEL),
        ],
        out_specs=pl.BlockSpec((tm, D_MODEL), lambda i, *_: (i, 0)),
        scratch_shapes=[pltpu.VMEM((2, _local_rows(tm), D_MODEL), ROW_DTYPE),
                        pltpu.VMEM((2, tm, _local_rows(tm)), BF16),
                        pltpu.SemaphoreType.DMA((2,))],
    )
    return pl.pallas_call(
        _combine_kernel,
        out_shape=jax.ShapeDtypeStruct((t, D_MODEL), F32),
        grid_spec=grid_spec,
        compiler_params=pltpu.CompilerParams(
            dimension_semantics=("arbitrary",),
            vmem_limit_bytes=VMEM_LIMIT_BYTES),
        name="combine",
    )(tables["base_in"], tables["lstart"], tables["cnt"], meta, tables["lstart_rows"],
      meta, tables["lstart_rows"], x1, p2, ys, g_ple, w_ple_gate_bf, w_ple_bf, g_final)


def _tiles(seq_len):
    return dict(
        ret_block=min(256, seq_len),
        route_rows=min(512, seq_len),
        expert_rows=512,
    )


def _slot_tables(tile_counts, blk, n_buf, nloc):
    cnt = tile_counts[:, :, 0].astype(jnp.int32)
    cnt = ((cnt + STRIP_ALIGN - 1) // STRIP_ALIGN) * STRIP_ALIGN
    seg_rows = jnp.sum(cnt, axis=1)
    counts = jnp.sum(cnt, axis=0)
    padded = ((counts + blk - 1) // blk) * blk
    ends_pad = jnp.cumsum(padded)
    starts_pad = ends_pad - padded
    tile_before = jnp.cumsum(cnt, axis=0) - cnt
    lstart = jnp.cumsum(cnt, axis=1) - cnt
    base = starts_pad[None, :] + tile_before
    with_filler = lambda a, col: jnp.concatenate([a, col[:, None]], axis=1).reshape(-1)
    blk_start = jnp.arange((n_buf + SPARE_ROWS) // blk, dtype=jnp.int32) * blk
    last_of_expert = jnp.any((padded > 0)[None, :]
                             & (blk_start[:, None] == (ends_pad - blk)[None, :]), axis=1)
    zflag = jnp.where(last_of_expert | (blk_start >= n_buf), ZERO_FIRST,
                      jnp.where(blk_start >= ends_pad[-1], ZERO_ANYTIME, 0)).astype(jnp.int32)
    filler_dst = n_buf + (jnp.arange(cnt.shape[0], dtype=jnp.int32) % DISPATCH_SLOTS) * FILLER_ROWS
    ids = jnp.arange(N_EXPERTS, dtype=jnp.int32)
    later = jnp.where((padded > 0)[None, :] & (ids[None, :] > ids[:, None]), ids[None, :], N_EXPERTS)
    next_expert = jnp.min(later, axis=1)
    next_expert = jnp.where(next_expert == N_EXPERTS, ids, next_expert).astype(jnp.int32)
    return dict(
        cnt=with_filler(cnt, nloc - seg_rows),
        lstart=with_filler(lstart, seg_rows),
        base_out=with_filler(base, filler_dst),
        base_in=with_filler(base, jnp.zeros_like(seg_rows)),
        lstart_rows=lstart.astype(F32)[:, None, :],
        lstart_cols=lstart.astype(F32)[:, :, None],
        zflag=zflag, ends_pad=ends_pad, next_expert=next_expert)


def _layer(x2, p2, seq_len, g_mix, w_in, w_pool, pool_scale, w_ret_o, w_out, g_ffn,
           w_router, b_router, w_gate_up, b_gate_up, w_down, b_down, g_ple,
           w_ple_gate, w_ple, g_out):
    t = x2.shape[0]
    cfg = _tiles(seq_len)
    row = lambda a: a.reshape(1, -1)

    wr_hi = w_router.astype(BF16)
    wr_lo = (w_router - wr_hi.astype(F32)).astype(BF16)
    wr_split = jnp.pad(jnp.concatenate([wr_hi, wr_lo], axis=1),
                       ((0, 0), (0, LANES - 2 * N_EXPERTS)))
    lane_pad = ((0, 0), (0, LANES - N_EXPERTS))
    tm = cfg["route_rows"]
    x1, h2, meta, tile_counts = _token_mixer(
        x2.reshape(t // seq_len, seq_len, D_MODEL), row(g_mix), w_in.astype(BF16),
        w_pool.astype(BF16), row(pool_scale), w_ret_o.astype(BF16), w_out.astype(BF16),
        row(g_ffn), wr_split, jnp.pad(row(b_router), lane_pad), tm, cfg["ret_block"])
    x1 = x1.reshape(t, D_MODEL)
    h2 = h2.reshape(t, D_MODEL)

    blk = cfg["expert_rows"]
    n_buf = t * TOP_K + (t // tm) * FILLER_ROWS + N_EXPERTS * blk
    n_blocks = n_buf // blk
    tables = _slot_tables(tile_counts, blk, n_buf, _local_rows(tm))
    blk_start = jnp.arange(n_blocks, dtype=jnp.int32) * blk
    block_e = jnp.minimum(
        jnp.sum((tables["ends_pad"][None, :] <= blk_start[:, None]).astype(jnp.int32), axis=1),
        N_EXPERTS - 1)
    n_used = tables["ends_pad"][-1:] // blk

    xs = _dispatch(h2, meta, tables, n_buf, blk, tm)
    ys = _experts(xs, n_buf, block_e, n_used, tables["next_expert"], w_gate_up, b_gate_up,
                  w_down, b_down, blk)
    return _combine(meta, tables, x1, p2, ys, row(g_ple), w_ple_gate.astype(BF16),
                    w_ple.astype(BF16), row(g_out), tm)


def kernel(x, p, g_mix, w_in, w_pool, pool_scale, w_ret_o, w_out, g_ffn, w_router,
           b_router, w_gate_up, b_gate_up, w_down, b_down, g_ple, w_ple_gate, w_ple,
           g_final):
    b, s, d = x.shape
    depth = p.shape[0]
    assert depth == 1 and d == D_MODEL
    x2 = x.reshape(b * s, d)
    out = _layer(x2, p[0].reshape(b * s, PLE_DIM), s, g_mix[0], w_in[0], w_pool[0],
                 pool_scale[0], w_ret_o[0], w_out[0], g_ffn[0], w_router[0],
                 b_router[0], w_gate_up[0], b_gate_up[0], w_down[0], b_down[0],
                 g_ple[0], w_ple_gate[0], w_ple[0], g_final)
    return out.reshape(b, s, d)
```

```python
import functools

import numpy as np
import jax
import jax.numpy as jnp
from jax import lax
from jax.experimental import pallas as pl
from jax.experimental.pallas import tpu as pltpu

F32 = jnp.float32
BF16 = jnp.bfloat16

D_MODEL = 1024
EPS = 1e-6
CHUNK = 64
PLE_DIM = 256
POOL_WINDOWS = (2, 4, 8, 16)
POOL_GROUP_W = D_MODEL // len(POOL_WINDOWS)
POOL_HALO = 16
RET_HEADS = 4
RET_QK_HEAD = 256
RET_V_HEAD = 512
RET_V_W = RET_HEADS * RET_V_HEAD
ROPE_BASE = 10000.0
IN_W = 9 * D_MODEL
N_EXPERTS = 32
TOP_K = 4
D_FF = D_MODEL
SWIGLU_ALPHA = 1.702
SWIGLU_LIMIT = 7.0

VMEM_LIMIT_BYTES = 56 * 1024 * 1024
LANES = 128
META_ROWS = 16
META_EXPERT, META_RANK, META_GATE = 0, TOP_K, 2 * TOP_K
ROW_DTYPE = F32
SUBLANES = 8
STRIP_ALIGN = SUBLANES * 4 // jnp.dtype(ROW_DTYPE).itemsize
FILLER_ROWS = N_EXPERTS * STRIP_ALIGN
N_SEG = N_EXPERTS + 1
ZERO_FIRST, ZERO_ANYTIME = 1, 2
DISPATCH_SLOTS = 3
SPARE_ROWS = 4 * FILLER_ROWS


def _rms(x, g):
    return x * lax.rsqrt(jnp.mean(x * x, axis=-1, keepdims=True) + EPS) * g


COL_U, COL_Q, COL_K = 0, 1, 2
COL_V, COL_G = (3, 4), (5, 6)
COL_GATE_A, COL_GATE_B = 7, 8
SIDE_U, SIDE_GATE_A, SIDE_GATE_B = 0, 1, 2


def _rotary_heads(y, cos, sin):
    half = RET_QK_HEAD // 2
    parts = []
    for h in range(RET_HEADS):
        x1 = y[:, h * RET_QK_HEAD:h * RET_QK_HEAD + half]
        x2 = y[:, h * RET_QK_HEAD + half:(h + 1) * RET_QK_HEAD]
        parts += [x1 * cos - x2 * sin, x2 * cos + x1 * sin]
    return jnp.concatenate(parts, axis=-1)


def _head_decay_logs():
    return [float(np.log(1.0 - 2.0 ** (-5.0 - h))) for h in range(RET_HEADS)]


def _token_mixer_kernel(blk_decay, ret_blk, x_ref, g_ref, w_ref, cos_ref, sin_ref,
                        d_ref, qd_ref, kd_ref,
                        wpool_ref, pscale_ref, wreto_ref, wout_ref, gffn_ref,
                        wr_ref, br_ref,
                        x1_ref, h2_ref, meta_ref, counts_ref,
                        side_ref, o_ref, q_s, k_s, v_s, g_s, state_ref, halo_ref):
    @pl.when(pl.program_id(1) == 0)
    def _():
        state_ref[...] = jnp.zeros_like(state_ref)
        halo_ref[...] = jnp.zeros_like(halo_ref)

    tm = x_ref.shape[0]
    h_in = _rms(x_ref[...], g_ref[...]).astype(BF16)

    def proj(c):
        return jnp.dot(h_in, w_ref[:, c * D_MODEL:(c + 1) * D_MODEL],
                       preferred_element_type=F32)

    def chunk(n):
        return slice(n * D_MODEL, (n + 1) * D_MODEL)

    side_ref[:, chunk(SIDE_U)] = proj(COL_U).astype(BF16)
    q_s[...] = _rotary_heads(proj(COL_Q), cos_ref[...], sin_ref[...]).astype(BF16)
    k_s[...] = (_rotary_heads(proj(COL_K), cos_ref[...], sin_ref[...])
                * (RET_QK_HEAD ** -0.5)).astype(BF16)
    for n, c in enumerate(COL_V):
        v_s[:, chunk(n)] = proj(c).astype(BF16)
    for n, c in enumerate(COL_G):
        y = proj(c)
        g_s[:, chunk(n)] = (y * jax.nn.sigmoid(y)).astype(BF16)
    side_ref[:, chunk(SIDE_GATE_A)] = proj(COL_GATE_A).astype(BF16)
    side_ref[:, chunk(SIDE_GATE_B)] = proj(COL_GATE_B).astype(BF16)

    for b in range(tm // ret_blk):
        rows = slice(b * ret_blk, (b + 1) * ret_blk)
        for h in range(RET_HEADS):
            qk_cols = slice(h * RET_QK_HEAD, (h + 1) * RET_QK_HEAD)
            v_cols = slice(h * RET_V_HEAD, (h + 1) * RET_V_HEAD)
            q = q_s[rows, qk_cols]
            k = k_s[rows, qk_cols]
            v = v_s[rows, v_cols]
            scores = lax.dot_general(q, k, (((1,), (1,)), ((), ())),
                                     preferred_element_type=F32) * d_ref[h]
            state = state_ref[h]
            o = jnp.dot(scores.astype(BF16), v, preferred_element_type=F32)
            o = o + jnp.dot((q.astype(F32) * qd_ref[h]).astype(BF16), state.astype(BF16),
                            preferred_element_type=F32)
            k_dec = (k.astype(F32) * kd_ref[h]).astype(BF16)
            state_ref[h] = state * blk_decay[h] + lax.dot_general(
                k_dec, v, (((0,), (0,)), ((), ())), preferred_element_type=F32)
            o = o * lax.rsqrt(jnp.mean(o * o, axis=-1, keepdims=True) + EPS)
            o_ref[rows, v_cols] = (o * g_s[rows, v_cols].astype(F32)).astype(BF16)

    u_bf = side_ref[:, chunk(SIDE_U)]
    _mix_route(pl.program_id(1) * tm, u_bf, halo_ref[...], side_ref[:, chunk(SIDE_GATE_A)],
               side_ref[:, chunk(SIDE_GATE_B)], o_ref[...], x_ref[...],
               wpool_ref, pscale_ref, wreto_ref, wout_ref, gffn_ref,
               wr_ref, br_ref, x1_ref, h2_ref, meta_ref, counts_ref)
    halo_ref[...] = u_bf[tm - POOL_HALO:, :]


def _rotary_tables(s):
    half = RET_QK_HEAD // 2
    pos = jnp.arange(s, dtype=F32)
    inv = ROPE_BASE ** (-jnp.linspace(0.0, 1.0, half, dtype=F32))
    ang = pos[:, None] * inv[None, :]
    return jnp.cos(ang), jnp.sin(ang)


def _retention_tables(blk):
    log_g = jnp.asarray(_head_decay_logs(), F32)
    idx = jnp.arange(blk, dtype=F32)
    diff = idx[:, None] - idx[None, :]
    chunk = jnp.arange(blk, dtype=jnp.int32) // CHUNK
    visible = chunk[None, :] <= chunk[:, None]
    dmask = jnp.where(visible[None], jnp.exp(log_g[:, None, None] * jnp.abs(diff)[None]), 0.0)
    q_dec = jnp.exp(log_g[:, None] * (idx + 1.0))[:, :, None]
    k_dec = jnp.exp(log_g[:, None] * (blk - 1.0 - idx))[:, :, None]
    q_dec = jnp.broadcast_to(q_dec, (RET_HEADS, blk, RET_QK_HEAD))
    k_dec = jnp.broadcast_to(k_dec, (RET_HEADS, blk, RET_QK_HEAD))
    return dmask.astype(F32), q_dec, k_dec


def _token_mixer(x3, g_mix, w_in_bf, w_pool_bf, pool_scale, w_ret_o_bf, w_out_bf,
                 g_ffn, wr_split, b_router, tm, ret_blk):
    b, s, _ = x3.shape
    half = RET_QK_HEAD // 2
    tiles_per_seq = s // tm
    n_tiles = b * tiles_per_seq
    cos, sin = _rotary_tables(s)
    dmask, q_dec, k_dec = _retention_tables(ret_blk)
    blk_decay = [float(np.exp(lg * ret_blk)) for lg in _head_decay_logs()]
    once = lambda *shape: pl.BlockSpec(shape, lambda bi, j: (0,) * len(shape),
                                       pipeline_mode=pl.Buffered(1))
    tile = lambda bi, j: bi * tiles_per_seq + j
    return pl.pallas_call(
        functools.partial(_token_mixer_kernel, blk_decay, ret_blk),
        out_shape=(
            jax.ShapeDtypeStruct((b, s, D_MODEL), F32),
            jax.ShapeDtypeStruct((b, s, D_MODEL), BF16),
            jax.ShapeDtypeStruct((n_tiles, META_ROWS, tm), F32),
            jax.ShapeDtypeStruct((n_tiles, N_EXPERTS, 1), F32),
        ),
        grid=(b, tiles_per_seq),
        in_specs=[
            pl.BlockSpec((None, tm, D_MODEL), lambda bi, j: (bi, j, 0)),
            pl.BlockSpec((1, D_MODEL), lambda bi, j: (0, 0)),
            once(D_MODEL, IN_W),
            pl.BlockSpec((tm, half), lambda bi, j: (j, 0)),
            pl.BlockSpec((tm, half), lambda bi, j: (j, 0)),
            once(RET_HEADS, ret_blk, ret_blk),
            once(RET_HEADS, ret_blk, RET_QK_HEAD),
            once(RET_HEADS, ret_blk, RET_QK_HEAD),
            once(len(POOL_WINDOWS), POOL_GROUP_W, POOL_GROUP_W),
            once(1, D_MODEL),
            once(RET_V_W, D_MODEL),
            once(D_MODEL, D_MODEL),
            once(1, D_MODEL),
            once(D_MODEL, LANES),
            once(1, LANES),
        ],
        out_specs=(
            pl.BlockSpec((None, tm, D_MODEL), lambda bi, j: (bi, j, 0)),
            pl.BlockSpec((None, tm, D_MODEL), lambda bi, j: (bi, j, 0)),
            pl.BlockSpec((None, META_ROWS, tm), lambda bi, j: (tile(bi, j), 0, 0)),
            pl.BlockSpec((None, N_EXPERTS, 1), lambda bi, j: (tile(bi, j), 0, 0)),
        ),
        scratch_shapes=[pltpu.VMEM((tm, 3 * D_MODEL), BF16),
                        pltpu.VMEM((tm, RET_V_W), BF16),
                        pltpu.VMEM((tm, D_MODEL), BF16),
                        pltpu.VMEM((tm, D_MODEL), BF16),
                        pltpu.VMEM((tm, RET_V_W), BF16),
                        pltpu.VMEM((tm, RET_V_W), BF16),
                        pltpu.VMEM((RET_HEADS, RET_QK_HEAD, RET_V_HEAD), F32),
                        pltpu.VMEM((POOL_HALO, D_MODEL), BF16)],
        compiler_params=pltpu.CompilerParams(
            dimension_semantics=("arbitrary", "arbitrary"),
            vmem_limit_bytes=VMEM_LIMIT_BYTES),
        name="token_mixer",
    )(x3, g_mix, w_in_bf, cos, sin, dmask, q_dec, k_dec, w_pool_bf, pool_scale,
      w_ret_o_bf, w_out_bf, g_ffn, wr_split, b_router)


def _window_sum(ext, w, tm):
    cur = ext
    span = 1
    while span < w:
        cur = cur[span:, :] + cur[:-span, :]
        span *= 2
    start = POOL_HALO + 1 - w
    return cur[start:start + tm, :]


def _mix_route(pos0, u_bf, halo_bf, gate_a, gate_b, o_gated, x,
               wpool_ref, pscale_ref, wreto_ref, wout_ref, gffn_ref,
               wr_ref, br_ref,
               x1_ref, h2_ref, meta_ref, counts_ref):
    tm = x.shape[0]

    u = u_bf.astype(F32)
    ext = jnp.concatenate([halo_bf.astype(F32), u], axis=0)
    pos = (pos0 + lax.broadcasted_iota(jnp.int32, (tm, 1), 0)).astype(F32)
    pooled_out = []
    for g, w in enumerate(POOL_WINDOWS):
        cols = slice(g * POOL_GROUP_W, (g + 1) * POOL_GROUP_W)
        ws = _window_sum(ext[:, cols], w, tm)
        count = jnp.minimum(pos + 1.0, float(w))
        pooled = ws / count - u[:, cols]
        pooled_out.append(jnp.dot(pooled.astype(BF16), wpool_ref[g],
                                  preferred_element_type=F32))
    y_pool = jnp.concatenate(pooled_out, axis=-1) * pscale_ref[...]

    y_ret = jnp.dot(o_gated, wreto_ref[...], preferred_element_type=F32)
    merged = (jax.nn.sigmoid(gate_a.astype(F32)) * y_pool
              + jax.nn.sigmoid(gate_b.astype(F32)) * y_ret)
    x1 = x + jnp.dot(merged.astype(BF16), wout_ref[...], preferred_element_type=F32)
    x1_ref[...] = x1
    h2 = _rms(x1, gffn_ref[...])
    h2_ref[...] = h2.astype(BF16)

    h_hi = h2.astype(BF16)
    h_lo = (h2 - h_hi.astype(F32)).astype(BF16)
    parts = (jnp.dot(h_hi, wr_ref[...], preferred_element_type=F32)
             + jnp.dot(h_lo, wr_ref[...], preferred_element_type=F32)
             + br_ref[...])
    parts_t = parts.T
    logits_t = parts_t[:N_EXPERTS, :] + parts_t[N_EXPERTS:2 * N_EXPERTS, :]

    sub = lax.broadcasted_iota(jnp.int32, (N_EXPERTS, tm), 0)
    work = logits_t
    vals, idxs, hots = [], [], []
    for _ in range(TOP_K):
        m = jnp.max(work, axis=0, keepdims=True)
        idx = jnp.min(jnp.where(work == m, sub, N_EXPERTS), axis=0, keepdims=True)
        hot = sub == idx
        vals.append(m)
        idxs.append(idx)
        hots.append(hot)
        work = jnp.where(hot, -jnp.inf, work)
    exps = [jnp.exp(v - vals[0]) for v in vals]
    denom = exps[0] + exps[1] + exps[2] + exps[3]
    gates = [e / denom for e in exps]

    sel = (jnp.where(hots[0], 1.0, 0.0) + jnp.where(hots[1], 1.0, 0.0)
           + jnp.where(hots[2], 1.0, 0.0) + jnp.where(hots[3], 1.0, 0.0))
    row = lax.broadcasted_iota(jnp.int32, (tm, tm), 0)
    col = lax.broadcasted_iota(jnp.int32, (tm, tm), 1)
    earlier = jnp.where(row < col, 1.0, 0.0).astype(BF16)
    before = jnp.dot(sel.astype(BF16), earlier, preferred_element_type=F32)
    counts_ref[...] = jnp.sum(sel, axis=1, keepdims=True)

    msub = lax.broadcasted_iota(jnp.int32, (META_ROWS, tm), 0)
    meta = jnp.zeros((META_ROWS, tm), F32)
    for r in range(TOP_K):
        rank_r = jnp.sum(jnp.where(hots[r], before, 0.0), axis=0, keepdims=True)
        meta = jnp.where(msub == META_EXPERT + r, idxs[r].astype(F32), meta)
        meta = jnp.where(msub == META_RANK + r, rank_r, meta)
        meta = jnp.where(msub == META_GATE + r, gates[r], meta)
    meta_ref[...] = meta


def _local_positions(meta, lstart):
    tm = meta.shape[0]
    lane_e = lax.broadcasted_iota(jnp.int32, (tm, N_EXPERTS), 1)
    out = []
    for r in range(TOP_K):
        e_r = meta[:, META_EXPERT + r:META_EXPERT + r + 1].astype(jnp.int32)
        first = jnp.sum(jnp.where(lane_e == e_r, lstart, 0.0), axis=-1, keepdims=True)
        out.append((meta[:, META_RANK + r:META_RANK + r + 1] + first).astype(jnp.int32))
    return out


def _local_rows(tm):
    return TOP_K * tm + FILLER_ROWS


def _strip_priority(rows):
    return (rows // STRIP_ALIGN).bit_length() % 2


def _for_each_strip(count, max_rows, start_copy):
    top = 1 << (max(max_rows, FILLER_ROWS).bit_length() - 1)
    pieces = [top >> k for k in range(top.bit_length()) if (top >> k) >= STRIP_ALIGN]
    rare_from = 2 * max_rows * TOP_K // N_EXPERTS
    rare = [p for p in pieces if p >= rare_from]
    rare_bits = sum(rare)

    def cover(sizes, off):
        for piece in sizes:
            take = count & piece

            @pl.when(take != 0)
            def _(off=off, piece=piece):
                start_copy(off, piece)

            off = off + take

    @pl.when((count & rare_bits) != 0)
    def _():
        cover(rare, jnp.int32(0))

    cover([p for p in pieces if p < rare_from], count & rare_bits)


def _dispatch_kernel(blk_rows, base_ref, lstart_ref, cnt_ref, zflag_ref,
                     h_ref, meta_ref, lstart_col_ref, xs_ref,
                     sorted_ref, zero_ref, sems, zsem, tail_sem):
    i = pl.program_id(0)
    tm = h_ref.shape[0]
    nrows = _local_rows(tm)
    slot = lax.rem(i, DISPATCH_SLOTS)

    n_zero_blocks = xs_ref.shape[0] // blk_rows

    def zcopy(b, sem):
        start = pl.multiple_of(b * blk_rows, blk_rows)
        return pltpu.make_async_copy(zero_ref, xs_ref.at[pl.ds(start, blk_rows), :], sem)

    def for_flagged(flag, sem, act):
        def body(b, c):
            @pl.when(zflag_ref[b] == flag)
            def _():
                act(zcopy(b, sem))
            return c
        lax.fori_loop(0, n_zero_blocks, body, 0)

    @pl.when(i == 0)
    def _():
        zero_ref[...] = jnp.zeros_like(zero_ref)
        for_flagged(ZERO_FIRST, zsem, lambda cp: cp.start())
        for_flagged(ZERO_ANYTIME, tail_sem, lambda cp: cp.start())
        for_flagged(ZERO_FIRST, zsem, lambda cp: cp.wait())

    meta_t = meta_ref[...]
    sub_e = lax.broadcasted_iota(jnp.int32, (N_EXPERTS, tm), 0)
    pos = []
    for r in range(TOP_K):
        e_r = meta_t[META_EXPERT + r:META_EXPERT + r + 1, :].astype(jnp.int32)
        first = jnp.sum(jnp.where(sub_e == e_r, lstart_col_ref[...], 0.0), axis=0, keepdims=True)
        pos.append((meta_t[META_RANK + r:META_RANK + r + 1, :] + first).astype(jnp.int16))
    jr = lax.broadcasted_iota(jnp.int16, (nrows, tm), 0)
    one, zero = jnp.ones((), BF16), jnp.zeros((), BF16)
    onehot = jnp.where(jr == pos[0], one, jnp.where(jr == pos[1], one,
             jnp.where(jr == pos[2], one, jnp.where(jr == pos[3], one, zero))))
    srt = jnp.dot(onehot, h_ref[...], preferred_element_type=F32)
    sorted_ref[slot] = srt.astype(ROW_DTYPE)

    def per_segment(e, c):
        k = i * N_SEG + e
        src0 = lstart_ref[k]
        dst0 = base_ref[k]

        def start_copy(off, rows):
            src = pl.multiple_of(src0 + off, STRIP_ALIGN)
            dst = pl.multiple_of(dst0 + off, STRIP_ALIGN)
            pltpu.make_async_copy(sorted_ref.at[slot, pl.ds(src, rows), :],
                                  xs_ref.at[pl.ds(dst, rows), :],
                                  sems.at[slot]).start(priority=_strip_priority(rows))

        _for_each_strip(cnt_ref[k], tm, start_copy)
        return c

    lax.fori_loop(0, N_SEG, per_segment, 0)

    def wait_tile(s):
        pltpu.make_async_copy(sorted_ref.at[s], xs_ref.at[pl.ds(0, nrows), :],
                              sems.at[s]).wait()

    @pl.when(i >= DISPATCH_SLOTS - 1)
    def _():
        wait_tile(lax.rem(i + 1, DISPATCH_SLOTS))

    @pl.when(i == pl.num_programs(0) - 1)
    def _():
        for age in range(DISPATCH_SLOTS - 2, -1, -1):
            @pl.when(i >= age)
            def _(age=age):
                wait_tile(lax.rem(i - age, DISPATCH_SLOTS))
        for_flagged(ZERO_ANYTIME, tail_sem, lambda cp: cp.wait())


def _dispatch(h2, meta, tables, n_buf, blk_rows, tm):
    t = h2.shape[0]
    grid_spec = pltpu.PrefetchScalarGridSpec(
        num_scalar_prefetch=4,
        grid=(t // tm,),
        in_specs=[
            pl.BlockSpec((tm, D_MODEL), lambda i, *_: (i, 0)),
            pl.BlockSpec((None, META_ROWS, tm), lambda i, *_: (i, 0, 0)),
            pl.BlockSpec((None, N_EXPERTS, 1), lambda i, *_: (i, 0, 0)),
        ],
        out_specs=pl.BlockSpec(memory_space=pl.ANY),
        scratch_shapes=[pltpu.VMEM((DISPATCH_SLOTS, _local_rows(tm), D_MODEL), ROW_DTYPE),
                        pltpu.VMEM((blk_rows, D_MODEL), ROW_DTYPE),
                        pltpu.SemaphoreType.DMA((DISPATCH_SLOTS,)),
                        pltpu.SemaphoreType.DMA(()),
                        pltpu.SemaphoreType.DMA(())],
    )
    return pl.pallas_call(
        functools.partial(_dispatch_kernel, blk_rows),
        out_shape=jax.ShapeDtypeStruct((n_buf + SPARE_ROWS, D_MODEL), ROW_DTYPE),
        grid_spec=grid_spec,
        compiler_params=pltpu.CompilerParams(
            dimension_semantics=("arbitrary",),
            vmem_limit_bytes=VMEM_LIMIT_BYTES,
            has_side_effects=True),
        name="dispatch",
    )(tables["base_out"], tables["lstart"], tables["cnt"], tables["zflag"],
      h2, meta, tables["lstart_cols"])


def _experts_kernel(be_ref, nu_ref, next_ref, x_ref, wgu_hbm, bgu_ref, wd_hbm, bd_ref,
                    y_ref, wgu_stage, wd_stage, wgu_bf, wd_bf, sems):
    i = pl.program_id(0)

    def weight_copies(e):
        return (pltpu.make_async_copy(wgu_hbm.at[e], wgu_stage, sems.at[0]),
                pltpu.make_async_copy(wd_hbm.at[e], wd_stage, sems.at[1]))

    @pl.when(i < nu_ref[0])
    def _():
        e = be_ref[i]
        prev = be_ref[jnp.maximum(i - 1, 0)]

        @pl.when(i == 0)
        def _():
            for cp in weight_copies(e):
                cp.start()

        @pl.when((i == 0) | (e != prev))
        def _():
            for cp in weight_copies(e):
                cp.wait()
            wgu_bf[...] = wgu_stage[...].astype(BF16)
            wd_bf[...] = wd_stage[...].astype(BF16)
            nxt = next_ref[e]

            @pl.when(nxt != e)
            def _():
                for cp in weight_copies(nxt):
                    cp.start()

        gu = jnp.dot(x_ref[...].astype(BF16), wgu_bf[...],
                     preferred_element_type=F32) + bgu_ref[...]
        glu = jnp.minimum(gu[:, :D_FF], SWIGLU_LIMIT)
        lin = jnp.clip(gu[:, D_FF:], -SWIGLU_LIMIT, SWIGLU_LIMIT)
        act = glu * jax.nn.sigmoid(SWIGLU_ALPHA * glu) * (lin + 1.0)
        y_ref[...] = (jnp.dot(act.astype(BF16), wd_bf[...],
                              preferred_element_type=F32) + bd_ref[...]).astype(ROW_DTYPE)


def _experts(xs, n_buf, block_e, n_used, next_expert, w_gate_up, b_gate_up, w_down, b_down,
             blk_rows):
    used = lambda i, be, nu, nx: jnp.minimum(i, nu[0] - 1)
    grid_spec = pltpu.PrefetchScalarGridSpec(
        num_scalar_prefetch=3,
        grid=(n_buf // blk_rows,),
        in_specs=[
            pl.BlockSpec((blk_rows, D_MODEL), lambda i, be, nu, nx: (used(i, be, nu, nx), 0)),
            pl.BlockSpec(memory_space=pl.ANY),
            pl.BlockSpec((None, 1, 2 * D_FF), lambda i, be, nu, nx: (be[i], 0, 0)),
            pl.BlockSpec(memory_space=pl.ANY),
            pl.BlockSpec((None, 1, D_MODEL), lambda i, be, nu, nx: (be[i], 0, 0)),
        ],
        out_specs=pl.BlockSpec((blk_rows, D_MODEL),
                               lambda i, be, nu, nx: (used(i, be, nu, nx), 0)),
        scratch_shapes=[pltpu.VMEM((D_MODEL, 2 * D_FF), F32),
                        pltpu.VMEM((D_FF, D_MODEL), F32),
                        pltpu.VMEM((D_MODEL, 2 * D_FF), BF16),
                        pltpu.VMEM((D_FF, D_MODEL), BF16),
                        pltpu.SemaphoreType.DMA((2,))],
    )
    return pl.pallas_call(
        _experts_kernel,
        out_shape=jax.ShapeDtypeStruct(xs.shape, xs.dtype),
        input_output_aliases={3: 0},
        grid_spec=grid_spec,
        compiler_params=pltpu.CompilerParams(
            dimension_semantics=("arbitrary",),
            vmem_limit_bytes=VMEM_LIMIT_BYTES),
        name="experts",
    )(block_e, n_used, next_expert, xs, w_gate_up, b_gate_up[:, None, :], w_down,
      b_down[:, None, :])


def _unsort_weights(meta_t, lstart_row, nrows):
    tm = meta_t.shape[1]
    meta = jnp.concatenate(
        [meta_t, jnp.zeros((LANES - META_ROWS, tm), F32)], axis=0).T
    pos = _local_positions(meta, lstart_row)
    jl = lax.broadcasted_iota(jnp.int16, (tm, nrows), 1)
    pos = [p.astype(jnp.int16) for p in pos]
    gate = [meta[:, META_GATE + r:META_GATE + r + 1].astype(BF16) for r in range(TOP_K)]
    zero = jnp.zeros((), BF16)
    return jnp.where(jl == pos[0], gate[0], jnp.where(jl == pos[1], gate[1],
           jnp.where(jl == pos[2], gate[2], jnp.where(jl == pos[3], gate[3], zero))))


def _combine_kernel(base_ref, lstart_ref, cnt_ref, meta_ref, lstart_row_ref,
                    meta_next_ref, lstart_row_next_ref, x1_ref,
                    p_ref, ys_ref, gple_ref, wpg_ref, wple_ref, gfin_ref, out_ref,
                    rows_ref, unsort_ref, sems):
    i = pl.program_id(0)
    n = pl.num_programs(0)
    tm = x1_ref.shape[0]
    nrows = _local_rows(tm)
    slot = lax.rem(i, 2)

    def fetch_tile(tile, s):
        def per_segment(e, c):
            k = tile * N_SEG + e
            src0 = base_ref[k]
            dst0 = lstart_ref[k]

            def start_copy(off, rows):
                src = pl.multiple_of(src0 + off, STRIP_ALIGN)
                dst = pl.multiple_of(dst0 + off, STRIP_ALIGN)
                pltpu.make_async_copy(ys_ref.at[pl.ds(src, rows), :],
                                      rows_ref.at[s, pl.ds(dst, rows), :],
                                      sems.at[s]).start(priority=_strip_priority(rows))

            _for_each_strip(cnt_ref[k], tm, start_copy)
            return c

        lax.fori_loop(0, N_SEG, per_segment, 0)

    @pl.when(i == 0)
    def _():
        fetch_tile(0, 0)
        unsort_ref[0] = _unsort_weights(meta_ref[...], lstart_row_ref[...], nrows)

    @pl.when(i + 1 < n)
    def _():
        fetch_tile(i + 1, 1 - slot)

    pltpu.make_async_copy(ys_ref.at[pl.ds(0, nrows), :], rows_ref.at[slot],
                          sems.at[slot]).wait()
    moe = jnp.dot(unsort_ref[slot], rows_ref[slot].astype(BF16),
                  preferred_element_type=F32)
    x2 = x1_ref[...] + moe
    u = _rms(x2, gple_ref[...])
    gate_ple = jax.nn.sigmoid(jnp.dot(u.astype(BF16), wpg_ref[...],
                                      preferred_element_type=F32))
    emb = jnp.dot(p_ref[...].astype(BF16), wple_ref[...], preferred_element_type=F32)
    x3 = x2 + gate_ple * emb
    out_ref[...] = _rms(x3, gfin_ref[...])

    unsort_ref[1 - slot] = _unsort_weights(meta_next_ref[...], lstart_row_next_ref[...], nrows)


def _combine(meta, tables, x1, p2, ys, g_ple, w_ple_gate_bf, w_ple_bf, g_final, tm):
    t = x1.shape[0]
    full = lambda *shape: pl.BlockSpec(shape, lambda i, *_: (0,) * len(shape))
    last = t // tm - 1
    nxt = lambda i: jnp.minimum(i + 1, last)
    grid_spec = pltpu.PrefetchScalarGridSpec(
        num_scalar_prefetch=3,
        grid=(t // tm,),
        in_specs=[
            pl.BlockSpec((None, META_ROWS, tm), lambda i, *_: (i, 0, 0)),
            pl.BlockSpec((None, 1, N_EXPERTS), lambda i, *_: (i, 0, 0)),
            pl.BlockSpec((None, META_ROWS, tm), lambda i, *_: (nxt(i), 0, 0)),
            pl.BlockSpec((None, 1, N_EXPERTS), lambda i, *_: (nxt(i), 0, 0)),
            pl.BlockSpec((tm, D_MODEL), lambda i, *_: (i, 0)),
            pl.BlockSpec((tm, PLE_DIM), lambda i, *_: (i, 0)),
            pl.BlockSpec(memory_space=pl.ANY),
            full(1, D_MODEL),
            full(D_MODEL, D_MODEL),
            full(PLE_DIM, D_MODEL),
            full(1, D_MODEL),
        ],
        out_specs=pl.BlockSpec((tm, D_MODEL), lambda i, *_: (i, 0)),
        scratch_shapes=[pltpu.VMEM((2, _local_rows(tm), D_MODEL), ROW_DTYPE),
                        pltpu.VMEM((2, tm, _local_rows(tm)), BF16),
                        pltpu.SemaphoreType.DMA((2,))],
    )
    return pl.pallas_call(
        _combine_kernel,
        out_shape=jax.ShapeDtypeStruct((t, D_MODEL), F32),
        grid_spec=grid_spec,
        compiler_params=pltpu.CompilerParams(
            dimension_semantics=("arbitrary",),
            vmem_limit_bytes=VMEM_LIMIT_BYTES),
        name="combine",
    )(tables["base_in"], tables["lstart"], tables["cnt"], meta, tables["lstart_rows"],
      meta, tables["lstart_rows"], x1, p2, ys, g_ple, w_ple_gate_bf, w_ple_bf, g_final)


def _tiles(seq_len):
    return dict(
        ret_block=min(256, seq_len),
        route_rows=min(512, seq_len),
        expert_rows=512,
    )


def _slot_tables(tile_counts, blk, n_buf, nloc):
    cnt = tile_counts[:, :, 0].astype(jnp.int32)
    cnt = ((cnt + STRIP_ALIGN - 1) // STRIP_ALIGN) * STRIP_ALIGN
    seg_rows = jnp.sum(cnt, axis=1)
    counts = jnp.sum(cnt, axis=0)
    padded = ((counts + blk - 1) // blk) * blk
    ends_pad = jnp.cumsum(padded)
    starts_pad = ends_pad - padded
    tile_before = jnp.cumsum(cnt, axis=0) - cnt
    lstart = jnp.cumsum(cnt, axis=1) - cnt
    base = starts_pad[None, :] + tile_before
    with_filler = lambda a, col: jnp.concatenate([a, col[:, None]], axis=1).reshape(-1)
    blk_start = jnp.arange((n_buf + SPARE_ROWS) // blk, dtype=jnp.int32) * blk
    last_of_expert = jnp.any((padded > 0)[None, :]
                             & (blk_start[:, None] == (ends_pad - blk)[None, :]), axis=1)
    zflag = jnp.where(last_of_expert | (blk_start >= n_buf), ZERO_FIRST,
                      jnp.where(blk_start >= ends_pad[-1], ZERO_ANYTIME, 0)).astype(jnp.int32)
    filler_dst = n_buf + (jnp.arange(cnt.shape[0], dtype=jnp.int32) % DISPATCH_SLOTS) * FILLER_ROWS
    ids = jnp.arange(N_EXPERTS, dtype=jnp.int32)
    later = jnp.where((padded > 0)[None, :] & (ids[None, :] > ids[:, None]), ids[None, :], N_EXPERTS)
    next_expert = jnp.min(later, axis=1)
    next_expert = jnp.where(next_expert == N_EXPERTS, ids, next_expert).astype(jnp.int32)
    return dict(
        cnt=with_filler(cnt, nloc - seg_rows),
        lstart=with_filler(lstart, seg_rows),
        base_out=with_filler(base, filler_dst),
        base_in=with_filler(base, jnp.zeros_like(seg_rows)),
        lstart_rows=lstart.astype(F32)[:, None, :],
        lstart_cols=lstart.astype(F32)[:, :, None],
        zflag=zflag, ends_pad=ends_pad, next_expert=next_expert)


def _layer(x2, p2, seq_len, g_mix, w_in, w_pool, pool_scale, w_ret_o, w_out, g_ffn,
           w_router, b_router, w_gate_up, b_gate_up, w_down, b_down, g_ple,
           w_ple_gate, w_ple, g_out):
    t = x2.shape[0]
    cfg = _tiles(seq_len)
    row = lambda a: a.reshape(1, -1)

    wr_hi = w_router.astype(BF16)
    wr_lo = (w_router - wr_hi.astype(F32)).astype(BF16)
    wr_split = jnp.pad(jnp.concatenate([wr_hi, wr_lo], axis=1),
                       ((0, 0), (0, LANES - 2 * N_EXPERTS)))
    lane_pad = ((0, 0), (0, LANES - N_EXPERTS))
    tm = cfg["route_rows"]
    x1, h2, meta, tile_counts = _token_mixer(
        x2.reshape(t // seq_len, seq_len, D_MODEL), row(g_mix), w_in.astype(BF16),
        w_pool.astype(BF16), row(pool_scale), w_ret_o.astype(BF16), w_out.astype(BF16),
        row(g_ffn), wr_split, jnp.pad(row(b_router), lane_pad), tm, cfg["ret_block"])
    x1 = x1.reshape(t, D_MODEL)
    h2 = h2.reshape(t, D_MODEL)

    blk = cfg["expert_rows"]
    n_buf = t * TOP_K + (t // tm) * FILLER_ROWS + N_EXPERTS * blk
    n_blocks = n_buf // blk
    tables = _slot_tables(tile_counts, blk, n_buf, _local_rows(tm))
    blk_start = jnp.arange(n_blocks, dtype=jnp.int32) * blk
    block_e = jnp.minimum(
        jnp.sum((tables["ends_pad"][None, :] <= blk_start[:, None]).astype(jnp.int32), axis=1),
        N_EXPERTS - 1)
    n_used = tables["ends_pad"][-1:] // blk

    xs = _dispatch(h2, meta, tables, n_buf, blk, tm)
    ys = _experts(xs, n_buf, block_e, n_used, tables["next_expert"], w_gate_up, b_gate_up,
                  w_down, b_down, blk)
    return _combine(meta, tables, x1, p2, ys, row(g_ple), w_ple_gate.astype(BF16),
                    w_ple.astype(BF16), row(g_out), tm)


def kernel(x, p, g_mix, w_in, w_pool, pool_scale, w_ret_o, w_out, g_ffn, w_router,
           b_router, w_gate_up, b_gate_up, w_down, b_down, g_ple, w_ple_gate, w_ple,
           g_final):
    b, s, d = x.shape
    depth = p.shape[0]
    assert depth == 1 and d == D_MODEL
    x2 = x.reshape(b * s, d)
    out = _layer(x2, p[0].reshape(b * s, PLE_DIM), s, g_mix[0], w_in[0], w_pool[0],
                 pool_scale[0], w_ret_o[0], w_out[0], g_ffn[0], w_router[0],
                 b_router[0], w_gate_up[0], b_gate_up[0], w_down[0], b_down[0],
                 g_ple[0], w_ple_gate[0], w_ple[0], g_final)
    return out.reshape(b, s, d)
```
